```python
import math
import jax, jax.numpy as jnp
from jax import lax
import numpy as np

D_MODEL = 1024
BATCH = 2
SEQ = 16384
DEPTH = 2

S5_WIDTH = D_MODEL // 2
S5_GROUP_CH = 16
S5_GROUPS = S5_WIDTH // S5_GROUP_CH
S5_STATE = 64
S5_DT_MIN = 0.001
S5_DT_MAX = 0.1
GLA_WIDTH = D_MODEL - S5_WIDTH
GLA_HEADS = 4
GLA_DV = GLA_WIDTH // GLA_HEADS
GLA_DK = GLA_DV // 2
GLA_KEY = GLA_HEADS * GLA_DK
GLA_GATE_RANK = 16
GLA_TAU = 16.0
GLA_CHUNK = 64
AB_SPLITS = [S5_WIDTH, S5_WIDTH + GLA_KEY, S5_WIDTH + 2 * GLA_KEY,
             S5_WIDTH + 2 * GLA_KEY + GLA_WIDTH, S5_WIDTH + 2 * GLA_KEY + 2 * GLA_WIDTH]
AB_IN = S5_WIDTH + 2 * GLA_KEY + 2 * GLA_WIDTH + 2 * GLA_GATE_RANK
HEAD_DIM = 64
N_HEADS = D_MODEL // HEAD_DIM
N_KV_HEADS = 4
GQA_GROUP = N_HEADS // N_KV_HEADS
ATT_DIM = N_HEADS * HEAD_DIM
KV_DIM = N_KV_HEADS * HEAD_DIM
WINDOW = 128
ATT_BLOCK = 128
ATT_SCALE = HEAD_DIM ** -0.5
REL_BUCKETS = 32
REL_MAX_DIST = 128
NEG_INF = -1e30
N_GROUPS = 4
EXPERTS_PER_GROUP = 8
N_EXPERTS = N_GROUPS * EXPERTS_PER_GROUP
TOP_K = 2
D_EXPERT = 512
MOE_BLOCK = 128
LN_EPS = 1e-5
RMS_EPS = 1e-6
DEEPNORM_ALPHA = (2 * DEPTH) ** 0.25
DEEPNORM_BETA = (8 * DEPTH) ** -0.25
N_EVEN = (DEPTH + 1) // 2
N_ODD = DEPTH // 2

kernel_name = "hybrid_s5_gla_swa_hmoe_deepnorm"


def layer_norm(x, g, b):
    xf = x.astype(jnp.float32)
    mu = jnp.mean(xf, -1, keepdims=True)
    var = jnp.mean(jnp.square(xf - mu), -1, keepdims=True)
    y = (xf - mu) * lax.rsqrt(var + LN_EPS) * g.astype(jnp.float32) + b.astype(jnp.float32)
    return y.astype(x.dtype)


def _complex_linear_combine(e1, e2):
    a1r, a1i, b1r, b1i = e1
    a2r, a2i, b2r, b2i = e2
    ar = a2r * a1r - a2i * a1i
    ai = a2r * a1i + a2i * a1r
    br = a2r * b1r - a2i * b1i + b2r
    bi = a2r * b1i + a2i * b1r + b2i
    return ar, ai, br, bi


def s5_bidir(u, lam_re, lam_im, log_dt, b_re, b_im, c_re, c_im, d, glu_w, glu_b):
    bsz, L, _ = u.shape
    uf = u.astype(jnp.float32).reshape(bsz, L, S5_GROUPS, S5_GROUP_CH)
    y = uf * d.astype(jnp.float32).reshape(S5_GROUPS, S5_GROUP_CH)
    for direction in range(2):
        lr = jnp.minimum(lam_re[direction].astype(jnp.float32), -1e-4)
        li = lam_im[direction].astype(jnp.float32)
        dt = jnp.exp(log_dt[direction].astype(jnp.float32))[:, None]
        mag = jnp.exp(lr * dt)
        ar = mag * jnp.cos(li * dt)
        ai = mag * jnp.sin(li * dt)
        den = lr * lr + li * li
        nr = ar - 1.0
        coef_r = (nr * lr + ai * li) / den
        coef_i = (ai * lr - nr * li) / den
        br_ = b_re[direction].astype(jnp.float32)
        bi_ = b_im[direction].astype(jnp.float32)
        bbr = coef_r[..., None] * br_ - coef_i[..., None] * bi_
        bbi = coef_r[..., None] * bi_ + coef_i[..., None] * br_
        xr = jnp.einsum('blgc,gpc->blgp', uf, bbr)
        xi = jnp.einsum('blgc,gpc->blgp', uf, bbi)
        a_r = jnp.broadcast_to(ar[None, None], (1, L, S5_GROUPS, S5_STATE))
        a_i = jnp.broadcast_to(ai[None, None], (1, L, S5_GROUPS, S5_STATE))
        _, _, hr, hi = lax.associative_scan(_complex_linear_combine, (a_r, a_i, xr, xi),
                                            reverse=(direction == 1), axis=1)
        y = y + (jnp.einsum('blgp,gcp->blgc', hr, c_re[direction].astype(jnp.float32))
                 - jnp.einsum('blgp,gcp->blgc', hi, c_im[direction].astype(jnp.float32)))
    y = jax.nn.gelu(y.reshape(bsz, L, S5_WIDTH))
    y = y * jax.nn.sigmoid(y @ glu_w.astype(jnp.float32) + glu_b.astype(jnp.float32))
    return y.astype(u.dtype)


def gla_chunked(q, k, v, log_a, strict):
    bsz, L, H, dk = q.shape
    dv = v.shape[-1]
    n = L // GLA_CHUNK
    r = lambda t: t.reshape(bsz, n, GLA_CHUNK, H, t.shape[-1])
    q, k, v, log_a = r(q), r(k), r(v), r(log_a)
    bc = jnp.cumsum(log_a, axis=2)
    b_last = bc[:, :, -1]
    qd = q * jnp.exp(bc)
    kd = k * jnp.exp(-bc)
    mask = jnp.tril(jnp.ones((GLA_CHUNK, GLA_CHUNK), bool), k=-1 if strict else 0)
    s = jnp.where(mask, jnp.einsum('bnihd,bnjhd->bnhij', qd, kd), 0.0)
    o_intra = jnp.einsum('bnhij,bnjhe->bnihe', s, v)
    kc = k * jnp.exp(b_last[:, :, None] - bc)
    chunk_kv = jnp.einsum('bnjhd,bnjhe->bnhde', kc, v)
    decay = jnp.exp(b_last)

    def step(state, inp):
        dec, kv = inp
        return state * dec[..., None] + kv, state

    s0 = jnp.zeros((bsz, H, dk, dv), q.dtype)
    _, s_prev = lax.scan(step, s0, (jnp.moveaxis(decay, 1, 0), jnp.moveaxis(chunk_kv, 1, 0)))
    s_prev = jnp.moveaxis(s_prev, 0, 1)
    o_inter = jnp.einsum('bnihd,bnhde->bnihe', qd, s_prev)
    return (o_intra + o_inter).reshape(bsz, L, H, dv)


def gla_bidir(q, k, v, go, lr, gate_w, gate_b, norm_g):
    bsz, L, _ = q.shape
    f32 = jnp.float32
    qf = q.astype(f32).reshape(bsz, L, GLA_HEADS, GLA_DK) * (GLA_DK ** -0.5)
    kf = k.astype(f32).reshape(bsz, L, GLA_HEADS, GLA_DK)
    vf = v.astype(f32).reshape(bsz, L, GLA_HEADS, GLA_DV)
    lrf = lr.astype(f32).reshape(bsz, L, 2, GLA_GATE_RANK)
    z = jnp.einsum('blsr,srk->blsk', lrf, gate_w.astype(f32)) + gate_b.astype(f32)
    log_a = (jax.nn.log_sigmoid(z) / GLA_TAU).reshape(bsz, L, 2, GLA_HEADS, GLA_DK)
    o = gla_chunked(qf, kf, vf, log_a[:, :, 0], strict=False)
    flip = lambda t: jnp.flip(t, axis=1)
    o = o + flip(gla_chunked(flip(qf), flip(kf), flip(vf), flip(log_a[:, :, 1]), strict=True))
    o = o * lax.rsqrt(jnp.mean(o * o, -1, keepdims=True) + RMS_EPS) * norm_g.astype(f32)
    o = o.reshape(bsz, L, GLA_WIDTH) * jax.nn.silu(go.astype(f32))
    return o.astype(q.dtype)


def ssm_gla_mixer(x, w_in, lam_re, lam_im, log_dt, b_re, b_im, c_re, c_im, d, glu_w, glu_b,
                  gate_w, gate_b, norm_g, w_out):
    h = x @ w_in
    u, q, k, v, go, lr = jnp.split(h, AB_SPLITS, axis=-1)
    ya = s5_bidir(u, lam_re, lam_im, log_dt, b_re, b_im, c_re, c_im, d, glu_w, glu_b)
    yb = gla_bidir(q, k, v, go, lr, gate_w, gate_b, norm_g)
    return jnp.concatenate([ya, yb], axis=-1) @ w_out


def t5_bucket(rel):
    nb = REL_BUCKETS // 2
    max_exact = nb // 2
    ret = (rel > 0).astype(jnp.int32) * nb
    n = jnp.abs(rel)
    large = max_exact + (jnp.log(jnp.maximum(n, 1).astype(jnp.float32) / max_exact)
                         / math.log(REL_MAX_DIST / max_exact) * (nb - max_exact)).astype(jnp.int32)
    large = jnp.minimum(large, nb - 1)
    return ret + jnp.where(n < max_exact, n, large)


def window_gqa_mixer(x, w_in, sink, w_out, rel_bias):
    bsz, L, _ = x.shape
    h = x @ w_in
    q, k, v = jnp.split(h, [ATT_DIM, ATT_DIM + KV_DIM], axis=-1)
    k = k.reshape(bsz, L, N_KV_HEADS, HEAD_DIM)
    v = v.reshape(bsz, L, N_KV_HEADS, HEAD_DIM)
    nb = L // ATT_BLOCK
    kw = ATT_BLOCK + 2 * WINDOW
    pad = ((0, 0), (WINDOW, WINDOW), (0, 0), (0, 0))
    kp, vp = jnp.pad(k, pad), jnp.pad(v, pad)
    rel = jnp.arange(kw)[None, :] - WINDOW - jnp.arange(ATT_BLOCK)[:, None]
    band = jnp.abs(rel) <= WINDOW
    bias = rel_bias.astype(jnp.float32)[t5_bucket(rel)]
    bias = jnp.transpose(bias, (2, 0, 1)).reshape(N_KV_HEADS, GQA_GROUP, ATT_BLOCK, kw)
    sink_f = sink.astype(jnp.float32).reshape(N_KV_HEADS, GQA_GROUP)[..., None]
    qb = jnp.moveaxis(q.reshape(bsz, nb, ATT_BLOCK, N_KV_HEADS, GQA_GROUP, HEAD_DIM), 1, 0)

    def attend(args):
        n, qblk = args
        start = n * ATT_BLOCK
        kb = lax.dynamic_slice_in_dim(kp, start, kw, axis=1)
        vb = lax.dynamic_slice_in_dim(vp, start, kw, axis=1)
        kpos = start - WINDOW + jnp.arange(kw)
        valid = band & ((kpos >= 0) & (kpos < L))[None, :]
        s = jnp.einsum('bikgd,bjkd->bkgij', qblk, kb).astype(jnp.float32) * ATT_SCALE + bias
        s = jnp.where(valid, s, NEG_INF)
        m = jnp.maximum(jnp.max(s, -1), sink_f)
        p = jnp.exp(s - m[..., None])
        den = jnp.sum(p, -1) + jnp.exp(sink_f - m)
        o = jnp.einsum('bkgij,bjkd->bkgid', p, vb.astype(jnp.float32)) / den[..., None]
        return jnp.transpose(o, (0, 3, 1, 2, 4)).astype(x.dtype)

    o = lax.map(attend, (jnp.arange(nb), qb))
    o = jnp.moveaxis(o, 0, 1).reshape(bsz, L, ATT_DIM)
    return o @ w_out


def hier_moe(x, w_group, b_group, w_router, b_router, w1, w3, w2):
    bsz, L, D = x.shape
    xt = x.reshape(-1, D)
    T = xt.shape[0]
    tok = jnp.arange(T)
    gp = jax.nn.softmax((xt @ w_group + b_group).astype(jnp.float32), axis=-1)
    grp = jnp.argmax(gp, axis=-1)
    p_grp = gp[tok, grp][:, None]
    el = (xt @ w_router + b_router).astype(jnp.float32).reshape(T, N_GROUPS, EXPERTS_PER_GROUP)
    el = el[tok, grp]
    top_l, top_i = lax.top_k(el, TOP_K)
    gate = jax.nn.softmax(top_l, axis=-1) * p_grp
    eid = grp[:, None] * EXPERTS_PER_GROUP + top_i

    n_assign = T * TOP_K
    fe = eid.reshape(-1)
    ft = jnp.repeat(tok, TOP_K)
    fw = gate.reshape(-1).astype(x.dtype)
    order = jnp.argsort(fe)
    se = fe[order]
    counts = jnp.bincount(fe, length=N_EXPERTS)
    padded = (counts + MOE_BLOCK - 1) // MOE_BLOCK * MOE_BLOCK
    pad_end = jnp.cumsum(padded)
    pad_start = pad_end - padded
    start = jnp.cumsum(counts) - counts
    dest = pad_start[se] + jnp.arange(n_assign) - start[se]
    cap = n_assign + N_EXPERTS * MOE_BLOCK
    nblk = cap // MOE_BLOCK
    buf_tok = jnp.zeros((cap,), jnp.int32).at[dest].set(ft[order].astype(jnp.int32))
    buf_w = jnp.zeros((cap,), x.dtype).at[dest].set(fw[order])
    blk_e = jnp.minimum(jnp.searchsorted(pad_end, jnp.arange(nblk) * MOE_BLOCK, side='right'),
                        N_EXPERTS - 1)

    def expert_block(args):
        t_idx, wt, e = args
        xb = xt[t_idx]
        hdn = jax.nn.silu(xb @ w1[e]) * (xb @ w3[e])
        return (hdn @ w2[e]) * wt[:, None]

    yb = lax.map(expert_block, (buf_tok.reshape(nblk, MOE_BLOCK), buf_w.reshape(nblk, MOE_BLOCK), blk_e))
    y = jnp.zeros_like(xt).at[buf_tok].add(yb.reshape(cap, D))
    return y.reshape(bsz, L, D)


def setup_inputs(seed: int = 0) -> dict:
    key = jax.random.key(seed)
    keys = iter(jax.random.split(key, 48))

    def nrm(shape, scale):
        return jax.random.normal(next(keys), shape, jnp.float32) * scale

    ne, no, nl = N_EVEN, N_ODD, DEPTH
    s5 = (ne, 2, S5_GROUPS, S5_STATE)
    return {
        "x": nrm((BATCH, SEQ, D_MODEL), 1.0),
        "ln_mix_g": 1.0 + nrm((nl, D_MODEL), 0.02),
        "ln_mix_b": nrm((nl, D_MODEL), 0.02),
        "ln_ffn_g": 1.0 + nrm((nl, D_MODEL), 0.02),
        "ln_ffn_b": nrm((nl, D_MODEL), 0.02),
        "ab_w_in": nrm((ne, D_MODEL, AB_IN), D_MODEL ** -0.5),
        "s5_lam_re": -0.5 * jnp.exp(nrm(s5, 0.02)),
        "s5_lam_im": jnp.pi * jnp.arange(S5_STATE, dtype=jnp.float32) + nrm(s5, 0.02),
        "s5_log_dt": jax.random.uniform(next(keys), (ne, 2, S5_GROUPS), jnp.float32,
                                        math.log(S5_DT_MIN), math.log(S5_DT_MAX)),
        "s5_b_re": nrm((ne, 2, S5_GROUPS, S5_STATE, S5_GROUP_CH), (2 * S5_GROUP_CH) ** -0.5),
        "s5_b_im": nrm((ne, 2, S5_GROUPS, S5_STATE, S5_GROUP_CH), (2 * S5_GROUP_CH) ** -0.5),
        "s5_c_re": nrm((ne, 2, S5_GROUPS, S5_GROUP_CH, S5_STATE), (2 * S5_STATE) ** -0.5),
        "s5_c_im": nrm((ne, 2, S5_GROUPS, S5_GROUP_CH, S5_STATE), (2 * S5_STATE) ** -0.5),
        "s5_d": nrm((ne, S5_WIDTH), 1.0),
        "s5_glu_w": nrm((ne, S5_WIDTH, S5_WIDTH), S5_WIDTH ** -0.5),
        "s5_glu_b": nrm((ne, S5_WIDTH), 0.02),
        "gla_gate_w": nrm((ne, 2, GLA_GATE_RANK, GLA_KEY), GLA_GATE_RANK ** -0.5),
        "gla_gate_b": nrm((ne, 2, GLA_KEY), 0.1),
        "gla_norm_g": 1.0 + nrm((ne, GLA_HEADS, GLA_DV), 0.02),
        "ab_w_out": nrm((ne, D_MODEL, D_MODEL), D_MODEL ** -0.5 * DEEPNORM_BETA),
        "c_w_in": nrm((no, D_MODEL, ATT_DIM + 2 * KV_DIM), D_MODEL ** -0.5),
        "c_sink": nrm((no, N_HEADS), 0.5),
        "c_w_out": nrm((no, ATT_DIM, D_MODEL), ATT_DIM ** -0.5 * DEEPNORM_BETA),
        "rel_bias": nrm((REL_BUCKETS, N_HEADS), 0.1),
        "moe_w_group": nrm((nl, D_MODEL, N_GROUPS), D_MODEL ** -0.5),
        "moe_b_group": nrm((nl, N_GROUPS), 0.01),
        "moe_w_router": nrm((nl, D_MODEL, N_EXPERTS), D_MODEL ** -0.5),
        "moe_b_router": nrm((nl, N_EXPERTS), 0.01),
        "moe_w1": nrm((nl, N_EXPERTS, D_MODEL, D_EXPERT), D_MODEL ** -0.5),
        "moe_w3": nrm((nl, N_EXPERTS, D_MODEL, D_EXPERT), D_MODEL ** -0.5),
        "moe_w2": nrm((nl, N_EXPERTS, D_EXPERT, D_MODEL), D_EXPERT ** -0.5 * DEEPNORM_BETA),
    }


def reference(x, ln_mix_g, ln_mix_b, ln_ffn_g, ln_ffn_b, ab_w_in, s5_lam_re, s5_lam_im,
              s5_log_dt, s5_b_re, s5_b_im, s5_c_re, s5_c_im, s5_d, s5_glu_w, s5_glu_b,
              gla_gate_w, gla_gate_b, gla_norm_g, ab_w_out, c_w_in, c_sink, c_w_out, rel_bias,
              moe_w_group, moe_b_group, moe_w_router, moe_b_router, moe_w1, moe_w3, moe_w2):
    h = x
    for layer in range(DEPTH):
        i = layer // 2
        if layer % 2 == 0:
            y = ssm_gla_mixer(h, ab_w_in[i], s5_lam_re[i], s5_lam_im[i], s5_log_dt[i],
                              s5_b_re[i], s5_b_im[i], s5_c_re[i], s5_c_im[i], s5_d[i],
                              s5_glu_w[i], s5_glu_b[i], gla_gate_w[i], gla_gate_b[i],
                              gla_norm_g[i], ab_w_out[i])
        else:
            y = window_gqa_mixer(h, c_w_in[i], c_sink[i], c_w_out[i], rel_bias)
        h = layer_norm(DEEPNORM_ALPHA * h + y, ln_mix_g[layer], ln_mix_b[layer])
        y = hier_moe(h, moe_w_group[layer], moe_b_group[layer], moe_w_router[layer],
                     moe_b_router[layer], moe_w1[layer], moe_w3[layer], moe_w2[layer])
        h = layer_norm(DEEPNORM_ALPHA * h + y, ln_ffn_g[layer], ln_ffn_b[layer])
    return h
```

```python
import functools
import math

import jax
import jax.numpy as jnp
from jax import lax
from jax.experimental import pallas as pl
from jax.experimental.pallas import tpu as pltpu

F32 = jnp.float32
BF16 = jnp.bfloat16

D_MODEL = 1024
S5_WIDTH = 512
S5_GROUP_CH = 16
S5_GROUPS = 32
S5_STATE = 64
GLA_HEADS = 4
GLA_DV = 128
GLA_DK = 64
GLA_KEY = 256
GLA_WIDTH = 512
GLA_RANK = 16
GLA_TAU = 16.0
GLA_CHUNK = 64
AB_IN = 2080
AB_IN_PAD = 2176
HEAD_DIM = 64
N_HEADS = 16
N_KV = 4
GQA = 4
ATT_DIM = 1024
KV_DIM = 256
WINDOW = 128
ATT_BLOCK = 128
ATT_SCALE = HEAD_DIM ** -0.5
REL_BUCKETS = 32
REL_MAX_DIST = 128
NEG_INF = -1e30
N_GROUPS = 4
EPG = 8
N_EXPERTS = 32
D_EXPERT = 512
LN_EPS = 1e-5
RMS_EPS = 1e-6
DEPTH = 2
ALPHA = (2 * DEPTH) ** 0.25

S5_CHUNK = 16
S5_SEGS = 8
ROW_TILE = 512
GLA_ROWS = 256
ATT_ROWS = 512
MOE_ROWS = 256
VMEM_LIMIT = 56 * 1024 * 1024


def _cparams(n_axes):
    return pltpu.CompilerParams(dimension_semantics=("arbitrary",) * n_axes,
                                vmem_limit_bytes=VMEM_LIMIT)


def _dot(a, b):
    return jnp.dot(a, b, preferred_element_type=F32)


def _dot_nt(a, b):
    return lax.dot_general(a, b, (((1,), (1,)), ((), ())), preferred_element_type=F32)


def _dot_tn(a, b):
    return lax.dot_general(a, b, (((0,), (0,)), ((), ())), preferred_element_type=F32)


def _layer_norm(r, g, b):
    mu = jnp.mean(r, axis=-1, keepdims=True)
    c = r - mu
    var = jnp.mean(c * c, axis=-1, keepdims=True)
    return c * lax.rsqrt(var + LN_EPS) * g + b


def _store_token_tiles(ref, val):
    n = val.shape[0]
    for k in range(D_MODEL // 128):
        ref[pl.ds(k, n, stride=8), :] = val[:, k * 128:(k + 1) * 128]


def _load_token_tiles(ref, n):
    return jnp.concatenate([ref[pl.ds(k, n, stride=8), :] for k in range(D_MODEL // 128)], axis=-1)


def _mm_kernel(x_ref, w_ref, o_ref):
    o_ref[...] = _dot(x_ref[...].astype(BF16), w_ref[...]).astype(o_ref.dtype)


def _matmul(x, w, out_dtype):
    m, k = x.shape
    n = w.shape[1]
    return pl.pallas_call(
        _mm_kernel,
        grid=(m // ROW_TILE,),
        in_specs=[pl.BlockSpec((ROW_TILE, k), lambda i: (i, 0)),
                  pl.BlockSpec((k, n), lambda i: (0, 0))],
        out_specs=pl.BlockSpec((ROW_TILE, n), lambda i: (i, 0)),
        out_shape=jax.ShapeDtypeStruct((m, n), out_dtype),
        compiler_params=_cparams(1),
        name="dense_matmul",
    )(x, w)


def _s5_prep(lam_re, lam_im, log_dt, b_re, b_im, c_re, c_im, d, n_steps):
    tc = S5_CHUNK
    g, p, c = S5_GROUPS, S5_STATE, S5_GROUP_CH
    lr = jnp.minimum(lam_re.astype(F32), -1e-4)
    li = lam_im.astype(F32)
    dt = jnp.exp(log_dt.astype(F32))[..., None]
    mag = jnp.exp(lr * dt)
    ar = mag * jnp.cos(li * dt)
    ai = mag * jnp.sin(li * dt)
    den = lr * lr + li * li
    nr = ar - 1.0
    coef_r = (nr * lr + ai * li) / den
    coef_i = (ai * lr - nr * li) / den
    br_ = b_re.astype(F32)
    bi_ = b_im.astype(F32)
    bbr = coef_r[..., None] * br_ - coef_i[..., None] * bi_
    bbi = coef_r[..., None] * bi_ + coef_i[..., None] * br_
    cr = c_re.astype(F32)
    ci = c_im.astype(F32)

    def apow(n):
        nn = n.astype(F32)[:, None, None, None]
        m_ = jnp.exp(nn * (lr * dt)[None])
        ang = nn * (li * dt)[None]
        return m_ * jnp.cos(ang), m_ * jnp.sin(ang)

    lags = jnp.arange(tc + 1)
    pr, pi = apow(lags)
    hr = pr[..., None] * bbr[None] - pi[..., None] * bbi[None]
    hi = pr[..., None] * bbi[None] + pi[..., None] * bbr[None]
    kk = (jnp.einsum('dgop,jdgpc->jdgoc', cr, hr) - jnp.einsum('dgop,jdgpc->jdgoc', ci, hi))
    s_idx = jnp.arange(tc)[:, None]
    t_idx = jnp.arange(tc)[None, :]
    lag_f = jnp.clip(t_idx - s_idx, 0, tc)
    lag_b = jnp.clip(s_idx - t_idx, 0, tc)
    kf = kk[:, 0][lag_f]
    kb = kk[:, 1][lag_b]
    kt = (jnp.where((t_idx >= s_idx)[..., None, None, None], kf, 0.0)
          + jnp.where((s_idx >= t_idx)[..., None, None, None], kb, 0.0))
    dd = d.astype(F32).reshape(g, c)
    eye_st = (s_idx == t_idx).astype(F32)
    eye_c = jnp.eye(c, dtype=F32)
    kt = kt + eye_st[:, :, None, None, None] * (dd[None, None, :, None, :] * eye_c[None, None, None])
    kt = jnp.transpose(kt, (2, 0, 4, 1, 3)).reshape(g, tc * c, tc * c)

    wf_r = jnp.transpose(hr[tc - 1 - jnp.arange(tc), 0], (1, 0, 3, 2)).reshape(g, tc * c, p)
    wf_i = jnp.transpose(hi[tc - 1 - jnp.arange(tc), 0], (1, 0, 3, 2)).reshape(g, tc * c, p)
    wb_r = jnp.transpose(hr[jnp.arange(tc), 1], (1, 0, 3, 2)).reshape(g, tc * c, p)
    wb_i = jnp.transpose(hi[jnp.arange(tc), 1], (1, 0, 3, 2)).reshape(g, tc * c, p)

    ef = jnp.arange(tc) + 1
    eb = tc - jnp.arange(tc)
    def readout(e, dirn):
        zr = cr[dirn][None] * pr[e, dirn][:, :, None, :] - ci[dirn][None] * pi[e, dirn][:, :, None, :]
        zi = cr[dirn][None] * pi[e, dirn][:, :, None, :] + ci[dirn][None] * pr[e, dirn][:, :, None, :]
        v_r = jnp.transpose(zr, (1, 3, 0, 2)).reshape(g, p, tc * c)
        v_i = -jnp.transpose(zi, (1, 3, 0, 2)).reshape(g, p, tc * c)
        return v_r, v_i
    vf_r, vf_i = readout(ef, 0)
    vb_r, vb_i = readout(eb, 1)

    npair = g // 2
    def pair_blockdiag(m):
        r_, c_ = m.shape[1], m.shape[2]
        m2 = m.reshape(npair, 2, r_, c_)
        z = jnp.zeros((npair, r_, c_), F32)
        top = jnp.concatenate([m2[:, 0], z], axis=2)
        bot = jnp.concatenate([z, m2[:, 1]], axis=2)
        return jnp.concatenate([top, bot], axis=1)
    r1 = jnp.concatenate([pair_blockdiag(kt), pair_blockdiag(wf_r), pair_blockdiag(wf_i),
                          pair_blockdiag(wb_r), pair_blockdiag(wb_i)], axis=2)
    def pair_rows(m):
        return pair_blockdiag(m)
    r2 = jnp.concatenate([pair_rows(vf_r), pair_rows(vf_i), pair_rows(vb_r), pair_rows(vb_i)], axis=1)

    steps = jnp.arange(n_steps)
    a16r, a16i = apow(jnp.array([tc]))
    apr, api = apow(steps * tc)
    anr, ani = apow(jnp.array([tc * n_steps]))
    def lanes(x):
        n_ = x.shape[0]
        return jnp.transpose(x.reshape(n_, npair, 2 * p), (1, 0, 2))
    coef = jnp.concatenate([lanes(a16r[:, 0]), lanes(a16i[:, 0]), lanes(a16r[:, 1]), lanes(a16i[:, 1]),
                            lanes(anr[:, 0]), lanes(ani[:, 0]), lanes(anr[:, 1]), lanes(ani[:, 1])], axis=1)
    rev = n_steps - 1 - steps
    ptab = jnp.stack([lanes(apr[:, 0]), lanes(api[:, 0]), lanes(apr[rev, 1]), lanes(api[rev, 1])], axis=1)
    return r1.astype(BF16), r2.astype(BF16), coef, ptab


def _gelu_tanh(x):
    return 0.5 * x * (1.0 + jnp.tanh(math.sqrt(2.0 / math.pi) * (x + 0.044715 * (x * x * x))))


def _s5_kernel(u_ref, r1_ref, r2_ref, coef_ref, ptab_ref, y_ref, yi_ref, e_ref, s_ref, *, n_batch, n_steps):
    rows_b = n_steps * S5_SEGS
    r = _dot(u_ref[...], r1_ref[...])
    yi_ref[...] = r[:, :512]
    e_ref[...] = r[:, 512:]
    coef = coef_ref[...]
    a16fr, a16fi, a16br, a16bi = coef[0:1], coef[1:2], coef[2:3], coef[3:4]
    anfr, anfi, anbr, anbi = coef[4:5], coef[5:6], coef[6:7], coef[7:8]
    zero = jnp.zeros((S5_SEGS, 128), F32)
    sub = lax.broadcasted_iota(jnp.int32, (S5_SEGS, 128), 0)

    for b in range(n_batch):
        base = b * rows_b

        def local_step(j, carry):
            sfr, sfi, sbr, sbi = carry
            rf = pl.multiple_of(base + j * S5_SEGS, S5_SEGS)
            rb = pl.multiple_of(base + (n_steps - 1 - j) * S5_SEGS, S5_SEGS)
            s_ref[pl.ds(rf, S5_SEGS), 0:128] = sfr
            s_ref[pl.ds(rf, S5_SEGS), 128:256] = sfi
            s_ref[pl.ds(rb, S5_SEGS), 256:384] = sbr
            s_ref[pl.ds(rb, S5_SEGS), 384:512] = sbi
            efr = e_ref[pl.ds(rf, S5_SEGS), 0:128]
            efi = e_ref[pl.ds(rf, S5_SEGS), 128:256]
            ebr = e_ref[pl.ds(rb, S5_SEGS), 256:384]
            ebi = e_ref[pl.ds(rb, S5_SEGS), 384:512]
            nfr = a16fr * sfr - a16fi * sfi + efr
            nfi = a16fr * sfi + a16fi * sfr + efi
            nbr = a16br * sbr - a16bi * sbi + ebr
            nbi = a16br * sbi + a16bi * sbr + ebi
            return nfr, nfi, nbr, nbi

        efr, efi, ebr, ebi = lax.fori_loop(0, n_steps, local_step, (zero, zero, zero, zero))

        cfr, cfi, cbr, cbi = zero, zero, zero, zero
        for _ in range(S5_SEGS - 1):
            tfr = anfr * cfr - anfi * cfi + efr
            tfi = anfr * cfi + anfi * cfr + efi
            cfr = jnp.where(sub == 0, 0.0, pltpu.roll(tfr, 1, 0))
            cfi = jnp.where(sub == 0, 0.0, pltpu.roll(tfi, 1, 0))
            tbr = anbr * cbr - anbi * cbi + ebr
            tbi = anbr * cbi + anbi * cbr + ebi
            cbr = jnp.where(sub == S5_SEGS - 1, 0.0, pltpu.roll(tbr, S5_SEGS - 1, 0))
            cbi = jnp.where(sub == S5_SEGS - 1, 0.0, pltpu.roll(tbi, S5_SEGS - 1, 0))

        def fix_step(j, _):
            rj = pl.multiple_of(base + j * S5_SEGS, S5_SEGS)
            pfr = ptab_ref[0, pl.ds(j, 1), :]
            pfi = ptab_ref[1, pl.ds(j, 1), :]
            pbr = ptab_ref[2, pl.ds(j, 1), :]
            pbi = ptab_ref[3, pl.ds(j, 1), :]
            s_ref[pl.ds(rj, S5_SEGS), 0:128] += pfr * cfr - pfi * cfi
            s_ref[pl.ds(rj, S5_SEGS), 128:256] += pfr * cfi + pfi * cfr
            s_ref[pl.ds(rj, S5_SEGS), 256:384] += pbr * cbr - pbi * cbi
            s_ref[pl.ds(rj, S5_SEGS), 384:512] += pbr * cbi + pbi * cbr
            return 0

        lax.fori_loop(0, n_steps, fix_step, 0)

    y = yi_ref[...] + _dot(s_ref[...].astype(BF16), r2_ref[...])
    y_ref[...] = _gelu_tanh(y).astype(y_ref.dtype)


def _s5_mixer(u_t, r1, r2, coef, ptab, n_batch, n_steps):
    npair, m, _ = u_t.shape
    kern = functools.partial(_s5_kernel, n_batch=n_batch, n_steps=n_steps)
    return pl.pallas_call(
        kern,
        grid=(npair,),
        in_specs=[pl.BlockSpec((None, m, 512), lambda i: (i, 0, 0)),
                  pl.BlockSpec((None, 512, 1024), lambda i: (i, 0, 0)),
                  pl.BlockSpec((None, 512, 512), lambda i: (i, 0, 0)),
                  pl.BlockSpec((None, 8, 128), lambda i: (i, 0, 0)),
                  pl.BlockSpec((None, 4, n_steps, 128), lambda i: (i, 0, 0, 0))],
        out_specs=pl.BlockSpec((None, m, 512), lambda i: (i, 0, 0)),
        out_shape=jax.ShapeDtypeStruct((npair, m, 512), BF16),
        scratch_shapes=[pltpu.VMEM((m, 512), F32), pltpu.VMEM((m, 512), F32), pltpu.VMEM((m, 512), F32)],
        compiler_params=_cparams(1),
        name="s5_scan",
    )(u_t, r1, r2, coef, ptab)


def _s5_to_chunks(u, n_batch, seq):
    j = seq // (S5_SEGS * S5_CHUNK)
    x = u.reshape(n_batch, S5_SEGS, j, S5_CHUNK, S5_GROUPS // 2, 2, S5_GROUP_CH)
    x = jnp.transpose(x, (4, 0, 2, 1, 5, 3, 6))
    return x.reshape(S5_GROUPS // 2, n_batch * j * S5_SEGS, 2 * S5_CHUNK * S5_GROUP_CH)


def _s5_from_chunks(y, n_batch, seq):
    j = seq // (S5_SEGS * S5_CHUNK)
    x = y.reshape(S5_GROUPS // 2, n_batch, j, S5_SEGS, 2, S5_CHUNK, S5_GROUP_CH)
    x = jnp.transpose(x, (1, 3, 2, 5, 0, 4, 6))
    return x.reshape(n_batch * seq, S5_WIDTH)


def _gla_kernel(qf_ref, kf_ref, vf_ref, lf_ref, qb_ref, kb_ref, vb_ref, lb_ref, wg_ref, bg_ref,
                of_ref, ob_ref, st_ref):
    @pl.when(pl.program_id(1) == 0)
    def _():
        st_ref[...] = jnp.zeros_like(st_ref)

    n_chunks = GLA_ROWS // GLA_CHUNK
    row = lax.broadcasted_iota(jnp.int32, (GLA_CHUNK, GLA_CHUNK), 0)
    col = lax.broadcasted_iota(jnp.int32, (GLA_CHUNK, GLA_CHUNK), 1)
    tri_f = (col <= row).astype(BF16)
    tri_b = (col >= row).astype(BF16)
    mask_f = col <= row
    mask_b = col > row
    wg = wg_ref[...]
    bg = bg_ref[...]

    def direction(q_ref, k_ref, v_ref, l_ref, o_ref, d):
        z = _dot(l_ref[...], wg[:, d * GLA_KEY:(d + 1) * GLA_KEY]) + bg[:, d * GLA_KEY:(d + 1) * GLA_KEY]
        log_a = jax.nn.log_sigmoid(z) * (1.0 / GLA_TAU)
        order = range(n_chunks) if d == 0 else range(n_chunks - 1, -1, -1)
        tri = tri_f if d == 0 else tri_b
        mask = mask_f if d == 0 else mask_b
        for c in order:
            sl = slice(c * GLA_CHUNK, (c + 1) * GLA_CHUNK)
            la = log_a[sl]
            la_hi = la.astype(BF16)
            la_lo = (la - la_hi.astype(F32)).astype(BF16)
            bc = _dot(tri, la_hi) + _dot(tri, la_lo)
            b_last = bc[GLA_CHUNK - 1:GLA_CHUNK] if d == 0 else bc[0:1]
            q = q_ref[sl, :].astype(F32) * (GLA_DK ** -0.5)
            k = k_ref[sl, :].astype(F32)
            v = v_ref[sl, :]
            qd = (q * jnp.exp(bc)).astype(BF16)
            kd = (k * jnp.exp(-bc)).astype(BF16)
            kc = (k * jnp.exp(b_last - bc)).astype(BF16)
            decay = jnp.exp(b_last)
            outs = []
            for h in range(GLA_HEADS):
                ks = slice(h * GLA_DK, (h + 1) * GLA_DK)
                vs = slice(h * GLA_DV, (h + 1) * GLA_DV)
                s = jnp.where(mask, _dot_nt(qd[:, ks], kd[:, ks]), 0.0).astype(BF16)
                st = st_ref[d, h]
                o = _dot(s, v[:, vs]) + _dot_nt(qd[:, ks], st.astype(BF16))
                st_ref[d, h] = st * decay[:, ks] + _dot_tn(v[:, vs], kc[:, ks])
                outs.append(o)
            o_ref[sl, :] = jnp.concatenate(outs, axis=-1).astype(o_ref.dtype)

    direction(qf_ref, kf_ref, vf_ref, lf_ref, of_ref, 0)
    direction(qb_ref, kb_ref, vb_ref, lb_ref, ob_ref, 1)


def _gla_mixer(h0, wg, bg, n_batch, seq):
    nb = seq // GLA_ROWS
    r = GLA_ROWS
    fwd = lambda b, i: b * nb + i
    bwd = lambda b, i: b * nb + (nb - 1 - i)
    def spec(width, colblk, rowfn):
        return pl.BlockSpec((r, width), lambda b, i: (rowfn(b, i), colblk))
    in_specs = [spec(256, 2, fwd), spec(256, 3, fwd), spec(512, 2, fwd), spec(128, 16, fwd),
                spec(256, 2, bwd), spec(256, 3, bwd), spec(512, 2, bwd), spec(128, 16, bwd),
                pl.BlockSpec((128, 512), lambda b, i: (0, 0)),
                pl.BlockSpec((1, 512), lambda b, i: (0, 0))]
    out_specs = [pl.BlockSpec((r, 512), lambda b, i: (fwd(b, i), 0)),
                 pl.BlockSpec((r, 512), lambda b, i: (bwd(b, i), 0))]
    t = n_batch * seq
    return pl.pallas_call(
        _gla_kernel,
        grid=(n_batch, nb),
        in_specs=in_specs,
        out_specs=out_specs,
        out_shape=[jax.ShapeDtypeStruct((t, 512), BF16), jax.ShapeDtypeStruct((t, 512), BF16)],
        scratch_shapes=[pltpu.VMEM((2, GLA_HEADS, GLA_DV, GLA_DK), F32)],
        compiler_params=_cparams(2),
        name="gla_chunked",
    )(h0, h0, h0, h0, h0, h0, h0, h0, wg, bg)


def _mix0_out_kernel(ya_ref, of_ref, ob_ref, go_ref, x_ref, gw_ref, gb_ref, ng_ref, wo_ref, lg_ref, lb_ref,
                     h_ref, h8_ref):
    ya = ya_ref[...]
    yaf = ya.astype(F32)
    gate = _dot(ya, gw_ref[...]) + gb_ref[...]
    ya2 = yaf * jax.nn.sigmoid(gate)
    o = of_ref[...].astype(F32) + ob_ref[...].astype(F32)
    ng = ng_ref[...]
    parts = []
    for h in range(GLA_HEADS):
        oh = o[:, h * GLA_DV:(h + 1) * GLA_DV]
        ms = jnp.mean(oh * oh, axis=-1, keepdims=True)
        parts.append(oh * lax.rsqrt(ms + RMS_EPS) * ng[:, h * GLA_DV:(h + 1) * GLA_DV])
    yb = jnp.concatenate(parts, axis=-1) * jax.nn.silu(go_ref[...].astype(F32))
    wo = wo_ref[...]
    y = _dot(ya2.astype(BF16), wo[:S5_WIDTH]) + _dot(yb.astype(BF16), wo[S5_WIDTH:])
    hn = _layer_norm(ALPHA * x_ref[...] + y, lg_ref[...], lb_ref[...])
    h_ref[...] = hn
    _store_token_tiles(h8_ref, hn)


def _mix0_out(ya, o_f, o_b, h0, x, glu_w, glu_b, norm_g, w_out, ln_g, ln_b):
    t = x.shape[0]
    tm = ROW_TILE
    row = lambda w: pl.BlockSpec((tm, w), lambda i: (i, 0))
    full = lambda a, b: pl.BlockSpec((a, b), lambda i: (0, 0))
    return pl.pallas_call(
        _mix0_out_kernel,
        grid=(t // tm,),
        in_specs=[row(512), row(512), row(512),
                  pl.BlockSpec((tm, 512), lambda i: (i, 3)),
                  row(1024), full(512, 512), full(1, 512), full(1, 512), full(1024, 1024),
                  full(1, 1024), full(1, 1024)],
        out_specs=[row(1024), pl.BlockSpec((tm * 8, 128), lambda i: (i, 0))],
        out_shape=[jax.ShapeDtypeStruct((t, D_MODEL), F32), jax.ShapeDtypeStruct((t * 8, 128), F32)],
        compiler_params=_cparams(1),
        name="mix0_out_ln",
    )(ya, o_f, o_b, h0, x, glu_w, glu_b, norm_g, w_out, ln_g, ln_b)


def _attn_kernel(q_ref, kc_ref, vc_ref, kp_ref, vp_ref, kn_ref, vn_ref, bias_ref, sink_ref, o_ref, *, seq):
    blk = pl.program_id(1)
    n_sub = ATT_ROWS // ATT_BLOCK
    kw = ATT_BLOCK + 2 * WINDOW
    qi = lax.broadcasted_iota(jnp.int32, (GQA * ATT_BLOCK, kw), 0) & (ATT_BLOCK - 1)
    kj = lax.broadcasted_iota(jnp.int32, (GQA * ATT_BLOCK, kw), 1)
    band = jnp.abs(kj - WINDOW - qi) <= WINDOW
    kall = jnp.concatenate([kp_ref[...], kc_ref[...], kn_ref[...]], axis=0)
    vall = jnp.concatenate([vp_ref[...], vc_ref[...], vn_ref[...]], axis=0)
    for s in range(n_sub):
        start = blk * ATT_ROWS + s * ATT_BLOCK
        kpos = start - WINDOW + kj
        valid4 = band & (kpos >= 0) & (kpos < seq)
        kwin = kall[s * ATT_BLOCK:s * ATT_BLOCK + kw]
        vwin = vall[s * ATT_BLOCK:s * ATT_BLOCK + kw]
        q = q_ref[s * ATT_BLOCK:(s + 1) * ATT_BLOCK, :]
        outs = []
        for kv in range(N_KV):
            qs = jnp.concatenate([q[:, (kv * GQA + g) * HEAD_DIM:(kv * GQA + g + 1) * HEAD_DIM]
                                  for g in range(GQA)], axis=0)
            kh = kwin[:, kv * HEAD_DIM:(kv + 1) * HEAD_DIM]
            vh = vwin[:, kv * HEAD_DIM:(kv + 1) * HEAD_DIM]
            sc = _dot_nt(qs, kh) * ATT_SCALE + bias_ref[kv]
            sc = jnp.where(valid4, sc, NEG_INF)
            sink = sink_ref[kv]
            m = jnp.maximum(jnp.max(sc, axis=-1, keepdims=True), sink)
            p = jnp.exp(sc - m)
            den = jnp.sum(p, axis=-1, keepdims=True) + jnp.exp(sink - m)
            o = _dot(p.astype(BF16), vh) / den
            for g in range(GQA):
                outs.append(o[g * ATT_BLOCK:(g + 1) * ATT_BLOCK])
        o_ref[s * ATT_BLOCK:(s + 1) * ATT_BLOCK, :] = jnp.concatenate(outs, axis=-1).astype(o_ref.dtype)


def _t5_bucket(rel):
    nb = REL_BUCKETS // 2
    max_exact = nb // 2
    ret = (rel > 0).astype(jnp.int32) * nb
    n = jnp.abs(rel)
    large = max_exact + (jnp.log(jnp.maximum(n, 1).astype(F32) / max_exact)
                         / math.log(REL_MAX_DIST / max_exact) * (nb - max_exact)).astype(jnp.int32)
    large = jnp.minimum(large, nb - 1)
    return ret + jnp.where(n < max_exact, n, large)


def _attn_mixer(h3, rel_bias, sink, n_batch, seq):
    kw = ATT_BLOCK + 2 * WINDOW
    rel = jnp.arange(kw)[None, :] - WINDOW - jnp.arange(ATT_BLOCK)[:, None]
    bias = rel_bias.astype(F32)[_t5_bucket(rel)]
    bias = jnp.transpose(bias, (2, 0, 1)).reshape(N_KV, GQA * ATT_BLOCK, kw)
    sink_rows = jnp.repeat(sink.astype(F32).reshape(N_KV, GQA), ATT_BLOCK, axis=1)[..., None]
    nblk = seq // ATT_ROWS
    sub = ATT_ROWS // ATT_BLOCK
    n128 = seq // ATT_BLOCK
    cur = lambda b, i: b * nblk + i
    prev = lambda b, i: b * n128 + jnp.maximum(i * sub - 1, 0)
    nxt = lambda b, i: b * n128 + jnp.minimum((i + 1) * sub, n128 - 1)
    t = n_batch * seq
    kern = functools.partial(_attn_kernel, seq=seq)
    return pl.pallas_call(
        kern,
        grid=(n_batch, nblk),
        in_specs=[pl.BlockSpec((ATT_ROWS, ATT_DIM), lambda b, i: (cur(b, i), 0)),
                  pl.BlockSpec((ATT_ROWS, KV_DIM), lambda b, i: (cur(b, i), 4)),
                  pl.BlockSpec((ATT_ROWS, KV_DIM), lambda b, i: (cur(b, i), 5)),
                  pl.BlockSpec((ATT_BLOCK, KV_DIM), lambda b, i: (prev(b, i), 4)),
                  pl.BlockSpec((ATT_BLOCK, KV_DIM), lambda b, i: (prev(b, i), 5)),
                  pl.BlockSpec((ATT_BLOCK, KV_DIM), lambda b, i: (nxt(b, i), 4)),
                  pl.BlockSpec((ATT_BLOCK, KV_DIM), lambda b, i: (nxt(b, i), 5)),
                  pl.BlockSpec((N_KV, GQA * ATT_BLOCK, kw), lambda b, i: (0, 0, 0)),
                  pl.BlockSpec((N_KV, GQA * ATT_BLOCK, 1), lambda b, i: (0, 0, 0))],
        out_specs=pl.BlockSpec((ATT_ROWS, ATT_DIM), lambda b, i: (cur(b, i), 0)),
        out_shape=jax.ShapeDtypeStruct((t, ATT_DIM), BF16),
        compiler_params=_cparams(2),
        name="window_gqa",
    )(h3, h3, h3, h3, h3, h3, h3, bias, sink_rows)


def _proj_ln_kernel(a_ref, w_ref, x_ref, g_ref, b_ref, h_ref, h8_ref):
    y = _dot(a_ref[...], w_ref[...])
    hn = _layer_norm(ALPHA * x_ref[...] + y, g_ref[...], b_ref[...])
    h_ref[...] = hn
    _store_token_tiles(h8_ref, hn)


def _proj_ln(a, w, x, g, b):
    t, k = a.shape
    tm = ROW_TILE
    return pl.pallas_call(
        _proj_ln_kernel,
        grid=(t // tm,),
        in_specs=[pl.BlockSpec((tm, k), lambda i: (i, 0)),
                  pl.BlockSpec((k, D_MODEL), lambda i: (0, 0)),
                  pl.BlockSpec((tm, D_MODEL), lambda i: (i, 0)),
                  pl.BlockSpec((1, D_MODEL), lambda i: (0, 0)),
                  pl.BlockSpec((1, D_MODEL), lambda i: (0, 0))],
        out_specs=[pl.BlockSpec((tm, D_MODEL), lambda i: (i, 0)), pl.BlockSpec((tm * 8, 128), lambda i: (i, 0))],
        out_shape=[jax.ShapeDtypeStruct((t, D_MODEL), F32), jax.ShapeDtypeStruct((t * 8, 128), F32)],
        compiler_params=_cparams(1),
        name="proj_ln",
    )(a, w, x, g, b)


def _router_kernel(h_ref, wh_ref, wl_ref, b_ref, o_ref):
    x = h_ref[...]
    xh = x.astype(BF16)
    xl = (x - xh.astype(F32)).astype(BF16)
    wh = wh_ref[...]
    logits = _dot(xh, wh) + (_dot(xl, wh) + _dot(xh, wl_ref[...])) + b_ref[...]
    lane = lax.broadcasted_iota(jnp.int32, logits.shape, 1)
    big = jnp.int32(1 << 20)
    neg = jnp.float32(-jnp.inf)
    is_g = lane < N_GROUPS
    gl = jnp.where(is_g, logits, neg)
    gmax = jnp.max(gl, axis=-1, keepdims=True)
    gsum = jnp.sum(jnp.where(is_g, jnp.exp(gl - gmax), 0.0), axis=-1, keepdims=True)
    grp = jnp.min(jnp.where(is_g & (gl == gmax), lane, big), axis=-1, keepdims=True)
    p_grp = 1.0 / gsum
    lo = N_GROUPS + grp * EPG
    in_grp = (lane >= lo) & (lane < lo + EPG)
    el = jnp.where(in_grp, logits, neg)
    m1 = jnp.max(el, axis=-1, keepdims=True)
    i1 = jnp.min(jnp.where(in_grp & (el == m1), lane, big), axis=-1, keepdims=True)
    el2 = jnp.where(lane == i1, neg, el)
    m2 = jnp.max(el2, axis=-1, keepdims=True)
    i2 = jnp.min(jnp.where(in_grp & (lane != i1) & (el2 == m2), lane, big), axis=-1, keepdims=True)
    e2 = jnp.exp(m2 - m1)
    g1 = p_grp / (1.0 + e2)
    g2 = p_grp * e2 / (1.0 + e2)
    out = jnp.where(lane == 0, (i1 - N_GROUPS).astype(F32),
                    jnp.where(lane == 1, (i2 - N_GROUPS).astype(F32),
                              jnp.where(lane == 2, g1, jnp.where(lane == 3, g2, 0.0))))
    o_ref[...] = out


def _router(h, w_group, b_group, w_router, b_router):
    t = h.shape[0]
    w = jnp.concatenate([w_group.astype(F32), w_router.astype(F32)], axis=1)
    w = jnp.pad(w, ((0, 0), (0, 128 - w.shape[1])))
    wh = w.astype(BF16)
    wl = (w - wh.astype(F32)).astype(BF16)
    bias = jnp.pad(jnp.concatenate([b_group.astype(F32), b_router.astype(F32)]), (0, 128 - 36))[None]
    tm = ROW_TILE
    out = pl.pallas_call(
        _router_kernel,
        grid=(t // tm,),
        in_specs=[pl.BlockSpec((tm, D_MODEL), lambda i: (i, 0)),
                  pl.BlockSpec((D_MODEL, 128), lambda i: (0, 0)),
                  pl.BlockSpec((D_MODEL, 128), lambda i: (0, 0)),
                  pl.BlockSpec((1, 128), lambda i: (0, 0))],
        out_specs=pl.BlockSpec((tm, 128), lambda i: (i, 0)),
        out_shape=jax.ShapeDtypeStruct((t, 128), F32),
        compiler_params=_cparams(1),
        name="moe_router",
    )(h, wh, wl, bias)
    eid = out[:, 0:2].astype(jnp.int32)
    gate = out[:, 2:4]
    return eid, gate


def _token_tile(ref, tok):
    return ref.at[pl.ds(pl.multiple_of(tok * 8, 8), 8)]


def _dispatch_kernel(dst_hbm, meta_ref, h_hbm, zero_hbm, xs_hbm, idx_smem, sem, isem):
    i = pl.program_id(0)
    tm = ROW_TILE

    @pl.when(i == 0)
    def _():
        def zstart(e, _):
            end = meta_ref[e]
            @pl.when(meta_ref[N_EXPERTS + e] > 0)
            def _():
                off = pl.multiple_of((end - MOE_ROWS) * 8, 8)
                pltpu.make_async_copy(zero_hbm, xs_hbm.at[pl.ds(off, MOE_ROWS * 8)], sem).start()
            return 0
        lax.fori_loop(0, N_EXPERTS, zstart, 0)
        def zwait(e, _):
            @pl.when(meta_ref[N_EXPERTS + e] > 0)
            def _():
                pltpu.make_async_copy(zero_hbm, xs_hbm.at[pl.ds(0, MOE_ROWS * 8)], sem).wait()
            return 0
        lax.fori_loop(0, N_EXPERTS, zwait, 0)
        n_used = meta_ref[2 * N_EXPERTS]
        n_blk = xs_hbm.shape[0] // (MOE_ROWS * 8)
        def tstart(blk, _):
            off = pl.multiple_of(blk * (MOE_ROWS * 8), 8)
            pltpu.make_async_copy(zero_hbm, xs_hbm.at[pl.ds(off, MOE_ROWS * 8)], sem).start()
            return 0
        lax.fori_loop(n_used, n_blk, tstart, 0)
        def twait(blk, _):
            pltpu.make_async_copy(zero_hbm, xs_hbm.at[pl.ds(0, MOE_ROWS * 8)], sem).wait()
            return 0
        lax.fori_loop(n_used, n_blk, twait, 0)

    cp = pltpu.make_async_copy(dst_hbm.at[pl.ds(i * 2 * tm, 2 * tm)], idx_smem, isem)
    cp.start()
    cp.wait()

    def issue(t, _):
        src = _token_tile(h_hbm, i * tm + t)
        pltpu.make_async_copy(src, _token_tile(xs_hbm, idx_smem[2 * t]), sem).start()
        pltpu.make_async_copy(src, _token_tile(xs_hbm, idx_smem[2 * t + 1]), sem).start()
        return 0
    lax.fori_loop(0, tm, issue, 0)
    for _ in range(2):
        pltpu.make_async_copy(h_hbm.at[pl.ds(0, tm * 8)], xs_hbm.at[pl.ds(0, tm * 8)], sem).wait()


def _dispatch(h8, dst, meta, cap):
    t = h8.shape[0] // 8
    zero = jnp.zeros((MOE_ROWS * 8, 128), F32)
    return pl.pallas_call(
        _dispatch_kernel,
        grid_spec=pltpu.PrefetchScalarGridSpec(
            num_scalar_prefetch=0,
            grid=(t // ROW_TILE,),
            in_specs=[pl.BlockSpec(memory_space=pl.ANY),
                      pl.BlockSpec(memory_space=pltpu.SMEM),
                      pl.BlockSpec(memory_space=pl.ANY),
                      pl.BlockSpec(memory_space=pl.ANY)],
            out_specs=pl.BlockSpec(memory_space=pl.ANY),
            scratch_shapes=[pltpu.SMEM((2 * ROW_TILE,), jnp.int32),
                            pltpu.SemaphoreType.DMA(()),
                            pltpu.SemaphoreType.DMA(())]),
        out_shape=jax.ShapeDtypeStruct((cap * 8, 128), F32),
        compiler_params=_cparams(1),
        name="moe_dispatch",
    )(dst, meta, h8, zero)


def _experts_kernel(be_ref, nu_ref, xs_ref, w1_ref, w3_ref, w2_ref, ys_ref, w13_s, w2_s):
    i = pl.program_id(0)
    used = i < nu_ref[0]
    prev = be_ref[jnp.maximum(i - 1, 0)]
    fresh = (i == 0) | (be_ref[i] != prev)

    @pl.when(used & fresh)
    def _():
        w13_s[:, :D_EXPERT] = w1_ref[...].astype(BF16)
        w13_s[:, D_EXPERT:] = w3_ref[...].astype(BF16)
        w2_s[...] = w2_ref[...].astype(BF16)

    @pl.when(used)
    def _():
        x = _load_token_tiles(xs_ref, MOE_ROWS).astype(BF16)
        h = _dot(x, w13_s[...])
        h1 = h[:, :D_EXPERT]
        hdn = (h1 * jax.nn.sigmoid(h1)) * h[:, D_EXPERT:]
        _store_token_tiles(ys_ref, _dot(hdn.astype(BF16), w2_s[...]))

    @pl.when(jnp.logical_not(used))
    def _():
        ys_ref[...] = jnp.zeros_like(ys_ref)


def _experts(xs, blk_e, n_used, w1, w3, w2):
    cap = xs.shape[0] // 8
    nblk = cap // MOE_ROWS
    last = lambda i, nu: jnp.minimum(i, nu[0] - 1)
    return pl.pallas_call(
        _experts_kernel,
        grid_spec=pltpu.PrefetchScalarGridSpec(
            num_scalar_prefetch=2,
            grid=(nblk,),
            in_specs=[pl.BlockSpec((MOE_ROWS * 8, 128), lambda i, be, nu: (last(i, nu), 0)),
                      pl.BlockSpec((None, D_MODEL, D_EXPERT), lambda i, be, nu: (be[i], 0, 0)),
                      pl.BlockSpec((None, D_MODEL, D_EXPERT), lambda i, be, nu: (be[i], 0, 0)),
                      pl.BlockSpec((None, D_EXPERT, D_MODEL), lambda i, be, nu: (be[i], 0, 0))],
            out_specs=pl.BlockSpec((MOE_ROWS * 8, 128), lambda i, be, nu: (i, 0)),
            scratch_shapes=[pltpu.VMEM((D_MODEL, 2 * D_EXPERT), BF16),
                            pltpu.VMEM((D_EXPERT, D_MODEL), BF16)]),
        out_shape=jax.ShapeDtypeStruct((cap * 8, 128), F32),
        compiler_params=_cparams(1),
        name="moe_experts",
    )(blk_e, n_used, xs, w1, w3, w2)


def _combine_kernel(dst_hbm, ys_hbm, h_ref, gate_ref, g_ref, b_ref, o_ref, idx_smem, y0_ref, y1_ref, sem, isem):
    i = pl.program_id(0)
    tm = ROW_TILE
    cp = pltpu.make_async_copy(dst_hbm.at[pl.ds(i * 2 * tm, 2 * tm)], idx_smem, isem)
    cp.start()
    cp.wait()

    def issue(t, _):
        pltpu.make_async_copy(_token_tile(ys_hbm, idx_smem[2 * t]), _token_tile(y0_ref, t), sem).start()
        pltpu.make_async_copy(_token_tile(ys_hbm, idx_smem[2 * t + 1]), _token_tile(y1_ref, t), sem).start()
        return 0
    lax.fori_loop(0, tm, issue, 0)
    pltpu.make_async_copy(ys_hbm.at[pl.ds(0, tm * 8)], y0_ref, sem).wait()
    pltpu.make_async_copy(ys_hbm.at[pl.ds(0, tm * 8)], y1_ref, sem).wait()
    gate = gate_ref[...]
    y = _load_token_tiles(y0_ref, tm) * gate[:, 0:1] + _load_token_tiles(y1_ref, tm) * gate[:, 1:2]
    o_ref[...] = _layer_norm(ALPHA * h_ref[...] + y, g_ref[...], b_ref[...])


def _combine_ln(dst, ys, h, gate, ln_g, ln_b):
    t = h.shape[0]
    tm = ROW_TILE
    return pl.pallas_call(
        _combine_kernel,
        grid_spec=pltpu.PrefetchScalarGridSpec(
            num_scalar_prefetch=0,
            grid=(t // tm,),
            in_specs=[pl.BlockSpec(memory_space=pl.ANY),
                      pl.BlockSpec(memory_space=pl.ANY),
                      pl.BlockSpec((tm, D_MODEL), lambda i: (i, 0)),
                      pl.BlockSpec((tm, 2), lambda i: (i, 0)),
                      pl.BlockSpec((1, D_MODEL), lambda i: (0, 0)),
                      pl.BlockSpec((1, D_MODEL), lambda i: (0, 0))],
            out_specs=pl.BlockSpec((tm, D_MODEL), lambda i: (i, 0)),
            scratch_shapes=[pltpu.SMEM((2 * tm,), jnp.int32),
                            pltpu.VMEM((tm * 8, 128), F32),
                            pltpu.VMEM((tm * 8, 128), F32),
                            pltpu.SemaphoreType.DMA(()),
                            pltpu.SemaphoreType.DMA(())]),
        out_shape=jax.ShapeDtypeStruct((t, D_MODEL), F32),
        compiler_params=_cparams(1),
        name="moe_combine_ln",
    )(dst, ys, h, gate, ln_g, ln_b)


def _moe_layer(h, h8, w_group, b_group, w_router, b_router, w1, w3, w2, ln_g, ln_b):
    t = h.shape[0]
    eid, gate = _router(h, w_group, b_group, w_router, b_router)
    fe = eid.reshape(-1)
    onehot = (fe[:, None] == jnp.arange(N_EXPERTS)[None, :]).astype(jnp.int32)
    csum = jnp.cumsum(onehot, axis=0)
    rank = jnp.take_along_axis(csum, fe[:, None], axis=1)[:, 0] - 1
    counts = csum[-1]
    padded = (counts + MOE_ROWS - 1) // MOE_ROWS * MOE_ROWS
    pad_end = jnp.cumsum(padded)
    pad_start = pad_end - padded
    dst = (pad_start[fe] + rank).astype(jnp.int32)
    cap = 2 * t + N_EXPERTS * MOE_ROWS
    nblk = cap // MOE_ROWS
    blk_e = jnp.minimum(jnp.searchsorted(pad_end, jnp.arange(nblk) * MOE_ROWS, side='right'),
                        N_EXPERTS - 1).astype(jnp.int32)
    n_used = (pad_end[-1] // MOE_ROWS).astype(jnp.int32).reshape(1)
    meta = jnp.concatenate([pad_end, padded, n_used]).astype(jnp.int32)
    xs = _dispatch(h8, dst, meta, cap)
    ys = _experts(xs, blk_e, n_used, w1, w3, w2)
    return _combine_ln(dst, ys, h, gate, ln_g, ln_b)


def kernel(x, ln_mix_g, ln_mix_b, ln_ffn_g, ln_ffn_b, ab_w_in, s5_lam_re, s5_lam_im, s5_log_dt, s5_b_re, s5_b_im,
           s5_c_re, s5_c_im, s5_d, s5_glu_w, s5_glu_b, gla_gate_w, gla_gate_b, gla_norm_g, ab_w_out, c_w_in,
           c_sink, c_w_out, rel_bias, moe_w_group, moe_b_group, moe_w_router, moe_b_router, moe_w1, moe_w3, moe_w2):
    n_batch, seq, _ = x.shape
    t = n_batch * seq
    xt = x.reshape(t, D_MODEL)
    row = lambda v: v.astype(F32).reshape(1, -1)

    w_in0 = jnp.pad(ab_w_in[0], ((0, 0), (0, AB_IN_PAD - AB_IN))).astype(BF16)
    h0 = _matmul(xt, w_in0, BF16)
    n_steps = seq // (S5_SEGS * S5_CHUNK)
    r1, r2, coef, ptab = _s5_prep(s5_lam_re[0], s5_lam_im[0], s5_log_dt[0], s5_b_re[0], s5_b_im[0],
                                  s5_c_re[0], s5_c_im[0], s5_d[0], n_steps)
    u_t = _s5_to_chunks(h0[:, :S5_WIDTH], n_batch, seq)
    ya = _s5_from_chunks(_s5_mixer(u_t, r1, r2, coef, ptab, n_batch, n_steps), n_batch, seq)
    gw = gla_gate_w[0].astype(F32)
    wg = jnp.zeros((128, 2 * GLA_KEY), F32)
    wg = wg.at[0:GLA_RANK, 0:GLA_KEY].set(gw[0]).at[GLA_RANK:2 * GLA_RANK, GLA_KEY:].set(gw[1]).astype(BF16)
    bg = gla_gate_b[0].astype(F32).reshape(1, 2 * GLA_KEY)
    o_f, o_b = _gla_mixer(h0, wg, bg, n_batch, seq)
    h, h8 = _mix0_out(ya, o_f, o_b, h0, xt, s5_glu_w[0].astype(BF16), row(s5_glu_b[0]), row(gla_norm_g[0]),
                      ab_w_out[0].astype(BF16), row(ln_mix_g[0]), row(ln_mix_b[0]))
    h = _moe_layer(h, h8, moe_w_group[0], moe_b_group[0], moe_w_router[0], moe_b_router[0],
                   moe_w1[0], moe_w3[0], moe_w2[0], row(ln_ffn_g[0]), row(ln_ffn_b[0]))

    h3 = _matmul(h, c_w_in[0].astype(BF16), BF16)
    o = _attn_mixer(h3, rel_bias, c_sink[0], n_batch, seq)
    h, h8 = _proj_ln(o, c_w_out[0].astype(BF16), h, row(ln_mix_g[1]), row(ln_mix_b[1]))
    h = _moe_layer(h, h8, moe_w_group[1], moe_b_group[1], moe_w_router[1], moe_b_router[1],
                   moe_w1[1], moe_w3[1], moe_w2[1], row(ln_ffn_g[1]), row(ln_ffn_b[1]))
    return h.reshape(n_batch, seq, D_MODEL)
```

```python
import functools
import math

import jax
import jax.numpy as jnp
from jax import lax
from jax.experimental import pallas as pl
from jax.experimental.pallas import tpu as pltpu

F32 = jnp.float32
BF16 = jnp.bfloat16

D_MODEL = 1024
S5_WIDTH = 512
S5_GROUP_CH = 16
S5_GROUPS = 32
S5_STATE = 64
GLA_HEADS = 4
GLA_DV = 128
GLA_DK = 64
GLA_KEY = 256
GLA_WIDTH = 512
GLA_RANK = 16
GLA_TAU = 16.0
GLA_CHUNK = 64
AB_IN = 2080
AB_IN_PAD = 2176
HEAD_DIM = 64
N_HEADS = 16
N_KV = 4
GQA = 4
ATT_DIM = 1024
KV_DIM = 256
WINDOW = 128
ATT_BLOCK = 128
ATT_SCALE = HEAD_DIM ** -0.5
REL_BUCKETS = 32
REL_MAX_DIST = 128
NEG_INF = -1e30
N_GROUPS = 4
EPG = 8
N_EXPERTS = 32
D_EXPERT = 512
LN_EPS = 1e-5
RMS_EPS = 1e-6
DEPTH = 2
ALPHA = (2 * DEPTH) ** 0.25

S5_CHUNK = 16
S5_SEGS = 8
ROW_TILE = 512
GLA_ROWS = 256
ATT_ROWS = 512
MOE_ROWS = 256
VMEM_LIMIT = 56 * 1024 * 1024


def _cparams(n_axes):
    return pltpu.CompilerParams(dimension_semantics=("arbitrary",) * n_axes,
                                vmem_limit_bytes=VMEM_LIMIT)


def _dot(a, b):
    return jnp.dot(a, b, preferred_element_type=F32)


def _dot_nt(a, b):
    return lax.dot_general(a, b, (((1,), (1,)), ((), ())), preferred_element_type=F32)


def _dot_tn(a, b):
    return lax.dot_general(a, b, (((0,), (0,)), ((), ())), preferred_element_type=F32)


def _layer_norm(r, g, b):
    mu = jnp.mean(r, axis=-1, keepdims=True)
    c = r - mu
    var = jnp.mean(c * c, axis=-1, keepdims=True)
    return c * lax.rsqrt(var + LN_EPS) * g + b


def _store_token_tiles(ref, val):
    n = val.shape[0]
    for k in range(D_MODEL // 128):
        ref[pl.ds(k, n, stride=8), :] = val[:, k * 128:(k + 1) * 128]


def _load_token_tiles(ref, n):
    return jnp.concatenate([ref[pl.ds(k, n, stride=8), :] for k in range(D_MODEL // 128)], axis=-1)


def _mm_kernel(x_ref, w_ref, o_ref):
    o_ref[...] = _dot(x_ref[...].astype(BF16), w_ref[...]).astype(o_ref.dtype)


def _matmul(x, w, out_dtype):
    m, k = x.shape
    n = w.shape[1]
    return pl.pallas_call(
        _mm_kernel,
        grid=(m // ROW_TILE,),
        in_specs=[pl.BlockSpec((ROW_TILE, k), lambda i: (i, 0)),
                  pl.BlockSpec((k, n), lambda i: (0, 0))],
        out_specs=pl.BlockSpec((ROW_TILE, n), lambda i: (i, 0)),
        out_shape=jax.ShapeDtypeStruct((m, n), out_dtype),
        compiler_params=_cparams(1),
        name="dense_matmul",
    )(x, w)


def _s5_prep(lam_re, lam_im, log_dt, b_re, b_im, c_re, c_im, d, n_steps):
    tc = S5_CHUNK
    g, p, c = S5_GROUPS, S5_STATE, S5_GROUP_CH
    lr = jnp.minimum(lam_re.astype(F32), -1e-4)
    li = lam_im.astype(F32)
    dt = jnp.exp(log_dt.astype(F32))[..., None]
    mag = jnp.exp(lr * dt)
    ar = mag * jnp.cos(li * dt)
    ai = mag * jnp.sin(li * dt)
    den = lr * lr + li * li
    nr = ar - 1.0
    coef_r = (nr * lr + ai * li) / den
    coef_i = (ai * lr - nr * li) / den
    br_ = b_re.astype(F32)
    bi_ = b_im.astype(F32)
    bbr = coef_r[..., None] * br_ - coef_i[..., None] * bi_
    bbi = coef_r[..., None] * bi_ + coef_i[..., None] * br_
    cr = c_re.astype(F32)
    ci = c_im.astype(F32)

    def apow(n):
        nn = n.astype(F32)[:, None, None, None]
        m_ = jnp.exp(nn * (lr * dt)[None])
        ang = nn * (li * dt)[None]
        return m_ * jnp.cos(ang), m_ * jnp.sin(ang)

    lags = jnp.arange(tc + 1)
    pr, pi = apow(lags)
    hr = pr[..., None] * bbr[None] - pi[..., None] * bbi[None]
    hi = pr[..., None] * bbi[None] + pi[..., None] * bbr[None]
    kk = (jnp.einsum('dgop,jdgpc->jdgoc', cr, hr) - jnp.einsum('dgop,jdgpc->jdgoc', ci, hi))
    s_idx = jnp.arange(tc)[:, None]
    t_idx = jnp.arange(tc)[None, :]
    lag_f = jnp.clip(t_idx - s_idx, 0, tc)
    lag_b = jnp.clip(s_idx - t_idx, 0, tc)
    kf = kk[:, 0][lag_f]
    kb = kk[:, 1][lag_b]
    kt = (jnp.where((t_idx >= s_idx)[..., None, None, None], kf, 0.0)
          + jnp.where((s_idx >= t_idx)[..., None, None, None], kb, 0.0))
    dd = d.astype(F32).reshape(g, c)
    eye_st = (s_idx == t_idx).astype(F32)
    eye_c = jnp.eye(c, dtype=F32)
    kt = kt + eye_st[:, :, None, None, None] * (dd[None, None, :, None, :] * eye_c[None, None, None])
    kt = jnp.transpose(kt, (2, 0, 4, 1, 3)).reshape(g, tc * c, tc * c)

    wf_r = jnp.transpose(hr[tc - 1 - jnp.arange(tc), 0], (1, 0, 3, 2)).reshape(g, tc * c, p)
    wf_i = jnp.transpose(hi[tc - 1 - jnp.arange(tc), 0], (1, 0, 3, 2)).reshape(g, tc * c, p)
    wb_r = jnp.transpose(hr[jnp.arange(tc), 1], (1, 0, 3, 2)).reshape(g, tc * c, p)
    wb_i = jnp.transpose(hi[jnp.arange(tc), 1], (1, 0, 3, 2)).reshape(g, tc * c, p)

    ef = jnp.arange(tc) + 1
    eb = tc - jnp.arange(tc)
    def readout(e, dirn):
        zr = cr[dirn][None] * pr[e, dirn][:, :, None, :] - ci[dirn][None] * pi[e, dirn][:, :, None, :]
        zi = cr[dirn][None] * pi[e, dirn][:, :, None, :] + ci[dirn][None] * pr[e, dirn][:, :, None, :]
        v_r = jnp.transpose(zr, (1, 3, 0, 2)).reshape(g, p, tc * c)
        v_i = -jnp.transpose(zi, (1, 3, 0, 2)).reshape(g, p, tc * c)
        return v_r, v_i
    vf_r, vf_i = readout(ef, 0)
    vb_r, vb_i = readout(eb, 1)

    npair = g // 2
    def pair_blockdiag(m):
        r_, c_ = m.shape[1], m.shape[2]
        m2 = m.reshape(npair, 2, r_, c_)
        z = jnp.zeros((npair, r_, c_), F32)
        top = jnp.concatenate([m2[:, 0], z], axis=2)
        bot = jnp.concatenate([z, m2[:, 1]], axis=2)
        return jnp.concatenate([top, bot], axis=1)
    r1 = jnp.concatenate([pair_blockdiag(kt), pair_blockdiag(wf_r), pair_blockdiag(wf_i),
                          pair_blockdiag(wb_r), pair_blockdiag(wb_i)], axis=2)
    def pair_rows(m):
        return pair_blockdiag(m)
    r2 = jnp.concatenate([pair_rows(vf_r), pair_rows(vf_i), pair_rows(vb_r), pair_rows(vb_i)], axis=1)

    steps = jnp.arange(n_steps)
    a16r, a16i = apow(jnp.array([tc]))
    apr, api = apow(steps * tc)
    anr, ani = apow(jnp.array([tc * n_steps]))
    def lanes(x):
        n_ = x.shape[0]
        return jnp.transpose(x.reshape(n_, npair, 2 * p), (1, 0, 2))
    coef = jnp.concatenate([lanes(a16r[:, 0]), lanes(a16i[:, 0]), lanes(a16r[:, 1]), lanes(a16i[:, 1]),
                            lanes(anr[:, 0]), lanes(ani[:, 0]), lanes(anr[:, 1]), lanes(ani[:, 1])], axis=1)
    rev = n_steps - 1 - steps
    ptab = jnp.stack([lanes(apr[:, 0]), lanes(api[:, 0]), lanes(apr[rev, 1]), lanes(api[rev, 1])], axis=1)
    return r1.astype(BF16), r2.astype(BF16), coef, ptab


def _gelu_tanh(x):
    return 0.5 * x * (1.0 + jnp.tanh(math.sqrt(2.0 / math.pi) * (x + 0.044715 * (x * x * x))))


def _s5_kernel(u_ref, r1_ref, r2_ref, coef_ref, ptab_ref, y_ref, yi_ref, e_ref, s_ref, *, n_batch, n_steps):
    rows_b = n_steps * S5_SEGS
    r = _dot(u_ref[...], r1_ref[...])
    yi_ref[...] = r[:, :512]
    e_ref[...] = r[:, 512:]
    coef = coef_ref[...]
    a16fr, a16fi, a16br, a16bi = coef[0:1], coef[1:2], coef[2:3], coef[3:4]
    anfr, anfi, anbr, anbi = coef[4:5], coef[5:6], coef[6:7], coef[7:8]
    zero = jnp.zeros((S5_SEGS, 128), F32)
    sub = lax.broadcasted_iota(jnp.int32, (S5_SEGS, 128), 0)

    for b in range(n_batch):
        base = b * rows_b

        def local_step(j, carry):
            sfr, sfi, sbr, sbi = carry
            rf = pl.multiple_of(base + j * S5_SEGS, S5_SEGS)
            rb = pl.multiple_of(base + (n_steps - 1 - j) * S5_SEGS, S5_SEGS)
            s_ref[pl.ds(rf, S5_SEGS), 0:128] = sfr
            s_ref[pl.ds(rf, S5_SEGS), 128:256] = sfi
            s_ref[pl.ds(rb, S5_SEGS), 256:384] = sbr
            s_ref[pl.ds(rb, S5_SEGS), 384:512] = sbi
            efr = e_ref[pl.ds(rf, S5_SEGS), 0:128]
            efi = e_ref[pl.ds(rf, S5_SEGS), 128:256]
            ebr = e_ref[pl.ds(rb, S5_SEGS), 256:384]
            ebi = e_ref[pl.ds(rb, S5_SEGS), 384:512]
            nfr = a16fr * sfr - a16fi * sfi + efr
            nfi = a16fr * sfi + a16fi * sfr + efi
            nbr = a16br * sbr - a16bi * sbi + ebr
            nbi = a16br * sbi + a16bi * sbr + ebi
            return nfr, nfi, nbr, nbi

        efr, efi, ebr, ebi = lax.fori_loop(0, n_steps, local_step, (zero, zero, zero, zero))

        cfr, cfi, cbr, cbi = zero, zero, zero, zero
        for _ in range(S5_SEGS - 1):
            tfr = anfr * cfr - anfi * cfi + efr
            tfi = anfr * cfi + anfi * cfr + efi
            cfr = jnp.where(sub == 0, 0.0, pltpu.roll(tfr, 1, 0))
            cfi = jnp.where(sub == 0, 0.0, pltpu.roll(tfi, 1, 0))
            tbr = anbr * cbr - anbi * cbi + ebr
            tbi = anbr * cbi + anbi * cbr + ebi
            cbr = jnp.where(sub == S5_SEGS - 1, 0.0, pltpu.roll(tbr, S5_SEGS - 1, 0))
            cbi = jnp.where(sub == S5_SEGS - 1, 0.0, pltpu.roll(tbi, S5_SEGS - 1, 0))

        def fix_step(j, _):
            rj = pl.multiple_of(base + j * S5_SEGS, S5_SEGS)
            pfr = ptab_ref[0, pl.ds(j, 1), :]
            pfi = ptab_ref[1, pl.ds(j, 1), :]
            pbr = ptab_ref[2, pl.ds(j, 1), :]
            pbi = ptab_ref[3, pl.ds(j, 1), :]
            s_ref[pl.ds(rj, S5_SEGS), 0:128] += pfr * cfr - pfi * cfi
            s_ref[pl.ds(rj, S5_SEGS), 128:256] += pfr * cfi + pfi * cfr
            s_ref[pl.ds(rj, S5_SEGS), 256:384] += pbr * cbr - pbi * cbi
            s_ref[pl.ds(rj, S5_SEGS), 384:512] += pbr * cbi + pbi * cbr
            return 0

        lax.fori_loop(0, n_steps, fix_step, 0)

    y = yi_ref[...] + _dot(s_ref[...].astype(BF16), r2_ref[...])
    y_ref[...] = _gelu_tanh(y).astype(y_ref.dtype)


def _s5_mixer(u_t, r1, r2, coef, ptab, n_batch, n_steps):
    npair, m, _ = u_t.shape
    kern = functools.partial(_s5_kernel, n_batch=n_batch, n_steps=n_steps)
    return pl.pallas_call(
        kern,
        grid=(npair,),
        in_specs=[pl.BlockSpec((None, m, 512), lambda i: (i, 0, 0)),
                  pl.BlockSpec((None, 512, 1024), lambda i: (i, 0, 0)),
                  pl.BlockSpec((None, 512, 512), lambda i: (i, 0, 0)),
                  pl.BlockSpec((None, 8, 128), lambda i: (i, 0, 0)),
                  pl.BlockSpec((None, 4, n_steps, 128), lambda i: (i, 0, 0, 0))],
        out_specs=pl.BlockSpec((None, m, 512), lambda i: (i, 0, 0)),
        out_shape=jax.ShapeDtypeStruct((npair, m, 512), BF16),
        scratch_shapes=[pltpu.VMEM((m, 512), F32), pltpu.VMEM((m, 512), F32), pltpu.VMEM((m, 512), F32)],
        compiler_params=_cparams(1),
        name="s5_scan",
    )(u_t, r1, r2, coef, ptab)


def _s5_to_chunks(u, n_batch, seq):
    j = seq // (S5_SEGS * S5_CHUNK)
    x = u.reshape(n_batch, S5_SEGS, j, S5_CHUNK, S5_GROUPS // 2, 2, S5_GROUP_CH)
    x = jnp.transpose(x, (4, 0, 2, 1, 5, 3, 6))
    return x.reshape(S5_GROUPS // 2, n_batch * j * S5_SEGS, 2 * S5_CHUNK * S5_GROUP_CH)


def _s5_from_chunks(y, n_batch, seq):
    j = seq // (S5_SEGS * S5_CHUNK)
    x = y.reshape(S5_GROUPS // 2, n_batch, j, S5_SEGS, 2, S5_CHUNK, S5_GROUP_CH)
    x = jnp.transpose(x, (1, 3, 2, 5, 0, 4, 6))
    return x.reshape(n_batch * seq, S5_WIDTH)


def _gla_kernel(qf_ref, kf_ref, vf_ref, lf_ref, qb_ref, kb_ref, vb_ref, lb_ref, wg_ref, bg_ref,
                of_ref, ob_ref, st_ref):
    @pl.when(pl.program_id(1) == 0)
    def _():
        st_ref[...] = jnp.zeros_like(st_ref)

    n_chunks = GLA_ROWS // GLA_CHUNK
    row = lax.broadcasted_iota(jnp.int32, (GLA_CHUNK, GLA_CHUNK), 0)
    col = lax.broadcasted_iota(jnp.int32, (GLA_CHUNK, GLA_CHUNK), 1)
    tri_f = (col <= row).astype(BF16)
    tri_b = (col >= row).astype(BF16)
    mask_f = col <= row
    mask_b = col > row
    wg = wg_ref[...]
    bg = bg_ref[...]

    def direction(q_ref, k_ref, v_ref, l_ref, o_ref, d):
        z = _dot(l_ref[...], wg[:, d * GLA_KEY:(d + 1) * GLA_KEY]) + bg[:, d * GLA_KEY:(d + 1) * GLA_KEY]
        log_a = jax.nn.log_sigmoid(z) * (1.0 / GLA_TAU)
        order = range(n_chunks) if d == 0 else range(n_chunks - 1, -1, -1)
        tri = tri_f if d == 0 else tri_b
        mask = mask_f if d == 0 else mask_b
        for c in order:
            sl = slice(c * GLA_CHUNK, (c + 1) * GLA_CHUNK)
            la = log_a[sl]
            la_hi = la.astype(BF16)
            la_lo = (la - la_hi.astype(F32)).astype(BF16)
            bc = _dot(tri, la_hi) + _dot(tri, la_lo)
            b_last = bc[GLA_CHUNK - 1:GLA_CHUNK] if d == 0 else bc[0:1]
            q = q_ref[sl, :].astype(F32) * (GLA_DK ** -0.5)
            k = k_ref[sl, :].astype(F32)
            v = v_ref[sl, :]
            qd = (q * jnp.exp(bc)).astype(BF16)
            kd = (k * jnp.exp(-bc)).astype(BF16)
            kc = (k * jnp.exp(b_last - bc)).astype(BF16)
            decay = jnp.exp(b_last)
            outs = []
            for h in range(GLA_HEADS):
                ks = slice(h * GLA_DK, (h + 1) * GLA_DK)
                vs = slice(h * GLA_DV, (h + 1) * GLA_DV)
                s = jnp.where(mask, _dot_nt(qd[:, ks], kd[:, ks]), 0.0).astype(BF16)
                st = st_ref[d, h]
                o = _dot(s, v[:, vs]) + _dot_nt(qd[:, ks], st.astype(BF16))
                st_ref[d, h] = st * decay[:, ks] + _dot_tn(v[:, vs], kc[:, ks])
                outs.append(o)
            o_ref[sl, :] = jnp.concatenate(outs, axis=-1).astype(o_ref.dtype)

    direction(qf_ref, kf_ref, vf_ref, lf_ref, of_ref, 0)
    direction(qb_ref, kb_ref, vb_ref, lb_ref, ob_ref, 1)


def _gla_mixer(h0, wg, bg, n_batch, seq):
    nb = seq // GLA_ROWS
    r = GLA_ROWS
    fwd = lambda b, i: b * nb + i
    bwd = lambda b, i: b * nb + (nb - 1 - i)
    def spec(width, colblk, rowfn):
        return pl.BlockSpec((r, width), lambda b, i: (rowfn(b, i), colblk))
    in_specs = [spec(256, 2, fwd), spec(256, 3, fwd), spec(512, 2, fwd), spec(128, 16, fwd),
                spec(256, 2, bwd), spec(256, 3, bwd), spec(512, 2, bwd), spec(128, 16, bwd),
                pl.BlockSpec((128, 512), lambda b, i: (0, 0)),
                pl.BlockSpec((1, 512), lambda b, i: (0, 0))]
    out_specs = [pl.BlockSpec((r, 512), lambda b, i: (fwd(b, i), 0)),
                 pl.BlockSpec((r, 512), lambda b, i: (bwd(b, i), 0))]
    t = n_batch * seq
    return pl.pallas_call(
        _gla_kernel,
        grid=(n_batch, nb),
        in_specs=in_specs,
        out_specs=out_specs,
        out_shape=[jax.ShapeDtypeStruct((t, 512), BF16), jax.ShapeDtypeStruct((t, 512), BF16)],
        scratch_shapes=[pltpu.VMEM((2, GLA_HEADS, GLA_DV, GLA_DK), F32)],
        compiler_params=_cparams(2),
        name="gla_chunked",
    )(h0, h0, h0, h0, h0, h0, h0, h0, wg, bg)


def _mix0_out_kernel(ya_ref, of_ref, ob_ref, go_ref, x_ref, gw_ref, gb_ref, ng_ref, wo_ref, lg_ref, lb_ref,
                     h_ref, h8_ref):
    ya = ya_ref[...]
    yaf = ya.astype(F32)
    gate = _dot(ya, gw_ref[...]) + gb_ref[...]
    ya2 = yaf * jax.nn.sigmoid(gate)
    o = of_ref[...].astype(F32) + ob_ref[...].astype(F32)
    ng = ng_ref[...]
    parts = []
    for h in range(GLA_HEADS):
        oh = o[:, h * GLA_DV:(h + 1) * GLA_DV]
        ms = jnp.mean(oh * oh, axis=-1, keepdims=True)
        parts.append(oh * lax.rsqrt(ms + RMS_EPS) * ng[:, h * GLA_DV:(h + 1) * GLA_DV])
    yb = jnp.concatenate(parts, axis=-1) * jax.nn.silu(go_ref[...].astype(F32))
    wo = wo_ref[...]
    y = _dot(ya2.astype(BF16), wo[:S5_WIDTH]) + _dot(yb.astype(BF16), wo[S5_WIDTH:])
    hn = _layer_norm(ALPHA * x_ref[...] + y, lg_ref[...], lb_ref[...])
    h_ref[...] = hn
    _store_token_tiles(h8_ref, hn)


def _mix0_out(ya, o_f, o_b, h0, x, glu_w, glu_b, norm_g, w_out, ln_g, ln_b):
    t = x.shape[0]
    tm = ROW_TILE
    row = lambda w: pl.BlockSpec((tm, w), lambda i: (i, 0))
    full = lambda a, b: pl.BlockSpec((a, b), lambda i: (0, 0))
    return pl.pallas_call(
        _mix0_out_kernel,
        grid=(t // tm,),
        in_specs=[row(512), row(512), row(512),
                  pl.BlockSpec((tm, 512), lambda i: (i, 3)),
                  row(1024), full(512, 512), full(1, 512), full(1, 512), full(1024, 1024),
                  full(1, 1024), full(1, 1024)],
        out_specs=[row(1024), pl.BlockSpec((tm * 8, 128), lambda i: (i, 0))],
        out_shape=[jax.ShapeDtypeStruct((t, D_MODEL), F32), jax.ShapeDtypeStruct((t * 8, 128), F32)],
        compiler_params=_cparams(1),
        name="mix0_out_ln",
    )(ya, o_f, o_b, h0, x, glu_w, glu_b, norm_g, w_out, ln_g, ln_b)


def _attn_kernel(q_ref, kc_ref, vc_ref, kp_ref, vp_ref, kn_ref, vn_ref, bias_ref, sink_ref, o_ref, *, seq):
    blk = pl.program_id(1)
    n_sub = ATT_ROWS // ATT_BLOCK
    kw = ATT_BLOCK + 2 * WINDOW
    qi = lax.broadcasted_iota(jnp.int32, (GQA * ATT_BLOCK, kw), 0) & (ATT_BLOCK - 1)
    kj = lax.broadcasted_iota(jnp.int32, (GQA * ATT_BLOCK, kw), 1)
    band = jnp.abs(kj - WINDOW - qi) <= WINDOW
    kall = jnp.concatenate([kp_ref[...], kc_ref[...], kn_ref[...]], axis=0)
    vall = jnp.concatenate([vp_ref[...], vc_ref[...], vn_ref[...]], axis=0)
    for s in range(n_sub):
        start = blk * ATT_ROWS + s * ATT_BLOCK
        kpos = start - WINDOW + kj
        valid4 = band & (kpos >= 0) & (kpos < seq)
        kwin = kall[s * ATT_BLOCK:s * ATT_BLOCK + kw]
        vwin = vall[s * ATT_BLOCK:s * ATT_BLOCK + kw]
        q = q_ref[s * ATT_BLOCK:(s + 1) * ATT_BLOCK, :]
        outs = []
        for kv in range(N_KV):
            qs = jnp.concatenate([q[:, (kv * GQA + g) * HEAD_DIM:(kv * GQA + g + 1) * HEAD_DIM]
                                  for g in range(GQA)], axis=0)
            kh = kwin[:, kv * HEAD_DIM:(kv + 1) * HEAD_DIM]
            vh = vwin[:, kv * HEAD_DIM:(kv + 1) * HEAD_DIM]
            sc = _dot_nt(qs, kh) * ATT_SCALE + bias_ref[kv]
            sc = jnp.where(valid4, sc, NEG_INF)
            sink = sink_ref[kv]
            m = jnp.maximum(jnp.max(sc, axis=-1, keepdims=True), sink)
            p = jnp.exp(sc - m)
            den = jnp.sum(p, axis=-1, keepdims=True) + jnp.exp(sink - m)
            o = _dot(p.astype(BF16), vh) / den
            for g in range(GQA):
                outs.append(o[g * ATT_BLOCK:(g + 1) * ATT_BLOCK])
        o_ref[s * ATT_BLOCK:(s + 1) * ATT_BLOCK, :] = jnp.concatenate(outs, axis=-1).astype(o_ref.dtype)


def _t5_bucket(rel):
    nb = REL_BUCKETS // 2
    max_exact = nb // 2
    ret = (rel > 0).astype(jnp.int32) * nb
    n = jnp.abs(rel)
    large = max_exact + (jnp.log(jnp.maximum(n, 1).astype(F32) / max_exact)
                         / math.log(REL_MAX_DIST / max_exact) * (nb - max_exact)).astype(jnp.int32)
    large = jnp.minimum(large, nb - 1)
    return ret + jnp.where(n < max_exact, n, large)


def _attn_mixer(h3, rel_bias, sink, n_batch, seq):
    kw = ATT_BLOCK + 2 * WINDOW
    rel = jnp.arange(kw)[None, :] - WINDOW - jnp.arange(ATT_BLOCK)[:, None]
    bias = rel_bias.astype(F32)[_t5_bucket(rel)]
    bias = jnp.transpose(bias, (2, 0, 1)).reshape(N_KV, GQA * ATT_BLOCK, kw)
    sink_rows = jnp.repeat(sink.astype(F32).reshape(N_KV, GQA), ATT_BLOCK, axis=1)[..., None]
    nblk = seq // ATT_ROWS
    sub = ATT_ROWS // ATT_BLOCK
    n128 = seq // ATT_BLOCK
    cur = lambda b, i: b * nblk + i
    prev = lambda b, i: b * n128 + jnp.maximum(i * sub - 1, 0)
    nxt = lambda b, i: b * n128 + jnp.minimum((i + 1) * sub, n128 - 1)
    t = n_batch * seq
    kern = functools.partial(_attn_kernel, seq=seq)
    return pl.pallas_call(
        kern,
        grid=(n_batch, nblk),
        in_specs=[pl.BlockSpec((ATT_ROWS, ATT_DIM), lambda b, i: (cur(b, i), 0)),
                  pl.BlockSpec((ATT_ROWS, KV_DIM), lambda b, i: (cur(b, i), 4)),
                  pl.BlockSpec((ATT_ROWS, KV_DIM), lambda b, i: (cur(b, i), 5)),
                  pl.BlockSpec((ATT_BLOCK, KV_DIM), lambda b, i: (prev(b, i), 4)),
                  pl.BlockSpec((ATT_BLOCK, KV_DIM), lambda b, i: (prev(b, i), 5)),
                  pl.BlockSpec((ATT_BLOCK, KV_DIM), lambda b, i: (nxt(b, i), 4)),
                  pl.BlockSpec((ATT_BLOCK, KV_DIM), lambda b, i: (nxt(b, i), 5)),
                  pl.BlockSpec((N_KV, GQA * ATT_BLOCK, kw), lambda b, i: (0, 0, 0)),
                  pl.BlockSpec((N_KV, GQA * ATT_BLOCK, 1), lambda b, i: (0, 0, 0))],
        out_specs=pl.BlockSpec((ATT_ROWS, ATT_DIM), lambda b, i: (cur(b, i), 0)),
        out_shape=jax.ShapeDtypeStruct((t, ATT_DIM), BF16),
        compiler_params=_cparams(2),
        name="window_gqa",
    )(h3, h3, h3, h3, h3, h3, h3, bias, sink_rows)


def _proj_ln_kernel(a_ref, w_ref, x_ref, g_ref, b_ref, h_ref, h8_ref):
    y = _dot(a_ref[...], w_ref[...])
    hn = _layer_norm(ALPHA * x_ref[...] + y, g_ref[...], b_ref[...])
    h_ref[...] = hn
    _store_token_tiles(h8_ref, hn)


def _proj_ln(a, w, x, g, b):
    t, k = a.shape
    tm = ROW_TILE
    return pl.pallas_call(
        _proj_ln_kernel,
        grid=(t // tm,),
        in_specs=[pl.BlockSpec((tm, k), lambda i: (i, 0)),
                  pl.BlockSpec((k, D_MODEL), lambda i: (0, 0)),
                  pl.BlockSpec((tm, D_MODEL), lambda i: (i, 0)),
                  pl.BlockSpec((1, D_MODEL), lambda i: (0, 0)),
                  pl.BlockSpec((1, D_MODEL), lambda i: (0, 0))],
        out_specs=[pl.BlockSpec((tm, D_MODEL), lambda i: (i, 0)), pl.BlockSpec((tm * 8, 128), lambda i: (i, 0))],
        out_shape=[jax.ShapeDtypeStruct((t, D_MODEL), F32), jax.ShapeDtypeStruct((t * 8, 128), F32)],
        compiler_params=_cparams(1),
        name="proj_ln",
    )(a, w, x, g, b)


def _router_kernel(h_ref, wh_ref, wl_ref, b_ref, o_ref):
    x = h_ref[...]
    xh = x.astype(BF16)
    xl = (x - xh.astype(F32)).astype(BF16)
    wh = wh_ref[...]
    logits = _dot(xh, wh) + (_dot(xl, wh) + _dot(xh, wl_ref[...])) + b_ref[...]
    lane = lax.broadcasted_iota(jnp.int32, logits.shape, 1)
    big = jnp.int32(1 << 20)
    neg = jnp.float32(-jnp.inf)
    is_g = lane < N_GROUPS
    gl = jnp.where(is_g, logits, neg)
    gmax = jnp.max(gl, axis=-1, keepdims=True)
    gsum = jnp.sum(jnp.where(is_g, jnp.exp(gl - gmax), 0.0), axis=-1, keepdims=True)
    grp = jnp.min(jnp.where(is_g & (gl == gmax), lane, big), axis=-1, keepdims=True)
    p_grp = 1.0 / gsum
    lo = N_GROUPS + grp * EPG
    in_grp = (lane >= lo) & (lane < lo + EPG)
    el = jnp.where(in_grp, logits, neg)
    m1 = jnp.max(el, axis=-1, keepdims=True)
    i1 = jnp.min(jnp.where(in_grp & (el == m1), lane, big), axis=-1, keepdims=True)
    el2 = jnp.where(lane == i1, neg, el)
    m2 = jnp.max(el2, axis=-1, keepdims=True)
    i2 = jnp.min(jnp.where(in_grp & (lane != i1) & (el2 == m2), lane, big), axis=-1, keepdims=True)
    e2 = jnp.exp(m2 - m1)
    g1 = p_grp / (1.0 + e2)
    g2 = p_grp * e2 / (1.0 + e2)
    out = jnp.where(lane == 0, (i1 - N_GROUPS).astype(F32),
                    jnp.where(lane == 1, (i2 - N_GROUPS).astype(F32),
                              jnp.where(lane == 2, g1, jnp.where(lane == 3, g2, 0.0))))
    o_ref[...] = out


def _router(h, w_group, b_group, w_router, b_router):
    t = h.shape[0]
    w = jnp.concatenate([w_group.astype(F32), w_router.astype(F32)], axis=1)
    w = jnp.pad(w, ((0, 0), (0, 128 - w.shape[1])))
    wh = w.astype(BF16)
    wl = (w - wh.astype(F32)).astype(BF16)
    bias = jnp.pad(jnp.concatenate([b_group.astype(F32), b_router.astype(F32)]), (0, 128 - 36))[None]
    tm = ROW_TILE
    out = pl.pallas_call(
        _router_kernel,
        grid=(t // tm,),
        in_specs=[pl.BlockSpec((tm, D_MODEL), lambda i: (i, 0)),
                  pl.BlockSpec((D_MODEL, 128), lambda i: (0, 0)),
                  pl.BlockSpec((D_MODEL, 128), lambda i: (0, 0)),
                  pl.BlockSpec((1, 128), lambda i: (0, 0))],
        out_specs=pl.BlockSpec((tm, 128), lambda i: (i, 0)),
        out_shape=jax.ShapeDtypeStruct((t, 128), F32),
        compiler_params=_cparams(1),
        name="moe_router",
    )(h, wh, wl, bias)
    eid = out[:, 0:2].astype(jnp.int32)
    gate = out[:, 2:4]
    return eid, gate


def _token_tile(ref, tok):
    return ref.at[pl.ds(pl.multiple_of(tok * 8, 8), 8)]


def _dispatch_kernel(dst_hbm, meta_ref, h_ref, zero_hbm, xs_hbm, idx_smem, sem, isem):
    i = pl.program_id(0)
    tm = ROW_TILE

    @pl.when(i == 0)
    def _():
        def zstart(e, _):
            end = meta_ref[e]
            @pl.when(meta_ref[N_EXPERTS + e] > 0)
            def _():
                off = pl.multiple_of((end - MOE_ROWS) * 8, 8)
                pltpu.make_async_copy(zero_hbm, xs_hbm.at[pl.ds(off, MOE_ROWS * 8)], sem).start()
            return 0
        lax.fori_loop(0, N_EXPERTS, zstart, 0)
        def zwait(e, _):
            @pl.when(meta_ref[N_EXPERTS + e] > 0)
            def _():
                pltpu.make_async_copy(zero_hbm, xs_hbm.at[pl.ds(0, MOE_ROWS * 8)], sem).wait()
            return 0
        lax.fori_loop(0, N_EXPERTS, zwait, 0)
        n_used = meta_ref[2 * N_EXPERTS]
        n_blk = xs_hbm.shape[0] // (MOE_ROWS * 8)
        def tstart(blk, _):
            off = pl.multiple_of(blk * (MOE_ROWS * 8), 8)
            pltpu.make_async_copy(zero_hbm, xs_hbm.at[pl.ds(off, MOE_ROWS * 8)], sem).start()
            return 0
        lax.fori_loop(n_used, n_blk, tstart, 0)
        def twait(blk, _):
            pltpu.make_async_copy(zero_hbm, xs_hbm.at[pl.ds(0, MOE_ROWS * 8)], sem).wait()
            return 0
        lax.fori_loop(n_used, n_blk, twait, 0)

    cp = pltpu.make_async_copy(dst_hbm.at[pl.ds(i * 2 * tm, 2 * tm)], idx_smem, isem)
    cp.start()
    cp.wait()

    def issue(t, _):
        src = _token_tile(h_ref, t)
        pltpu.make_async_copy(src, _token_tile(xs_hbm, idx_smem[2 * t]), sem).start()
        pltpu.make_async_copy(src, _token_tile(xs_hbm, idx_smem[2 * t + 1]), sem).start()
        return 0
    lax.fori_loop(0, tm, issue, 0, unroll=8)
    for _ in range(2):
        pltpu.make_async_copy(h_ref, xs_hbm.at[pl.ds(0, tm * 8)], sem).wait()


def _dispatch(h8, dst, meta, cap):
    t = h8.shape[0] // 8
    zero = jnp.zeros((MOE_ROWS * 8, 128), F32)
    return pl.pallas_call(
        _dispatch_kernel,
        grid_spec=pltpu.PrefetchScalarGridSpec(
            num_scalar_prefetch=0,
            grid=(t // ROW_TILE,),
            in_specs=[pl.BlockSpec(memory_space=pl.ANY),
                      pl.BlockSpec(memory_space=pltpu.SMEM),
                      pl.BlockSpec((ROW_TILE * 8, 128), lambda i: (i, 0)),
                      pl.BlockSpec(memory_space=pl.ANY)],
            out_specs=pl.BlockSpec(memory_space=pl.ANY),
            scratch_shapes=[pltpu.SMEM((2 * ROW_TILE,), jnp.int32),
                            pltpu.SemaphoreType.DMA(()),
                            pltpu.SemaphoreType.DMA(())]),
        out_shape=jax.ShapeDtypeStruct((cap * 8, 128), F32),
        compiler_params=_cparams(1),
        name="moe_dispatch",
    )(dst, meta, h8, zero)


def _experts_kernel(be_ref, nu_ref, xs_ref, w1_ref, w3_ref, w2_ref, ys_ref, w13_s, w2_s):
    i = pl.program_id(0)
    used = i < nu_ref[0]
    prev = be_ref[jnp.maximum(i - 1, 0)]
    fresh = (i == 0) | (be_ref[i] != prev)

    @pl.when(used & fresh)
    def _():
        w13_s[:, :D_EXPERT] = w1_ref[...].astype(BF16)
        w13_s[:, D_EXPERT:] = w3_ref[...].astype(BF16)
        w2_s[...] = w2_ref[...].astype(BF16)

    @pl.when(used)
    def _():
        x = _load_token_tiles(xs_ref, MOE_ROWS).astype(BF16)
        h = _dot(x, w13_s[...])
        h1 = h[:, :D_EXPERT]
        hdn = (h1 * jax.nn.sigmoid(h1)) * h[:, D_EXPERT:]
        _store_token_tiles(ys_ref, _dot(hdn.astype(BF16), w2_s[...]))

    @pl.when(jnp.logical_not(used))
    def _():
        ys_ref[...] = jnp.zeros_like(ys_ref)


def _experts(xs, blk_e, n_used, w1, w3, w2):
    cap = xs.shape[0] // 8
    nblk = cap // MOE_ROWS
    last = lambda i, nu: jnp.minimum(i, nu[0] - 1)
    return pl.pallas_call(
        _experts_kernel,
        grid_spec=pltpu.PrefetchScalarGridSpec(
            num_scalar_prefetch=2,
            grid=(nblk,),
            in_specs=[pl.BlockSpec((MOE_ROWS * 8, 128), lambda i, be, nu: (last(i, nu), 0)),
                      pl.BlockSpec((None, D_MODEL, D_EXPERT), lambda i, be, nu: (be[i], 0, 0)),
                      pl.BlockSpec((None, D_MODEL, D_EXPERT), lambda i, be, nu: (be[i], 0, 0)),
                      pl.BlockSpec((None, D_EXPERT, D_MODEL), lambda i, be, nu: (be[i], 0, 0))],
            out_specs=pl.BlockSpec((MOE_ROWS * 8, 128), lambda i, be, nu: (i, 0)),
            scratch_shapes=[pltpu.VMEM((D_MODEL, 2 * D_EXPERT), BF16),
                            pltpu.VMEM((D_EXPERT, D_MODEL), BF16)]),
        out_shape=jax.ShapeDtypeStruct((cap * 8, 128), F32),
        compiler_params=_cparams(1),
        name="moe_experts",
    )(blk_e, n_used, xs, w1, w3, w2)


def _combine_kernel(dst_hbm, ys_hbm, h_ref, gate_ref, g_ref, b_ref, o_ref, idx_smem, y0_ref, y1_ref, sem, isem):
    i = pl.program_id(0)
    tm = ROW_TILE
    cp = pltpu.make_async_copy(dst_hbm.at[pl.ds(i * 2 * tm, 2 * tm)], idx_smem, isem)
    cp.start()
    cp.wait()

    def issue(t, _):
        pltpu.make_async_copy(_token_tile(ys_hbm, idx_smem[2 * t]), _token_tile(y0_ref, t), sem).start()
        pltpu.make_async_copy(_token_tile(ys_hbm, idx_smem[2 * t + 1]), _token_tile(y1_ref, t), sem).start()
        return 0
    lax.fori_loop(0, tm, issue, 0, unroll=8)
    pltpu.make_async_copy(ys_hbm.at[pl.ds(0, tm * 8)], y0_ref, sem).wait()
    pltpu.make_async_copy(ys_hbm.at[pl.ds(0, tm * 8)], y1_ref, sem).wait()
    gate = gate_ref[...]
    y = _load_token_tiles(y0_ref, tm) * gate[:, 0:1] + _load_token_tiles(y1_ref, tm) * gate[:, 1:2]
    o_ref[...] = _layer_norm(ALPHA * h_ref[...] + y, g_ref[...], b_ref[...])


def _combine_ln(dst, ys, h, gate, ln_g, ln_b):
    t = h.shape[0]
    tm = ROW_TILE
    return pl.pallas_call(
        _combine_kernel,
        grid_spec=pltpu.PrefetchScalarGridSpec(
            num_scalar_prefetch=0,
            grid=(t // tm,),
            in_specs=[pl.BlockSpec(memory_space=pl.ANY),
                      pl.BlockSpec(memory_space=pl.ANY),
                      pl.BlockSpec((tm, D_MODEL), lambda i: (i, 0)),
                      pl.BlockSpec((tm, 2), lambda i: (i, 0)),
                      pl.BlockSpec((1, D_MODEL), lambda i: (0, 0)),
                      pl.BlockSpec((1, D_MODEL), lambda i: (0, 0))],
            out_specs=pl.BlockSpec((tm, D_MODEL), lambda i: (i, 0)),
            scratch_shapes=[pltpu.SMEM((2 * tm,), jnp.int32),
                            pltpu.VMEM((tm * 8, 128), F32),
                            pltpu.VMEM((tm * 8, 128), F32),
                            pltpu.SemaphoreType.DMA(()),
                            pltpu.SemaphoreType.DMA(())]),
        out_shape=jax.ShapeDtypeStruct((t, D_MODEL), F32),
        compiler_params=_cparams(1),
        name="moe_combine_ln",
    )(dst, ys, h, gate, ln_g, ln_b)


def _moe_layer(h, h8, w_group, b_group, w_router, b_router, w1, w3, w2, ln_g, ln_b):
    t = h.shape[0]
    eid, gate = _router(h, w_group, b_group, w_router, b_router)
    fe = eid.reshape(-1)
    onehot = (fe[:, None] == jnp.arange(N_EXPERTS)[None, :]).astype(jnp.int32)
    csum = jnp.cumsum(onehot, axis=0)
    rank = jnp.take_along_axis(csum, fe[:, None], axis=1)[:, 0] - 1
    counts = csum[-1]
    padded = (counts + MOE_ROWS - 1) // MOE_ROWS * MOE_ROWS
    pad_end = jnp.cumsum(padded)
    pad_start = pad_end - padded
    dst = (pad_start[fe] + rank).astype(jnp.int32)
    cap = 2 * t + N_EXPERTS * MOE_ROWS
    nblk = cap // MOE_ROWS
    blk_e = jnp.minimum(jnp.searchsorted(pad_end, jnp.arange(nblk) * MOE_ROWS, side='right'),
                        N_EXPERTS - 1).astype(jnp.int32)
    n_used = (pad_end[-1] // MOE_ROWS).astype(jnp.int32).reshape(1)
    meta = jnp.concatenate([pad_end, padded, n_used]).astype(jnp.int32)
    xs = _dispatch(h8, dst, meta, cap)
    ys = _experts(xs, blk_e, n_used, w1, w3, w2)
    return _combine_ln(dst, ys, h, gate, ln_g, ln_b)


def kernel(x, ln_mix_g, ln_mix_b, ln_ffn_g, ln_ffn_b, ab_w_in, s5_lam_re, s5_lam_im, s5_log_dt, s5_b_re, s5_b_im,
           s5_c_re, s5_c_im, s5_d, s5_glu_w, s5_glu_b, gla_gate_w, gla_gate_b, gla_norm_g, ab_w_out, c_w_in,
           c_sink, c_w_out, rel_bias, moe_w_group, moe_b_group, moe_w_router, moe_b_router, moe_w1, moe_w3, moe_w2):
    n_batch, seq, _ = x.shape
    t = n_batch * seq
    xt = x.reshape(t, D_MODEL)
    row = lambda v: v.astype(F32).reshape(1, -1)

    w_in0 = jnp.pad(ab_w_in[0], ((0, 0), (0, AB_IN_PAD - AB_IN))).astype(BF16)
    h0 = _matmul(xt, w_in0, BF16)
    n_steps = seq // (S5_SEGS * S5_CHUNK)
    r1, r2, coef, ptab = _s5_prep(s5_lam_re[0], s5_lam_im[0], s5_log_dt[0], s5_b_re[0], s5_b_im[0],
                                  s5_c_re[0], s5_c_im[0], s5_d[0], n_steps)
    u_t = _s5_to_chunks(h0[:, :S5_WIDTH], n_batch, seq)
    ya = _s5_from_chunks(_s5_mixer(u_t, r1, r2, coef, ptab, n_batch, n_steps), n_batch, seq)
    gw = gla_gate_w[0].astype(F32)
    wg = jnp.zeros((128, 2 * GLA_KEY), F32)
    wg = wg.at[0:GLA_RANK, 0:GLA_KEY].set(gw[0]).at[GLA_RANK:2 * GLA_RANK, GLA_KEY:].set(gw[1]).astype(BF16)
    bg = gla_gate_b[0].astype(F32).reshape(1, 2 * GLA_KEY)
    o_f, o_b = _gla_mixer(h0, wg, bg, n_batch, seq)
    h, h8 = _mix0_out(ya, o_f, o_b, h0, xt, s5_glu_w[0].astype(BF16), row(s5_glu_b[0]), row(gla_norm_g[0]),
                      ab_w_out[0].astype(BF16), row(ln_mix_g[0]), row(ln_mix_b[0]))
    h = _moe_layer(h, h8, moe_w_group[0], moe_b_group[0], moe_w_router[0], moe_b_router[0],
                   moe_w1[0], moe_w3[0], moe_w2[0], row(ln_ffn_g[0]), row(ln_ffn_b[0]))

    h3 = _matmul(h, c_w_in[0].astype(BF16), BF16)
    o = _attn_mixer(h3, rel_bias, c_sink[0], n_batch, seq)
    h, h8 = _proj_ln(o, c_w_out[0].astype(BF16), h, row(ln_mix_g[1]), row(ln_mix_b[1]))
    h = _moe_layer(h, h8, moe_w_group[1], moe_b_group[1], moe_w_router[1], moe_b_router[1],
                   moe_w1[1], moe_w3[1], moe_w2[1], row(ln_ffn_g[1]), row(ln_ffn_b[1]))
    return h.reshape(n_batch, seq, D_MODEL)
```

```python
import functools
import math

import jax
import jax.numpy as jnp
from jax import lax
from jax.experimental import pallas as pl
from jax.experimental.pallas import tpu as pltpu

F32 = jnp.float32
BF16 = jnp.bfloat16

D_MODEL = 1024
S5_WIDTH = 512
S5_GROUP_CH = 16
S5_GROUPS = 32
S5_STATE = 64
GLA_HEADS = 4
GLA_DV = 128
GLA_DK = 64
GLA_KEY = 256
GLA_WIDTH = 512
GLA_RANK = 16
GLA_TAU = 16.0
GLA_CHUNK = 64
AB_IN = 2080
AB_IN_PAD = 2176
HEAD_DIM = 64
N_HEADS = 16
N_KV = 4
GQA = 4
ATT_DIM = 1024
KV_DIM = 256
WINDOW = 128
ATT_BLOCK = 128
ATT_SCALE = HEAD_DIM ** -0.5
REL_BUCKETS = 32
REL_MAX_DIST = 128
NEG_INF = -1e30
N_GROUPS = 4
EPG = 8
N_EXPERTS = 32
D_EXPERT = 512
LN_EPS = 1e-5
RMS_EPS = 1e-6
DEPTH = 2
ALPHA = (2 * DEPTH) ** 0.25

S5_CHUNK = 16
S5_SEGS = 8
ROW_TILE = 512
GLA_ROWS = 256
ATT_ROWS = 512
MOE_ROWS = 256
IDX_PAD = 1024
VMEM_LIMIT = 56 * 1024 * 1024


def _cparams(n_axes):
    return pltpu.CompilerParams(dimension_semantics=("arbitrary",) * n_axes,
                                vmem_limit_bytes=VMEM_LIMIT)


def _dot(a, b):
    return jnp.dot(a, b, preferred_element_type=F32)


def _dot_nt(a, b):
    return lax.dot_general(a, b, (((1,), (1,)), ((), ())), preferred_element_type=F32)


def _dot_tn(a, b):
    return lax.dot_general(a, b, (((0,), (0,)), ((), ())), preferred_element_type=F32)


def _layer_norm(r, g, b):
    mu = jnp.mean(r, axis=-1, keepdims=True)
    c = r - mu
    var = jnp.mean(c * c, axis=-1, keepdims=True)
    return c * lax.rsqrt(var + LN_EPS) * g + b


def _store_token_tiles(ref, val):
    n = val.shape[0]
    for k in range(D_MODEL // 128):
        ref[pl.ds(k, n, stride=8), :] = val[:, k * 128:(k + 1) * 128]


def _load_token_tiles(ref, n):
    return jnp.concatenate([ref[pl.ds(k, n, stride=8), :] for k in range(D_MODEL // 128)], axis=-1)


def _mm_kernel(x_ref, w_ref, o_ref):
    o_ref[...] = _dot(x_ref[...].astype(BF16), w_ref[...]).astype(o_ref.dtype)


def _matmul(x, w, out_dtype):
    m, k = x.shape
    n = w.shape[1]
    return pl.pallas_call(
        _mm_kernel,
        grid=(m // ROW_TILE,),
        in_specs=[pl.BlockSpec((ROW_TILE, k), lambda i: (i, 0)),
                  pl.BlockSpec((k, n), lambda i: (0, 0))],
        out_specs=pl.BlockSpec((ROW_TILE, n), lambda i: (i, 0)),
        out_shape=jax.ShapeDtypeStruct((m, n), out_dtype),
        compiler_params=_cparams(1),
        name="dense_matmul",
    )(x, w)


def _s5_prep(lam_re, lam_im, log_dt, b_re, b_im, c_re, c_im, d, n_steps):
    tc = S5_CHUNK
    g, p, c = S5_GROUPS, S5_STATE, S5_GROUP_CH
    lr = jnp.minimum(lam_re.astype(F32), -1e-4)
    li = lam_im.astype(F32)
    dt = jnp.exp(log_dt.astype(F32))[..., None]
    mag = jnp.exp(lr * dt)
    ar = mag * jnp.cos(li * dt)
    ai = mag * jnp.sin(li * dt)
    den = lr * lr + li * li
    nr = ar - 1.0
    coef_r = (nr * lr + ai * li) / den
    coef_i = (ai * lr - nr * li) / den
    br_ = b_re.astype(F32)
    bi_ = b_im.astype(F32)
    bbr = coef_r[..., None] * br_ - coef_i[..., None] * bi_
    bbi = coef_r[..., None] * bi_ + coef_i[..., None] * br_
    cr = c_re.astype(F32)
    ci = c_im.astype(F32)

    def apow(n):
        nn = n.astype(F32)[:, None, None, None]
        m_ = jnp.exp(nn * (lr * dt)[None])
        ang = nn * (li * dt)[None]
        return m_ * jnp.cos(ang), m_ * jnp.sin(ang)

    lags = jnp.arange(tc + 1)
    pr, pi = apow(lags)
    hr = pr[..., None] * bbr[None] - pi[..., None] * bbi[None]
    hi = pr[..., None] * bbi[None] + pi[..., None] * bbr[None]
    kk = (jnp.einsum('dgop,jdgpc->jdgoc', cr, hr) - jnp.einsum('dgop,jdgpc->jdgoc', ci, hi))
    s_idx = jnp.arange(tc)[:, None]
    t_idx = jnp.arange(tc)[None, :]
    lag_f = jnp.clip(t_idx - s_idx, 0, tc)
    lag_b = jnp.clip(s_idx - t_idx, 0, tc)
    kf = kk[:, 0][lag_f]
    kb = kk[:, 1][lag_b]
    kt = (jnp.where((t_idx >= s_idx)[..., None, None, None], kf, 0.0)
          + jnp.where((s_idx >= t_idx)[..., None, None, None], kb, 0.0))
    dd = d.astype(F32).reshape(g, c)
    eye_st = (s_idx == t_idx).astype(F32)
    eye_c = jnp.eye(c, dtype=F32)
    kt = kt + eye_st[:, :, None, None, None] * (dd[None, None, :, None, :] * eye_c[None, None, None])
    kt = jnp.transpose(kt, (2, 0, 4, 1, 3)).reshape(g, tc * c, tc * c)

    wf_r = jnp.transpose(hr[tc - 1 - jnp.arange(tc), 0], (1, 0, 3, 2)).reshape(g, tc * c, p)
    wf_i = jnp.transpose(hi[tc - 1 - jnp.arange(tc), 0], (1, 0, 3, 2)).reshape(g, tc * c, p)
    wb_r = jnp.transpose(hr[jnp.arange(tc), 1], (1, 0, 3, 2)).reshape(g, tc * c, p)
    wb_i = jnp.transpose(hi[jnp.arange(tc), 1], (1, 0, 3, 2)).reshape(g, tc * c, p)

    ef = jnp.arange(tc) + 1
    eb = tc - jnp.arange(tc)
    def readout(e, dirn):
        zr = cr[dirn][None] * pr[e, dirn][:, :, None, :] - ci[dirn][None] * pi[e, dirn][:, :, None, :]
        zi = cr[dirn][None] * pi[e, dirn][:, :, None, :] + ci[dirn][None] * pr[e, dirn][:, :, None, :]
        v_r = jnp.transpose(zr, (1, 3, 0, 2)).reshape(g, p, tc * c)
        v_i = -jnp.transpose(zi, (1, 3, 0, 2)).reshape(g, p, tc * c)
        return v_r, v_i
    vf_r, vf_i = readout(ef, 0)
    vb_r, vb_i = readout(eb, 1)

    npair = g // 2
    def pair_blockdiag(m):
        r_, c_ = m.shape[1], m.shape[2]
        m2 = m.reshape(npair, 2, r_, c_)
        z = jnp.zeros((npair, r_, c_), F32)
        top = jnp.concatenate([m2[:, 0], z], axis=2)
        bot = jnp.concatenate([z, m2[:, 1]], axis=2)
        return jnp.concatenate([top, bot], axis=1)
    def tok_major(m, axis):
        shp = m.shape
        m = m.reshape(shp[:axis] + (2, tc, c) + shp[axis + 1:])
        return jnp.swapaxes(m, axis, axis + 1).reshape(shp)
    r1 = jnp.concatenate([tok_major(pair_blockdiag(kt), 2), pair_blockdiag(wf_r), pair_blockdiag(wf_i),
                          pair_blockdiag(wb_r), pair_blockdiag(wb_i)], axis=2)
    r1 = tok_major(r1, 1)
    r2 = jnp.concatenate([pair_blockdiag(vf_r), pair_blockdiag(vf_i), pair_blockdiag(vb_r), pair_blockdiag(vb_i)],
                         axis=1)
    r2 = tok_major(r2, 2)

    steps = jnp.arange(n_steps)
    a16r, a16i = apow(jnp.array([tc]))
    apr, api = apow(steps * tc)
    anr, ani = apow(jnp.array([tc * n_steps]))
    def lanes(x):
        n_ = x.shape[0]
        return jnp.transpose(x.reshape(n_, npair, 2 * p), (1, 0, 2))
    coef = jnp.concatenate([lanes(a16r[:, 0]), lanes(a16i[:, 0]), lanes(a16r[:, 1]), lanes(a16i[:, 1]),
                            lanes(anr[:, 0]), lanes(ani[:, 0]), lanes(anr[:, 1]), lanes(ani[:, 1])], axis=1)
    rev = n_steps - 1 - steps
    ptab = jnp.stack([lanes(apr[:, 0]), lanes(api[:, 0]), lanes(apr[rev, 1]), lanes(api[rev, 1])], axis=1)
    return r1.astype(BF16), r2.astype(BF16), coef, ptab


def _gelu_tanh(x):
    return 0.5 * x * (1.0 + jnp.tanh(math.sqrt(2.0 / math.pi) * (x + 0.044715 * (x * x * x))))


def _s5_kernel(u_ref, r1_ref, r2_ref, coef_ref, ptab_ref, y_ref, yi_ref, e_ref, s_ref, *, n_batch, n_steps):
    rows_b = n_steps * S5_SEGS
    r = _dot(u_ref[...], r1_ref[...])
    yi_ref[...] = r[:, :512]
    for k in range(4):
        e_ref[k] = r[:, 512 + k * 128:512 + (k + 1) * 128]
    coef = coef_ref[...]
    a16fr, a16fi, a16br, a16bi = coef[0:1], coef[1:2], coef[2:3], coef[3:4]
    anfr, anfi, anbr, anbi = coef[4:5], coef[5:6], coef[6:7], coef[7:8]
    zero = jnp.zeros((S5_SEGS, 128), F32)
    sub = lax.broadcasted_iota(jnp.int32, (S5_SEGS, 128), 0)

    for b in range(n_batch):
        base = b * rows_b

        def seg_rows(j):
            return pl.ds(base + j, S5_SEGS, stride=n_steps)

        def local_step(j, carry):
            sfr, sfi, sbr, sbi = carry
            rf = seg_rows(j)
            rb = seg_rows(n_steps - 1 - j)
            s_ref[0, rf, :] = sfr
            s_ref[1, rf, :] = sfi
            s_ref[2, rb, :] = sbr
            s_ref[3, rb, :] = sbi
            efr = e_ref[0, rf, :]
            efi = e_ref[1, rf, :]
            ebr = e_ref[2, rb, :]
            ebi = e_ref[3, rb, :]
            nfr = a16fr * sfr - a16fi * sfi + efr
            nfi = a16fr * sfi + a16fi * sfr + efi
            nbr = a16br * sbr - a16bi * sbi + ebr
            nbi = a16br * sbi + a16bi * sbr + ebi
            return nfr, nfi, nbr, nbi

        efr, efi, ebr, ebi = lax.fori_loop(0, n_steps, local_step, (zero, zero, zero, zero))

        cfr, cfi, cbr, cbi = zero, zero, zero, zero
        for _ in range(S5_SEGS - 1):
            tfr = anfr * cfr - anfi * cfi + efr
            tfi = anfr * cfi + anfi * cfr + efi
            cfr = jnp.where(sub == 0, 0.0, pltpu.roll(tfr, 1, 0))
            cfi = jnp.where(sub == 0, 0.0, pltpu.roll(tfi, 1, 0))
            tbr = anbr * cbr - anbi * cbi + ebr
            tbi = anbr * cbi + anbi * cbr + ebi
            cbr = jnp.where(sub == S5_SEGS - 1, 0.0, pltpu.roll(tbr, S5_SEGS - 1, 0))
            cbi = jnp.where(sub == S5_SEGS - 1, 0.0, pltpu.roll(tbi, S5_SEGS - 1, 0))

        def fix_step(j, _):
            rj = seg_rows(j)
            pfr = ptab_ref[0, pl.ds(j, 1), :]
            pfi = ptab_ref[1, pl.ds(j, 1), :]
            pbr = ptab_ref[2, pl.ds(j, 1), :]
            pbi = ptab_ref[3, pl.ds(j, 1), :]
            s_ref[0, rj, :] = s_ref[0, rj, :] + (pfr * cfr - pfi * cfi)
            s_ref[1, rj, :] = s_ref[1, rj, :] + (pfr * cfi + pfi * cfr)
            s_ref[2, rj, :] = s_ref[2, rj, :] + (pbr * cbr - pbi * cbi)
            s_ref[3, rj, :] = s_ref[3, rj, :] + (pbr * cbi + pbi * cbr)
            return 0

        lax.fori_loop(0, n_steps, fix_step, 0)

    s_all = jnp.concatenate([s_ref[k] for k in range(4)], axis=-1)
    y = yi_ref[...] + _dot(s_all.astype(BF16), r2_ref[...])
    y_ref[...] = _gelu_tanh(y).astype(y_ref.dtype)


def _s5_mixer(u_t, r1, r2, coef, ptab, n_batch, n_steps):
    npair, m, _ = u_t.shape
    kern = functools.partial(_s5_kernel, n_batch=n_batch, n_steps=n_steps)
    return pl.pallas_call(
        kern,
        grid=(npair,),
        in_specs=[pl.BlockSpec((None, m, 512), lambda i: (i, 0, 0)),
                  pl.BlockSpec((None, 512, 1024), lambda i: (i, 0, 0)),
                  pl.BlockSpec((None, 512, 512), lambda i: (i, 0, 0)),
                  pl.BlockSpec((None, 8, 128), lambda i: (i, 0, 0)),
                  pl.BlockSpec((None, 4, n_steps, 128), lambda i: (i, 0, 0, 0))],
        out_specs=pl.BlockSpec((None, m, 512), lambda i: (i, 0, 0)),
        out_shape=jax.ShapeDtypeStruct((npair, m, 512), BF16),
        scratch_shapes=[pltpu.VMEM((m, 512), F32), pltpu.VMEM((4, m, 128), F32), pltpu.VMEM((4, m, 128), F32)],
        compiler_params=_cparams(1),
        name="s5_scan",
    )(u_t, r1, r2, coef, ptab)


def _inproj0_kernel(x_ref, w_ref, h_ref, ut_ref, u_s):
    h = _dot(x_ref[...].astype(BF16), w_ref[...])
    h_ref[...] = h[:, S5_WIDTH:].astype(h_ref.dtype)
    n_lane_blk = S5_WIDTH // 128
    for k in range(n_lane_blk):
        u_s[k] = h[:, k * 128:(k + 1) * 128]
    n_chunk = ROW_TILE // S5_CHUNK
    rows = [jnp.concatenate([u_s[k, pl.ds(s, n_chunk, stride=S5_CHUNK), :] for k in range(n_lane_blk)], axis=-1)
            for s in range(S5_CHUNK)]
    for p in range(S5_GROUPS // 2):
        ut_ref[p] = jnp.concatenate([r[:, p * 32:(p + 1) * 32] for r in rows], axis=-1).astype(ut_ref.dtype)


def _inproj0(x, w):
    t, k = x.shape
    n = w.shape[1]
    tm = ROW_TILE
    npair = S5_GROUPS // 2
    return pl.pallas_call(
        _inproj0_kernel,
        grid=(t // tm,),
        in_specs=[pl.BlockSpec((tm, k), lambda i: (i, 0)),
                  pl.BlockSpec((k, n), lambda i: (0, 0))],
        out_specs=[pl.BlockSpec((tm, n - S5_WIDTH), lambda i: (i, 0)),
                   pl.BlockSpec((npair, tm // S5_CHUNK, 512), lambda i: (0, i, 0))],
        out_shape=[jax.ShapeDtypeStruct((t, n - S5_WIDTH), BF16),
                   jax.ShapeDtypeStruct((npair, t // S5_CHUNK, 512), BF16)],
        scratch_shapes=[pltpu.VMEM((S5_WIDTH // 128, tm, 128), F32)],
        compiler_params=_cparams(1),
        name="inproj0",
    )(x, w)


def _gla_kernel(qf_ref, kf_ref, vf_ref, lf_ref, qb_ref, kb_ref, vb_ref, lb_ref, wg_ref, bg_ref,
                of_ref, ob_ref, st_ref):
    @pl.when(pl.program_id(1) == 0)
    def _():
        st_ref[...] = jnp.zeros_like(st_ref)

    n_chunks = GLA_ROWS // GLA_CHUNK
    row = lax.broadcasted_iota(jnp.int32, (GLA_CHUNK, GLA_CHUNK), 0)
    col = lax.broadcasted_iota(jnp.int32, (GLA_CHUNK, GLA_CHUNK), 1)
    tri_f = (col <= row).astype(BF16)
    tri_b = (col >= row).astype(BF16)
    mask_f = col <= row
    mask_b = col > row
    wg = wg_ref[...]
    bg = bg_ref[...]

    def direction(q_ref, k_ref, v_ref, l_ref, o_ref, d):
        z = _dot(l_ref[...], wg[:, d * GLA_KEY:(d + 1) * GLA_KEY]) + bg[:, d * GLA_KEY:(d + 1) * GLA_KEY]
        log_a = jax.nn.log_sigmoid(z) * (1.0 / GLA_TAU)
        order = range(n_chunks) if d == 0 else range(n_chunks - 1, -1, -1)
        tri = tri_f if d == 0 else tri_b
        mask = mask_f if d == 0 else mask_b
        for c in order:
            sl = slice(c * GLA_CHUNK, (c + 1) * GLA_CHUNK)
            la = log_a[sl]
            la_hi = la.astype(BF16)
            la_lo = (la - la_hi.astype(F32)).astype(BF16)
            bc = _dot(tri, la_hi) + _dot(tri, la_lo)
            b_last = bc[GLA_CHUNK - 1:GLA_CHUNK] if d == 0 else bc[0:1]
            q = q_ref[sl, :].astype(F32) * (GLA_DK ** -0.5)
            k = k_ref[sl, :].astype(F32)
            v = v_ref[sl, :]
            qd = (q * jnp.exp(bc)).astype(BF16)
            kd = (k * jnp.exp(-bc)).astype(BF16)
            kc = (k * jnp.exp(b_last - bc)).astype(BF16)
            decay = jnp.exp(b_last)
            outs = []
            for h in range(GLA_HEADS):
                ks = slice(h * GLA_DK, (h + 1) * GLA_DK)
                vs = slice(h * GLA_DV, (h + 1) * GLA_DV)
                s = jnp.where(mask, _dot_nt(qd[:, ks], kd[:, ks]), 0.0).astype(BF16)
                st = st_ref[d, h]
                o = _dot(s, v[:, vs]) + _dot_nt(qd[:, ks], st.astype(BF16))
                st_ref[d, h] = st * decay[:, ks] + _dot_tn(v[:, vs], kc[:, ks])
                outs.append(o)
            o_ref[sl, :] = jnp.concatenate(outs, axis=-1).astype(o_ref.dtype)

    direction(qf_ref, kf_ref, vf_ref, lf_ref, of_ref, 0)
    direction(qb_ref, kb_ref, vb_ref, lb_ref, ob_ref, 1)


def _gla_mixer(h0, wg, bg, n_batch, seq):
    nb = seq // GLA_ROWS
    r = GLA_ROWS
    fwd = lambda b, i: b * nb + i
    bwd = lambda b, i: b * nb + (nb - 1 - i)
    def spec(width, colblk, rowfn):
        return pl.BlockSpec((r, width), lambda b, i: (rowfn(b, i), colblk))
    in_specs = [spec(256, 0, fwd), spec(256, 1, fwd), spec(512, 1, fwd), spec(128, 12, fwd),
                spec(256, 0, bwd), spec(256, 1, bwd), spec(512, 1, bwd), spec(128, 12, bwd),
                pl.BlockSpec((128, 512), lambda b, i: (0, 0)),
                pl.BlockSpec((1, 512), lambda b, i: (0, 0))]
    out_specs = [pl.BlockSpec((r, 512), lambda b, i: (fwd(b, i), 0)),
                 pl.BlockSpec((r, 512), lambda b, i: (bwd(b, i), 0))]
    t = n_batch * seq
    return pl.pallas_call(
        _gla_kernel,
        grid=(n_batch, nb),
        in_specs=in_specs,
        out_specs=out_specs,
        out_shape=[jax.ShapeDtypeStruct((t, 512), BF16), jax.ShapeDtypeStruct((t, 512), BF16)],
        scratch_shapes=[pltpu.VMEM((2, GLA_HEADS, GLA_DV, GLA_DK), F32)],
        compiler_params=_cparams(2),
        name="gla_chunked",
    )(h0, h0, h0, h0, h0, h0, h0, h0, wg, bg)


def _mix0_out_kernel(y_ref, of_ref, ob_ref, go_ref, x_ref, gw_ref, gb_ref, ng_ref, wo_ref, lg_ref, lb_ref,
                     h_ref, h8_ref, ya_s):
    n_chunk = ROW_TILE // S5_CHUNK
    ys = [y_ref[p].astype(F32) for p in range(S5_GROUPS // 2)]
    n_lane_blk = S5_WIDTH // 128
    for t in range(S5_CHUNK):
        for k in range(n_lane_blk):
            ya_s[k, pl.ds(t, n_chunk, stride=S5_CHUNK), :] = jnp.concatenate(
                [y[:, t * 32:(t + 1) * 32] for y in ys[4 * k:4 * k + 4]], axis=-1)
    yaf = jnp.concatenate([ya_s[k] for k in range(n_lane_blk)], axis=-1)
    ya = yaf.astype(BF16)
    gate = _dot(ya, gw_ref[...]) + gb_ref[...]
    ya2 = yaf * jax.nn.sigmoid(gate)
    o = of_ref[...].astype(F32) + ob_ref[...].astype(F32)
    ng = ng_ref[...]
    parts = []
    for h in range(GLA_HEADS):
        oh = o[:, h * GLA_DV:(h + 1) * GLA_DV]
        ms = jnp.mean(oh * oh, axis=-1, keepdims=True)
        parts.append(oh * lax.rsqrt(ms + RMS_EPS) * ng[:, h * GLA_DV:(h + 1) * GLA_DV])
    yb = jnp.concatenate(parts, axis=-1) * jax.nn.silu(go_ref[...].astype(F32))
    wo = wo_ref[...]
    y = _dot(ya2.astype(BF16), wo[:S5_WIDTH]) + _dot(yb.astype(BF16), wo[S5_WIDTH:])
    hn = _layer_norm(ALPHA * x_ref[...] + y, lg_ref[...], lb_ref[...])
    h_ref[...] = hn
    _store_token_tiles(h8_ref, hn)


def _mix0_out(ya, o_f, o_b, h0, x, glu_w, glu_b, norm_g, w_out, ln_g, ln_b):
    t = x.shape[0]
    tm = ROW_TILE
    row = lambda w: pl.BlockSpec((tm, w), lambda i: (i, 0))
    full = lambda a, b: pl.BlockSpec((a, b), lambda i: (0, 0))
    return pl.pallas_call(
        _mix0_out_kernel,
        grid=(t // tm,),
        in_specs=[pl.BlockSpec((S5_GROUPS // 2, tm // S5_CHUNK, 512), lambda i: (0, i, 0)),
                  row(512), row(512),
                  pl.BlockSpec((tm, 512), lambda i: (i, 2)),
                  row(1024), full(512, 512), full(1, 512), full(1, 512), full(1024, 1024),
                  full(1, 1024), full(1, 1024)],
        out_specs=[row(1024), pl.BlockSpec((tm * 8, 128), lambda i: (i, 0))],
        out_shape=[jax.ShapeDtypeStruct((t, D_MODEL), F32), jax.ShapeDtypeStruct((t * 8, 128), F32)],
        scratch_shapes=[pltpu.VMEM((S5_WIDTH // 128, tm, 128), F32)],
        compiler_params=_cparams(1),
        name="mix0_out_ln",
    )(ya, o_f, o_b, h0, x, glu_w, glu_b, norm_g, w_out, ln_g, ln_b)


def _attn_kernel(q_ref, kc_ref, vc_ref, kp_ref, vp_ref, kn_ref, vn_ref, bias_ref, sink_ref, o_ref, *, seq):
    blk = pl.program_id(1)
    n_sub = ATT_ROWS // ATT_BLOCK
    kw = ATT_BLOCK + 2 * WINDOW
    qi = lax.broadcasted_iota(jnp.int32, (GQA * ATT_BLOCK, kw), 0) & (ATT_BLOCK - 1)
    kj = lax.broadcasted_iota(jnp.int32, (GQA * ATT_BLOCK, kw), 1)
    band = jnp.abs(kj - WINDOW - qi) <= WINDOW
    kall = jnp.concatenate([kp_ref[...], kc_ref[...], kn_ref[...]], axis=0)
    vall = jnp.concatenate([vp_ref[...], vc_ref[...], vn_ref[...]], axis=0)
    for s in range(n_sub):
        start = blk * ATT_ROWS + s * ATT_BLOCK
        kpos = start - WINDOW + kj
        valid4 = band & (kpos >= 0) & (kpos < seq)
        kwin = kall[s * ATT_BLOCK:s * ATT_BLOCK + kw]
        vwin = vall[s * ATT_BLOCK:s * ATT_BLOCK + kw]
        q = q_ref[s * ATT_BLOCK:(s + 1) * ATT_BLOCK, :]
        outs = []
        for kv in range(N_KV):
            qs = jnp.concatenate([q[:, (kv * GQA + g) * HEAD_DIM:(kv * GQA + g + 1) * HEAD_DIM]
                                  for g in range(GQA)], axis=0)
            kh = kwin[:, kv * HEAD_DIM:(kv + 1) * HEAD_DIM]
            vh = vwin[:, kv * HEAD_DIM:(kv + 1) * HEAD_DIM]
            sc = _dot_nt(qs, kh) * ATT_SCALE + bias_ref[kv]
            sc = jnp.where(valid4, sc, NEG_INF)
            sink = sink_ref[kv]
            m = jnp.maximum(jnp.max(sc, axis=-1, keepdims=True), sink)
            p = jnp.exp(sc - m)
            den = jnp.sum(p, axis=-1, keepdims=True) + jnp.exp(sink - m)
            o = _dot(p.astype(BF16), vh) / den
            for g in range(GQA):
                outs.append(o[g * ATT_BLOCK:(g + 1) * ATT_BLOCK])
        o_ref[s * ATT_BLOCK:(s + 1) * ATT_BLOCK, :] = jnp.concatenate(outs, axis=-1).astype(o_ref.dtype)


def _t5_bucket(rel):
    nb = REL_BUCKETS // 2
    max_exact = nb // 2
    ret = (rel > 0).astype(jnp.int32) * nb
    n = jnp.abs(rel)
    large = max_exact + (jnp.log(jnp.maximum(n, 1).astype(F32) / max_exact)
                         / math.log(REL_MAX_DIST / max_exact) * (nb - max_exact)).astype(jnp.int32)
    large = jnp.minimum(large, nb - 1)
    return ret + jnp.where(n < max_exact, n, large)


def _attn_mixer(h3, rel_bias, sink, n_batch, seq):
    kw = ATT_BLOCK + 2 * WINDOW
    rel = jnp.arange(kw)[None, :] - WINDOW - jnp.arange(ATT_BLOCK)[:, None]
    bias = rel_bias.astype(F32)[_t5_bucket(rel)]
    bias = jnp.transpose(bias, (2, 0, 1)).reshape(N_KV, GQA * ATT_BLOCK, kw)
    sink_rows = jnp.repeat(sink.astype(F32).reshape(N_KV, GQA), ATT_BLOCK, axis=1)[..., None]
    nblk = seq // ATT_ROWS
    sub = ATT_ROWS // ATT_BLOCK
    n128 = seq // ATT_BLOCK
    cur = lambda b, i: b * nblk + i
    prev = lambda b, i: b * n128 + jnp.maximum(i * sub - 1, 0)
    nxt = lambda b, i: b * n128 + jnp.minimum((i + 1) * sub, n128 - 1)
    t = n_batch * seq
    kern = functools.partial(_attn_kernel, seq=seq)
    return pl.pallas_call(
        kern,
        grid=(n_batch, nblk),
        in_specs=[pl.BlockSpec((ATT_ROWS, ATT_DIM), lambda b, i: (cur(b, i), 0)),
                  pl.BlockSpec((ATT_ROWS, KV_DIM), lambda b, i: (cur(b, i), 4)),
                  pl.BlockSpec((ATT_ROWS, KV_DIM), lambda b, i: (cur(b, i), 5)),
                  pl.BlockSpec((ATT_BLOCK, KV_DIM), lambda b, i: (prev(b, i), 4)),
                  pl.BlockSpec((ATT_BLOCK, KV_DIM), lambda b, i: (prev(b, i), 5)),
                  pl.BlockSpec((ATT_BLOCK, KV_DIM), lambda b, i: (nxt(b, i), 4)),
                  pl.BlockSpec((ATT_BLOCK, KV_DIM), lambda b, i: (nxt(b, i), 5)),
                  pl.BlockSpec((N_KV, GQA * ATT_BLOCK, kw), lambda b, i: (0, 0, 0)),
                  pl.BlockSpec((N_KV, GQA * ATT_BLOCK, 1), lambda b, i: (0, 0, 0))],
        out_specs=pl.BlockSpec((ATT_ROWS, ATT_DIM), lambda b, i: (cur(b, i), 0)),
        out_shape=jax.ShapeDtypeStruct((t, ATT_DIM), BF16),
        compiler_params=_cparams(2),
        name="window_gqa",
    )(h3, h3, h3, h3, h3, h3, h3, bias, sink_rows)


def _proj_ln_kernel(a_ref, w_ref, x_ref, g_ref, b_ref, h_ref, h8_ref):
    y = _dot(a_ref[...], w_ref[...])
    hn = _layer_norm(ALPHA * x_ref[...] + y, g_ref[...], b_ref[...])
    h_ref[...] = hn
    _store_token_tiles(h8_ref, hn)


def _proj_ln(a, w, x, g, b):
    t, k = a.shape
    tm = ROW_TILE
    return pl.pallas_call(
        _proj_ln_kernel,
        grid=(t // tm,),
        in_specs=[pl.BlockSpec((tm, k), lambda i: (i, 0)),
                  pl.BlockSpec((k, D_MODEL), lambda i: (0, 0)),
                  pl.BlockSpec((tm, D_MODEL), lambda i: (i, 0)),
                  pl.BlockSpec((1, D_MODEL), lambda i: (0, 0)),
                  pl.BlockSpec((1, D_MODEL), lambda i: (0, 0))],
        out_specs=[pl.BlockSpec((tm, D_MODEL), lambda i: (i, 0)), pl.BlockSpec((tm * 8, 128), lambda i: (i, 0))],
        out_shape=[jax.ShapeDtypeStruct((t, D_MODEL), F32), jax.ShapeDtypeStruct((t * 8, 128), F32)],
        compiler_params=_cparams(1),
        name="proj_ln",
    )(a, w, x, g, b)


def _router_kernel(h_ref, wh_ref, wl_ref, b_ref, o_ref, cnt_ref, run_ref):
    @pl.when(pl.program_id(0) == 0)
    def _():
        run_ref[...] = jnp.zeros_like(run_ref)

    x = h_ref[...]
    xh = x.astype(BF16)
    xl = (x - xh.astype(F32)).astype(BF16)
    wh = wh_ref[...]
    logits = _dot(xh, wh) + (_dot(xl, wh) + _dot(xh, wl_ref[...])) + b_ref[...]
    lane = lax.broadcasted_iota(jnp.int32, logits.shape, 1)
    big = jnp.int32(1 << 20)
    neg = jnp.float32(-jnp.inf)
    is_g = lane < N_GROUPS
    gl = jnp.where(is_g, logits, neg)
    gmax = jnp.max(gl, axis=-1, keepdims=True)
    gsum = jnp.sum(jnp.where(is_g, jnp.exp(gl - gmax), 0.0), axis=-1, keepdims=True)
    grp = jnp.min(jnp.where(is_g & (gl == gmax), lane, big), axis=-1, keepdims=True)
    p_grp = 1.0 / gsum
    lo = N_GROUPS + grp * EPG
    in_grp = (lane >= lo) & (lane < lo + EPG)
    el = jnp.where(in_grp, logits, neg)
    m1 = jnp.max(el, axis=-1, keepdims=True)
    i1 = jnp.min(jnp.where(in_grp & (el == m1), lane, big), axis=-1, keepdims=True)
    el2 = jnp.where(lane == i1, neg, el)
    m2 = jnp.max(el2, axis=-1, keepdims=True)
    i2 = jnp.min(jnp.where(in_grp & (lane != i1) & (el2 == m2), lane, big), axis=-1, keepdims=True)
    e2 = jnp.exp(m2 - m1)
    g1 = p_grp / (1.0 + e2)
    g2 = p_grp * e2 / (1.0 + e2)
    hit1 = lane == i1
    hit2 = lane == i2
    onehot = (hit1 | hit2).astype(BF16)
    tm = onehot.shape[0]
    lower = (lax.broadcasted_iota(jnp.int32, (tm, tm), 1) < lax.broadcasted_iota(jnp.int32, (tm, tm), 0)).astype(BF16)
    before = _dot(lower, onehot) + run_ref[...]
    r1 = jnp.sum(jnp.where(hit1, before, 0.0), axis=-1, keepdims=True)
    r2 = jnp.sum(jnp.where(hit2, before, 0.0), axis=-1, keepdims=True)
    run_ref[...] += jnp.sum(onehot.astype(F32), axis=0, keepdims=True)
    cnt_ref[...] = run_ref[...]
    vals = ((i1 - N_GROUPS).astype(F32), (i2 - N_GROUPS).astype(F32), g1, g2, r1, r2)
    out = jnp.zeros_like(logits)
    for k, v in enumerate(vals):
        out = jnp.where(lane == k, v, out)
    o_ref[...] = out


def _router(h, w_group, b_group, w_router, b_router):
    t = h.shape[0]
    w = jnp.concatenate([w_group.astype(F32), w_router.astype(F32)], axis=1)
    w = jnp.pad(w, ((0, 0), (0, 128 - w.shape[1])))
    wh = w.astype(BF16)
    wl = (w - wh.astype(F32)).astype(BF16)
    bias = jnp.pad(jnp.concatenate([b_group.astype(F32), b_router.astype(F32)]), (0, 128 - 36))[None]
    tm = ROW_TILE
    out, cnt = pl.pallas_call(
        _router_kernel,
        grid=(t // tm,),
        in_specs=[pl.BlockSpec((tm, D_MODEL), lambda i: (i, 0)),
                  pl.BlockSpec((D_MODEL, 128), lambda i: (0, 0)),
                  pl.BlockSpec((D_MODEL, 128), lambda i: (0, 0)),
                  pl.BlockSpec((1, 128), lambda i: (0, 0))],
        out_specs=[pl.BlockSpec((tm, 128), lambda i: (i, 0)),
                   pl.BlockSpec((1, 128), lambda i: (0, 0))],
        out_shape=[jax.ShapeDtypeStruct((t, 128), F32), jax.ShapeDtypeStruct((1, 128), F32)],
        scratch_shapes=[pltpu.VMEM((1, 128), F32)],
        compiler_params=_cparams(1),
        name="moe_router",
    )(h, wh, wl, bias)
    eid = out[:, 0:2].astype(jnp.int32)
    gate = out[:, 2:4]
    rank = out[:, 4:6].astype(jnp.int32)
    counts = cnt[0, N_GROUPS:N_GROUPS + N_EXPERTS].astype(jnp.int32)
    return eid, gate, rank, counts


def _token_tile(ref, tok):
    return ref.at[pl.ds(pl.multiple_of(tok * 8, 8), 8)]


def _experts_kernel(be_ref, nu_ref, src_hbm, h8_hbm, w1_ref, w3_ref, w2_ref, ys_ref,
                    w13_s, w2_s, xbuf, idx_smem, sem, isem):
    i = pl.program_id(0)
    n_used = nu_ref[0]
    used = i < n_used
    slot = i % 2

    def gather_block(blk, slot_):
        cp = pltpu.make_async_copy(src_hbm.at[pl.ds(blk * IDX_PAD, IDX_PAD)], idx_smem, isem)
        cp.start()
        cp.wait()
        def issue(r8, _):
            for u in range(8):
                r = r8 * 8 + u
                pltpu.make_async_copy(_token_tile(h8_hbm, idx_smem[r]), _token_tile(xbuf.at[slot_], r),
                                      sem.at[slot_]).start(priority=u % 2)
            return 0
        lax.fori_loop(0, MOE_ROWS // 8, issue, 0)

    @pl.when(i == 0)
    def _():
        gather_block(0, 0)

    @pl.when(used)
    def _():
        pltpu.make_async_copy(h8_hbm.at[pl.ds(0, MOE_ROWS * 8)], xbuf.at[slot], sem.at[slot]).wait()

    @pl.when(i + 1 < n_used)
    def _():
        gather_block(i + 1, 1 - slot)

    prev = be_ref[jnp.maximum(i - 1, 0)]
    fresh = (i == 0) | (be_ref[i] != prev)

    @pl.when(used & fresh)
    def _():
        w13_s[:, :D_EXPERT] = w1_ref[...].astype(BF16)
        w13_s[:, D_EXPERT:] = w3_ref[...].astype(BF16)
        w2_s[...] = w2_ref[...].astype(BF16)

    @pl.when(used)
    def _():
        x = _load_token_tiles(xbuf.at[slot], MOE_ROWS).astype(BF16)
        h = _dot(x, w13_s[...])
        h1 = h[:, :D_EXPERT]
        hdn = (h1 * jax.nn.sigmoid(h1)) * h[:, D_EXPERT:]
        _store_token_tiles(ys_ref, _dot(hdn.astype(BF16), w2_s[...]))

    @pl.when(jnp.logical_not(used))
    def _():
        ys_ref[...] = jnp.zeros_like(ys_ref)


def _experts(h8, src, blk_e, n_used, w1, w3, w2, cap):
    nblk = cap // MOE_ROWS
    return pl.pallas_call(
        _experts_kernel,
        grid_spec=pltpu.PrefetchScalarGridSpec(
            num_scalar_prefetch=2,
            grid=(nblk,),
            in_specs=[pl.BlockSpec(memory_space=pl.ANY),
                      pl.BlockSpec(memory_space=pl.ANY),
                      pl.BlockSpec((None, D_MODEL, D_EXPERT), lambda i, be, nu: (be[i], 0, 0)),
                      pl.BlockSpec((None, D_MODEL, D_EXPERT), lambda i, be, nu: (be[i], 0, 0)),
                      pl.BlockSpec((None, D_EXPERT, D_MODEL), lambda i, be, nu: (be[i], 0, 0))],
            out_specs=pl.BlockSpec((MOE_ROWS * 8, 128), lambda i, be, nu: (i, 0)),
            scratch_shapes=[pltpu.VMEM((D_MODEL, 2 * D_EXPERT), BF16),
                            pltpu.VMEM((D_EXPERT, D_MODEL), BF16),
                            pltpu.VMEM((2, MOE_ROWS * 8, 128), F32),
                            pltpu.SMEM((IDX_PAD,), jnp.int32),
                            pltpu.SemaphoreType.DMA((2,)),
                            pltpu.SemaphoreType.DMA(())]),
        out_shape=jax.ShapeDtypeStruct((cap * 8, 128), F32),
        compiler_params=_cparams(1),
        name="moe_experts",
    )(blk_e, n_used, src, h8, w1, w3, w2)


def _combine_kernel(dst_hbm, ys_hbm, h_ref, gate_ref, g_ref, b_ref, o_ref, idx_smem, y0_ref, y1_ref, sem, isem):
    i = pl.program_id(0)
    tm = ROW_TILE
    cp = pltpu.make_async_copy(dst_hbm.at[pl.ds(i * 2 * tm, 2 * tm)], idx_smem, isem)
    cp.start()
    cp.wait()

    def issue(t, _):
        pltpu.make_async_copy(_token_tile(ys_hbm, idx_smem[2 * t]), _token_tile(y0_ref, t), sem).start(priority=0)
        pltpu.make_async_copy(_token_tile(ys_hbm, idx_smem[2 * t + 1]), _token_tile(y1_ref, t), sem).start(priority=1)
        return 0
    lax.fori_loop(0, tm, issue, 0, unroll=8)
    pltpu.make_async_copy(ys_hbm.at[pl.ds(0, tm * 8)], y0_ref, sem).wait()
    pltpu.make_async_copy(ys_hbm.at[pl.ds(0, tm * 8)], y1_ref, sem).wait()
    gate = gate_ref[...]
    y = _load_token_tiles(y0_ref, tm) * gate[:, 0:1] + _load_token_tiles(y1_ref, tm) * gate[:, 1:2]
    o_ref[...] = _layer_norm(ALPHA * h_ref[...] + y, g_ref[...], b_ref[...])


def _combine_ln(dst, ys, h, gate, ln_g, ln_b):
    t = h.shape[0]
    tm = ROW_TILE
    return pl.pallas_call(
        _combine_kernel,
        grid_spec=pltpu.PrefetchScalarGridSpec(
            num_scalar_prefetch=0,
            grid=(t // tm,),
            in_specs=[pl.BlockSpec(memory_space=pl.ANY),
                      pl.BlockSpec(memory_space=pl.ANY),
                      pl.BlockSpec((tm, D_MODEL), lambda i: (i, 0)),
                      pl.BlockSpec((tm, 2), lambda i: (i, 0)),
                      pl.BlockSpec((1, D_MODEL), lambda i: (0, 0)),
                      pl.BlockSpec((1, D_MODEL), lambda i: (0, 0))],
            out_specs=pl.BlockSpec((tm, D_MODEL), lambda i: (i, 0)),
            scratch_shapes=[pltpu.SMEM((2 * tm,), jnp.int32),
                            pltpu.VMEM((tm * 8, 128), F32),
                            pltpu.VMEM((tm * 8, 128), F32),
                            pltpu.SemaphoreType.DMA(()),
                            pltpu.SemaphoreType.DMA(())]),
        out_shape=jax.ShapeDtypeStruct((t, D_MODEL), F32),
        compiler_params=_cparams(1),
        name="moe_combine_ln",
    )(dst, ys, h, gate, ln_g, ln_b)


def _moe_layer(h, h8, w_group, b_group, w_router, b_router, w1, w3, w2, ln_g, ln_b):
    t = h.shape[0]
    eid, gate, rank, counts = _router(h, w_group, b_group, w_router, b_router)
    padded = (counts + MOE_ROWS - 1) // MOE_ROWS * MOE_ROWS
    pad_end = jnp.cumsum(padded)
    pad_start = pad_end - padded
    dst = (pad_start[eid] + rank).reshape(-1).astype(jnp.int32)
    cap = 2 * t + N_EXPERTS * MOE_ROWS
    nblk = cap // MOE_ROWS
    src = jnp.zeros((cap,), jnp.int32).at[dst].set(jnp.arange(2 * t, dtype=jnp.int32) // 2, unique_indices=True)
    src = jnp.pad(src.reshape(nblk, MOE_ROWS), ((0, 0), (0, IDX_PAD - MOE_ROWS))).reshape(-1)
    blk_e = jnp.minimum(jnp.searchsorted(pad_end, jnp.arange(nblk) * MOE_ROWS, side='right'),
                        N_EXPERTS - 1).astype(jnp.int32)
    n_used = (pad_end[-1] // MOE_ROWS).astype(jnp.int32).reshape(1)
    ys = _experts(h8, src, blk_e, n_used, w1, w3, w2, cap)
    return _combine_ln(dst, ys, h, gate, ln_g, ln_b)


def kernel(x, ln_mix_g, ln_mix_b, ln_ffn_g, ln_ffn_b, ab_w_in, s5_lam_re, s5_lam_im, s5_log_dt, s5_b_re, s5_b_im,
           s5_c_re, s5_c_im, s5_d, s5_glu_w, s5_glu_b, gla_gate_w, gla_gate_b, gla_norm_g, ab_w_out, c_w_in,
           c_sink, c_w_out, rel_bias, moe_w_group, moe_b_group, moe_w_router, moe_b_router, moe_w1, moe_w3, moe_w2):
    n_batch, seq, _ = x.shape
    t = n_batch * seq
    xt = x.reshape(t, D_MODEL)
    row = lambda v: v.astype(F32).reshape(1, -1)

    w_in0 = jnp.pad(ab_w_in[0], ((0, 0), (0, AB_IN_PAD - AB_IN))).astype(BF16)
    h0, u_t = _inproj0(xt, w_in0)
    n_steps = seq // (S5_SEGS * S5_CHUNK)
    r1, r2, coef, ptab = _s5_prep(s5_lam_re[0], s5_lam_im[0], s5_log_dt[0], s5_b_re[0], s5_b_im[0],
                                  s5_c_re[0], s5_c_im[0], s5_d[0], n_steps)
    ya = _s5_mixer(u_t, r1, r2, coef, ptab, n_batch, n_steps)
    gw = gla_gate_w[0].astype(F32)
    wg = jnp.zeros((128, 2 * GLA_KEY), F32)
    wg = wg.at[0:GLA_RANK, 0:GLA_KEY].set(gw[0]).at[GLA_RANK:2 * GLA_RANK, GLA_KEY:].set(gw[1]).astype(BF16)
    bg = gla_gate_b[0].astype(F32).reshape(1, 2 * GLA_KEY)
    o_f, o_b = _gla_mixer(h0, wg, bg, n_batch, seq)
    h, h8 = _mix0_out(ya, o_f, o_b, h0, xt, s5_glu_w[0].astype(BF16), row(s5_glu_b[0]), row(gla_norm_g[0]),
                      ab_w_out[0].astype(BF16), row(ln_mix_g[0]), row(ln_mix_b[0]))
    h = _moe_layer(h, h8, moe_w_group[0], moe_b_group[0], moe_w_router[0], moe_b_router[0],
                   moe_w1[0], moe_w3[0], moe_w2[0], row(ln_ffn_g[0]), row(ln_ffn_b[0]))

    h3 = _matmul(h, c_w_in[0].astype(BF16), BF16)
    o = _attn_mixer(h3, rel_bias, c_sink[0], n_batch, seq)
    h, h8 = _proj_ln(o, c_w_out[0].astype(BF16), h, row(ln_mix_g[1]), row(ln_mix_b[1]))
    h = _moe_layer(h, h8, moe_w_group[1], moe_b_group[1], moe_w_router[1], moe_b_router[1],
                   moe_w1[1], moe_w3[1], moe_w2[1], row(ln_ffn_g[1]), row(ln_ffn_b[1]))
    return h.reshape(n_batch, seq, D_MODEL)
```

```python
import functools
import math

import jax
import jax.numpy as jnp
from jax import lax
from jax.experimental import pallas as pl
from jax.experimental.pallas import tpu as pltpu

F32 = jnp.float32
BF16 = jnp.bfloat16

D_MODEL = 1024
S5_WIDTH = 512
S5_GROUP_CH = 16
S5_GROUPS = 32
S5_STATE = 64
GLA_HEADS = 4
GLA_DV = 128
GLA_DK = 64
GLA_KEY = 256
GLA_WIDTH = 512
GLA_RANK = 16
GLA_TAU = 16.0
GLA_CHUNK = 64
AB_IN = 2080
AB_IN_PAD = 2176
HEAD_DIM = 64
N_HEADS = 16
N_KV = 4
GQA = 4
ATT_DIM = 1024
KV_DIM = 256
WINDOW = 128
ATT_BLOCK = 128
ATT_SCALE = HEAD_DIM ** -0.5
REL_BUCKETS = 32
REL_MAX_DIST = 128
NEG_INF = -1e30
N_GROUPS = 4
EPG = 8
N_EXPERTS = 32
D_EXPERT = 512
LN_EPS = 1e-5
RMS_EPS = 1e-6
DEPTH = 2
ALPHA = (2 * DEPTH) ** 0.25

S5_CHUNK = 16
S5_SEGS = 8
ROW_TILE = 512
GLA_ROWS = 256
ATT_ROWS = 512
MOE_ROWS = 256
IDX_FETCH = 1024
VMEM_LIMIT = 56 * 1024 * 1024


def _cparams(n_axes):
    return pltpu.CompilerParams(dimension_semantics=("arbitrary",) * n_axes,
                                vmem_limit_bytes=VMEM_LIMIT)


def _dot(a, b):
    return jnp.dot(a, b, preferred_element_type=F32)


def _dot_nt(a, b):
    return lax.dot_general(a, b, (((1,), (1,)), ((), ())), preferred_element_type=F32)


def _dot_tn(a, b):
    return lax.dot_general(a, b, (((0,), (0,)), ((), ())), preferred_element_type=F32)


def _layer_norm(r, g, b):
    mu = jnp.mean(r, axis=-1, keepdims=True)
    c = r - mu
    var = jnp.mean(c * c, axis=-1, keepdims=True)
    return c * lax.rsqrt(var + LN_EPS) * g + b


def _store_token_tiles(ref, val):
    n = val.shape[0]
    for k in range(D_MODEL // 128):
        ref[pl.ds(k, n, stride=8), :] = val[:, k * 128:(k + 1) * 128]


def _load_token_tiles(ref, n):
    return jnp.concatenate([ref[pl.ds(k, n, stride=8), :] for k in range(D_MODEL // 128)], axis=-1)


def _mm_kernel(x_ref, w_ref, o_ref):
    o_ref[...] = _dot(x_ref[...].astype(BF16), w_ref[...]).astype(o_ref.dtype)


def _matmul(x, w, out_dtype):
    m, k = x.shape
    n = w.shape[1]
    return pl.pallas_call(
        _mm_kernel,
        grid=(m // ROW_TILE,),
        in_specs=[pl.BlockSpec((ROW_TILE, k), lambda i: (i, 0)),
                  pl.BlockSpec((k, n), lambda i: (0, 0))],
        out_specs=pl.BlockSpec((ROW_TILE, n), lambda i: (i, 0)),
        out_shape=jax.ShapeDtypeStruct((m, n), out_dtype),
        compiler_params=_cparams(1),
        name="dense_matmul",
    )(x, w)


def _s5_prep(lam_re, lam_im, log_dt, b_re, b_im, c_re, c_im, d, n_steps):
    tc = S5_CHUNK
    g, p, c = S5_GROUPS, S5_STATE, S5_GROUP_CH
    lr = jnp.minimum(lam_re.astype(F32), -1e-4)
    li = lam_im.astype(F32)
    dt = jnp.exp(log_dt.astype(F32))[..., None]
    mag = jnp.exp(lr * dt)
    ar = mag * jnp.cos(li * dt)
    ai = mag * jnp.sin(li * dt)
    den = lr * lr + li * li
    nr = ar - 1.0
    coef_r = (nr * lr + ai * li) / den
    coef_i = (ai * lr - nr * li) / den
    br_ = b_re.astype(F32)
    bi_ = b_im.astype(F32)
    bbr = coef_r[..., None] * br_ - coef_i[..., None] * bi_
    bbi = coef_r[..., None] * bi_ + coef_i[..., None] * br_
    cr = c_re.astype(F32)
    ci = c_im.astype(F32)

    def apow(n):
        nn = n.astype(F32)[:, None, None, None]
        m_ = jnp.exp(nn * (lr * dt)[None])
        ang = nn * (li * dt)[None]
        return m_ * jnp.cos(ang), m_ * jnp.sin(ang)

    lags = jnp.arange(tc + 1)
    pr, pi = apow(lags)
    hr = pr[..., None] * bbr[None] - pi[..., None] * bbi[None]
    hi = pr[..., None] * bbi[None] + pi[..., None] * bbr[None]
    kk = (jnp.einsum('dgop,jdgpc->jdgco', cr, hr) - jnp.einsum('dgop,jdgpc->jdgco', ci, hi))
    npair = g // 2
    eye2 = jnp.eye(2, dtype=F32)
    s_idx = jnp.arange(tc)[:, None]
    t_idx = jnp.arange(tc)[None, :]
    causal = (t_idx >= s_idx)[..., None, None, None]
    anti = (s_idx >= t_idx)[..., None, None, None]
    kt = (jnp.where(causal, kk[:, 0][jnp.clip(t_idx - s_idx, 0, tc)], 0.0)
          + jnp.where(anti, kk[:, 1][jnp.clip(s_idx - t_idx, 0, tc)], 0.0))
    dd = d.astype(F32).reshape(g, c)
    kt = kt + ((s_idx == t_idx).astype(F32)[:, :, None, None, None]
               * (dd[None, None, :, :, None] * jnp.eye(c, dtype=F32)[None, None, None]))
    kt = jnp.transpose(kt.reshape(tc, tc, npair, 2, c, c), (2, 0, 3, 4, 1, 5))
    kt = (kt[:, :, :, :, :, None, :] * eye2[None, None, :, None, None, :, None]).reshape(npair, 2 * tc * c, 2 * tc * c)

    def w_pair(h_, s_sel, dirn):
        w = h_[s_sel, dirn].reshape(tc, npair, 2, p, c)
        w = jnp.transpose(w, (1, 0, 2, 4, 3))
        return (w[:, :, :, :, None, :] * eye2[None, None, :, None, :, None]).reshape(npair, 2 * tc * c, 2 * p)
    s_fwd = tc - 1 - jnp.arange(tc)
    s_bwd = jnp.arange(tc)
    r1 = jnp.concatenate([kt, w_pair(hr, s_fwd, 0), w_pair(hi, s_fwd, 0), w_pair(hr, s_bwd, 1), w_pair(hi, s_bwd, 1)],
                         axis=2)

    def readout(e, dirn):
        zr = cr[dirn][None] * pr[e, dirn][:, :, None, :] - ci[dirn][None] * pi[e, dirn][:, :, None, :]
        zi = cr[dirn][None] * pi[e, dirn][:, :, None, :] + ci[dirn][None] * pr[e, dirn][:, :, None, :]
        def pair(z):
            z = jnp.transpose(z.reshape(tc, npair, 2, c, p), (1, 2, 4, 0, 3))
            return (z[:, :, :, :, None, :] * eye2[None, :, None, None, :, None]).reshape(npair, 2 * p, 2 * tc * c)
        return pair(zr), pair(-zi)
    vf_r, vf_i = readout(jnp.arange(tc) + 1, 0)
    vb_r, vb_i = readout(tc - jnp.arange(tc), 1)
    r2 = jnp.concatenate([vf_r, vf_i, vb_r, vb_i], axis=1)

    steps = jnp.arange(n_steps)
    a16r, a16i = apow(jnp.array([tc]))
    apr, api = apow(steps * tc)
    anr, ani = apow(jnp.array([tc * n_steps]))
    def lanes(x):
        n_ = x.shape[0]
        return jnp.transpose(x.reshape(n_, npair, 2 * p), (1, 0, 2))
    coef = jnp.concatenate([lanes(a16r[:, 0]), lanes(a16i[:, 0]), lanes(a16r[:, 1]), lanes(a16i[:, 1]),
                            lanes(anr[:, 0]), lanes(ani[:, 0]), lanes(anr[:, 1]), lanes(ani[:, 1])], axis=1)
    rev = n_steps - 1 - steps
    ptab = jnp.stack([lanes(apr[:, 0]), lanes(api[:, 0]), lanes(apr[rev, 1]), lanes(api[rev, 1])], axis=1)
    return r1.astype(BF16), r2.astype(BF16), coef, ptab


def _gelu_tanh(x):
    return 0.5 * x * (1.0 + jnp.tanh(math.sqrt(2.0 / math.pi) * (x + 0.044715 * (x * x * x))))


def _s5_kernel(u_ref, r1_ref, r2_ref, coef_ref, ptab_ref, y_ref, yi_ref, e_ref, s_ref, *, n_batch, n_steps):
    rows_b = n_steps * S5_SEGS
    r = _dot(u_ref[...], r1_ref[...])
    yi_ref[...] = r[:, :512]
    for k in range(4):
        e_ref[k] = r[:, 512 + k * 128:512 + (k + 1) * 128]
    coef = coef_ref[...]
    a16fr, a16fi, a16br, a16bi = coef[0:1], coef[1:2], coef[2:3], coef[3:4]
    anfr, anfi, anbr, anbi = coef[4:5], coef[5:6], coef[6:7], coef[7:8]
    zero = jnp.zeros((S5_SEGS, 128), F32)
    sub = lax.broadcasted_iota(jnp.int32, (S5_SEGS, 128), 0)

    def seg_rows(b, j):
        return pl.ds(b * rows_b + j, S5_SEGS, stride=n_steps)

    def local_step(j, carry):
        out = []
        for b in range(n_batch):
            sfr, sfi, sbr, sbi = carry[b]
            rf = seg_rows(b, j)
            rb = seg_rows(b, n_steps - 1 - j)
            s_ref[0, rf, :] = sfr
            s_ref[1, rf, :] = sfi
            s_ref[2, rb, :] = sbr
            s_ref[3, rb, :] = sbi
            efr = e_ref[0, rf, :]
            efi = e_ref[1, rf, :]
            ebr = e_ref[2, rb, :]
            ebi = e_ref[3, rb, :]
            out.append((a16fr * sfr - a16fi * sfi + efr, a16fr * sfi + a16fi * sfr + efi,
                        a16br * sbr - a16bi * sbi + ebr, a16br * sbi + a16bi * sbr + ebi))
        return tuple(out)

    ends = lax.fori_loop(0, n_steps, local_step, tuple((zero, zero, zero, zero) for _ in range(n_batch)))

    carries = []
    for b in range(n_batch):
        efr, efi, ebr, ebi = ends[b]
        cfr, cfi, cbr, cbi = zero, zero, zero, zero
        for _ in range(S5_SEGS - 1):
            tfr = anfr * cfr - anfi * cfi + efr
            tfi = anfr * cfi + anfi * cfr + efi
            cfr = jnp.where(sub == 0, 0.0, pltpu.roll(tfr, 1, 0))
            cfi = jnp.where(sub == 0, 0.0, pltpu.roll(tfi, 1, 0))
            tbr = anbr * cbr - anbi * cbi + ebr
            tbi = anbr * cbi + anbi * cbr + ebi
            cbr = jnp.where(sub == S5_SEGS - 1, 0.0, pltpu.roll(tbr, S5_SEGS - 1, 0))
            cbi = jnp.where(sub == S5_SEGS - 1, 0.0, pltpu.roll(tbi, S5_SEGS - 1, 0))
        carries.append((cfr, cfi, cbr, cbi))

    def fix_step(j, _):
        pfr = ptab_ref[0, pl.ds(j, 1), :]
        pfi = ptab_ref[1, pl.ds(j, 1), :]
        pbr = ptab_ref[2, pl.ds(j, 1), :]
        pbi = ptab_ref[3, pl.ds(j, 1), :]
        for b in range(n_batch):
            cfr, cfi, cbr, cbi = carries[b]
            rj = seg_rows(b, j)
            s_ref[0, rj, :] = s_ref[0, rj, :] + (pfr * cfr - pfi * cfi)
            s_ref[1, rj, :] = s_ref[1, rj, :] + (pfr * cfi + pfi * cfr)
            s_ref[2, rj, :] = s_ref[2, rj, :] + (pbr * cbr - pbi * cbi)
            s_ref[3, rj, :] = s_ref[3, rj, :] + (pbr * cbi + pbi * cbr)
        return 0

    lax.fori_loop(0, n_steps, fix_step, 0)

    s_all = jnp.concatenate([s_ref[k] for k in range(4)], axis=-1)
    y = yi_ref[...] + _dot(s_all.astype(BF16), r2_ref[...])
    y_ref[...] = _gelu_tanh(y).astype(y_ref.dtype)


def _s5_mixer(u_t, r1, r2, coef, ptab, n_batch, n_steps):
    npair, m, _ = u_t.shape
    kern = functools.partial(_s5_kernel, n_batch=n_batch, n_steps=n_steps)
    return pl.pallas_call(
        kern,
        grid=(npair,),
        in_specs=[pl.BlockSpec((None, m, 512), lambda i: (i, 0, 0)),
                  pl.BlockSpec((None, 512, 1024), lambda i: (i, 0, 0)),
                  pl.BlockSpec((None, 512, 512), lambda i: (i, 0, 0)),
                  pl.BlockSpec((None, 8, 128), lambda i: (i, 0, 0)),
                  pl.BlockSpec((None, 4, n_steps, 128), lambda i: (i, 0, 0, 0))],
        out_specs=pl.BlockSpec((None, m, 512), lambda i: (i, 0, 0)),
        out_shape=jax.ShapeDtypeStruct((npair, m, 512), BF16),
        scratch_shapes=[pltpu.VMEM((m, 512), F32), pltpu.VMEM((4, m, 128), F32), pltpu.VMEM((4, m, 128), F32)],
        compiler_params=_cparams(1),
        name="s5_scan",
    )(u_t, r1, r2, coef, ptab)


def _inproj0_kernel(x_ref, w_ref, h_ref, ut_ref, u_s):
    h = _dot(x_ref[...].astype(BF16), w_ref[...])
    h_ref[...] = h[:, S5_WIDTH:].astype(h_ref.dtype)
    n_lane_blk = S5_WIDTH // 128
    for k in range(n_lane_blk):
        u_s[k] = h[:, k * 128:(k + 1) * 128]
    n_chunk = ROW_TILE // S5_CHUNK
    rows = [jnp.concatenate([u_s[k, pl.ds(s, n_chunk, stride=S5_CHUNK), :] for k in range(n_lane_blk)], axis=-1)
            for s in range(S5_CHUNK)]
    for p in range(S5_GROUPS // 2):
        ut_ref[p] = jnp.concatenate([r[:, p * 32:(p + 1) * 32] for r in rows], axis=-1).astype(ut_ref.dtype)


def _inproj0(x, w):
    t, k = x.shape
    n = w.shape[1]
    tm = ROW_TILE
    npair = S5_GROUPS // 2
    return pl.pallas_call(
        _inproj0_kernel,
        grid=(t // tm,),
        in_specs=[pl.BlockSpec((tm, k), lambda i: (i, 0)),
                  pl.BlockSpec((k, n), lambda i: (0, 0))],
        out_specs=[pl.BlockSpec((tm, n - S5_WIDTH), lambda i: (i, 0)),
                   pl.BlockSpec((npair, tm // S5_CHUNK, 512), lambda i: (0, i, 0))],
        out_shape=[jax.ShapeDtypeStruct((t, n - S5_WIDTH), BF16),
                   jax.ShapeDtypeStruct((npair, t // S5_CHUNK, 512), BF16)],
        scratch_shapes=[pltpu.VMEM((S5_WIDTH // 128, tm, 128), F32)],
        compiler_params=_cparams(1),
        name="inproj0",
    )(x, w)


def _gla_kernel(qf_ref, kf_ref, vf_ref, lf_ref, qb_ref, kb_ref, vb_ref, lb_ref, wg_ref, bg_ref,
                of_ref, ob_ref, st_ref):
    @pl.when(pl.program_id(1) == 0)
    def _():
        st_ref[...] = jnp.zeros_like(st_ref)

    n_chunks = GLA_ROWS // GLA_CHUNK
    row = lax.broadcasted_iota(jnp.int32, (GLA_CHUNK, GLA_CHUNK), 0)
    col = lax.broadcasted_iota(jnp.int32, (GLA_CHUNK, GLA_CHUNK), 1)
    mask_f = col <= row
    mask_b = col > row
    brow = lax.broadcasted_iota(jnp.int32, (GLA_ROWS, GLA_ROWS), 0)
    bcol = lax.broadcasted_iota(jnp.int32, (GLA_ROWS, GLA_ROWS), 1)
    shift = GLA_CHUNK.bit_length() - 1
    same = (brow >> shift) == (bcol >> shift)
    ones_blk = same.astype(BF16)
    tri_f = (same & (bcol <= brow)).astype(BF16)
    tri_b = (same & (bcol >= brow)).astype(BF16)
    wg = wg_ref[...]
    bg = bg_ref[...]

    def direction(q_ref, k_ref, v_ref, l_ref, o_ref, d):
        z = _dot(l_ref[...], wg[:, d * GLA_KEY:(d + 1) * GLA_KEY]) + bg[:, d * GLA_KEY:(d + 1) * GLA_KEY]
        log_a = jax.nn.log_sigmoid(z) * (1.0 / GLA_TAU)
        la_hi = log_a.astype(BF16)
        la_lo = (log_a - la_hi.astype(F32)).astype(BF16)
        tri = tri_f if d == 0 else tri_b
        bc = _dot(tri, la_hi) + _dot(tri, la_lo)
        tot = _dot(ones_blk, la_hi) + _dot(ones_blk, la_lo)
        q = q_ref[...].astype(F32) * (GLA_DK ** -0.5)
        k = k_ref[...].astype(F32)
        qd_all = (q * jnp.exp(bc)).astype(BF16)
        kd_all = (k * jnp.exp(-bc)).astype(BF16)
        kc_all = (k * jnp.exp(tot - bc)).astype(BF16)
        decay_all = jnp.exp(tot)
        order = range(n_chunks) if d == 0 else range(n_chunks - 1, -1, -1)
        mask = mask_f if d == 0 else mask_b
        for c in order:
            sl = slice(c * GLA_CHUNK, (c + 1) * GLA_CHUNK)
            qd, kd, kc = qd_all[sl], kd_all[sl], kc_all[sl]
            decay = decay_all[c * GLA_CHUNK:c * GLA_CHUNK + 1]
            v = v_ref[sl, :]
            outs = []
            for h in range(GLA_HEADS):
                ks = slice(h * GLA_DK, (h + 1) * GLA_DK)
                vs = slice(h * GLA_DV, (h + 1) * GLA_DV)
                s = jnp.where(mask, _dot_nt(qd[:, ks], kd[:, ks]), 0.0).astype(BF16)
                st = st_ref[d, h]
                o = _dot(s, v[:, vs]) + _dot_nt(qd[:, ks], st.astype(BF16))
                st_ref[d, h] = st * decay[:, ks] + _dot_tn(v[:, vs], kc[:, ks])
                outs.append(o)
            o_ref[sl, :] = jnp.concatenate(outs, axis=-1).astype(o_ref.dtype)

    direction(qf_ref, kf_ref, vf_ref, lf_ref, of_ref, 0)
    direction(qb_ref, kb_ref, vb_ref, lb_ref, ob_ref, 1)


def _gla_mixer(h0, wg, bg, n_batch, seq):
    nb = seq // GLA_ROWS
    r = GLA_ROWS
    fwd = lambda b, i: b * nb + i
    bwd = lambda b, i: b * nb + (nb - 1 - i)
    def spec(width, colblk, rowfn):
        return pl.BlockSpec((r, width), lambda b, i: (rowfn(b, i), colblk))
    in_specs = [spec(256, 0, fwd), spec(256, 1, fwd), spec(512, 1, fwd), spec(128, 12, fwd),
                spec(256, 0, bwd), spec(256, 1, bwd), spec(512, 1, bwd), spec(128, 12, bwd),
                pl.BlockSpec((128, 512), lambda b, i: (0, 0)),
                pl.BlockSpec((1, 512), lambda b, i: (0, 0))]
    out_specs = [pl.BlockSpec((r, 512), lambda b, i: (fwd(b, i), 0)),
                 pl.BlockSpec((r, 512), lambda b, i: (bwd(b, i), 0))]
    t = n_batch * seq
    return pl.pallas_call(
        _gla_kernel,
        grid=(n_batch, nb),
        in_specs=in_specs,
        out_specs=out_specs,
        out_shape=[jax.ShapeDtypeStruct((t, 512), BF16), jax.ShapeDtypeStruct((t, 512), BF16)],
        scratch_shapes=[pltpu.VMEM((2, GLA_HEADS, GLA_DV, GLA_DK), F32)],
        compiler_params=_cparams(2),
        name="gla_chunked",
    )(h0, h0, h0, h0, h0, h0, h0, h0, wg, bg)


def _mix0_out_kernel(y_ref, of_ref, ob_ref, go_ref, x_ref, gw_ref, gb_ref, ng_ref, wo_ref, lg_ref, lb_ref,
                     h_ref, h8_ref, ya_s):
    n_chunk = ROW_TILE // S5_CHUNK
    ys = [y_ref[p].astype(F32) for p in range(S5_GROUPS // 2)]
    n_lane_blk = S5_WIDTH // 128
    for t in range(S5_CHUNK):
        for k in range(n_lane_blk):
            ya_s[k, pl.ds(t, n_chunk, stride=S5_CHUNK), :] = jnp.concatenate(
                [y[:, t * 32:(t + 1) * 32] for y in ys[4 * k:4 * k + 4]], axis=-1)
    yaf = jnp.concatenate([ya_s[k] for k in range(n_lane_blk)], axis=-1)
    ya = yaf.astype(BF16)
    gate = _dot(ya, gw_ref[...]) + gb_ref[...]
    ya2 = yaf * jax.nn.sigmoid(gate)
    o = of_ref[...].astype(F32) + ob_ref[...].astype(F32)
    ng = ng_ref[...]
    parts = []
    for h in range(GLA_HEADS):
        oh = o[:, h * GLA_DV:(h + 1) * GLA_DV]
        ms = jnp.mean(oh * oh, axis=-1, keepdims=True)
        parts.append(oh * lax.rsqrt(ms + RMS_EPS) * ng[:, h * GLA_DV:(h + 1) * GLA_DV])
    yb = jnp.concatenate(parts, axis=-1) * jax.nn.silu(go_ref[...].astype(F32))
    wo = wo_ref[...]
    y = _dot(ya2.astype(BF16), wo[:S5_WIDTH]) + _dot(yb.astype(BF16), wo[S5_WIDTH:])
    hn = _layer_norm(ALPHA * x_ref[...] + y, lg_ref[...], lb_ref[...])
    h_ref[...] = hn
    _store_token_tiles(h8_ref, hn)


def _mix0_out(ya, o_f, o_b, h0, x, glu_w, glu_b, norm_g, w_out, ln_g, ln_b):
    t = x.shape[0]
    tm = ROW_TILE
    row = lambda w: pl.BlockSpec((tm, w), lambda i: (i, 0))
    full = lambda a, b: pl.BlockSpec((a, b), lambda i: (0, 0))
    return pl.pallas_call(
        _mix0_out_kernel,
        grid=(t // tm,),
        in_specs=[pl.BlockSpec((S5_GROUPS // 2, tm // S5_CHUNK, 512), lambda i: (0, i, 0)),
                  row(512), row(512),
                  pl.BlockSpec((tm, 512), lambda i: (i, 2)),
                  row(1024), full(512, 512), full(1, 512), full(1, 512), full(1024, 1024),
                  full(1, 1024), full(1, 1024)],
        out_specs=[row(1024), pl.BlockSpec((tm * 8, 128), lambda i: (i, 0))],
        out_shape=[jax.ShapeDtypeStruct((t, D_MODEL), F32), jax.ShapeDtypeStruct((t * 8, 128), F32)],
        scratch_shapes=[pltpu.VMEM((S5_WIDTH // 128, tm, 128), F32)],
        compiler_params=_cparams(1),
        name="mix0_out_ln",
    )(ya, o_f, o_b, h0, x, glu_w, glu_b, norm_g, w_out, ln_g, ln_b)


def _attn_kernel(q_ref, kc_ref, vc_ref, kp_ref, vp_ref, kn_ref, vn_ref, bias_ref, sink_ref, o_ref, *, seq):
    blk = pl.program_id(1)
    n_blk = pl.num_programs(1)
    n_sub = ATT_ROWS // ATT_BLOCK
    kw = ATT_BLOCK + 2 * WINDOW
    nq = GQA * ATT_BLOCK
    key_row = lax.broadcasted_iota(jnp.int32, (kw, nq), 0)
    pen_first = jnp.where((blk == 0) & (key_row < WINDOW), NEG_INF, 0.0)
    pen_last = jnp.where((blk == n_blk - 1) & (key_row >= ATT_BLOCK + WINDOW), NEG_INF, 0.0)
    kall = jnp.concatenate([kp_ref[...], kc_ref[...], kn_ref[...]], axis=0)
    vall = jnp.concatenate([vp_ref[...], vc_ref[...], vn_ref[...]], axis=0)
    for s in range(n_sub):
        kwin = kall[s * ATT_BLOCK:s * ATT_BLOCK + kw]
        vwin = vall[s * ATT_BLOCK:s * ATT_BLOCK + kw]
        q = q_ref[s * ATT_BLOCK:(s + 1) * ATT_BLOCK, :] * ATT_SCALE
        outs_t = []
        for kv in range(N_KV):
            qs = jnp.concatenate([q[:, (kv * GQA + g) * HEAD_DIM:(kv * GQA + g + 1) * HEAD_DIM]
                                  for g in range(GQA)], axis=0)
            kh = kwin[:, kv * HEAD_DIM:(kv + 1) * HEAD_DIM]
            vh = vwin[:, kv * HEAD_DIM:(kv + 1) * HEAD_DIM]
            st = _dot_nt(kh, qs) + bias_ref[kv]
            if s == 0:
                st = st + pen_first
            if s == n_sub - 1:
                st = st + pen_last
            sink = sink_ref[kv]
            m = jnp.maximum(jnp.max(st, axis=0, keepdims=True), sink)
            p = jnp.exp(st - m)
            den = jnp.sum(p, axis=0, keepdims=True) + jnp.exp(sink - m)
            outs_t.append(_dot_tn(vh, p.astype(BF16)) / den)
        o = jnp.concatenate(outs_t, axis=0).T
        pieces = [o[g * ATT_BLOCK:(g + 1) * ATT_BLOCK, kv * HEAD_DIM:(kv + 1) * HEAD_DIM]
                  for kv in range(N_KV) for g in range(GQA)]
        o_ref[s * ATT_BLOCK:(s + 1) * ATT_BLOCK, :] = jnp.concatenate(pieces, axis=-1).astype(o_ref.dtype)


def _t5_bucket(rel):
    nb = REL_BUCKETS // 2
    max_exact = nb // 2
    ret = (rel > 0).astype(jnp.int32) * nb
    n = jnp.abs(rel)
    large = max_exact + (jnp.log(jnp.maximum(n, 1).astype(F32) / max_exact)
                         / math.log(REL_MAX_DIST / max_exact) * (nb - max_exact)).astype(jnp.int32)
    large = jnp.minimum(large, nb - 1)
    return ret + jnp.where(n < max_exact, n, large)


def _attn_mixer(h3, rel_bias, sink, n_batch, seq):
    kw = ATT_BLOCK + 2 * WINDOW
    n_rel = kw + ATT_BLOCK - 1
    rel = jnp.arange(n_rel) - (n_rel - 1) // 2
    tb = rel_bias.astype(F32)[_t5_bucket(rel)]
    tb = jnp.where((jnp.abs(rel) <= WINDOW)[:, None], tb, NEG_INF)
    tb = jnp.pad(tb.T, ((0, 0), (0, 1)))
    flat = jnp.tile(tb, (1, ATT_BLOCK))[:, ATT_BLOCK - 1:ATT_BLOCK - 1 + ATT_BLOCK * n_rel]
    bias = flat.reshape(N_HEADS, ATT_BLOCK, n_rel)[:, :, :kw]
    bias = jnp.transpose(bias.reshape(N_KV, GQA, ATT_BLOCK, kw), (0, 3, 1, 2)).reshape(N_KV, kw, GQA * ATT_BLOCK)
    sink_rows = jnp.repeat(sink.astype(F32).reshape(N_KV, GQA), ATT_BLOCK, axis=1)[:, None, :]
    nblk = seq // ATT_ROWS
    sub = ATT_ROWS // ATT_BLOCK
    n128 = seq // ATT_BLOCK
    cur = lambda b, i: b * nblk + i
    prev = lambda b, i: b * n128 + jnp.maximum(i * sub - 1, 0)
    nxt = lambda b, i: b * n128 + jnp.minimum((i + 1) * sub, n128 - 1)
    t = n_batch * seq
    kern = functools.partial(_attn_kernel, seq=seq)
    return pl.pallas_call(
        kern,
        grid=(n_batch, nblk),
        in_specs=[pl.BlockSpec((ATT_ROWS, ATT_DIM), lambda b, i: (cur(b, i), 0)),
                  pl.BlockSpec((ATT_ROWS, KV_DIM), lambda b, i: (cur(b, i), 4)),
                  pl.BlockSpec((ATT_ROWS, KV_DIM), lambda b, i: (cur(b, i), 5)),
                  pl.BlockSpec((ATT_BLOCK, KV_DIM), lambda b, i: (prev(b, i), 4)),
                  pl.BlockSpec((ATT_BLOCK, KV_DIM), lambda b, i: (prev(b, i), 5)),
                  pl.BlockSpec((ATT_BLOCK, KV_DIM), lambda b, i: (nxt(b, i), 4)),
                  pl.BlockSpec((ATT_BLOCK, KV_DIM), lambda b, i: (nxt(b, i), 5)),
                  pl.BlockSpec((N_KV, kw, GQA * ATT_BLOCK), lambda b, i: (0, 0, 0)),
                  pl.BlockSpec((N_KV, 1, GQA * ATT_BLOCK), lambda b, i: (0, 0, 0))],
        out_specs=pl.BlockSpec((ATT_ROWS, ATT_DIM), lambda b, i: (cur(b, i), 0)),
        out_shape=jax.ShapeDtypeStruct((t, ATT_DIM), BF16),
        compiler_params=_cparams(2),
        name="window_gqa",
    )(h3, h3, h3, h3, h3, h3, h3, bias, sink_rows)


def _proj_ln_kernel(a_ref, w_ref, x_ref, g_ref, b_ref, h_ref, h8_ref):
    y = _dot(a_ref[...], w_ref[...])
    hn = _layer_norm(ALPHA * x_ref[...] + y, g_ref[...], b_ref[...])
    h_ref[...] = hn
    _store_token_tiles(h8_ref, hn)


def _proj_ln(a, w, x, g, b):
    t, k = a.shape
    tm = ROW_TILE
    return pl.pallas_call(
        _proj_ln_kernel,
        grid=(t // tm,),
        in_specs=[pl.BlockSpec((tm, k), lambda i: (i, 0)),
                  pl.BlockSpec((k, D_MODEL), lambda i: (0, 0)),
                  pl.BlockSpec((tm, D_MODEL), lambda i: (i, 0)),
                  pl.BlockSpec((1, D_MODEL), lambda i: (0, 0)),
                  pl.BlockSpec((1, D_MODEL), lambda i: (0, 0))],
        out_specs=[pl.BlockSpec((tm, D_MODEL), lambda i: (i, 0)), pl.BlockSpec((tm * 8, 128), lambda i: (i, 0))],
        out_shape=[jax.ShapeDtypeStruct((t, D_MODEL), F32), jax.ShapeDtypeStruct((t * 8, 128), F32)],
        compiler_params=_cparams(1),
        name="proj_ln",
    )(a, w, x, g, b)


def _router_kernel(h_ref, wh_ref, wl_ref, b_ref, o_ref, ot_ref, cnt_ref, run_ref):
    @pl.when(pl.program_id(0) == 0)
    def _():
        run_ref[...] = jnp.zeros_like(run_ref)

    x = h_ref[...]
    xh = x.astype(BF16)
    xl = (x - xh.astype(F32)).astype(BF16)
    wh = wh_ref[...]
    logits = _dot(xh, wh) + (_dot(xl, wh) + _dot(xh, wl_ref[...])) + b_ref[...]
    lane = lax.broadcasted_iota(jnp.int32, logits.shape, 1)
    big = jnp.int32(1 << 20)
    neg = jnp.float32(-jnp.inf)
    is_g = lane < N_GROUPS
    gl = jnp.where(is_g, logits, neg)
    gmax = jnp.max(gl, axis=-1, keepdims=True)
    gsum = jnp.sum(jnp.where(is_g, jnp.exp(gl - gmax), 0.0), axis=-1, keepdims=True)
    grp = jnp.min(jnp.where(is_g & (gl == gmax), lane, big), axis=-1, keepdims=True)
    p_grp = 1.0 / gsum
    lo = N_GROUPS + grp * EPG
    in_grp = (lane >= lo) & (lane < lo + EPG)
    el = jnp.where(in_grp, logits, neg)
    m1 = jnp.max(el, axis=-1, keepdims=True)
    i1 = jnp.min(jnp.where(in_grp & (el == m1), lane, big), axis=-1, keepdims=True)
    el2 = jnp.where(lane == i1, neg, el)
    m2 = jnp.max(el2, axis=-1, keepdims=True)
    i2 = jnp.min(jnp.where(in_grp & (lane != i1) & (el2 == m2), lane, big), axis=-1, keepdims=True)
    e2 = jnp.exp(m2 - m1)
    g1 = p_grp / (1.0 + e2)
    g2 = p_grp * e2 / (1.0 + e2)
    hit1 = lane == i1
    hit2 = lane == i2
    onehot = (hit1 | hit2).astype(BF16)
    tm = onehot.shape[0]
    lower = (lax.broadcasted_iota(jnp.int32, (tm, tm), 1) < lax.broadcasted_iota(jnp.int32, (tm, tm), 0)).astype(BF16)
    before = _dot(lower, onehot) + run_ref[...]
    r1 = jnp.sum(jnp.where(hit1, before, 0.0), axis=-1, keepdims=True)
    r2 = jnp.sum(jnp.where(hit2, before, 0.0), axis=-1, keepdims=True)
    run_ref[...] += jnp.sum(onehot.astype(F32), axis=0, keepdims=True)
    cnt_ref[...] = run_ref[...]
    vals = ((i1 - N_GROUPS).astype(F32), (i2 - N_GROUPS).astype(F32), g1, g2, r1, r2)
    out = jnp.zeros_like(logits)
    for k, v in enumerate(vals):
        out = jnp.where(lane == k, v, out)
    o_ref[...] = out
    ot_ref[...] = out.T[0:8]


def _router(h, w_group, b_group, w_router, b_router):
    t = h.shape[0]
    w = jnp.concatenate([w_group.astype(F32), w_router.astype(F32)], axis=1)
    w = jnp.pad(w, ((0, 0), (0, 128 - w.shape[1])))
    wh = w.astype(BF16)
    wl = (w - wh.astype(F32)).astype(BF16)
    bias = jnp.pad(jnp.concatenate([b_group.astype(F32), b_router.astype(F32)]), (0, 128 - 36))[None]
    tm = ROW_TILE
    out, out_t, cnt = pl.pallas_call(
        _router_kernel,
        grid=(t // tm,),
        in_specs=[pl.BlockSpec((tm, D_MODEL), lambda i: (i, 0)),
                  pl.BlockSpec((D_MODEL, 128), lambda i: (0, 0)),
                  pl.BlockSpec((D_MODEL, 128), lambda i: (0, 0)),
                  pl.BlockSpec((1, 128), lambda i: (0, 0))],
        out_specs=[pl.BlockSpec((tm, 128), lambda i: (i, 0)),
                   pl.BlockSpec((8, tm), lambda i: (0, i)),
                   pl.BlockSpec((1, 128), lambda i: (0, 0))],
        out_shape=[jax.ShapeDtypeStruct((t, 128), F32), jax.ShapeDtypeStruct((8, t), F32),
                   jax.ShapeDtypeStruct((1, 128), F32)],
        scratch_shapes=[pltpu.VMEM((1, 128), F32)],
        compiler_params=_cparams(1),
        name="moe_router",
    )(h, wh, wl, bias)
    eid = out_t[0:2].astype(jnp.int32)
    rank = out_t[4:6].astype(jnp.int32)
    counts = cnt[0, N_GROUPS:N_GROUPS + N_EXPERTS].astype(jnp.int32)
    return out, eid, rank, counts


def _token_tile(ref, tok):
    return ref.at[pl.ds(pl.multiple_of(tok * 8, 8), 8)]


def _experts_kernel(be_ref, nu_ref, src_hbm, h8_hbm, w1_ref, w3_ref, w2_ref, ys_ref,
                    w13_s, w2_s, xbuf, idx_smem, sem, isem):
    i = pl.program_id(0)
    n_used = nu_ref[0]
    used = i < n_used
    slot = i % 2

    blk_per_fetch = IDX_FETCH // MOE_ROWS

    def gather_block(blk, slot_):
        @pl.when(blk % blk_per_fetch == 0)
        def _():
            off = pl.multiple_of(blk * MOE_ROWS, IDX_FETCH)
            cp = pltpu.make_async_copy(src_hbm.at[pl.ds(off, IDX_FETCH)], idx_smem, isem)
            cp.start()
            cp.wait()
        base = (blk % blk_per_fetch) * MOE_ROWS
        def issue(r8, _):
            for u in range(8):
                r = r8 * 8 + u
                pltpu.make_async_copy(_token_tile(h8_hbm, idx_smem[base + r]), _token_tile(xbuf.at[slot_], r),
                                      sem.at[slot_]).start(priority=u % 2)
            return 0
        lax.fori_loop(0, MOE_ROWS // 8, issue, 0)

    @pl.when(i == 0)
    def _():
        gather_block(0, 0)

    @pl.when(used)
    def _():
        pltpu.make_async_copy(h8_hbm.at[pl.ds(0, MOE_ROWS * 8)], xbuf.at[slot], sem.at[slot]).wait()

    @pl.when(i + 1 < n_used)
    def _():
        gather_block(i + 1, 1 - slot)

    prev = be_ref[jnp.maximum(i - 1, 0)]
    fresh = (i == 0) | (be_ref[i] != prev)

    @pl.when(used & fresh)
    def _():
        w13_s[:, :D_EXPERT] = w1_ref[...].astype(BF16)
        w13_s[:, D_EXPERT:] = w3_ref[...].astype(BF16)
        w2_s[...] = w2_ref[...].astype(BF16)

    @pl.when(used)
    def _():
        x = _load_token_tiles(xbuf.at[slot], MOE_ROWS).astype(BF16)
        h = _dot(x, w13_s[...])
        h1 = h[:, :D_EXPERT]
        hdn = (h1 * jax.nn.sigmoid(h1)) * h[:, D_EXPERT:]
        _store_token_tiles(ys_ref, _dot(hdn.astype(BF16), w2_s[...]))

    @pl.when(jnp.logical_not(used))
    def _():
        ys_ref[...] = jnp.zeros_like(ys_ref)


def _experts(h8, src, blk_e, n_used, w1, w3, w2, cap):
    nblk = cap // MOE_ROWS
    return pl.pallas_call(
        _experts_kernel,
        grid_spec=pltpu.PrefetchScalarGridSpec(
            num_scalar_prefetch=2,
            grid=(nblk,),
            in_specs=[pl.BlockSpec(memory_space=pl.ANY),
                      pl.BlockSpec(memory_space=pl.ANY),
                      pl.BlockSpec((None, D_MODEL, D_EXPERT), lambda i, be, nu: (be[i], 0, 0)),
                      pl.BlockSpec((None, D_MODEL, D_EXPERT), lambda i, be, nu: (be[i], 0, 0)),
                      pl.BlockSpec((None, D_EXPERT, D_MODEL), lambda i, be, nu: (be[i], 0, 0))],
            out_specs=pl.BlockSpec((MOE_ROWS * 8, 128), lambda i, be, nu: (i, 0)),
            scratch_shapes=[pltpu.VMEM((D_MODEL, 2 * D_EXPERT), BF16),
                            pltpu.VMEM((D_EXPERT, D_MODEL), BF16),
                            pltpu.VMEM((2, MOE_ROWS * 8, 128), F32),
                            pltpu.SMEM((IDX_FETCH,), jnp.int32),
                            pltpu.SemaphoreType.DMA((2,)),
                            pltpu.SemaphoreType.DMA(())]),
        out_shape=jax.ShapeDtypeStruct((cap * 8, 128), F32),
        compiler_params=_cparams(1),
        name="moe_experts",
    )(blk_e, n_used, src, h8, w1, w3, w2)


def _combine_kernel(dst_hbm, ys_hbm, h_ref, gate_ref, g_ref, b_ref, o_ref, idx_smem, y0_ref, y1_ref, sem, isem):
    i = pl.program_id(0)
    tm = ROW_TILE
    cp = pltpu.make_async_copy(dst_hbm.at[pl.ds(i * 2 * tm, 2 * tm)], idx_smem, isem)
    cp.start()
    cp.wait()

    def issue(t, _):
        pltpu.make_async_copy(_token_tile(ys_hbm, idx_smem[t]), _token_tile(y0_ref, t), sem).start(priority=0)
        pltpu.make_async_copy(_token_tile(ys_hbm, idx_smem[tm + t]), _token_tile(y1_ref, t), sem).start(priority=1)
        return 0
    lax.fori_loop(0, tm, issue, 0, unroll=8)
    pltpu.make_async_copy(ys_hbm.at[pl.ds(0, tm * 8)], y0_ref, sem).wait()
    pltpu.make_async_copy(ys_hbm.at[pl.ds(0, tm * 8)], y1_ref, sem).wait()
    gate = gate_ref[...]
    y = _load_token_tiles(y0_ref, tm) * gate[:, 2:3] + _load_token_tiles(y1_ref, tm) * gate[:, 3:4]
    o_ref[...] = _layer_norm(ALPHA * h_ref[...] + y, g_ref[...], b_ref[...])


def _combine_ln(dst, ys, h, rout, ln_g, ln_b):
    t = h.shape[0]
    tm = ROW_TILE
    return pl.pallas_call(
        _combine_kernel,
        grid_spec=pltpu.PrefetchScalarGridSpec(
            num_scalar_prefetch=0,
            grid=(t // tm,),
            in_specs=[pl.BlockSpec(memory_space=pl.ANY),
                      pl.BlockSpec(memory_space=pl.ANY),
                      pl.BlockSpec((tm, D_MODEL), lambda i: (i, 0)),
                      pl.BlockSpec((tm, 128), lambda i: (i, 0)),
                      pl.BlockSpec((1, D_MODEL), lambda i: (0, 0)),
                      pl.BlockSpec((1, D_MODEL), lambda i: (0, 0))],
            out_specs=pl.BlockSpec((tm, D_MODEL), lambda i: (i, 0)),
            scratch_shapes=[pltpu.SMEM((2 * tm,), jnp.int32),
                            pltpu.VMEM((tm * 8, 128), F32),
                            pltpu.VMEM((tm * 8, 128), F32),
                            pltpu.SemaphoreType.DMA(()),
                            pltpu.SemaphoreType.DMA(())]),
        out_shape=jax.ShapeDtypeStruct((t, D_MODEL), F32),
        compiler_params=_cparams(1),
        name="moe_combine_ln",
    )(dst, ys, h, rout, ln_g, ln_b)


def _moe_layer(h, h8, w_group, b_group, w_router, b_router, w1, w3, w2, ln_g, ln_b):
    t = h.shape[0]
    rout, eid, rank, counts = _router(h, w_group, b_group, w_router, b_router)
    padded = (counts + MOE_ROWS - 1) // MOE_ROWS * MOE_ROWS
    pad_end = jnp.cumsum(padded)
    pad_start = pad_end - padded
    start_of = jnp.sum(jnp.where(eid[..., None] == jnp.arange(N_EXPERTS), pad_start, 0), axis=-1)
    dst = (start_of + rank).astype(jnp.int32)
    cap = 2 * t + N_EXPERTS * MOE_ROWS
    nblk = cap // MOE_ROWS
    tok = jnp.broadcast_to(jnp.arange(t, dtype=jnp.int32), (2, t))
    src = jnp.zeros((cap,), jnp.int32).at[dst.reshape(-1)].set(tok.reshape(-1), unique_indices=True)
    blk_e = jnp.minimum(jnp.sum(pad_end[None, :] <= (jnp.arange(nblk) * MOE_ROWS)[:, None], axis=1),
                        N_EXPERTS - 1).astype(jnp.int32)
    n_used = (pad_end[-1] // MOE_ROWS).astype(jnp.int32).reshape(1)
    dst_tiles = jnp.transpose(dst.reshape(2, t // ROW_TILE, ROW_TILE), (1, 0, 2)).reshape(-1)
    ys = _experts(h8, src, blk_e, n_used, w1, w3, w2, cap)
    return _combine_ln(dst_tiles, ys, h, rout, ln_g, ln_b)


def kernel(x, ln_mix_g, ln_mix_b, ln_ffn_g, ln_ffn_b, ab_w_in, s5_lam_re, s5_lam_im, s5_log_dt, s5_b_re, s5_b_im,
           s5_c_re, s5_c_im, s5_d, s5_glu_w, s5_glu_b, gla_gate_w, gla_gate_b, gla_norm_g, ab_w_out, c_w_in,
           c_sink, c_w_out, rel_bias, moe_w_group, moe_b_group, moe_w_router, moe_b_router, moe_w1, moe_w3, moe_w2):
    n_batch, seq, _ = x.shape
    t = n_batch * seq
    xt = x.reshape(t, D_MODEL)
    row = lambda v: v.astype(F32).reshape(1, -1)

    w_in0 = jnp.pad(ab_w_in[0], ((0, 0), (0, AB_IN_PAD - AB_IN))).astype(BF16)
    h0, u_t = _inproj0(xt, w_in0)
    n_steps = seq // (S5_SEGS * S5_CHUNK)
    r1, r2, coef, ptab = _s5_prep(s5_lam_re[0], s5_lam_im[0], s5_log_dt[0], s5_b_re[0], s5_b_im[0],
                                  s5_c_re[0], s5_c_im[0], s5_d[0], n_steps)
    ya = _s5_mixer(u_t, r1, r2, coef, ptab, n_batch, n_steps)
    gw = gla_gate_w[0].astype(F32)
    wg = jnp.zeros((128, 2 * GLA_KEY), F32)
    wg = wg.at[0:GLA_RANK, 0:GLA_KEY].set(gw[0]).at[GLA_RANK:2 * GLA_RANK, GLA_KEY:].set(gw[1]).astype(BF16)
    bg = gla_gate_b[0].astype(F32).reshape(1, 2 * GLA_KEY)
    o_f, o_b = _gla_mixer(h0, wg, bg, n_batch, seq)
    h, h8 = _mix0_out(ya, o_f, o_b, h0, xt, s5_glu_w[0].astype(BF16), row(s5_glu_b[0]), row(gla_norm_g[0]),
                      ab_w_out[0].astype(BF16), row(ln_mix_g[0]), row(ln_mix_b[0]))
    h = _moe_layer(h, h8, moe_w_group[0], moe_b_group[0], moe_w_router[0], moe_b_router[0],
                   moe_w1[0], moe_w3[0], moe_w2[0], row(ln_ffn_g[0]), row(ln_ffn_b[0]))

    h3 = _matmul(h, c_w_in[0].astype(BF16), BF16)
    o = _attn_mixer(h3, rel_bias, c_sink[0], n_batch, seq)
    h, h8 = _proj_ln(o, c_w_out[0].astype(BF16), h, row(ln_mix_g[1]), row(ln_mix_b[1]))
    h = _moe_layer(h, h8, moe_w_group[1], moe_b_group[1], moe_w_router[1], moe_b_router[1],
                   moe_w1[1], moe_w3[1], moe_w2[1], row(ln_ffn_g[1]), row(ln_ffn_b[1]))
    return h.reshape(n_batch, seq, D_MODEL)
```

```python
import functools
import math

import jax
import jax.numpy as jnp
from jax import lax
from jax.experimental import pallas as pl
from jax.experimental.pallas import tpu as pltpu

F32 = jnp.float32
BF16 = jnp.bfloat16

D_MODEL = 1024
S5_WIDTH = 512
S5_GROUP_CH = 16
S5_GROUPS = 32
S5_STATE = 64
GLA_HEADS = 4
GLA_DV = 128
GLA_DK = 64
GLA_KEY = 256
GLA_WIDTH = 512
GLA_RANK = 16
GLA_TAU = 16.0
GLA_CHUNK = 64
AB_IN = 2080
AB_IN_PAD = 2176
HEAD_DIM = 64
N_HEADS = 16
N_KV = 4
GQA = 4
ATT_DIM = 1024
KV_DIM = 256
WINDOW = 128
ATT_BLOCK = 128
ATT_SCALE = HEAD_DIM ** -0.5
REL_BUCKETS = 32
REL_MAX_DIST = 128
NEG_INF = -1e30
N_GROUPS = 4
EPG = 8
N_EXPERTS = 32
D_EXPERT = 512
LN_EPS = 1e-5
RMS_EPS = 1e-6
DEPTH = 2
ALPHA = (2 * DEPTH) ** 0.25

S5_CHUNK = 16
S5_SEGS = 8
ROW_TILE = 512
GLA_ROWS = 256
ATT_ROWS = 512
MOE_ROWS = 256
IDX_FETCH = 1024
N_ROW_BUFS = 3
VMEM_LIMIT = 56 * 1024 * 1024


def _cparams(n_axes):
    return pltpu.CompilerParams(dimension_semantics=("arbitrary",) * n_axes,
                                vmem_limit_bytes=VMEM_LIMIT)


def _dot(a, b):
    return jnp.dot(a, b, preferred_element_type=F32)


def _dot_nt(a, b):
    return lax.dot_general(a, b, (((1,), (1,)), ((), ())), preferred_element_type=F32)


def _dot_tn(a, b):
    return lax.dot_general(a, b, (((0,), (0,)), ((), ())), preferred_element_type=F32)


def _layer_norm(r, g, b):
    mu = jnp.mean(r, axis=-1, keepdims=True)
    c = r - mu
    var = jnp.mean(c * c, axis=-1, keepdims=True)
    return c * lax.rsqrt(var + LN_EPS) * g + b


def _store_token_tiles(ref, val):
    n = val.shape[0]
    for k in range(D_MODEL // 128):
        ref[pl.ds(k, n, stride=8), :] = val[:, k * 128:(k + 1) * 128]


def _load_token_tiles(ref, n):
    return jnp.concatenate([ref[pl.ds(k, n, stride=8), :] for k in range(D_MODEL // 128)], axis=-1)


def _mm_kernel(x_ref, w_ref, o_ref):
    o_ref[...] = _dot(x_ref[...].astype(BF16), w_ref[...]).astype(o_ref.dtype)


def _matmul(x, w, out_dtype):
    m, k = x.shape
    n = w.shape[1]
    return pl.pallas_call(
        _mm_kernel,
        grid=(m // ROW_TILE,),
        in_specs=[pl.BlockSpec((ROW_TILE, k), lambda i: (i, 0)),
                  pl.BlockSpec((k, n), lambda i: (0, 0))],
        out_specs=pl.BlockSpec((ROW_TILE, n), lambda i: (i, 0)),
        out_shape=jax.ShapeDtypeStruct((m, n), out_dtype),
        compiler_params=_cparams(1),
        name="dense_matmul",
    )(x, w)


def _s5_prep(lam_re, lam_im, log_dt, b_re, b_im, c_re, c_im, d, n_steps):
    tc = S5_CHUNK
    g, p, c = S5_GROUPS, S5_STATE, S5_GROUP_CH
    lr = jnp.minimum(lam_re.astype(F32), -1e-4)
    li = lam_im.astype(F32)
    dt = jnp.exp(log_dt.astype(F32))[..., None]
    mag = jnp.exp(lr * dt)
    ar = mag * jnp.cos(li * dt)
    ai = mag * jnp.sin(li * dt)
    den = lr * lr + li * li
    nr = ar - 1.0
    coef_r = (nr * lr + ai * li) / den
    coef_i = (ai * lr - nr * li) / den
    br_ = b_re.astype(F32)
    bi_ = b_im.astype(F32)
    bbr = coef_r[..., None] * br_ - coef_i[..., None] * bi_
    bbi = coef_r[..., None] * bi_ + coef_i[..., None] * br_
    cr = c_re.astype(F32)
    ci = c_im.astype(F32)
    bbr_t = jnp.swapaxes(bbr, -1, -2)
    bbi_t = jnp.swapaxes(bbi, -1, -2)
    npair = g // 2

    def apow(n):
        nn = n.astype(F32)[:, None, None, None]
        m_ = jnp.exp(nn * (lr * dt)[None])
        ang = nn * (li * dt)[None]
        return m_ * jnp.cos(ang), m_ * jnp.sin(ang)

    pr, pi = apow(jnp.arange(tc + 1))
    hr = pr[:, :, :, None, :] * bbr_t[None] - pi[:, :, :, None, :] * bbi_t[None]
    hi = pr[:, :, :, None, :] * bbi_t[None] + pi[:, :, :, None, :] * bbr_t[None]
    kk = jnp.einsum('dgoq,jdgcq->jdgco', cr, hr) - jnp.einsum('dgoq,jdgcq->jdgco', ci, hi)
    k0 = kk[0, 0] + kk[0, 1] + d.astype(F32).reshape(g, c)[:, :, None] * jnp.eye(c, dtype=F32)
    slab = jnp.concatenate([kk[tc - 1:0:-1, 1], k0[None], kk[1:tc, 0]], axis=0)
    kcat = jnp.transpose(slab, (1, 2, 0, 3)).reshape(g, c, (2 * tc - 1) * c)
    kcat = jnp.pad(kcat, ((0, 0), (0, 0), (0, 2 * tc * c - kcat.shape[-1]))).reshape(npair, 2, c, 2 * tc * c)

    def pair_blockdiag(m):
        m2 = m.reshape((npair, 2) + m.shape[1:])
        z = jnp.zeros_like(m2[:, 0])
        return jnp.concatenate([jnp.concatenate([m2[:, 0], z], axis=-1),
                                jnp.concatenate([z, m2[:, 1]], axis=-1)], axis=1)

    def w_plane(h_, lag_sel, dirn):
        w = jnp.transpose(h_[lag_sel, dirn], (1, 0, 2, 3)).reshape(g, tc * c, p)
        return pair_blockdiag(w)
    s_fwd = tc - 1 - jnp.arange(tc)
    s_bwd = jnp.arange(tc)
    r1e = jnp.concatenate([w_plane(hr, s_fwd, 0), w_plane(hi, s_fwd, 0), w_plane(hr, s_bwd, 1), w_plane(hi, s_bwd, 1)],
                          axis=2)

    lane = jnp.arange(tc * c)
    rep_t = (lane[None, :] // c == jnp.arange(tc)[:, None]).astype(F32)
    tile_co = (lane[None, :] % c == jnp.arange(c)[:, None]).astype(F32)
    spread = lambda a, sel: jnp.dot(a, sel, precision=lax.Precision.HIGHEST)
    def readout(e, dirn):
        pr_l = spread(jnp.transpose(pr[e, dirn], (1, 2, 0)).reshape(g * p, tc), rep_t)
        pi_l = spread(jnp.transpose(pi[e, dirn], (1, 2, 0)).reshape(g * p, tc), rep_t)
        cr_l = spread(jnp.transpose(cr[dirn], (0, 2, 1)).reshape(g * p, c), tile_co)
        ci_l = spread(jnp.transpose(ci[dirn], (0, 2, 1)).reshape(g * p, c), tile_co)
        zr = (cr_l * pr_l - ci_l * pi_l).reshape(g, p, tc * c)
        zi = (cr_l * pi_l + ci_l * pr_l).reshape(g, p, tc * c)
        return pair_blockdiag(zr), pair_blockdiag(-zi)
    vf_r, vf_i = readout(jnp.arange(tc) + 1, 0)
    vb_r, vb_i = readout(tc - jnp.arange(tc), 1)
    r2 = jnp.concatenate([vf_r, vf_i, vb_r, vb_i], axis=1)

    steps = jnp.arange(n_steps)
    a16r, a16i = apow(jnp.array([tc]))
    apr, api = apow(steps * tc)
    anr, ani = apow(jnp.array([tc * n_steps]))
    def lanes(x):
        n_ = x.shape[0]
        return jnp.transpose(x.reshape(n_, npair, 2 * p), (1, 0, 2))
    coef = jnp.concatenate([lanes(a16r[:, 0]), lanes(a16i[:, 0]), lanes(a16r[:, 1]), lanes(a16i[:, 1]),
                            lanes(anr[:, 0]), lanes(ani[:, 0]), lanes(anr[:, 1]), lanes(ani[:, 1])], axis=1)
    rev = n_steps - 1 - steps
    ptab = jnp.stack([lanes(apr[:, 0]), lanes(api[:, 0]), lanes(apr[rev, 1]), lanes(api[rev, 1])], axis=1)
    return kcat, r1e.astype(BF16), r2.astype(BF16), coef, ptab


def _gelu_tanh(x):
    return 0.5 * x * (1.0 + jnp.tanh(math.sqrt(2.0 / math.pi) * (x + 0.044715 * (x * x * x))))


def _s5_kernel(u_ref, kcat_ref, r1e_ref, r2_ref, coef_ref, ptab_ref, y_ref, kt_ref, yi_ref, e_ref, s_ref,
               *, n_batch, n_steps):
    rows_b = n_steps * S5_SEGS
    blk = S5_CHUNK * S5_GROUP_CH
    kt_ref[...] = jnp.zeros_like(kt_ref)
    for gi in range(2):
        slab = kcat_ref[gi]
        for s in range(S5_CHUNK):
            off = (S5_CHUNK - 1 - s) * S5_GROUP_CH
            kt_ref[gi * blk + s * S5_GROUP_CH:gi * blk + (s + 1) * S5_GROUP_CH, gi * blk:(gi + 1) * blk] = (
                slab[:, off:off + blk].astype(BF16))
    u = u_ref[...]
    yi_ref[...] = _dot(u, kt_ref[...])
    e = _dot(u, r1e_ref[...])
    for k in range(4):
        e_ref[k] = e[:, k * 128:(k + 1) * 128]
    coef = coef_ref[...]
    a16fr, a16fi, a16br, a16bi = coef[0:1], coef[1:2], coef[2:3], coef[3:4]
    anfr, anfi, anbr, anbi = coef[4:5], coef[5:6], coef[6:7], coef[7:8]
    zero = jnp.zeros((S5_SEGS, 128), F32)
    sub = lax.broadcasted_iota(jnp.int32, (S5_SEGS, 128), 0)

    def seg_rows(b, j):
        return pl.ds(b * rows_b + j, S5_SEGS, stride=n_steps)

    def local_step(j, carry):
        out = []
        for b in range(n_batch):
            sfr, sfi, sbr, sbi = carry[b]
            rf = seg_rows(b, j)
            rb = seg_rows(b, n_steps - 1 - j)
            s_ref[0, rf, :] = sfr
            s_ref[1, rf, :] = sfi
            s_ref[2, rb, :] = sbr
            s_ref[3, rb, :] = sbi
            efr = e_ref[0, rf, :]
            efi = e_ref[1, rf, :]
            ebr = e_ref[2, rb, :]
            ebi = e_ref[3, rb, :]
            out.append((a16fr * sfr - a16fi * sfi + efr, a16fr * sfi + a16fi * sfr + efi,
                        a16br * sbr - a16bi * sbi + ebr, a16br * sbi + a16bi * sbr + ebi))
        return tuple(out)

    ends = lax.fori_loop(0, n_steps, local_step, tuple((zero, zero, zero, zero) for _ in range(n_batch)))

    carries = []
    for b in range(n_batch):
        efr, efi, ebr, ebi = ends[b]
        cfr, cfi, cbr, cbi = zero, zero, zero, zero
        for _ in range(S5_SEGS - 1):
            tfr = anfr * cfr - anfi * cfi + efr
            tfi = anfr * cfi + anfi * cfr + efi
            cfr = jnp.where(sub == 0, 0.0, pltpu.roll(tfr, 1, 0))
            cfi = jnp.where(sub == 0, 0.0, pltpu.roll(tfi, 1, 0))
            tbr = anbr * cbr - anbi * cbi + ebr
            tbi = anbr * cbi + anbi * cbr + ebi
            cbr = jnp.where(sub == S5_SEGS - 1, 0.0, pltpu.roll(tbr, S5_SEGS - 1, 0))
            cbi = jnp.where(sub == S5_SEGS - 1, 0.0, pltpu.roll(tbi, S5_SEGS - 1, 0))
        carries.append((cfr, cfi, cbr, cbi))

    def fix_step(j, _):
        pfr = ptab_ref[0, pl.ds(j, 1), :]
        pfi = ptab_ref[1, pl.ds(j, 1), :]
        pbr = ptab_ref[2, pl.ds(j, 1), :]
        pbi = ptab_ref[3, pl.ds(j, 1), :]
        for b in range(n_batch):
            cfr, cfi, cbr, cbi = carries[b]
            rj = seg_rows(b, j)
            s_ref[0, rj, :] = s_ref[0, rj, :] + (pfr * cfr - pfi * cfi)
            s_ref[1, rj, :] = s_ref[1, rj, :] + (pfr * cfi + pfi * cfr)
            s_ref[2, rj, :] = s_ref[2, rj, :] + (pbr * cbr - pbi * cbi)
            s_ref[3, rj, :] = s_ref[3, rj, :] + (pbr * cbi + pbi * cbr)
        return 0

    lax.fori_loop(0, n_steps, fix_step, 0)

    s_all = jnp.concatenate([s_ref[k] for k in range(4)], axis=-1)
    y = yi_ref[...] + _dot(s_all.astype(BF16), r2_ref[...])
    y_ref[...] = _gelu_tanh(y).astype(y_ref.dtype)


def _s5_mixer(u_t, kcat, r1e, r2, coef, ptab, n_batch, n_steps):
    npair, m, _ = u_t.shape
    kern = functools.partial(_s5_kernel, n_batch=n_batch, n_steps=n_steps)
    return pl.pallas_call(
        kern,
        grid=(npair,),
        in_specs=[pl.BlockSpec((None, m, 512), lambda i: (i, 0, 0)),
                  pl.BlockSpec((None, 2, S5_GROUP_CH, 512), lambda i: (i, 0, 0, 0)),
                  pl.BlockSpec((None, 512, 512), lambda i: (i, 0, 0)),
                  pl.BlockSpec((None, 512, 512), lambda i: (i, 0, 0)),
                  pl.BlockSpec((None, 8, 128), lambda i: (i, 0, 0)),
                  pl.BlockSpec((None, 4, n_steps, 128), lambda i: (i, 0, 0, 0))],
        out_specs=pl.BlockSpec((None, m, 512), lambda i: (i, 0, 0)),
        out_shape=jax.ShapeDtypeStruct((npair, m, 512), BF16),
        scratch_shapes=[pltpu.VMEM((512, 512), BF16), pltpu.VMEM((m, 512), F32),
                        pltpu.VMEM((4, m, 128), F32), pltpu.VMEM((4, m, 128), F32)],
        compiler_params=_cparams(1),
        name="s5_scan",
    )(u_t, kcat, r1e, r2, coef, ptab)


def _inproj0_kernel(x_ref, w_ref, h_ref, ut_ref, u_s):
    h = _dot(x_ref[...].astype(BF16), w_ref[...])
    h_ref[...] = h[:, S5_WIDTH:].astype(h_ref.dtype)
    n_lane_blk = S5_WIDTH // 128
    for k in range(n_lane_blk):
        u_s[k] = h[:, k * 128:(k + 1) * 128]
    n_chunk = ROW_TILE // S5_CHUNK
    rows = [jnp.concatenate([u_s[k, pl.ds(s, n_chunk, stride=S5_CHUNK), :] for k in range(n_lane_blk)], axis=-1)
            for s in range(S5_CHUNK)]
    ch = S5_GROUP_CH
    for p in range(S5_GROUPS // 2):
        ut_ref[p] = jnp.concatenate([r[:, (2 * p + gi) * ch:(2 * p + gi + 1) * ch] for gi in range(2) for r in rows],
                                    axis=-1).astype(ut_ref.dtype)


def _inproj0(x, w):
    t, k = x.shape
    n = w.shape[1]
    tm = ROW_TILE
    npair = S5_GROUPS // 2
    return pl.pallas_call(
        _inproj0_kernel,
        grid=(t // tm,),
        in_specs=[pl.BlockSpec((tm, k), lambda i: (i, 0)),
                  pl.BlockSpec((k, n), lambda i: (0, 0))],
        out_specs=[pl.BlockSpec((tm, n - S5_WIDTH), lambda i: (i, 0)),
                   pl.BlockSpec((npair, tm // S5_CHUNK, 512), lambda i: (0, i, 0))],
        out_shape=[jax.ShapeDtypeStruct((t, n - S5_WIDTH), BF16),
                   jax.ShapeDtypeStruct((npair, t // S5_CHUNK, 512), BF16)],
        scratch_shapes=[pltpu.VMEM((S5_WIDTH // 128, tm, 128), F32)],
        compiler_params=_cparams(1),
        name="inproj0",
    )(x, w)


def _gla_kernel(qf_ref, kf_ref, vf_ref, lf_ref, qb_ref, kb_ref, vb_ref, lb_ref, wg_ref, bg_ref,
                of_ref, ob_ref, st_ref):
    @pl.when(pl.program_id(1) == 0)
    def _():
        st_ref[...] = jnp.zeros_like(st_ref)

    n_chunks = GLA_ROWS // GLA_CHUNK
    row = lax.broadcasted_iota(jnp.int32, (GLA_CHUNK, GLA_CHUNK), 0)
    col = lax.broadcasted_iota(jnp.int32, (GLA_CHUNK, GLA_CHUNK), 1)
    mask_f = col <= row
    mask_b = col > row
    brow = lax.broadcasted_iota(jnp.int32, (GLA_ROWS, GLA_ROWS), 0)
    bcol = lax.broadcasted_iota(jnp.int32, (GLA_ROWS, GLA_ROWS), 1)
    shift = GLA_CHUNK.bit_length() - 1
    same = (brow >> shift) == (bcol >> shift)
    ones_blk = same.astype(BF16)
    tri_f = (same & (bcol <= brow)).astype(BF16)
    tri_b = (same & (bcol >= brow)).astype(BF16)
    wg = wg_ref[...]
    bg = bg_ref[...]

    def direction(q_ref, k_ref, v_ref, l_ref, o_ref, d):
        z = _dot(l_ref[...], wg[:, d * GLA_KEY:(d + 1) * GLA_KEY]) + bg[:, d * GLA_KEY:(d + 1) * GLA_KEY]
        log_a = jax.nn.log_sigmoid(z) * (1.0 / GLA_TAU)
        la_hi = log_a.astype(BF16)
        la_lo = (log_a - la_hi.astype(F32)).astype(BF16)
        tri = tri_f if d == 0 else tri_b
        bc = _dot(tri, la_hi) + _dot(tri, la_lo)
        tot = _dot(ones_blk, la_hi) + _dot(ones_blk, la_lo)
        q = q_ref[...].astype(F32) * (GLA_DK ** -0.5)
        k = k_ref[...].astype(F32)
        qd_all = (q * jnp.exp(bc)).astype(BF16)
        kd_all = (k * jnp.exp(-bc)).astype(BF16)
        kc_all = (k * jnp.exp(tot - bc)).astype(BF16)
        decay_all = jnp.exp(tot)
        order = range(n_chunks) if d == 0 else range(n_chunks - 1, -1, -1)
        mask = mask_f if d == 0 else mask_b
        for c in order:
            sl = slice(c * GLA_CHUNK, (c + 1) * GLA_CHUNK)
            qd, kd, kc = qd_all[sl], kd_all[sl], kc_all[sl]
            decay = decay_all[c * GLA_CHUNK:c * GLA_CHUNK + 1]
            v = v_ref[sl, :]
            outs = []
            for h in range(GLA_HEADS):
                ks = slice(h * GLA_DK, (h + 1) * GLA_DK)
                vs = slice(h * GLA_DV, (h + 1) * GLA_DV)
                s = jnp.where(mask, _dot_nt(qd[:, ks], kd[:, ks]), 0.0).astype(BF16)
                st = st_ref[d, h]
                o = _dot(s, v[:, vs]) + _dot_nt(qd[:, ks], st.astype(BF16))
                st_ref[d, h] = st * decay[:, ks] + _dot_tn(v[:, vs], kc[:, ks])
                outs.append(o)
            o_ref[sl, :] = jnp.concatenate(outs, axis=-1).astype(o_ref.dtype)

    direction(qf_ref, kf_ref, vf_ref, lf_ref, of_ref, 0)
    direction(qb_ref, kb_ref, vb_ref, lb_ref, ob_ref, 1)


def _gla_mixer(h0, wg, bg, n_batch, seq):
    nb = seq // GLA_ROWS
    r = GLA_ROWS
    fwd = lambda b, i: b * nb + i
    bwd = lambda b, i: b * nb + (nb - 1 - i)
    def spec(width, colblk, rowfn):
        return pl.BlockSpec((r, width), lambda b, i: (rowfn(b, i), colblk))
    in_specs = [spec(256, 0, fwd), spec(256, 1, fwd), spec(512, 1, fwd), spec(128, 12, fwd),
                spec(256, 0, bwd), spec(256, 1, bwd), spec(512, 1, bwd), spec(128, 12, bwd),
                pl.BlockSpec((128, 512), lambda b, i: (0, 0)),
                pl.BlockSpec((1, 512), lambda b, i: (0, 0))]
    out_specs = [pl.BlockSpec((r, 512), lambda b, i: (fwd(b, i), 0)),
                 pl.BlockSpec((r, 512), lambda b, i: (bwd(b, i), 0))]
    t = n_batch * seq
    return pl.pallas_call(
        _gla_kernel,
        grid=(n_batch, nb),
        in_specs=in_specs,
        out_specs=out_specs,
        out_shape=[jax.ShapeDtypeStruct((t, 512), BF16), jax.ShapeDtypeStruct((t, 512), BF16)],
        scratch_shapes=[pltpu.VMEM((2, GLA_HEADS, GLA_DV, GLA_DK), F32)],
        compiler_params=_cparams(2),
        name="gla_chunked",
    )(h0, h0, h0, h0, h0, h0, h0, h0, wg, bg)


def _mix0_out_kernel(y_ref, of_ref, ob_ref, go_ref, x_ref, gw_ref, gb_ref, ng_ref, wo_ref, lg_ref, lb_ref,
                     h_ref, h8_ref, ya_s):
    n_chunk = ROW_TILE // S5_CHUNK
    ys = [y_ref[p].astype(F32) for p in range(S5_GROUPS // 2)]
    n_lane_blk = S5_WIDTH // 128
    ch = S5_GROUP_CH
    blk = S5_CHUNK * ch
    for t in range(S5_CHUNK):
        for k in range(n_lane_blk):
            ya_s[k, pl.ds(t, n_chunk, stride=S5_CHUNK), :] = jnp.concatenate(
                [y[:, gi * blk + t * ch:gi * blk + (t + 1) * ch] for y in ys[4 * k:4 * k + 4] for gi in range(2)],
                axis=-1)
    yaf = jnp.concatenate([ya_s[k] for k in range(n_lane_blk)], axis=-1)
    ya = yaf.astype(BF16)
    gate = _dot(ya, gw_ref[...]) + gb_ref[...]
    ya2 = yaf * jax.nn.sigmoid(gate)
    o = of_ref[...].astype(F32) + ob_ref[...].astype(F32)
    ng = ng_ref[...]
    parts = []
    for h in range(GLA_HEADS):
        oh = o[:, h * GLA_DV:(h + 1) * GLA_DV]
        ms = jnp.mean(oh * oh, axis=-1, keepdims=True)
        parts.append(oh * lax.rsqrt(ms + RMS_EPS) * ng[:, h * GLA_DV:(h + 1) * GLA_DV])
    yb = jnp.concatenate(parts, axis=-1) * jax.nn.silu(go_ref[...].astype(F32))
    wo = wo_ref[...]
    y = _dot(ya2.astype(BF16), wo[:S5_WIDTH]) + _dot(yb.astype(BF16), wo[S5_WIDTH:])
    hn = _layer_norm(ALPHA * x_ref[...] + y, lg_ref[...], lb_ref[...])
    h_ref[...] = hn
    _store_token_tiles(h8_ref, hn)


def _mix0_out(ya, o_f, o_b, h0, x, glu_w, glu_b, norm_g, w_out, ln_g, ln_b):
    t = x.shape[0]
    tm = ROW_TILE
    row = lambda w: pl.BlockSpec((tm, w), lambda i: (i, 0))
    full = lambda a, b: pl.BlockSpec((a, b), lambda i: (0, 0))
    return pl.pallas_call(
        _mix0_out_kernel,
        grid=(t // tm,),
        in_specs=[pl.BlockSpec((S5_GROUPS // 2, tm // S5_CHUNK, 512), lambda i: (0, i, 0)),
                  row(512), row(512),
                  pl.BlockSpec((tm, 512), lambda i: (i, 2)),
                  row(1024), full(512, 512), full(1, 512), full(1, 512), full(1024, 1024),
                  full(1, 1024), full(1, 1024)],
        out_specs=[row(1024), pl.BlockSpec((tm * 8, 128), lambda i: (i, 0))],
        out_shape=[jax.ShapeDtypeStruct((t, D_MODEL), F32), jax.ShapeDtypeStruct((t * 8, 128), F32)],
        scratch_shapes=[pltpu.VMEM((S5_WIDTH // 128, tm, 128), F32)],
        compiler_params=_cparams(1),
        name="mix0_out_ln",
    )(ya, o_f, o_b, h0, x, glu_w, glu_b, norm_g, w_out, ln_g, ln_b)


def _attn_kernel(q_ref, kc_ref, vc_ref, kp_ref, vp_ref, kn_ref, vn_ref, bias_ref, sink_ref, o_ref, *, seq):
    blk = pl.program_id(1)
    n_blk = pl.num_programs(1)
    n_sub = ATT_ROWS // ATT_BLOCK
    kw = ATT_BLOCK + 2 * WINDOW
    nq = GQA * ATT_BLOCK
    key_row = lax.broadcasted_iota(jnp.int32, (kw, nq), 0)
    pen_first = jnp.where((blk == 0) & (key_row < WINDOW), NEG_INF, 0.0)
    pen_last = jnp.where((blk == n_blk - 1) & (key_row >= ATT_BLOCK + WINDOW), NEG_INF, 0.0)
    kall = jnp.concatenate([kp_ref[...], kc_ref[...], kn_ref[...]], axis=0)
    vall = jnp.concatenate([vp_ref[...], vc_ref[...], vn_ref[...]], axis=0)
    for s in range(n_sub):
        kwin = kall[s * ATT_BLOCK:s * ATT_BLOCK + kw]
        vwin = vall[s * ATT_BLOCK:s * ATT_BLOCK + kw]
        q = q_ref[s * ATT_BLOCK:(s + 1) * ATT_BLOCK, :] * ATT_SCALE
        outs_t = []
        for kv in range(N_KV):
            qs = jnp.concatenate([q[:, (kv * GQA + g) * HEAD_DIM:(kv * GQA + g + 1) * HEAD_DIM]
                                  for g in range(GQA)], axis=0)
            kh = kwin[:, kv * HEAD_DIM:(kv + 1) * HEAD_DIM]
            vh = vwin[:, kv * HEAD_DIM:(kv + 1) * HEAD_DIM]
            st = _dot_nt(kh, qs) + bias_ref[kv]
            if s == 0:
                st = st + pen_first
            if s == n_sub - 1:
                st = st + pen_last
            sink = sink_ref[kv]
            m = jnp.maximum(jnp.max(st, axis=0, keepdims=True), sink)
            p = jnp.exp(st - m)
            den = jnp.sum(p, axis=0, keepdims=True) + jnp.exp(sink - m)
            outs_t.append(_dot_tn(vh, p.astype(BF16)) / den)
        o = jnp.concatenate(outs_t, axis=0).T
        pieces = [o[g * ATT_BLOCK:(g + 1) * ATT_BLOCK, kv * HEAD_DIM:(kv + 1) * HEAD_DIM]
                  for kv in range(N_KV) for g in range(GQA)]
        o_ref[s * ATT_BLOCK:(s + 1) * ATT_BLOCK, :] = jnp.concatenate(pieces, axis=-1).astype(o_ref.dtype)


def _t5_bucket(rel):
    nb = REL_BUCKETS // 2
    max_exact = nb // 2
    ret = (rel > 0).astype(jnp.int32) * nb
    n = jnp.abs(rel)
    large = max_exact + (jnp.log(jnp.maximum(n, 1).astype(F32) / max_exact)
                         / math.log(REL_MAX_DIST / max_exact) * (nb - max_exact)).astype(jnp.int32)
    large = jnp.minimum(large, nb - 1)
    return ret + jnp.where(n < max_exact, n, large)


def _attn_mixer(h3, rel_bias, sink, n_batch, seq):
    kw = ATT_BLOCK + 2 * WINDOW
    n_rel = kw + ATT_BLOCK - 1
    rel = jnp.arange(n_rel) - (n_rel - 1) // 2
    tb = rel_bias.astype(F32)[_t5_bucket(rel)]
    tb = jnp.where((jnp.abs(rel) <= WINDOW)[:, None], tb, NEG_INF)
    tb = jnp.pad(tb.T, ((0, 0), (0, 1)))
    flat = jnp.tile(tb, (1, ATT_BLOCK))[:, ATT_BLOCK - 1:ATT_BLOCK - 1 + ATT_BLOCK * n_rel]
    bias = flat.reshape(N_HEADS, ATT_BLOCK, n_rel)[:, :, :kw]
    bias = jnp.transpose(bias.reshape(N_KV, GQA, ATT_BLOCK, kw), (0, 3, 1, 2)).reshape(N_KV, kw, GQA * ATT_BLOCK)
    sink_rows = jnp.repeat(sink.astype(F32).reshape(N_KV, GQA), ATT_BLOCK, axis=1)[:, None, :]
    nblk = seq // ATT_ROWS
    sub = ATT_ROWS // ATT_BLOCK
    n128 = seq // ATT_BLOCK
    cur = lambda b, i: b * nblk + i
    prev = lambda b, i: b * n128 + jnp.maximum(i * sub - 1, 0)
    nxt = lambda b, i: b * n128 + jnp.minimum((i + 1) * sub, n128 - 1)
    t = n_batch * seq
    kern = functools.partial(_attn_kernel, seq=seq)
    return pl.pallas_call(
        kern,
        grid=(n_batch, nblk),
        in_specs=[pl.BlockSpec((ATT_ROWS, ATT_DIM), lambda b, i: (cur(b, i), 0)),
                  pl.BlockSpec((ATT_ROWS, KV_DIM), lambda b, i: (cur(b, i), 4)),
                  pl.BlockSpec((ATT_ROWS, KV_DIM), lambda b, i: (cur(b, i), 5)),
                  pl.BlockSpec((ATT_BLOCK, KV_DIM), lambda b, i: (prev(b, i), 4)),
                  pl.BlockSpec((ATT_BLOCK, KV_DIM), lambda b, i: (prev(b, i), 5)),
                  pl.BlockSpec((ATT_BLOCK, KV_DIM), lambda b, i: (nxt(b, i), 4)),
                  pl.BlockSpec((ATT_BLOCK, KV_DIM), lambda b, i: (nxt(b, i), 5)),
                  pl.BlockSpec((N_KV, kw, GQA * ATT_BLOCK), lambda b, i: (0, 0, 0)),
                  pl.BlockSpec((N_KV, 1, GQA * ATT_BLOCK), lambda b, i: (0, 0, 0))],
        out_specs=pl.BlockSpec((ATT_ROWS, ATT_DIM), lambda b, i: (cur(b, i), 0)),
        out_shape=jax.ShapeDtypeStruct((t, ATT_DIM), BF16),
        compiler_params=_cparams(2),
        name="window_gqa",
    )(h3, h3, h3, h3, h3, h3, h3, bias, sink_rows)


def _proj_ln_kernel(a_ref, w_ref, x_ref, g_ref, b_ref, h_ref, h8_ref):
    y = _dot(a_ref[...], w_ref[...])
    hn = _layer_norm(ALPHA * x_ref[...] + y, g_ref[...], b_ref[...])
    h_ref[...] = hn
    _store_token_tiles(h8_ref, hn)


def _proj_ln(a, w, x, g, b):
    t, k = a.shape
    tm = ROW_TILE
    return pl.pallas_call(
        _proj_ln_kernel,
        grid=(t // tm,),
        in_specs=[pl.BlockSpec((tm, k), lambda i: (i, 0)),
                  pl.BlockSpec((k, D_MODEL), lambda i: (0, 0)),
                  pl.BlockSpec((tm, D_MODEL), lambda i: (i, 0)),
                  pl.BlockSpec((1, D_MODEL), lambda i: (0, 0)),
                  pl.BlockSpec((1, D_MODEL), lambda i: (0, 0))],
        out_specs=[pl.BlockSpec((tm, D_MODEL), lambda i: (i, 0)), pl.BlockSpec((tm * 8, 128), lambda i: (i, 0))],
        out_shape=[jax.ShapeDtypeStruct((t, D_MODEL), F32), jax.ShapeDtypeStruct((t * 8, 128), F32)],
        compiler_params=_cparams(1),
        name="proj_ln",
    )(a, w, x, g, b)


def _router_kernel(h_ref, wh_ref, wl_ref, b_ref, o_ref, ot_ref, cnt_ref, run_ref):
    @pl.when(pl.program_id(0) == 0)
    def _():
        run_ref[...] = jnp.zeros_like(run_ref)

    x = h_ref[...]
    xh = x.astype(BF16)
    xl = (x - xh.astype(F32)).astype(BF16)
    wh = wh_ref[...]
    logits = _dot(xh, wh) + (_dot(xl, wh) + _dot(xh, wl_ref[...])) + b_ref[...]
    lane = lax.broadcasted_iota(jnp.int32, logits.shape, 1)
    big = jnp.int32(1 << 20)
    neg = jnp.float32(-jnp.inf)
    is_g = lane < N_GROUPS
    gl = jnp.where(is_g, logits, neg)
    gmax = jnp.max(gl, axis=-1, keepdims=True)
    gsum = jnp.sum(jnp.where(is_g, jnp.exp(gl - gmax), 0.0), axis=-1, keepdims=True)
    grp = jnp.min(jnp.where(is_g & (gl == gmax), lane, big), axis=-1, keepdims=True)
    p_grp = 1.0 / gsum
    lo = N_GROUPS + grp * EPG
    in_grp = (lane >= lo) & (lane < lo + EPG)
    el = jnp.where(in_grp, logits, neg)
    m1 = jnp.max(el, axis=-1, keepdims=True)
    i1 = jnp.min(jnp.where(in_grp & (el == m1), lane, big), axis=-1, keepdims=True)
    el2 = jnp.where(lane == i1, neg, el)
    m2 = jnp.max(el2, axis=-1, keepdims=True)
    i2 = jnp.min(jnp.where(in_grp & (lane != i1) & (el2 == m2), lane, big), axis=-1, keepdims=True)
    e2 = jnp.exp(m2 - m1)
    g1 = p_grp / (1.0 + e2)
    g2 = p_grp * e2 / (1.0 + e2)
    hit1 = lane == i1
    hit2 = lane == i2
    onehot = (hit1 | hit2).astype(BF16)
    tm = onehot.shape[0]
    lower = (lax.broadcasted_iota(jnp.int32, (tm, tm), 1) < lax.broadcasted_iota(jnp.int32, (tm, tm), 0)).astype(BF16)
    before = _dot(lower, onehot) + run_ref[...]
    r1 = jnp.sum(jnp.where(hit1, before, 0.0), axis=-1, keepdims=True)
    r2 = jnp.sum(jnp.where(hit2, before, 0.0), axis=-1, keepdims=True)
    run_ref[...] += jnp.sum(onehot.astype(F32), axis=0, keepdims=True)
    cnt_ref[...] = run_ref[...]
    vals = ((i1 - N_GROUPS).astype(F32), (i2 - N_GROUPS).astype(F32), g1, g2, r1, r2)
    out = jnp.zeros_like(logits)
    for k, v in enumerate(vals):
        out = jnp.where(lane == k, v, out)
    o_ref[...] = out
    ot_ref[...] = out.T[0:8]


def _router(h, w_group, b_group, w_router, b_router):
    t = h.shape[0]
    w = jnp.concatenate([w_group.astype(F32), w_router.astype(F32)], axis=1)
    w = jnp.pad(w, ((0, 0), (0, 128 - w.shape[1])))
    wh = w.astype(BF16)
    wl = (w - wh.astype(F32)).astype(BF16)
    bias = jnp.pad(jnp.concatenate([b_group.astype(F32), b_router.astype(F32)]), (0, 128 - 36))[None]
    tm = ROW_TILE
    out, out_t, cnt = pl.pallas_call(
        _router_kernel,
        grid=(t // tm,),
        in_specs=[pl.BlockSpec((tm, D_MODEL), lambda i: (i, 0)),
                  pl.BlockSpec((D_MODEL, 128), lambda i: (0, 0)),
                  pl.BlockSpec((D_MODEL, 128), lambda i: (0, 0)),
                  pl.BlockSpec((1, 128), lambda i: (0, 0))],
        out_specs=[pl.BlockSpec((tm, 128), lambda i: (i, 0)),
                   pl.BlockSpec((8, tm), lambda i: (0, i)),
                   pl.BlockSpec((1, 128), lambda i: (0, 0))],
        out_shape=[jax.ShapeDtypeStruct((t, 128), F32), jax.ShapeDtypeStruct((8, t), F32),
                   jax.ShapeDtypeStruct((1, 128), F32)],
        scratch_shapes=[pltpu.VMEM((1, 128), F32)],
        compiler_params=_cparams(1),
        name="moe_router",
    )(h, wh, wl, bias)
    eid = out_t[0:2].astype(jnp.int32)
    rank = out_t[4:6].astype(jnp.int32)
    counts = cnt[0, N_GROUPS:N_GROUPS + N_EXPERTS].astype(jnp.int32)
    return out, eid, rank, counts


def _token_tile(ref, tok):
    return ref.at[pl.ds(pl.multiple_of(tok * 8, 8), 8)]


def _experts_kernel(be_ref, nu_ref, src_hbm, h8_hbm, w1_ref, w3_ref, w2_ref, ys_ref,
                    w13_s, w2_s, xbuf, idx_smem, sem, isem):
    i = pl.program_id(0)
    n_used = nu_ref[0]
    used = i < n_used
    slot = i % N_ROW_BUFS

    blk_per_fetch = IDX_FETCH // MOE_ROWS

    def gather_block(blk, slot_):
        @pl.when(blk % blk_per_fetch == 0)
        def _():
            off = pl.multiple_of(blk * MOE_ROWS, IDX_FETCH)
            cp = pltpu.make_async_copy(src_hbm.at[pl.ds(off, IDX_FETCH)], idx_smem, isem)
            cp.start()
            cp.wait()
        base = (blk % blk_per_fetch) * MOE_ROWS
        def issue(r8, _):
            for u in range(8):
                r = r8 * 8 + u
                pltpu.make_async_copy(_token_tile(h8_hbm, idx_smem[base + r]), _token_tile(xbuf.at[slot_], r),
                                      sem.at[slot_]).start(priority=u % 2)
            return 0
        lax.fori_loop(0, MOE_ROWS // 8, issue, 0)

    @pl.when(i == 0)
    def _():
        gather_block(0, 0)
        @pl.when(n_used > 1)
        def _():
            gather_block(1, 1)

    @pl.when(i + 2 < n_used)
    def _():
        gather_block(i + 2, (i + 2) % N_ROW_BUFS)

    @pl.when(used)
    def _():
        pltpu.make_async_copy(h8_hbm.at[pl.ds(0, MOE_ROWS * 8)], xbuf.at[slot], sem.at[slot]).wait()

    prev = be_ref[jnp.maximum(i - 1, 0)]
    fresh = (i == 0) | (be_ref[i] != prev)

    @pl.when(used & fresh)
    def _():
        w13_s[:, :D_EXPERT] = w1_ref[...].astype(BF16)
        w13_s[:, D_EXPERT:] = w3_ref[...].astype(BF16)
        w2_s[...] = w2_ref[...].astype(BF16)

    @pl.when(used)
    def _():
        x = _load_token_tiles(xbuf.at[slot], MOE_ROWS).astype(BF16)
        h = _dot(x, w13_s[...])
        h1 = h[:, :D_EXPERT]
        hdn = (h1 * jax.nn.sigmoid(h1)) * h[:, D_EXPERT:]
        _store_token_tiles(ys_ref, _dot(hdn.astype(BF16), w2_s[...]))

    @pl.when(jnp.logical_not(used))
    def _():
        ys_ref[...] = jnp.zeros_like(ys_ref)


def _experts(h8, src, blk_e, n_used, w1, w3, w2, layer, cap):
    nblk = cap // MOE_ROWS
    return pl.pallas_call(
        _experts_kernel,
        grid_spec=pltpu.PrefetchScalarGridSpec(
            num_scalar_prefetch=2,
            grid=(nblk,),
            in_specs=[pl.BlockSpec(memory_space=pl.ANY),
                      pl.BlockSpec(memory_space=pl.ANY),
                      pl.BlockSpec((None, None, D_MODEL, D_EXPERT), lambda i, be, nu: (layer, be[i], 0, 0)),
                      pl.BlockSpec((None, None, D_MODEL, D_EXPERT), lambda i, be, nu: (layer, be[i], 0, 0)),
                      pl.BlockSpec((None, None, D_EXPERT, D_MODEL), lambda i, be, nu: (layer, be[i], 0, 0))],
            out_specs=pl.BlockSpec((MOE_ROWS * 8, 128), lambda i, be, nu: (i, 0)),
            scratch_shapes=[pltpu.VMEM((D_MODEL, 2 * D_EXPERT), BF16),
                            pltpu.VMEM((D_EXPERT, D_MODEL), BF16),
                            pltpu.VMEM((N_ROW_BUFS, MOE_ROWS * 8, 128), F32),
                            pltpu.SMEM((IDX_FETCH,), jnp.int32),
                            pltpu.SemaphoreType.DMA((N_ROW_BUFS,)),
                            pltpu.SemaphoreType.DMA(())]),
        out_shape=jax.ShapeDtypeStruct((cap * 8, 128), F32),
        compiler_params=_cparams(1),
        name="moe_experts",
    )(blk_e, n_used, src, h8, w1, w3, w2)


def _combine_kernel(dst_hbm, ys_hbm, h_ref, gate_ref, g_ref, b_ref, o_ref, idx_smem, y0_ref, y1_ref, sem, isem):
    i = pl.program_id(0)
    tm = ROW_TILE
    slot = i % 2

    def gather_tile(tile, slot_):
        cp = pltpu.make_async_copy(dst_hbm.at[pl.ds(tile * 2 * tm, 2 * tm)], idx_smem, isem)
        cp.start()
        cp.wait()
        def issue(t, _):
            pltpu.make_async_copy(_token_tile(ys_hbm, idx_smem[t]), _token_tile(y0_ref.at[slot_], t),
                                  sem.at[slot_]).start(priority=0)
            pltpu.make_async_copy(_token_tile(ys_hbm, idx_smem[tm + t]), _token_tile(y1_ref.at[slot_], t),
                                  sem.at[slot_]).start(priority=1)
            return 0
        lax.fori_loop(0, tm, issue, 0, unroll=8)

    @pl.when(i == 0)
    def _():
        gather_tile(0, 0)

    @pl.when(i + 1 < pl.num_programs(0))
    def _():
        gather_tile(i + 1, 1 - slot)

    pltpu.make_async_copy(ys_hbm.at[pl.ds(0, tm * 8)], y0_ref.at[slot], sem.at[slot]).wait()
    pltpu.make_async_copy(ys_hbm.at[pl.ds(0, tm * 8)], y1_ref.at[slot], sem.at[slot]).wait()
    gate = gate_ref[...]
    y = (_load_token_tiles(y0_ref.at[slot], tm) * gate[:, 2:3]
         + _load_token_tiles(y1_ref.at[slot], tm) * gate[:, 3:4])
    o_ref[...] = _layer_norm(ALPHA * h_ref[...] + y, g_ref[...], b_ref[...])


def _combine_ln(dst, ys, h, rout, ln_g, ln_b):
    t = h.shape[0]
    tm = ROW_TILE
    return pl.pallas_call(
        _combine_kernel,
        grid_spec=pltpu.PrefetchScalarGridSpec(
            num_scalar_prefetch=0,
            grid=(t // tm,),
            in_specs=[pl.BlockSpec(memory_space=pl.ANY),
                      pl.BlockSpec(memory_space=pl.ANY),
                      pl.BlockSpec((tm, D_MODEL), lambda i: (i, 0)),
                      pl.BlockSpec((tm, 128), lambda i: (i, 0)),
                      pl.BlockSpec((1, D_MODEL), lambda i: (0, 0)),
                      pl.BlockSpec((1, D_MODEL), lambda i: (0, 0))],
            out_specs=pl.BlockSpec((tm, D_MODEL), lambda i: (i, 0)),
            scratch_shapes=[pltpu.SMEM((2 * tm,), jnp.int32),
                            pltpu.VMEM((2, tm * 8, 128), F32),
                            pltpu.VMEM((2, tm * 8, 128), F32),
                            pltpu.SemaphoreType.DMA((2,)),
                            pltpu.SemaphoreType.DMA(())]),
        out_shape=jax.ShapeDtypeStruct((t, D_MODEL), F32),
        compiler_params=_cparams(1),
        name="moe_combine_ln",
    )(dst, ys, h, rout, ln_g, ln_b)


def _moe_layer(h, h8, w_group, b_group, w_router, b_router, w1, w3, w2, layer, ln_g, ln_b):
    t = h.shape[0]
    rout, eid, rank, counts = _router(h, w_group, b_group, w_router, b_router)
    padded = (counts + MOE_ROWS - 1) // MOE_ROWS * MOE_ROWS
    pad_end = jnp.cumsum(padded)
    pad_start = pad_end - padded
    start_of = jnp.sum(jnp.where(eid[..., None] == jnp.arange(N_EXPERTS), pad_start, 0), axis=-1)
    dst = (start_of + rank).astype(jnp.int32)
    cap = 2 * t + N_EXPERTS * MOE_ROWS
    nblk = cap // MOE_ROWS
    tok = jnp.broadcast_to(jnp.arange(t, dtype=jnp.int32), (2, t))
    src = jnp.zeros((cap,), jnp.int32).at[dst.reshape(-1)].set(tok.reshape(-1), unique_indices=True)
    blk_e = jnp.minimum(jnp.sum(pad_end[None, :] <= (jnp.arange(nblk) * MOE_ROWS)[:, None], axis=1),
                        N_EXPERTS - 1).astype(jnp.int32)
    n_used = (pad_end[-1] // MOE_ROWS).astype(jnp.int32).reshape(1)
    dst_tiles = jnp.transpose(dst.reshape(2, t // ROW_TILE, ROW_TILE), (1, 0, 2)).reshape(-1)
    ys = _experts(h8, src, blk_e, n_used, w1, w3, w2, layer, cap)
    return _combine_ln(dst_tiles, ys, h, rout, ln_g, ln_b)


def kernel(x, ln_mix_g, ln_mix_b, ln_ffn_g, ln_ffn_b, ab_w_in, s5_lam_re, s5_lam_im, s5_log_dt, s5_b_re, s5_b_im,
           s5_c_re, s5_c_im, s5_d, s5_glu_w, s5_glu_b, gla_gate_w, gla_gate_b, gla_norm_g, ab_w_out, c_w_in,
           c_sink, c_w_out, rel_bias, moe_w_group, moe_b_group, moe_w_router, moe_b_router, moe_w1, moe_w3, moe_w2):
    n_batch, seq, _ = x.shape
    t = n_batch * seq
    xt = x.reshape(t, D_MODEL)
    row = lambda v: v.astype(F32).reshape(1, -1)

    w_in0 = jnp.pad(ab_w_in[0], ((0, 0), (0, AB_IN_PAD - AB_IN))).astype(BF16)
    h0, u_t = _inproj0(xt, w_in0)
    n_steps = seq // (S5_SEGS * S5_CHUNK)
    kcat, r1e, r2, coef, ptab = _s5_prep(s5_lam_re[0], s5_lam_im[0], s5_log_dt[0], s5_b_re[0], s5_b_im[0],
                                         s5_c_re[0], s5_c_im[0], s5_d[0], n_steps)
    ya = _s5_mixer(u_t, kcat, r1e, r2, coef, ptab, n_batch, n_steps)
    gw = gla_gate_w[0].astype(F32)
    wg = jnp.zeros((128, 2 * GLA_KEY), F32)
    wg = wg.at[0:GLA_RANK, 0:GLA_KEY].set(gw[0]).at[GLA_RANK:2 * GLA_RANK, GLA_KEY:].set(gw[1]).astype(BF16)
    bg = gla_gate_b[0].astype(F32).reshape(1, 2 * GLA_KEY)
    o_f, o_b = _gla_mixer(h0, wg, bg, n_batch, seq)
    h, h8 = _mix0_out(ya, o_f, o_b, h0, xt, s5_glu_w[0].astype(BF16), row(s5_glu_b[0]), row(gla_norm_g[0]),
                      ab_w_out[0].astype(BF16), row(ln_mix_g[0]), row(ln_mix_b[0]))
    h = _moe_layer(h, h8, moe_w_group[0], moe_b_group[0], moe_w_router[0], moe_b_router[0],
                   moe_w1, moe_w3, moe_w2, 0, row(ln_ffn_g[0]), row(ln_ffn_b[0]))

    h3 = _matmul(h, c_w_in[0].astype(BF16), BF16)
    o = _attn_mixer(h3, rel_bias, c_sink[0], n_batch, seq)
    h, h8 = _proj_ln(o, c_w_out[0].astype(BF16), h, row(ln_mix_g[1]), row(ln_mix_b[1]))
    h = _moe_layer(h, h8, moe_w_group[1], moe_b_group[1], moe_w_router[1], moe_b_router[1],
                   moe_w1, moe_w3, moe_w2, 1, row(ln_ffn_g[1]), row(ln_ffn_b[1]))
    return h.reshape(n_batch, seq, D_MODEL)
```

```python
import functools
import math

import jax
import jax.numpy as jnp
from jax import lax
from jax.experimental import pallas as pl
from jax.experimental.pallas import tpu as pltpu

F32 = jnp.float32
BF16 = jnp.bfloat16

D_MODEL = 1024
S5_WIDTH = 512
S5_GROUP_CH = 16
S5_GROUPS = 32
S5_STATE = 64
GLA_HEADS = 4
GLA_DV = 128
GLA_DK = 64
GLA_KEY = 256
GLA_WIDTH = 512
GLA_RANK = 16
GLA_TAU = 16.0
GLA_CHUNK = 64
AB_IN = 2080
AB_IN_PAD = 2176
HEAD_DIM = 64
N_HEADS = 16
N_KV = 4
GQA = 4
ATT_DIM = 1024
KV_DIM = 256
WINDOW = 128
ATT_BLOCK = 128
ATT_SCALE = HEAD_DIM ** -0.5
REL_BUCKETS = 32
REL_MAX_DIST = 128
NEG_INF = -1e30
N_GROUPS = 4
EPG = 8
N_EXPERTS = 32
D_EXPERT = 512
LN_EPS = 1e-5
RMS_EPS = 1e-6
DEPTH = 2
ALPHA = (2 * DEPTH) ** 0.25

S5_CHUNK = 16
S5_SEGS = 8
S5_SEG_PAD = 4
ROW_TILE = 512
GLA_ROWS = 256
ATT_ROWS = 512
MOE_ROWS = 512
IDX_FETCH = 1024
N_ROW_BUFS = 3
MXU_COLS = 256
VMEM_LIMIT = 56 * 1024 * 1024


def _cparams(n_axes):
    return pltpu.CompilerParams(dimension_semantics=("arbitrary",) * n_axes,
                                vmem_limit_bytes=VMEM_LIMIT)


def _dot(a, b):
    return jnp.dot(a, b, preferred_element_type=F32)


def _dot_nt(a, b):
    return lax.dot_general(a, b, (((1,), (1,)), ((), ())), preferred_element_type=F32)


def _dot_tn(a, b):
    return lax.dot_general(a, b, (((0,), (0,)), ((), ())), preferred_element_type=F32)


def _layer_norm(r, g, b):
    mu = jnp.mean(r, axis=-1, keepdims=True)
    c = r - mu
    var = jnp.mean(c * c, axis=-1, keepdims=True)
    return c * lax.rsqrt(var + LN_EPS) * g + b


def _store_token_tiles(ref, val):
    n = val.shape[0]
    for k in range(D_MODEL // 128):
        ref[pl.ds(k, n, stride=8), :] = val[:, k * 128:(k + 1) * 128]


def _load_token_tiles(ref, n):
    return jnp.concatenate([ref[pl.ds(k, n, stride=8), :] for k in range(D_MODEL // 128)], axis=-1)


def _mm_kernel(x_ref, w_ref, o_ref):
    o_ref[...] = _dot(x_ref[...].astype(BF16), w_ref[...]).astype(o_ref.dtype)


def _matmul(x, w, out_dtype):
    m, k = x.shape
    n = w.shape[1]
    return pl.pallas_call(
        _mm_kernel,
        grid=(m // ROW_TILE,),
        in_specs=[pl.BlockSpec((ROW_TILE, k), lambda i: (i, 0)),
                  pl.BlockSpec((k, n), lambda i: (0, 0))],
        out_specs=pl.BlockSpec((ROW_TILE, n), lambda i: (i, 0)),
        out_shape=jax.ShapeDtypeStruct((m, n), out_dtype),
        compiler_params=_cparams(1),
        name="dense_matmul",
    )(x, w)


def _s5_prep(lam_re, lam_im, log_dt, b_re, b_im, c_re, c_im, d, n_steps):
    tc = S5_CHUNK
    g, p, c = S5_GROUPS, S5_STATE, S5_GROUP_CH
    lr = jnp.minimum(lam_re.astype(F32), -1e-4)
    li = lam_im.astype(F32)
    dt = jnp.exp(log_dt.astype(F32))[..., None]
    mag = jnp.exp(lr * dt)
    ar = mag * jnp.cos(li * dt)
    ai = mag * jnp.sin(li * dt)
    den = lr * lr + li * li
    nr = ar - 1.0
    coef_r = (nr * lr + ai * li) / den
    coef_i = (ai * lr - nr * li) / den
    br_ = b_re.astype(F32)
    bi_ = b_im.astype(F32)
    bbr = coef_r[..., None] * br_ - coef_i[..., None] * bi_
    bbi = coef_r[..., None] * bi_ + coef_i[..., None] * br_
    cr = c_re.astype(F32)
    ci = c_im.astype(F32)
    bbr_t = jnp.swapaxes(bbr, -1, -2)
    bbi_t = jnp.swapaxes(bbi, -1, -2)
    npair = g // 2

    def apow(n):
        nn = n.astype(F32)[:, None, None, None]
        m_ = jnp.exp(nn * (lr * dt)[None])
        ang = nn * (li * dt)[None]
        return m_ * jnp.cos(ang), m_ * jnp.sin(ang)

    pr, pi = apow(jnp.arange(tc + 1))
    hr = pr[:, :, :, None, :] * bbr_t[None] - pi[:, :, :, None, :] * bbi_t[None]
    hi = pr[:, :, :, None, :] * bbi_t[None] + pi[:, :, :, None, :] * bbr_t[None]
    kk = jnp.einsum('dgoq,jdgcq->jdgco', cr, hr) - jnp.einsum('dgoq,jdgcq->jdgco', ci, hi)
    k0 = kk[0, 0] + kk[0, 1] + d.astype(F32).reshape(g, c)[:, :, None] * jnp.eye(c, dtype=F32)
    slab = jnp.concatenate([kk[tc - 1:0:-1, 1], k0[None], kk[1:tc, 0]], axis=0)
    kcat = jnp.transpose(slab, (1, 2, 0, 3)).reshape(g, c, (2 * tc - 1) * c)
    kcat = jnp.pad(kcat, ((0, 0), (0, 0), (0, 2 * tc * c - kcat.shape[-1]))).reshape(npair, 2, c, 2 * tc * c)

    def pair_blockdiag(m):
        m2 = m.reshape((npair, 2) + m.shape[1:])
        z = jnp.zeros_like(m2[:, 0])
        return jnp.concatenate([jnp.concatenate([m2[:, 0], z], axis=-1),
                                jnp.concatenate([z, m2[:, 1]], axis=-1)], axis=1)

    def w_plane(h_, lag_sel, dirn):
        w = jnp.transpose(h_[lag_sel, dirn], (1, 0, 2, 3)).reshape(g, tc * c, p)
        return pair_blockdiag(w)
    s_fwd = tc - 1 - jnp.arange(tc)
    s_bwd = jnp.arange(tc)
    r1e = jnp.concatenate([w_plane(hr, s_fwd, 0), w_plane(hi, s_fwd, 0), w_plane(hr, s_bwd, 1), w_plane(hi, s_bwd, 1)],
                          axis=2)

    lane = jnp.arange(tc * c)
    rep_t = (lane[None, :] // c == jnp.arange(tc)[:, None]).astype(F32)
    tile_co = (lane[None, :] % c == jnp.arange(c)[:, None]).astype(F32)
    spread = lambda a, sel: jnp.dot(a, sel, precision=lax.Precision.HIGHEST)
    def readout(e, dirn):
        pr_l = spread(jnp.transpose(pr[e, dirn], (1, 2, 0)).reshape(g * p, tc), rep_t)
        pi_l = spread(jnp.transpose(pi[e, dirn], (1, 2, 0)).reshape(g * p, tc), rep_t)
        cr_l = spread(jnp.transpose(cr[dirn], (0, 2, 1)).reshape(g * p, c), tile_co)
        ci_l = spread(jnp.transpose(ci[dirn], (0, 2, 1)).reshape(g * p, c), tile_co)
        zr = (cr_l * pr_l - ci_l * pi_l).reshape(g, p, tc * c)
        zi = (cr_l * pi_l + ci_l * pr_l).reshape(g, p, tc * c)
        return pair_blockdiag(zr), pair_blockdiag(-zi)
    vf_r, vf_i = readout(jnp.arange(tc) + 1, 0)
    vb_r, vb_i = readout(tc - jnp.arange(tc), 1)
    r2 = jnp.concatenate([vf_r, vf_i, vb_r, vb_i], axis=1)

    steps = jnp.arange(n_steps)
    a16r, a16i = apow(jnp.array([tc]))
    apr, api = apow(steps * tc)
    anr, ani = apow(jnp.array([tc * n_steps]))
    def lanes(x):
        n_ = x.shape[0]
        return jnp.transpose(x.reshape(n_, npair, 2 * p), (1, 0, 2))
    coef = jnp.concatenate([lanes(a16r[:, 0]), lanes(a16i[:, 0]), lanes(a16r[:, 1]), lanes(a16i[:, 1]),
                            lanes(anr[:, 0]), lanes(ani[:, 0]), lanes(anr[:, 1]), lanes(ani[:, 1])], axis=1)
    rev = n_steps - 1 - steps
    ptab = jnp.stack([lanes(apr[:, 0]), lanes(api[:, 0]), lanes(apr[rev, 1]), lanes(api[rev, 1])], axis=1)
    return kcat, r1e.astype(BF16), r2.astype(BF16), coef, ptab


def _gelu_tanh(x):
    return 0.5 * x * (1.0 + jnp.tanh(math.sqrt(2.0 / math.pi) * (x + 0.044715 * (x * x * x))))


def _s5_kernel(u_ref, kcat_ref, r1e_ref, r2_ref, coef_ref, ptab_ref, y_ref, kt_ref, yi_ref, e_ref, s_ref,
               *, n_batch, n_steps):
    pitch = n_steps + S5_SEG_PAD
    blk = S5_CHUNK * S5_GROUP_CH
    kt_ref[...] = jnp.zeros_like(kt_ref)
    for gi in range(2):
        slab = kcat_ref[gi]
        for s in range(S5_CHUNK):
            off = (S5_CHUNK - 1 - s) * S5_GROUP_CH
            kt_ref[gi * blk + s * S5_GROUP_CH:gi * blk + (s + 1) * S5_GROUP_CH, gi * blk:(gi + 1) * blk] = (
                slab[:, off:off + blk].astype(BF16))
    u = u_ref[...]
    yi_ref[...] = _dot(u, kt_ref[...])
    e = _dot(u, r1e_ref[...])
    n_seg = n_batch * S5_SEGS
    for k in range(4):
        for sg in range(n_seg):
            e_ref[k, sg * pitch:sg * pitch + n_steps, :] = e[sg * n_steps:(sg + 1) * n_steps, k * 128:(k + 1) * 128]
    coef = coef_ref[...]
    a16fr, a16fi, a16br, a16bi = coef[0:1], coef[1:2], coef[2:3], coef[3:4]
    anfr, anfi, anbr, anbi = coef[4:5], coef[5:6], coef[6:7], coef[7:8]
    zero = jnp.zeros((S5_SEGS, 128), F32)
    sub = lax.broadcasted_iota(jnp.int32, (S5_SEGS, 128), 0)

    def seg_rows(b, j):
        return pl.ds(b * S5_SEGS * pitch + j, S5_SEGS, stride=pitch)

    def local_step(j, carry):
        out = []
        for b in range(n_batch):
            sfr, sfi, sbr, sbi = carry[b]
            rf = seg_rows(b, j)
            rb = seg_rows(b, n_steps - 1 - j)
            s_ref[0, rf, :] = sfr
            s_ref[1, rf, :] = sfi
            s_ref[2, rb, :] = sbr
            s_ref[3, rb, :] = sbi
            efr = e_ref[0, rf, :]
            efi = e_ref[1, rf, :]
            ebr = e_ref[2, rb, :]
            ebi = e_ref[3, rb, :]
            out.append((a16fr * sfr - a16fi * sfi + efr, a16fr * sfi + a16fi * sfr + efi,
                        a16br * sbr - a16bi * sbi + ebr, a16br * sbi + a16bi * sbr + ebi))
        return tuple(out)

    ends = lax.fori_loop(0, n_steps, local_step, tuple((zero, zero, zero, zero) for _ in range(n_batch)))

    carries = []
    for b in range(n_batch):
        efr, efi, ebr, ebi = ends[b]
        cfr, cfi, cbr, cbi = zero, zero, zero, zero
        for _ in range(S5_SEGS - 1):
            tfr = anfr * cfr - anfi * cfi + efr
            tfi = anfr * cfi + anfi * cfr + efi
            cfr = jnp.where(sub == 0, 0.0, pltpu.roll(tfr, 1, 0))
            cfi = jnp.where(sub == 0, 0.0, pltpu.roll(tfi, 1, 0))
            tbr = anbr * cbr - anbi * cbi + ebr
            tbi = anbr * cbi + anbi * cbr + ebi
            cbr = jnp.where(sub == S5_SEGS - 1, 0.0, pltpu.roll(tbr, S5_SEGS - 1, 0))
            cbi = jnp.where(sub == S5_SEGS - 1, 0.0, pltpu.roll(tbi, S5_SEGS - 1, 0))
        carries.append((cfr, cfi, cbr, cbi))

    def fix_step(j, _):
        pfr = ptab_ref[0, pl.ds(j, 1), :]
        pfi = ptab_ref[1, pl.ds(j, 1), :]
        pbr = ptab_ref[2, pl.ds(j, 1), :]
        pbi = ptab_ref[3, pl.ds(j, 1), :]
        for b in range(n_batch):
            cfr, cfi, cbr, cbi = carries[b]
            rj = seg_rows(b, j)
            s_ref[0, rj, :] = s_ref[0, rj, :] + (pfr * cfr - pfi * cfi)
            s_ref[1, rj, :] = s_ref[1, rj, :] + (pfr * cfi + pfi * cfr)
            s_ref[2, rj, :] = s_ref[2, rj, :] + (pbr * cbr - pbi * cbi)
            s_ref[3, rj, :] = s_ref[3, rj, :] + (pbr * cbi + pbi * cbr)
        return 0

    lax.fori_loop(0, n_steps, fix_step, 0)

    s_all = jnp.concatenate(
        [jnp.concatenate([s_ref[k, sg * pitch:sg * pitch + n_steps, :] for sg in range(n_seg)], axis=0)
         for k in range(4)], axis=-1)
    y = yi_ref[...] + _dot(s_all.astype(BF16), r2_ref[...])
    y_ref[...] = _gelu_tanh(y).astype(y_ref.dtype)


def _s5_mixer(u_t, kcat, r1e, r2, coef, ptab, n_batch, n_steps):
    npair, m, _ = u_t.shape
    scan_rows = n_batch * S5_SEGS * (n_steps + S5_SEG_PAD)
    kern = functools.partial(_s5_kernel, n_batch=n_batch, n_steps=n_steps)
    return pl.pallas_call(
        kern,
        grid=(npair,),
        in_specs=[pl.BlockSpec((None, m, 512), lambda i: (i, 0, 0)),
                  pl.BlockSpec((None, 2, S5_GROUP_CH, 512), lambda i: (i, 0, 0, 0)),
                  pl.BlockSpec((None, 512, 512), lambda i: (i, 0, 0)),
                  pl.BlockSpec((None, 512, 512), lambda i: (i, 0, 0)),
                  pl.BlockSpec((None, 8, 128), lambda i: (i, 0, 0)),
                  pl.BlockSpec((None, 4, n_steps, 128), lambda i: (i, 0, 0, 0))],
        out_specs=pl.BlockSpec((None, m, 512), lambda i: (i, 0, 0)),
        out_shape=jax.ShapeDtypeStruct((npair, m, 512), BF16),
        scratch_shapes=[pltpu.VMEM((512, 512), BF16), pltpu.VMEM((m, 512), F32),
                        pltpu.VMEM((4, scan_rows, 128), F32), pltpu.VMEM((4, scan_rows, 128), F32)],
        compiler_params=_cparams(1),
        name="s5_scan",
    )(u_t, kcat, r1e, r2, coef, ptab)


def _inproj0_kernel(x_ref, w_ref, h_ref, ut_ref, u_s):
    h = _dot(x_ref[...].astype(BF16), w_ref[...])
    h_ref[...] = h[:, S5_WIDTH:].astype(h_ref.dtype)
    n_lane_blk = S5_WIDTH // 128
    for k in range(n_lane_blk):
        u_s[k] = h[:, k * 128:(k + 1) * 128]
    n_chunk = ROW_TILE // S5_CHUNK
    rows = [jnp.concatenate([u_s[k, pl.ds(s, n_chunk, stride=S5_CHUNK), :] for k in range(n_lane_blk)], axis=-1)
            for s in range(S5_CHUNK)]
    ch = S5_GROUP_CH
    for p in range(S5_GROUPS // 2):
        ut_ref[p] = jnp.concatenate([r[:, (2 * p + gi) * ch:(2 * p + gi + 1) * ch] for gi in range(2) for r in rows],
                                    axis=-1).astype(ut_ref.dtype)


def _inproj0(x, w):
    t, k = x.shape
    n = w.shape[1]
    tm = ROW_TILE
    npair = S5_GROUPS // 2
    return pl.pallas_call(
        _inproj0_kernel,
        grid=(t // tm,),
        in_specs=[pl.BlockSpec((tm, k), lambda i: (i, 0)),
                  pl.BlockSpec((k, n), lambda i: (0, 0))],
        out_specs=[pl.BlockSpec((tm, n - S5_WIDTH), lambda i: (i, 0)),
                   pl.BlockSpec((npair, tm // S5_CHUNK, 512), lambda i: (0, i, 0))],
        out_shape=[jax.ShapeDtypeStruct((t, n - S5_WIDTH), BF16),
                   jax.ShapeDtypeStruct((npair, t // S5_CHUNK, 512), BF16)],
        scratch_shapes=[pltpu.VMEM((S5_WIDTH // 128, tm, 128), F32)],
        compiler_params=_cparams(1),
        name="inproj0",
    )(x, w)


def _gla_kernel(qf_ref, kf_ref, vf_ref, lf_ref, qb_ref, kb_ref, vb_ref, lb_ref, wg_ref, bg_ref,
                of_ref, ob_ref, st_ref):
    @pl.when(pl.program_id(1) == 0)
    def _():
        st_ref[...] = jnp.zeros_like(st_ref)

    n_chunks = GLA_ROWS // GLA_CHUNK
    row = lax.broadcasted_iota(jnp.int32, (GLA_CHUNK, GLA_CHUNK), 0)
    col = lax.broadcasted_iota(jnp.int32, (GLA_CHUNK, GLA_CHUNK), 1)
    mask_f = col <= row
    mask_b = col > row
    brow = lax.broadcasted_iota(jnp.int32, (GLA_ROWS, GLA_ROWS), 0)
    bcol = lax.broadcasted_iota(jnp.int32, (GLA_ROWS, GLA_ROWS), 1)
    shift = GLA_CHUNK.bit_length() - 1
    same = (brow >> shift) == (bcol >> shift)
    ones_blk = same.astype(BF16)
    tri_f = (same & (bcol <= brow)).astype(BF16)
    tri_b = (same & (bcol >= brow)).astype(BF16)
    wg = wg_ref[...]
    bg = bg_ref[...]

    def direction(q_ref, k_ref, v_ref, l_ref, o_ref, d):
        z = _dot(l_ref[...], wg[:, d * GLA_KEY:(d + 1) * GLA_KEY]) + bg[:, d * GLA_KEY:(d + 1) * GLA_KEY]
        log_a = jax.nn.log_sigmoid(z) * (1.0 / GLA_TAU)
        la_hi = log_a.astype(BF16)
        la_lo = (log_a - la_hi.astype(F32)).astype(BF16)
        tri = tri_f if d == 0 else tri_b
        bc = _dot(tri, la_hi) + _dot(tri, la_lo)
        tot = _dot(ones_blk, la_hi) + _dot(ones_blk, la_lo)
        q = q_ref[...].astype(F32) * (GLA_DK ** -0.5)
        k = k_ref[...].astype(F32)
        qd_all = (q * jnp.exp(bc)).astype(BF16)
        kd_all = (k * jnp.exp(-bc)).astype(BF16)
        kc_all = (k * jnp.exp(tot - bc)).astype(BF16)
        decay_all = jnp.exp(tot)
        order = range(n_chunks) if d == 0 else range(n_chunks - 1, -1, -1)
        mask = mask_f if d == 0 else mask_b
        states = [st_ref[d, h] for h in range(GLA_HEADS)]
        for c in order:
            sl = slice(c * GLA_CHUNK, (c + 1) * GLA_CHUNK)
            qd, kd, kc = qd_all[sl], kd_all[sl], kc_all[sl]
            decay = decay_all[c * GLA_CHUNK:c * GLA_CHUNK + 1]
            v = v_ref[sl, :]
            outs = []
            for h in range(GLA_HEADS):
                ks = slice(h * GLA_DK, (h + 1) * GLA_DK)
                vs = slice(h * GLA_DV, (h + 1) * GLA_DV)
                s = jnp.where(mask, _dot_nt(qd[:, ks], kd[:, ks]), 0.0).astype(BF16)
                st = states[h]
                o = _dot(s, v[:, vs]) + _dot_nt(qd[:, ks], st.astype(BF16))
                states[h] = st * decay[:, ks] + _dot_tn(v[:, vs], kc[:, ks])
                outs.append(o)
            o_ref[sl, :] = jnp.concatenate(outs, axis=-1).astype(o_ref.dtype)
        for h in range(GLA_HEADS):
            st_ref[d, h] = states[h]

    direction(qf_ref, kf_ref, vf_ref, lf_ref, of_ref, 0)
    direction(qb_ref, kb_ref, vb_ref, lb_ref, ob_ref, 1)


def _gla_mixer(h0, wg, bg, n_batch, seq):
    nb = seq // GLA_ROWS
    r = GLA_ROWS
    fwd = lambda b, i: b * nb + i
    bwd = lambda b, i: b * nb + (nb - 1 - i)
    def spec(width, colblk, rowfn):
        return pl.BlockSpec((r, width), lambda b, i: (rowfn(b, i), colblk))
    in_specs = [spec(256, 0, fwd), spec(256, 1, fwd), spec(512, 1, fwd), spec(128, 12, fwd),
                spec(256, 0, bwd), spec(256, 1, bwd), spec(512, 1, bwd), spec(128, 12, bwd),
                pl.BlockSpec((128, 512), lambda b, i: (0, 0)),
                pl.BlockSpec((1, 512), lambda b, i: (0, 0))]
    out_specs = [pl.BlockSpec((r, 512), lambda b, i: (fwd(b, i), 0)),
                 pl.BlockSpec((r, 512), lambda b, i: (bwd(b, i), 0))]
    t = n_batch * seq
    return pl.pallas_call(
        _gla_kernel,
        grid=(n_batch, nb),
        in_specs=in_specs,
        out_specs=out_specs,
        out_shape=[jax.ShapeDtypeStruct((t, 512), BF16), jax.ShapeDtypeStruct((t, 512), BF16)],
        scratch_shapes=[pltpu.VMEM((2, GLA_HEADS, GLA_DV, GLA_DK), F32)],
        compiler_params=_cparams(2),
        name="gla_chunked",
    )(h0, h0, h0, h0, h0, h0, h0, h0, wg, bg)


def _mix0_out_kernel(y_ref, of_ref, ob_ref, go_ref, x_ref, gw_ref, gb_ref, ng_ref, wo_ref, lg_ref, lb_ref,
                     h8_ref, ya_s):
    n_chunk = ROW_TILE // S5_CHUNK
    ys = [y_ref[p].astype(F32) for p in range(S5_GROUPS // 2)]
    n_lane_blk = S5_WIDTH // 128
    ch = S5_GROUP_CH
    blk = S5_CHUNK * ch
    for t in range(S5_CHUNK):
        for k in range(n_lane_blk):
            ya_s[k, pl.ds(t, n_chunk, stride=S5_CHUNK), :] = jnp.concatenate(
                [y[:, gi * blk + t * ch:gi * blk + (t + 1) * ch] for y in ys[4 * k:4 * k + 4] for gi in range(2)],
                axis=-1)
    yaf = jnp.concatenate([ya_s[k] for k in range(n_lane_blk)], axis=-1)
    ya = yaf.astype(BF16)
    gate = _dot(ya, gw_ref[...]) + gb_ref[...]
    ya2 = yaf * jax.nn.sigmoid(gate)
    o = of_ref[...].astype(F32) + ob_ref[...].astype(F32)
    ng = ng_ref[...]
    parts = []
    for h in range(GLA_HEADS):
        oh = o[:, h * GLA_DV:(h + 1) * GLA_DV]
        ms = jnp.mean(oh * oh, axis=-1, keepdims=True)
        parts.append(oh * lax.rsqrt(ms + RMS_EPS) * ng[:, h * GLA_DV:(h + 1) * GLA_DV])
    yb = jnp.concatenate(parts, axis=-1) * jax.nn.silu(go_ref[...].astype(F32))
    wo = wo_ref[...]
    y = _dot(ya2.astype(BF16), wo[:S5_WIDTH]) + _dot(yb.astype(BF16), wo[S5_WIDTH:])
    _store_token_tiles(h8_ref, _layer_norm(ALPHA * x_ref[...] + y, lg_ref[...], lb_ref[...]))


def _mix0_out(ya, o_f, o_b, h0, x, glu_w, glu_b, norm_g, w_out, ln_g, ln_b):
    t = x.shape[0]
    tm = ROW_TILE
    row = lambda w: pl.BlockSpec((tm, w), lambda i: (i, 0))
    full = lambda a, b: pl.BlockSpec((a, b), lambda i: (0, 0))
    return pl.pallas_call(
        _mix0_out_kernel,
        grid=(t // tm,),
        in_specs=[pl.BlockSpec((S5_GROUPS // 2, tm // S5_CHUNK, 512), lambda i: (0, i, 0)),
                  row(512), row(512),
                  pl.BlockSpec((tm, 512), lambda i: (i, 2)),
                  row(1024), full(512, 512), full(1, 512), full(1, 512), full(1024, 1024),
                  full(1, 1024), full(1, 1024)],
        out_specs=pl.BlockSpec((tm * 8, 128), lambda i: (i, 0)),
        out_shape=jax.ShapeDtypeStruct((t * 8, 128), F32),
        scratch_shapes=[pltpu.VMEM((S5_WIDTH // 128, tm, 128), F32)],
        compiler_params=_cparams(1),
        name="mix0_out_ln",
    )(ya, o_f, o_b, h0, x, glu_w, glu_b, norm_g, w_out, ln_g, ln_b)


def _attn_kernel(q_ref, kc_ref, vc_ref, kp_ref, vp_ref, kn_ref, vn_ref, bias_ref, sink_ref, o_ref, *, seq):
    blk = pl.program_id(1)
    n_blk = pl.num_programs(1)
    n_sub = ATT_ROWS // ATT_BLOCK
    kw = ATT_BLOCK + 2 * WINDOW
    nq = GQA * ATT_BLOCK
    key_row = lax.broadcasted_iota(jnp.int32, (kw, nq), 0)
    pen_first = jnp.where((blk == 0) & (key_row < WINDOW), NEG_INF, 0.0)
    pen_last = jnp.where((blk == n_blk - 1) & (key_row >= ATT_BLOCK + WINDOW), NEG_INF, 0.0)
    kall = jnp.concatenate([kp_ref[...], kc_ref[...], kn_ref[...]], axis=0)
    vall = jnp.concatenate([vp_ref[...], vc_ref[...], vn_ref[...]], axis=0)
    for s in range(n_sub):
        kwin = kall[s * ATT_BLOCK:s * ATT_BLOCK + kw]
        vwin = vall[s * ATT_BLOCK:s * ATT_BLOCK + kw]
        q = q_ref[s * ATT_BLOCK:(s + 1) * ATT_BLOCK, :] * ATT_SCALE
        outs_t = []
        for kv in range(N_KV):
            qs = jnp.concatenate([q[:, (kv * GQA + g) * HEAD_DIM:(kv * GQA + g + 1) * HEAD_DIM]
                                  for g in range(GQA)], axis=0)
            kh = kwin[:, kv * HEAD_DIM:(kv + 1) * HEAD_DIM]
            vh = vwin[:, kv * HEAD_DIM:(kv + 1) * HEAD_DIM]
            st = _dot_nt(kh, qs) + bias_ref[kv]
            if s == 0:
                st = st + pen_first
            if s == n_sub - 1:
                st = st + pen_last
            sink = sink_ref[kv]
            m = jnp.maximum(jnp.max(st, axis=0, keepdims=True), sink)
            p = jnp.exp(st - m)
            den = jnp.sum(p, axis=0, keepdims=True) + jnp.exp(sink - m)
            outs_t.append(_dot_tn(vh, p.astype(BF16)) / den)
        o = jnp.concatenate(outs_t, axis=0).T
        pieces = [o[g * ATT_BLOCK:(g + 1) * ATT_BLOCK, kv * HEAD_DIM:(kv + 1) * HEAD_DIM]
                  for kv in range(N_KV) for g in range(GQA)]
        o_ref[s * ATT_BLOCK:(s + 1) * ATT_BLOCK, :] = jnp.concatenate(pieces, axis=-1).astype(o_ref.dtype)


def _t5_bucket(rel):
    nb = REL_BUCKETS // 2
    max_exact = nb // 2
    ret = (rel > 0).astype(jnp.int32) * nb
    n = jnp.abs(rel)
    large = max_exact + (jnp.log(jnp.maximum(n, 1).astype(F32) / max_exact)
                         / math.log(REL_MAX_DIST / max_exact) * (nb - max_exact)).astype(jnp.int32)
    large = jnp.minimum(large, nb - 1)
    return ret + jnp.where(n < max_exact, n, large)


def _attn_mixer(h3, rel_bias, sink, n_batch, seq):
    kw = ATT_BLOCK + 2 * WINDOW
    n_rel = kw + ATT_BLOCK - 1
    rel = jnp.arange(n_rel) - (n_rel - 1) // 2
    tb = rel_bias.astype(F32)[_t5_bucket(rel)]
    tb = jnp.where((jnp.abs(rel) <= WINDOW)[:, None], tb, NEG_INF)
    tb = jnp.pad(tb.T, ((0, 0), (0, 1)))
    flat = jnp.tile(tb, (1, ATT_BLOCK))[:, ATT_BLOCK - 1:ATT_BLOCK - 1 + ATT_BLOCK * n_rel]
    bias = flat.reshape(N_HEADS, ATT_BLOCK, n_rel)[:, :, :kw]
    bias = jnp.transpose(bias.reshape(N_KV, GQA, ATT_BLOCK, kw), (0, 3, 1, 2)).reshape(N_KV, kw, GQA * ATT_BLOCK)
    sink_rows = jnp.repeat(sink.astype(F32).reshape(N_KV, GQA), ATT_BLOCK, axis=1)[:, None, :]
    nblk = seq // ATT_ROWS
    sub = ATT_ROWS // ATT_BLOCK
    n128 = seq // ATT_BLOCK
    cur = lambda b, i: b * nblk + i
    prev = lambda b, i: b * n128 + jnp.maximum(i * sub - 1, 0)
    nxt = lambda b, i: b * n128 + jnp.minimum((i + 1) * sub, n128 - 1)
    t = n_batch * seq
    kern = functools.partial(_attn_kernel, seq=seq)
    return pl.pallas_call(
        kern,
        grid=(n_batch, nblk),
        in_specs=[pl.BlockSpec((ATT_ROWS, ATT_DIM), lambda b, i: (cur(b, i), 0)),
                  pl.BlockSpec((ATT_ROWS, KV_DIM), lambda b, i: (cur(b, i), 4)),
                  pl.BlockSpec((ATT_ROWS, KV_DIM), lambda b, i: (cur(b, i), 5)),
                  pl.BlockSpec((ATT_BLOCK, KV_DIM), lambda b, i: (prev(b, i), 4)),
                  pl.BlockSpec((ATT_BLOCK, KV_DIM), lambda b, i: (prev(b, i), 5)),
                  pl.BlockSpec((ATT_BLOCK, KV_DIM), lambda b, i: (nxt(b, i), 4)),
                  pl.BlockSpec((ATT_BLOCK, KV_DIM), lambda b, i: (nxt(b, i), 5)),
                  pl.BlockSpec((N_KV, kw, GQA * ATT_BLOCK), lambda b, i: (0, 0, 0)),
                  pl.BlockSpec((N_KV, 1, GQA * ATT_BLOCK), lambda b, i: (0, 0, 0))],
        out_specs=pl.BlockSpec((ATT_ROWS, ATT_DIM), lambda b, i: (cur(b, i), 0)),
        out_shape=jax.ShapeDtypeStruct((t, ATT_DIM), BF16),
        compiler_params=_cparams(2),
        name="window_gqa",
    )(h3, h3, h3, h3, h3, h3, h3, bias, sink_rows)


def _proj_ln_kernel(a_ref, w_ref, x_ref, g_ref, b_ref, h8_ref):
    y = _dot(a_ref[...], w_ref[...])
    _store_token_tiles(h8_ref, _layer_norm(ALPHA * x_ref[...] + y, g_ref[...], b_ref[...]))


def _proj_ln(a, w, x, g, b):
    t, k = a.shape
    tm = ROW_TILE
    return pl.pallas_call(
        _proj_ln_kernel,
        grid=(t // tm,),
        in_specs=[pl.BlockSpec((tm, k), lambda i: (i, 0)),
                  pl.BlockSpec((k, D_MODEL), lambda i: (0, 0)),
                  pl.BlockSpec((tm, D_MODEL), lambda i: (i, 0)),
                  pl.BlockSpec((1, D_MODEL), lambda i: (0, 0)),
                  pl.BlockSpec((1, D_MODEL), lambda i: (0, 0))],
        out_specs=pl.BlockSpec((tm * 8, 128), lambda i: (i, 0)),
        out_shape=jax.ShapeDtypeStruct((t * 8, 128), F32),
        compiler_params=_cparams(1),
        name="proj_ln",
    )(a, w, x, g, b)


def _router_kernel(h_ref, wh_ref, wl_ref, b_ref, o_ref, ot_ref, cnt_ref, run_ref):
    @pl.when(pl.program_id(0) == 0)
    def _():
        run_ref[...] = jnp.zeros_like(run_ref)

    x = _load_token_tiles(h_ref, ROW_TILE)
    xh = x.astype(BF16)
    xl = (x - xh.astype(F32)).astype(BF16)
    wh = wh_ref[...]
    lt = _dot_nt(wh, xh) + (_dot_nt(wh, xl) + _dot_nt(wl_ref[...], xh)) + b_ref[...][:, 0:1]
    tm = lt.shape[1]
    neg = jnp.float32(-jnp.inf)
    row8 = lax.broadcasted_iota(jnp.int32, (EPG, tm), 0).astype(F32)
    is_g = row8 < N_GROUPS
    gl = jnp.where(is_g, lt[0:EPG], neg)
    gmax = jnp.max(gl, axis=0, keepdims=True)
    gsum = jnp.sum(jnp.where(is_g, jnp.exp(gl - gmax), 0.0), axis=0, keepdims=True)
    grp = jnp.min(jnp.where(gl == gmax, row8, float(EPG)), axis=0, keepdims=True)
    p_grp = 1.0 / gsum
    el = lt[EPG:2 * EPG]
    for g in range(1, N_GROUPS):
        el = jnp.where(grp == float(g), lt[EPG * (g + 1):EPG * (g + 2)], el)
    m1 = jnp.max(el, axis=0, keepdims=True)
    i1 = jnp.min(jnp.where(el == m1, row8, float(EPG)), axis=0, keepdims=True)
    el2 = jnp.where(row8 == i1, neg, el)
    m2 = jnp.max(el2, axis=0, keepdims=True)
    i2 = jnp.min(jnp.where((row8 != i1) & (el2 == m2), row8, float(EPG)), axis=0, keepdims=True)
    e2 = jnp.exp(m2 - m1)
    g1 = p_grp / (1.0 + e2)
    g2 = p_grp * e2 / (1.0 + e2)
    eid1 = grp * EPG + i1
    eid2 = grp * EPG + i2
    rowe = lax.broadcasted_iota(jnp.int32, (N_EXPERTS, tm), 0).astype(F32)
    hit1 = rowe == eid1
    hit2 = rowe == eid2
    onehot = (hit1 | hit2).astype(BF16)
    upper = (lax.broadcasted_iota(jnp.int32, (tm, tm), 0) < lax.broadcasted_iota(jnp.int32, (tm, tm), 1)).astype(BF16)
    before = _dot(onehot, upper) + run_ref[...][:, 0:1]
    r1 = jnp.sum(jnp.where(hit1, before, 0.0), axis=0, keepdims=True)
    r2 = jnp.sum(jnp.where(hit2, before, 0.0), axis=0, keepdims=True)
    run_ref[...] += jnp.sum(onehot.astype(F32), axis=1, keepdims=True)
    cnt_ref[...] = run_ref[...]
    out_t = jnp.concatenate([eid1, eid2, g1, g2, r1, r2, jnp.zeros((2, tm), F32)], axis=0)
    ot_ref[...] = out_t
    o_ref[...] = jnp.concatenate([out_t, jnp.zeros((128 - 8, tm), F32)], axis=0).T


def _router(h8, w_group, b_group, w_router, b_router):
    t = h8.shape[0] // 8
    w = jnp.zeros((128, D_MODEL), F32)
    w = w.at[0:N_GROUPS].set(w_group.astype(F32).T).at[EPG:EPG + N_EXPERTS].set(w_router.astype(F32).T)
    wh = w.astype(BF16)
    wl = (w - wh.astype(F32)).astype(BF16)
    bias = jnp.zeros((128,), F32).at[0:N_GROUPS].set(b_group.astype(F32)).at[EPG:EPG + N_EXPERTS].set(
        b_router.astype(F32))
    bias = jnp.broadcast_to(bias[:, None], (128, 128))
    tm = ROW_TILE
    out, out_t, cnt = pl.pallas_call(
        _router_kernel,
        grid=(t // tm,),
        in_specs=[pl.BlockSpec((tm * 8, 128), lambda i: (i, 0)),
                  pl.BlockSpec((128, D_MODEL), lambda i: (0, 0)),
                  pl.BlockSpec((128, D_MODEL), lambda i: (0, 0)),
                  pl.BlockSpec((128, 128), lambda i: (0, 0))],
        out_specs=[pl.BlockSpec((tm, 128), lambda i: (i, 0)),
                   pl.BlockSpec((8, tm), lambda i: (0, i)),
                   pl.BlockSpec((N_EXPERTS, 128), lambda i: (0, 0))],
        out_shape=[jax.ShapeDtypeStruct((t, 128), F32), jax.ShapeDtypeStruct((8, t), F32),
                   jax.ShapeDtypeStruct((N_EXPERTS, 128), F32)],
        scratch_shapes=[pltpu.VMEM((N_EXPERTS, 128), F32)],
        compiler_params=_cparams(1),
        name="moe_router",
    )(h8, wh, wl, bias)
    eid = out_t[0:2].astype(jnp.int32)
    rank = out_t[4:6].astype(jnp.int32)
    counts = cnt[:, 0].astype(jnp.int32)
    return out, eid, rank, counts


def _token_tile(ref, tok):
    return ref.at[pl.ds(pl.multiple_of(tok * 8, 8), 8)]


def _experts_kernel(be_ref, nu_ref, src_hbm, h8_hbm, w1_ref, w3_ref, w2_ref, ys_ref,
                    w13_s, w2_s, xbuf, idx_smem, sem, isem):
    i = pl.program_id(0)
    n_used = nu_ref[0]
    used = i < n_used
    slot = i % N_ROW_BUFS

    blk_per_fetch = IDX_FETCH // MOE_ROWS

    def fetch_indices(blk):
        @pl.when(blk % blk_per_fetch == 0)
        def _():
            off = pl.multiple_of(blk * MOE_ROWS, IDX_FETCH)
            cp = pltpu.make_async_copy(src_hbm.at[pl.ds(off, IDX_FETCH)], idx_smem, isem)
            cp.start()
            cp.wait()

    def start_rows(blk, slot_, r0, n):
        base = (blk % blk_per_fetch) * MOE_ROWS
        toks = [idx_smem[base + r0 + u] for u in range(n)]
        for u in range(n):
            pltpu.make_async_copy(_token_tile(h8_hbm, toks[u]), _token_tile(xbuf.at[slot_], r0 + u),
                                  sem.at[slot_]).start(priority=u % 2)

    def gather_block(blk, slot_):
        fetch_indices(blk)
        def issue(r8, _):
            start_rows(blk, slot_, r8 * 8, 8)
            return 0
        lax.fori_loop(0, MOE_ROWS // 8, issue, 0)

    @pl.when(i == 0)
    def _():
        gather_block(0, 0)
        @pl.when(n_used > 1)
        def _():
            gather_block(1, 1)

    nxt = jnp.minimum(i + 2, n_used - 1)
    slot_n = (i + 2) % N_ROW_BUFS

    @pl.when(used)
    def _():
        fetch_indices(nxt)
        pltpu.make_async_copy(h8_hbm.at[pl.ds(0, MOE_ROWS * 8)], xbuf.at[slot], sem.at[slot]).wait()

    prev = be_ref[jnp.maximum(i - 1, 0)]
    fresh = (i == 0) | (be_ref[i] != prev)

    @pl.when(used & fresh)
    def _():
        w13_s[:, :D_EXPERT] = w1_ref[...].astype(BF16)
        w13_s[:, D_EXPERT:] = w3_ref[...].astype(BF16)
        w2_s[...] = w2_ref[...].astype(BF16)

    @pl.when(used)
    def _():
        n_tiles = 2 * D_EXPERT // MXU_COLS
        n_tiles2 = D_MODEL // MXU_COLS
        rows_per_group = MOE_ROWS // (n_tiles + n_tiles2)
        x = _load_token_tiles(xbuf.at[slot], MOE_ROWS).astype(BF16)
        parts = []
        for c in range(n_tiles):
            start_rows(nxt, slot_n, c * rows_per_group, rows_per_group)
            parts.append(_dot(x, w13_s[:, c * MXU_COLS:(c + 1) * MXU_COLS]))
        h = jnp.concatenate(parts, axis=-1)
        h1 = h[:, :D_EXPERT]
        hdn = ((h1 * jax.nn.sigmoid(h1)) * h[:, D_EXPERT:]).astype(BF16)
        for c in range(n_tiles2):
            start_rows(nxt, slot_n, (n_tiles + c) * rows_per_group, rows_per_group)
            yc = _dot(hdn, w2_s[:, c * MXU_COLS:(c + 1) * MXU_COLS])
            for k in range(MXU_COLS // 128):
                ys_ref[pl.ds(c * (MXU_COLS // 128) + k, MOE_ROWS, stride=8), :] = yc[:, k * 128:(k + 1) * 128]

    def wait_rows(slot_):
        pltpu.make_async_copy(h8_hbm.at[pl.ds(0, MOE_ROWS * 8)], xbuf.at[slot_], sem.at[slot_]).wait()

    @pl.when(i == n_used - 1)
    def _():
        wait_rows(slot_n)
        @pl.when(n_used > 1)
        def _():
            wait_rows((i + 1) % N_ROW_BUFS)

    @pl.when(jnp.logical_not(used))
    def _():
        ys_ref[...] = jnp.zeros_like(ys_ref)


def _experts(h8, src, blk_e, n_used, w1, w3, w2, layer, cap):
    nblk = cap // MOE_ROWS
    return pl.pallas_call(
        _experts_kernel,
        grid_spec=pltpu.PrefetchScalarGridSpec(
            num_scalar_prefetch=2,
            grid=(nblk,),
            in_specs=[pl.BlockSpec(memory_space=pl.ANY),
                      pl.BlockSpec(memory_space=pl.ANY),
                      pl.BlockSpec((None, None, D_MODEL, D_EXPERT), lambda i, be, nu: (layer, be[i], 0, 0)),
                      pl.BlockSpec((None, None, D_MODEL, D_EXPERT), lambda i, be, nu: (layer, be[i], 0, 0)),
                      pl.BlockSpec((None, None, D_EXPERT, D_MODEL), lambda i, be, nu: (layer, be[i], 0, 0))],
            out_specs=pl.BlockSpec((MOE_ROWS * 8, 128), lambda i, be, nu: (i, 0)),
            scratch_shapes=[pltpu.VMEM((D_MODEL, 2 * D_EXPERT), BF16),
                            pltpu.VMEM((D_EXPERT, D_MODEL), BF16),
                            pltpu.VMEM((N_ROW_BUFS, MOE_ROWS * 8, 128), F32),
                            pltpu.SMEM((IDX_FETCH,), jnp.int32),
                            pltpu.SemaphoreType.DMA((N_ROW_BUFS,)),
                            pltpu.SemaphoreType.DMA(())]),
        out_shape=jax.ShapeDtypeStruct((cap * 8, 128), F32),
        compiler_params=_cparams(1),
        name="moe_experts",
    )(blk_e, n_used, src, h8, w1, w3, w2)


def _combine_kernel(dst_hbm, ys_hbm, h_ref, gate_ref, g_ref, b_ref, o_ref, idx_smem, y0_ref, y1_ref, sem, isem):
    i = pl.program_id(0)
    tm = ROW_TILE
    slot = i % 2

    def gather_tile(tile, slot_):
        cp = pltpu.make_async_copy(dst_hbm.at[pl.ds(tile * 2 * tm, 2 * tm)], idx_smem, isem)
        cp.start()
        cp.wait()
        def issue(t, _):
            pltpu.make_async_copy(_token_tile(ys_hbm, idx_smem[t]), _token_tile(y0_ref.at[slot_], t),
                                  sem.at[slot_]).start(priority=0)
            pltpu.make_async_copy(_token_tile(ys_hbm, idx_smem[tm + t]), _token_tile(y1_ref.at[slot_], t),
                                  sem.at[slot_]).start(priority=1)
            return 0
        lax.fori_loop(0, tm, issue, 0, unroll=8)

    @pl.when(i == 0)
    def _():
        gather_tile(0, 0)

    @pl.when(i + 1 < pl.num_programs(0))
    def _():
        gather_tile(i + 1, 1 - slot)

    pltpu.make_async_copy(ys_hbm.at[pl.ds(0, tm * 8)], y0_ref.at[slot], sem.at[slot]).wait()
    pltpu.make_async_copy(ys_hbm.at[pl.ds(0, tm * 8)], y1_ref.at[slot], sem.at[slot]).wait()
    gate = gate_ref[...]
    y = (_load_token_tiles(y0_ref.at[slot], tm) * gate[:, 2:3]
         + _load_token_tiles(y1_ref.at[slot], tm) * gate[:, 3:4])
    o_ref[...] = _layer_norm(ALPHA * _load_token_tiles(h_ref, tm) + y, g_ref[...], b_ref[...])


def _combine_ln(dst, ys, h8, rout, ln_g, ln_b):
    t = h8.shape[0] // 8
    tm = ROW_TILE
    return pl.pallas_call(
        _combine_kernel,
        grid_spec=pltpu.PrefetchScalarGridSpec(
            num_scalar_prefetch=0,
            grid=(t // tm,),
            in_specs=[pl.BlockSpec(memory_space=pl.ANY),
                      pl.BlockSpec(memory_space=pl.ANY),
                      pl.BlockSpec((tm * 8, 128), lambda i: (i, 0)),
                      pl.BlockSpec((tm, 128), lambda i: (i, 0)),
                      pl.BlockSpec((1, D_MODEL), lambda i: (0, 0)),
                      pl.BlockSpec((1, D_MODEL), lambda i: (0, 0))],
            out_specs=pl.BlockSpec((tm, D_MODEL), lambda i: (i, 0)),
            scratch_shapes=[pltpu.SMEM((2 * tm,), jnp.int32),
                            pltpu.VMEM((2, tm * 8, 128), F32),
                            pltpu.VMEM((2, tm * 8, 128), F32),
                            pltpu.SemaphoreType.DMA((2,)),
                            pltpu.SemaphoreType.DMA(())]),
        out_shape=jax.ShapeDtypeStruct((t, D_MODEL), F32),
        compiler_params=_cparams(1),
        name="moe_combine_ln",
    )(dst, ys, h8, rout, ln_g, ln_b)


def _moe_layer(h8, w_group, b_group, w_router, b_router, w1, w3, w2, layer, ln_g, ln_b):
    t = h8.shape[0] // 8
    rout, eid, rank, counts = _router(h8, w_group, b_group, w_router, b_router)
    padded = (counts + MOE_ROWS - 1) // MOE_ROWS * MOE_ROWS
    pad_end = jnp.cumsum(padded)
    pad_start = pad_end - padded
    start_of = jnp.sum(jnp.where(eid[..., None] == jnp.arange(N_EXPERTS), pad_start, 0), axis=-1)
    dst = (start_of + rank).astype(jnp.int32)
    cap = 2 * t + N_EXPERTS * MOE_ROWS
    nblk = cap // MOE_ROWS
    tok = jnp.broadcast_to(jnp.arange(t, dtype=jnp.int32), (2, t))
    src = jnp.zeros((cap,), jnp.int32).at[dst.reshape(-1)].set(tok.reshape(-1), unique_indices=True)
    blk_e = jnp.minimum(jnp.sum(pad_end[None, :] <= (jnp.arange(nblk) * MOE_ROWS)[:, None], axis=1),
                        N_EXPERTS - 1).astype(jnp.int32)
    n_used = (pad_end[-1] // MOE_ROWS).astype(jnp.int32).reshape(1)
    dst_tiles = jnp.transpose(dst.reshape(2, t // ROW_TILE, ROW_TILE), (1, 0, 2)).reshape(-1)
    ys = _experts(h8, src, blk_e, n_used, w1, w3, w2, layer, cap)
    return _combine_ln(dst_tiles, ys, h8, rout, ln_g, ln_b)


def kernel(x, ln_mix_g, ln_mix_b, ln_ffn_g, ln_ffn_b, ab_w_in, s5_lam_re, s5_lam_im, s5_log_dt, s5_b_re, s5_b_im,
           s5_c_re, s5_c_im, s5_d, s5_glu_w, s5_glu_b, gla_gate_w, gla_gate_b, gla_norm_g, ab_w_out, c_w_in,
           c_sink, c_w_out, rel_bias, moe_w_group, moe_b_group, moe_w_router, moe_b_router, moe_w1, moe_w3, moe_w2):
    n_batch, seq, _ = x.shape
    t = n_batch * seq
    xt = x.reshape(t, D_MODEL)
    row = lambda v: v.astype(F32).reshape(1, -1)

    w_in0 = jnp.pad(ab_w_in[0], ((0, 0), (0, AB_IN_PAD - AB_IN))).astype(BF16)
    h0, u_t = _inproj0(xt, w_in0)
    n_steps = seq // (S5_SEGS * S5_CHUNK)
    kcat, r1e, r2, coef, ptab = _s5_prep(s5_lam_re[0], s5_lam_im[0], s5_log_dt[0], s5_b_re[0], s5_b_im[0],
                                         s5_c_re[0], s5_c_im[0], s5_d[0], n_steps)
    ya = _s5_mixer(u_t, kcat, r1e, r2, coef, ptab, n_batch, n_steps)
    gw = gla_gate_w[0].astype(F32)
    wg = jnp.zeros((128, 2 * GLA_KEY), F32)
    wg = wg.at[0:GLA_RANK, 0:GLA_KEY].set(gw[0]).at[GLA_RANK:2 * GLA_RANK, GLA_KEY:].set(gw[1]).astype(BF16)
    bg = gla_gate_b[0].astype(F32).reshape(1, 2 * GLA_KEY)
    o_f, o_b = _gla_mixer(h0, wg, bg, n_batch, seq)
    h8 = _mix0_out(ya, o_f, o_b, h0, xt, s5_glu_w[0].astype(BF16), row(s5_glu_b[0]), row(gla_norm_g[0]),
                      ab_w_out[0].astype(BF16), row(ln_mix_g[0]), row(ln_mix_b[0]))
    h = _moe_layer(h8, moe_w_group[0], moe_b_group[0], moe_w_router[0], moe_b_router[0],
                   moe_w1, moe_w3, moe_w2, 0, row(ln_ffn_g[0]), row(ln_ffn_b[0]))

    h3 = _matmul(h, c_w_in[0].astype(BF16), BF16)
    o = _attn_mixer(h3, rel_bias, c_sink[0], n_batch, seq)
    h8 = _proj_ln(o, c_w_out[0].astype(BF16), h, row(ln_mix_g[1]), row(ln_mix_b[1]))
    h = _moe_layer(h8, moe_w_group[1], moe_b_group[1], moe_w_router[1], moe_b_router[1],
                   moe_w1, moe_w3, moe_w2, 1, row(ln_ffn_g[1]), row(ln_ffn_b[1]))
    return h.reshape(n_batch, seq, D_MODEL)
```

```python
import functools
import math

import jax
import jax.numpy as jnp
from jax import lax
from jax.experimental import pallas as pl
from jax.experimental.pallas import tpu as pltpu

F32 = jnp.float32
BF16 = jnp.bfloat16

D_MODEL = 1024
S5_WIDTH = 512
S5_GROUP_CH = 16
S5_GROUPS = 32
S5_STATE = 64
GLA_HEADS = 4
GLA_DV = 128
GLA_DK = 64
GLA_KEY = 256
GLA_WIDTH = 512
GLA_RANK = 16
GLA_TAU = 16.0
GLA_CHUNK = 64
AB_IN = 2080
AB_IN_PAD = 2176
HEAD_DIM = 64
N_HEADS = 16
N_KV = 4
GQA = 4
ATT_DIM = 1024
KV_DIM = 256
WINDOW = 128
ATT_BLOCK = 128
ATT_SCALE = HEAD_DIM ** -0.5
REL_BUCKETS = 32
REL_MAX_DIST = 128
NEG_INF = -1e30
N_GROUPS = 4
EPG = 8
N_EXPERTS = 32
D_EXPERT = 512
LN_EPS = 1e-5
RMS_EPS = 1e-6
DEPTH = 2
ALPHA = (2 * DEPTH) ** 0.25

S5_CHUNK = 16
S5_SEGS = 8
S5_SEG_PAD = 4
ROW_TILE = 512
GLA_ROWS = 512
ATT_ROWS = 512
MOE_ROWS = 256
IDX_FETCH = 1024
N_ROW_BUFS = 3
MXU_COLS = 256
VMEM_LIMIT = 56 * 1024 * 1024


def _cparams(n_axes):
    return pltpu.CompilerParams(dimension_semantics=("arbitrary",) * n_axes,
                                vmem_limit_bytes=VMEM_LIMIT)


def _dot(a, b):
    return jnp.dot(a, b, preferred_element_type=F32)


def _dot_nt(a, b):
    return lax.dot_general(a, b, (((1,), (1,)), ((), ())), preferred_element_type=F32)


def _dot_tn(a, b):
    return lax.dot_general(a, b, (((0,), (0,)), ((), ())), preferred_element_type=F32)


def _layer_norm(r, g, b):
    mu = jnp.mean(r, axis=-1, keepdims=True)
    c = r - mu
    var = jnp.mean(c * c, axis=-1, keepdims=True)
    return c * lax.rsqrt(var + LN_EPS) * g + b


def _store_token_tiles(ref, val):
    n = val.shape[0]
    for k in range(D_MODEL // 128):
        ref[pl.ds(k, n, stride=8), :] = val[:, k * 128:(k + 1) * 128]


def _load_token_tiles(ref, n):
    return jnp.concatenate([ref[pl.ds(k, n, stride=8), :] for k in range(D_MODEL // 128)], axis=-1)


def _mm_kernel(x_ref, w_ref, o_ref):
    o_ref[...] = _dot(x_ref[...].astype(BF16), w_ref[...]).astype(o_ref.dtype)


def _matmul(x, w, out_dtype):
    m, k = x.shape
    n = w.shape[1]
    return pl.pallas_call(
        _mm_kernel,
        grid=(m // ROW_TILE,),
        in_specs=[pl.BlockSpec((ROW_TILE, k), lambda i: (i, 0)),
                  pl.BlockSpec((k, n), lambda i: (0, 0))],
        out_specs=pl.BlockSpec((ROW_TILE, n), lambda i: (i, 0)),
        out_shape=jax.ShapeDtypeStruct((m, n), out_dtype),
        compiler_params=_cparams(1),
        name="dense_matmul",
    )(x, w)


def _s5_prep(lam_re, lam_im, log_dt, b_re, b_im, c_re, c_im, d, n_steps):
    tc = S5_CHUNK
    g, p, c = S5_GROUPS, S5_STATE, S5_GROUP_CH
    lr = jnp.minimum(lam_re.astype(F32), -1e-4)
    li = lam_im.astype(F32)
    dt = jnp.exp(log_dt.astype(F32))[..., None]
    mag = jnp.exp(lr * dt)
    ar = mag * jnp.cos(li * dt)
    ai = mag * jnp.sin(li * dt)
    den = lr * lr + li * li
    nr = ar - 1.0
    coef_r = (nr * lr + ai * li) / den
    coef_i = (ai * lr - nr * li) / den
    br_ = b_re.astype(F32)
    bi_ = b_im.astype(F32)
    bbr = coef_r[..., None] * br_ - coef_i[..., None] * bi_
    bbi = coef_r[..., None] * bi_ + coef_i[..., None] * br_
    cr = c_re.astype(F32)
    ci = c_im.astype(F32)
    bbr_t = jnp.swapaxes(bbr, -1, -2)
    bbi_t = jnp.swapaxes(bbi, -1, -2)
    npair = g // 2

    def apow(n):
        nn = n.astype(F32)[:, None, None, None]
        m_ = jnp.exp(nn * (lr * dt)[None])
        ang = nn * (li * dt)[None]
        return m_ * jnp.cos(ang), m_ * jnp.sin(ang)

    pr, pi = apow(jnp.arange(tc + 1))
    hr = pr[:, :, :, None, :] * bbr_t[None] - pi[:, :, :, None, :] * bbi_t[None]
    hi = pr[:, :, :, None, :] * bbi_t[None] + pi[:, :, :, None, :] * bbr_t[None]
    kk = jnp.einsum('dgoq,jdgcq->jdgco', cr, hr) - jnp.einsum('dgoq,jdgcq->jdgco', ci, hi)
    k0 = kk[0, 0] + kk[0, 1] + d.astype(F32).reshape(g, c)[:, :, None] * jnp.eye(c, dtype=F32)
    slab = jnp.concatenate([kk[tc - 1:0:-1, 1], k0[None], kk[1:tc, 0]], axis=0)
    kcat = jnp.transpose(slab, (1, 2, 0, 3)).reshape(g, c, (2 * tc - 1) * c)
    kcat = jnp.pad(kcat, ((0, 0), (0, 0), (0, 2 * tc * c - kcat.shape[-1]))).reshape(npair, 2, c, 2 * tc * c)

    def pair_blockdiag(m):
        m2 = m.reshape((npair, 2) + m.shape[1:])
        z = jnp.zeros_like(m2[:, 0])
        return jnp.concatenate([jnp.concatenate([m2[:, 0], z], axis=-1),
                                jnp.concatenate([z, m2[:, 1]], axis=-1)], axis=1)

    def w_plane(h_, lag_sel, dirn):
        w = jnp.transpose(h_[lag_sel, dirn], (1, 0, 2, 3)).reshape(g, tc * c, p)
        return pair_blockdiag(w)
    s_fwd = tc - 1 - jnp.arange(tc)
    s_bwd = jnp.arange(tc)
    r1e = jnp.concatenate([w_plane(hr, s_fwd, 0), w_plane(hi, s_fwd, 0), w_plane(hr, s_bwd, 1), w_plane(hi, s_bwd, 1)],
                          axis=2)

    lane = jnp.arange(tc * c)
    rep_t = (lane[None, :] // c == jnp.arange(tc)[:, None]).astype(F32)
    tile_co = (lane[None, :] % c == jnp.arange(c)[:, None]).astype(F32)
    spread = lambda a, sel: jnp.dot(a, sel, precision=lax.Precision.HIGHEST)
    def readout(e, dirn):
        pr_l = spread(jnp.transpose(pr[e, dirn], (1, 2, 0)).reshape(g * p, tc), rep_t)
        pi_l = spread(jnp.transpose(pi[e, dirn], (1, 2, 0)).reshape(g * p, tc), rep_t)
        cr_l = spread(jnp.transpose(cr[dirn], (0, 2, 1)).reshape(g * p, c), tile_co)
        ci_l = spread(jnp.transpose(ci[dirn], (0, 2, 1)).reshape(g * p, c), tile_co)
        zr = (cr_l * pr_l - ci_l * pi_l).reshape(g, p, tc * c)
        zi = (cr_l * pi_l + ci_l * pr_l).reshape(g, p, tc * c)
        return pair_blockdiag(zr), pair_blockdiag(-zi)
    vf_r, vf_i = readout(jnp.arange(tc) + 1, 0)
    vb_r, vb_i = readout(tc - jnp.arange(tc), 1)
    r2 = jnp.concatenate([vf_r, vf_i, vb_r, vb_i], axis=1)

    steps = jnp.arange(n_steps)
    a16r, a16i = apow(jnp.array([tc]))
    apr, api = apow(steps * tc)
    anr, ani = apow(jnp.array([tc * n_steps]))
    def lanes(x):
        n_ = x.shape[0]
        return jnp.transpose(x.reshape(n_, npair, 2 * p), (1, 0, 2))
    coef = jnp.concatenate([lanes(a16r[:, 0]), lanes(a16i[:, 0]), lanes(a16r[:, 1]), lanes(a16i[:, 1]),
                            lanes(anr[:, 0]), lanes(ani[:, 0]), lanes(anr[:, 1]), lanes(ani[:, 1])], axis=1)
    rev = n_steps - 1 - steps
    ptab = jnp.stack([lanes(apr[:, 0]), lanes(api[:, 0]), lanes(apr[rev, 1]), lanes(api[rev, 1])], axis=1)
    return kcat, r1e.astype(BF16), r2.astype(BF16), coef, ptab


def _gelu_tanh(x):
    return 0.5 * x * (1.0 + jnp.tanh(math.sqrt(2.0 / math.pi) * (x + 0.044715 * (x * x * x))))


def _s5_kernel(u_ref, kcat_ref, r1e_ref, r2_ref, coef_ref, ptab_ref, y_ref, kt_ref, yi_ref, e_ref, s_ref,
               *, n_batch, n_steps):
    pitch = n_steps + S5_SEG_PAD
    blk = S5_CHUNK * S5_GROUP_CH
    kt_ref[...] = jnp.zeros_like(kt_ref)
    for gi in range(2):
        slab = kcat_ref[gi]
        for s in range(S5_CHUNK):
            off = (S5_CHUNK - 1 - s) * S5_GROUP_CH
            kt_ref[gi * blk + s * S5_GROUP_CH:gi * blk + (s + 1) * S5_GROUP_CH, gi * blk:(gi + 1) * blk] = (
                slab[:, off:off + blk].astype(BF16))
    u = u_ref[...]
    yi_ref[...] = _dot(u, kt_ref[...])
    e = _dot(u, r1e_ref[...])
    n_seg = n_batch * S5_SEGS
    for k in range(4):
        for sg in range(n_seg):
            e_ref[k, sg * pitch:sg * pitch + n_steps, :] = e[sg * n_steps:(sg + 1) * n_steps, k * 128:(k + 1) * 128]
    coef = coef_ref[...]
    a16fr, a16fi, a16br, a16bi = coef[0:1], coef[1:2], coef[2:3], coef[3:4]
    anfr, anfi, anbr, anbi = coef[4:5], coef[5:6], coef[6:7], coef[7:8]
    zero = jnp.zeros((S5_SEGS, 128), F32)
    sub = lax.broadcasted_iota(jnp.int32, (S5_SEGS, 128), 0)

    def seg_rows(b, j):
        return pl.ds(b * S5_SEGS * pitch + j, S5_SEGS, stride=pitch)

    def local_step(j, carry):
        out = []
        for b in range(n_batch):
            sfr, sfi, sbr, sbi = carry[b]
            rf = seg_rows(b, j)
            rb = seg_rows(b, n_steps - 1 - j)
            s_ref[0, rf, :] = sfr
            s_ref[1, rf, :] = sfi
            s_ref[2, rb, :] = sbr
            s_ref[3, rb, :] = sbi
            efr = e_ref[0, rf, :]
            efi = e_ref[1, rf, :]
            ebr = e_ref[2, rb, :]
            ebi = e_ref[3, rb, :]
            out.append((a16fr * sfr - a16fi * sfi + efr, a16fr * sfi + a16fi * sfr + efi,
                        a16br * sbr - a16bi * sbi + ebr, a16br * sbi + a16bi * sbr + ebi))
        return tuple(out)

    ends = lax.fori_loop(0, n_steps, local_step, tuple((zero, zero, zero, zero) for _ in range(n_batch)))

    carries = []
    for b in range(n_batch):
        efr, efi, ebr, ebi = ends[b]
        cfr, cfi, cbr, cbi = zero, zero, zero, zero
        for _ in range(S5_SEGS - 1):
            tfr = anfr * cfr - anfi * cfi + efr
            tfi = anfr * cfi + anfi * cfr + efi
            cfr = jnp.where(sub == 0, 0.0, pltpu.roll(tfr, 1, 0))
            cfi = jnp.where(sub == 0, 0.0, pltpu.roll(tfi, 1, 0))
            tbr = anbr * cbr - anbi * cbi + ebr
            tbi = anbr * cbi + anbi * cbr + ebi
            cbr = jnp.where(sub == S5_SEGS - 1, 0.0, pltpu.roll(tbr, S5_SEGS - 1, 0))
            cbi = jnp.where(sub == S5_SEGS - 1, 0.0, pltpu.roll(tbi, S5_SEGS - 1, 0))
        carries.append((cfr, cfi, cbr, cbi))

    def fix_step(j, _):
        pfr = ptab_ref[0, pl.ds(j, 1), :]
        pfi = ptab_ref[1, pl.ds(j, 1), :]
        pbr = ptab_ref[2, pl.ds(j, 1), :]
        pbi = ptab_ref[3, pl.ds(j, 1), :]
        for b in range(n_batch):
            cfr, cfi, cbr, cbi = carries[b]
            rj = seg_rows(b, j)
            s_ref[0, rj, :] = s_ref[0, rj, :] + (pfr * cfr - pfi * cfi)
            s_ref[1, rj, :] = s_ref[1, rj, :] + (pfr * cfi + pfi * cfr)
            s_ref[2, rj, :] = s_ref[2, rj, :] + (pbr * cbr - pbi * cbi)
            s_ref[3, rj, :] = s_ref[3, rj, :] + (pbr * cbi + pbi * cbr)
        return 0

    lax.fori_loop(0, n_steps, fix_step, 0)

    s_all = jnp.concatenate(
        [jnp.concatenate([s_ref[k, sg * pitch:sg * pitch + n_steps, :] for sg in range(n_seg)], axis=0)
         for k in range(4)], axis=-1)
    y = yi_ref[...] + _dot(s_all.astype(BF16), r2_ref[...])
    y_ref[...] = _gelu_tanh(y).astype(y_ref.dtype)


def _s5_mixer(u_t, kcat, r1e, r2, coef, ptab, n_batch, n_steps):
    npair, m, _ = u_t.shape
    scan_rows = n_batch * S5_SEGS * (n_steps + S5_SEG_PAD)
    kern = functools.partial(_s5_kernel, n_batch=n_batch, n_steps=n_steps)
    return pl.pallas_call(
        kern,
        grid=(npair,),
        in_specs=[pl.BlockSpec((None, m, 512), lambda i: (i, 0, 0)),
                  pl.BlockSpec((None, 2, S5_GROUP_CH, 512), lambda i: (i, 0, 0, 0)),
                  pl.BlockSpec((None, 512, 512), lambda i: (i, 0, 0)),
                  pl.BlockSpec((None, 512, 512), lambda i: (i, 0, 0)),
                  pl.BlockSpec((None, 8, 128), lambda i: (i, 0, 0)),
                  pl.BlockSpec((None, 4, n_steps, 128), lambda i: (i, 0, 0, 0))],
        out_specs=pl.BlockSpec((None, m, 512), lambda i: (i, 0, 0)),
        out_shape=jax.ShapeDtypeStruct((npair, m, 512), BF16),
        scratch_shapes=[pltpu.VMEM((512, 512), BF16), pltpu.VMEM((m, 512), F32),
                        pltpu.VMEM((4, scan_rows, 128), F32), pltpu.VMEM((4, scan_rows, 128), F32)],
        compiler_params=_cparams(1),
        name="s5_scan",
    )(u_t, kcat, r1e, r2, coef, ptab)


def _inproj0_kernel(x_ref, w_ref, h_ref, ut_ref, u_s):
    h = _dot(x_ref[...].astype(BF16), w_ref[...])
    h_ref[...] = h[:, S5_WIDTH:].astype(h_ref.dtype)
    n_lane_blk = S5_WIDTH // 128
    for k in range(n_lane_blk):
        u_s[k] = h[:, k * 128:(k + 1) * 128]
    n_chunk = ROW_TILE // S5_CHUNK
    rows = [jnp.concatenate([u_s[k, pl.ds(s, n_chunk, stride=S5_CHUNK), :] for k in range(n_lane_blk)], axis=-1)
            for s in range(S5_CHUNK)]
    ch = S5_GROUP_CH
    for p in range(S5_GROUPS // 2):
        ut_ref[p] = jnp.concatenate([r[:, (2 * p + gi) * ch:(2 * p + gi + 1) * ch] for gi in range(2) for r in rows],
                                    axis=-1).astype(ut_ref.dtype)


def _inproj0(x, w):
    t, k = x.shape
    n = w.shape[1]
    tm = ROW_TILE
    npair = S5_GROUPS // 2
    return pl.pallas_call(
        _inproj0_kernel,
        grid=(t // tm,),
        in_specs=[pl.BlockSpec((tm, k), lambda i: (i, 0)),
                  pl.BlockSpec((k, n), lambda i: (0, 0))],
        out_specs=[pl.BlockSpec((tm, n - S5_WIDTH), lambda i: (i, 0)),
                   pl.BlockSpec((npair, tm // S5_CHUNK, 512), lambda i: (0, i, 0))],
        out_shape=[jax.ShapeDtypeStruct((t, n - S5_WIDTH), BF16),
                   jax.ShapeDtypeStruct((npair, t // S5_CHUNK, 512), BF16)],
        scratch_shapes=[pltpu.VMEM((S5_WIDTH // 128, tm, 128), F32)],
        compiler_params=_cparams(1),
        name="inproj0",
    )(x, w)


def _gla_kernel(qf_ref, kf_ref, vf_ref, lf_ref, qb_ref, kb_ref, vb_ref, lb_ref, wg_ref, bg_ref,
                of_ref, ob_ref, st_ref):
    @pl.when(pl.program_id(1) == 0)
    def _():
        st_ref[...] = jnp.zeros_like(st_ref)

    n_chunks = GLA_ROWS // GLA_CHUNK
    row = lax.broadcasted_iota(jnp.int32, (GLA_CHUNK, GLA_CHUNK), 0)
    col = lax.broadcasted_iota(jnp.int32, (GLA_CHUNK, GLA_CHUNK), 1)
    mask_f = col <= row
    mask_b = col > row
    brow = lax.broadcasted_iota(jnp.int32, (GLA_ROWS, GLA_ROWS), 0)
    bcol = lax.broadcasted_iota(jnp.int32, (GLA_ROWS, GLA_ROWS), 1)
    shift = GLA_CHUNK.bit_length() - 1
    same = (brow >> shift) == (bcol >> shift)
    ones_blk = same.astype(BF16)
    tri_f = (same & (bcol <= brow)).astype(BF16)
    tri_b = (same & (bcol >= brow)).astype(BF16)
    wg = wg_ref[...]
    bg = bg_ref[...]

    def direction(q_ref, k_ref, v_ref, l_ref, o_ref, d):
        z = _dot(l_ref[...], wg[:, d * GLA_KEY:(d + 1) * GLA_KEY]) + bg[:, d * GLA_KEY:(d + 1) * GLA_KEY]
        log_a = jax.nn.log_sigmoid(z) * (1.0 / GLA_TAU)
        la_hi = log_a.astype(BF16)
        la_lo = (log_a - la_hi.astype(F32)).astype(BF16)
        tri = tri_f if d == 0 else tri_b
        bc = _dot(tri, la_hi) + _dot(tri, la_lo)
        tot = _dot(ones_blk, la_hi) + _dot(ones_blk, la_lo)
        q = q_ref[...].astype(F32) * (GLA_DK ** -0.5)
        k = k_ref[...].astype(F32)
        qd_all = (q * jnp.exp(bc)).astype(BF16)
        kd_all = (k * jnp.exp(-bc)).astype(BF16)
        kc_all = (k * jnp.exp(tot - bc)).astype(BF16)
        decay_all = jnp.exp(tot)
        order = range(n_chunks) if d == 0 else range(n_chunks - 1, -1, -1)
        mask = mask_f if d == 0 else mask_b
        states = [st_ref[d, h] for h in range(GLA_HEADS)]
        for c in order:
            sl = slice(c * GLA_CHUNK, (c + 1) * GLA_CHUNK)
            qd, kd, kc = qd_all[sl], kd_all[sl], kc_all[sl]
            decay = decay_all[c * GLA_CHUNK:c * GLA_CHUNK + 1]
            v = v_ref[sl, :]
            outs = []
            for h in range(GLA_HEADS):
                ks = slice(h * GLA_DK, (h + 1) * GLA_DK)
                vs = slice(h * GLA_DV, (h + 1) * GLA_DV)
                s = jnp.where(mask, _dot_nt(qd[:, ks], kd[:, ks]), 0.0).astype(BF16)
                st = states[h]
                o = _dot(s, v[:, vs]) + _dot_nt(qd[:, ks], st.astype(BF16))
                states[h] = st * decay[:, ks] + _dot_tn(v[:, vs], kc[:, ks])
                outs.append(o)
            o_ref[sl, :] = jnp.concatenate(outs, axis=-1).astype(o_ref.dtype)
        for h in range(GLA_HEADS):
            st_ref[d, h] = states[h]

    direction(qf_ref, kf_ref, vf_ref, lf_ref, of_ref, 0)
    direction(qb_ref, kb_ref, vb_ref, lb_ref, ob_ref, 1)


def _gla_mixer(h0, wg, bg, n_batch, seq):
    nb = seq // GLA_ROWS
    r = GLA_ROWS
    fwd = lambda b, i: b * nb + i
    bwd = lambda b, i: b * nb + (nb - 1 - i)
    def spec(width, colblk, rowfn):
        return pl.BlockSpec((r, width), lambda b, i: (rowfn(b, i), colblk))
    in_specs = [spec(256, 0, fwd), spec(256, 1, fwd), spec(512, 1, fwd), spec(128, 12, fwd),
                spec(256, 0, bwd), spec(256, 1, bwd), spec(512, 1, bwd), spec(128, 12, bwd),
                pl.BlockSpec((128, 512), lambda b, i: (0, 0)),
                pl.BlockSpec((1, 512), lambda b, i: (0, 0))]
    out_specs = [pl.BlockSpec((r, 512), lambda b, i: (fwd(b, i), 0)),
                 pl.BlockSpec((r, 512), lambda b, i: (bwd(b, i), 0))]
    t = n_batch * seq
    return pl.pallas_call(
        _gla_kernel,
        grid=(n_batch, nb),
        in_specs=in_specs,
        out_specs=out_specs,
        out_shape=[jax.ShapeDtypeStruct((t, 512), BF16), jax.ShapeDtypeStruct((t, 512), BF16)],
        scratch_shapes=[pltpu.VMEM((2, GLA_HEADS, GLA_DV, GLA_DK), F32)],
        compiler_params=_cparams(2),
        name="gla_chunked",
    )(h0, h0, h0, h0, h0, h0, h0, h0, wg, bg)


def _mix0_out_kernel(y_ref, of_ref, ob_ref, go_ref, x_ref, gw_ref, gb_ref, ng_ref, wo_ref, lg_ref, lb_ref,
                     h8_ref, ya_s):
    n_chunk = ROW_TILE // S5_CHUNK
    ys = [y_ref[p].astype(F32) for p in range(S5_GROUPS // 2)]
    n_lane_blk = S5_WIDTH // 128
    ch = S5_GROUP_CH
    blk = S5_CHUNK * ch
    for t in range(S5_CHUNK):
        for k in range(n_lane_blk):
            ya_s[k, pl.ds(t, n_chunk, stride=S5_CHUNK), :] = jnp.concatenate(
                [y[:, gi * blk + t * ch:gi * blk + (t + 1) * ch] for y in ys[4 * k:4 * k + 4] for gi in range(2)],
                axis=-1)
    yaf = jnp.concatenate([ya_s[k] for k in range(n_lane_blk)], axis=-1)
    ya = yaf.astype(BF16)
    gate = _dot(ya, gw_ref[...]) + gb_ref[...]
    ya2 = yaf * jax.nn.sigmoid(gate)
    o = of_ref[...].astype(F32) + ob_ref[...].astype(F32)
    ng = ng_ref[...]
    parts = []
    for h in range(GLA_HEADS):
        oh = o[:, h * GLA_DV:(h + 1) * GLA_DV]
        ms = jnp.mean(oh * oh, axis=-1, keepdims=True)
        parts.append(oh * lax.rsqrt(ms + RMS_EPS) * ng[:, h * GLA_DV:(h + 1) * GLA_DV])
    yb = jnp.concatenate(parts, axis=-1) * jax.nn.silu(go_ref[...].astype(F32))
    wo = wo_ref[...]
    y = _dot(ya2.astype(BF16), wo[:S5_WIDTH]) + _dot(yb.astype(BF16), wo[S5_WIDTH:])
    _store_token_tiles(h8_ref, _layer_norm(ALPHA * x_ref[...] + y, lg_ref[...], lb_ref[...]))


def _mix0_out(ya, o_f, o_b, h0, x, glu_w, glu_b, norm_g, w_out, ln_g, ln_b):
    t = x.shape[0]
    tm = ROW_TILE
    row = lambda w: pl.BlockSpec((tm, w), lambda i: (i, 0))
    full = lambda a, b: pl.BlockSpec((a, b), lambda i: (0, 0))
    return pl.pallas_call(
        _mix0_out_kernel,
        grid=(t // tm,),
        in_specs=[pl.BlockSpec((S5_GROUPS // 2, tm // S5_CHUNK, 512), lambda i: (0, i, 0)),
                  row(512), row(512),
                  pl.BlockSpec((tm, 512), lambda i: (i, 2)),
                  row(1024), full(512, 512), full(1, 512), full(1, 512), full(1024, 1024),
                  full(1, 1024), full(1, 1024)],
        out_specs=pl.BlockSpec((tm * 8, 128), lambda i: (i, 0)),
        out_shape=jax.ShapeDtypeStruct((t * 8, 128), F32),
        scratch_shapes=[pltpu.VMEM((S5_WIDTH // 128, tm, 128), F32)],
        compiler_params=_cparams(1),
        name="mix0_out_ln",
    )(ya, o_f, o_b, h0, x, glu_w, glu_b, norm_g, w_out, ln_g, ln_b)


def _attn_kernel(q_ref, kc_ref, vc_ref, kp_ref, vp_ref, kn_ref, vn_ref, bias_ref, sink_ref, o_ref, *, seq):
    blk = pl.program_id(1)
    n_blk = pl.num_programs(1)
    n_sub = ATT_ROWS // ATT_BLOCK
    kw = ATT_BLOCK + 2 * WINDOW
    nq = GQA * ATT_BLOCK
    key_row = lax.broadcasted_iota(jnp.int32, (kw, nq), 0)
    pen_first = jnp.where((blk == 0) & (key_row < WINDOW), NEG_INF, 0.0)
    pen_last = jnp.where((blk == n_blk - 1) & (key_row >= ATT_BLOCK + WINDOW), NEG_INF, 0.0)
    kall = jnp.concatenate([kp_ref[...], kc_ref[...], kn_ref[...]], axis=0)
    vall = jnp.concatenate([vp_ref[...], vc_ref[...], vn_ref[...]], axis=0)
    for s in range(n_sub):
        kwin = kall[s * ATT_BLOCK:s * ATT_BLOCK + kw]
        vwin = vall[s * ATT_BLOCK:s * ATT_BLOCK + kw]
        q = q_ref[s * ATT_BLOCK:(s + 1) * ATT_BLOCK, :] * ATT_SCALE
        outs_t = []
        for kv in range(N_KV):
            qs = jnp.concatenate([q[:, (kv * GQA + g) * HEAD_DIM:(kv * GQA + g + 1) * HEAD_DIM]
                                  for g in range(GQA)], axis=0)
            kh = kwin[:, kv * HEAD_DIM:(kv + 1) * HEAD_DIM]
            vh = vwin[:, kv * HEAD_DIM:(kv + 1) * HEAD_DIM]
            st = _dot_nt(kh, qs) + bias_ref[kv]
            if s == 0:
                st = st + pen_first
            if s == n_sub - 1:
                st = st + pen_last
            sink = sink_ref[kv]
            m = jnp.maximum(jnp.max(st, axis=0, keepdims=True), sink)
            p = jnp.exp(st - m)
            den = jnp.sum(p, axis=0, keepdims=True) + jnp.exp(sink - m)
            outs_t.append(_dot_tn(vh, p.astype(BF16)) / den)
        o = jnp.concatenate(outs_t, axis=0).T
        pieces = [o[g * ATT_BLOCK:(g + 1) * ATT_BLOCK, kv * HEAD_DIM:(kv + 1) * HEAD_DIM]
                  for kv in range(N_KV) for g in range(GQA)]
        o_ref[s * ATT_BLOCK:(s + 1) * ATT_BLOCK, :] = jnp.concatenate(pieces, axis=-1).astype(o_ref.dtype)


def _t5_bucket(rel):
    nb = REL_BUCKETS // 2
    max_exact = nb // 2
    ret = (rel > 0).astype(jnp.int32) * nb
    n = jnp.abs(rel)
    large = max_exact + (jnp.log(jnp.maximum(n, 1).astype(F32) / max_exact)
                         / math.log(REL_MAX_DIST / max_exact) * (nb - max_exact)).astype(jnp.int32)
    large = jnp.minimum(large, nb - 1)
    return ret + jnp.where(n < max_exact, n, large)


def _attn_mixer(h3, rel_bias, sink, n_batch, seq):
    kw = ATT_BLOCK + 2 * WINDOW
    n_rel = kw + ATT_BLOCK - 1
    rel = jnp.arange(n_rel) - (n_rel - 1) // 2
    tb = rel_bias.astype(F32)[_t5_bucket(rel)]
    tb = jnp.where((jnp.abs(rel) <= WINDOW)[:, None], tb, NEG_INF)
    tb = jnp.pad(tb.T, ((0, 0), (0, 1)))
    flat = jnp.tile(tb, (1, ATT_BLOCK))[:, ATT_BLOCK - 1:ATT_BLOCK - 1 + ATT_BLOCK * n_rel]
    bias = flat.reshape(N_HEADS, ATT_BLOCK, n_rel)[:, :, :kw]
    bias = jnp.transpose(bias.reshape(N_KV, GQA, ATT_BLOCK, kw), (0, 3, 1, 2)).reshape(N_KV, kw, GQA * ATT_BLOCK)
    sink_rows = jnp.repeat(sink.astype(F32).reshape(N_KV, GQA), ATT_BLOCK, axis=1)[:, None, :]
    nblk = seq // ATT_ROWS
    sub = ATT_ROWS // ATT_BLOCK
    n128 = seq // ATT_BLOCK
    cur = lambda b, i: b * nblk + i
    prev = lambda b, i: b * n128 + jnp.maximum(i * sub - 1, 0)
    nxt = lambda b, i: b * n128 + jnp.minimum((i + 1) * sub, n128 - 1)
    t = n_batch * seq
    kern = functools.partial(_attn_kernel, seq=seq)
    return pl.pallas_call(
        kern,
        grid=(n_batch, nblk),
        in_specs=[pl.BlockSpec((ATT_ROWS, ATT_DIM), lambda b, i: (cur(b, i), 0)),
                  pl.BlockSpec((ATT_ROWS, KV_DIM), lambda b, i: (cur(b, i), 4)),
                  pl.BlockSpec((ATT_ROWS, KV_DIM), lambda b, i: (cur(b, i), 5)),
                  pl.BlockSpec((ATT_BLOCK, KV_DIM), lambda b, i: (prev(b, i), 4)),
                  pl.BlockSpec((ATT_BLOCK, KV_DIM), lambda b, i: (prev(b, i), 5)),
                  pl.BlockSpec((ATT_BLOCK, KV_DIM), lambda b, i: (nxt(b, i), 4)),
                  pl.BlockSpec((ATT_BLOCK, KV_DIM), lambda b, i: (nxt(b, i), 5)),
                  pl.BlockSpec((N_KV, kw, GQA * ATT_BLOCK), lambda b, i: (0, 0, 0)),
                  pl.BlockSpec((N_KV, 1, GQA * ATT_BLOCK), lambda b, i: (0, 0, 0))],
        out_specs=pl.BlockSpec((ATT_ROWS, ATT_DIM), lambda b, i: (cur(b, i), 0)),
        out_shape=jax.ShapeDtypeStruct((t, ATT_DIM), BF16),
        compiler_params=_cparams(2),
        name="window_gqa",
    )(h3, h3, h3, h3, h3, h3, h3, bias, sink_rows)


def _proj_ln_kernel(a_ref, w_ref, x_ref, g_ref, b_ref, h8_ref):
    y = _dot(a_ref[...], w_ref[...])
    _store_token_tiles(h8_ref, _layer_norm(ALPHA * x_ref[...] + y, g_ref[...], b_ref[...]))


def _proj_ln(a, w, x, g, b):
    t, k = a.shape
    tm = ROW_TILE
    return pl.pallas_call(
        _proj_ln_kernel,
        grid=(t // tm,),
        in_specs=[pl.BlockSpec((tm, k), lambda i: (i, 0)),
                  pl.BlockSpec((k, D_MODEL), lambda i: (0, 0)),
                  pl.BlockSpec((tm, D_MODEL), lambda i: (i, 0)),
                  pl.BlockSpec((1, D_MODEL), lambda i: (0, 0)),
                  pl.BlockSpec((1, D_MODEL), lambda i: (0, 0))],
        out_specs=pl.BlockSpec((tm * 8, 128), lambda i: (i, 0)),
        out_shape=jax.ShapeDtypeStruct((t * 8, 128), F32),
        compiler_params=_cparams(1),
        name="proj_ln",
    )(a, w, x, g, b)


def _router_kernel(h_ref, wh_ref, wl_ref, b_ref, o_ref, ot_ref, cnt_ref, run_ref):
    @pl.when(pl.program_id(0) == 0)
    def _():
        run_ref[...] = jnp.zeros_like(run_ref)

    x = _load_token_tiles(h_ref, ROW_TILE)
    xh = x.astype(BF16)
    xl = (x - xh.astype(F32)).astype(BF16)
    wh = wh_ref[...]
    lt = _dot_nt(wh, xh) + (_dot_nt(wh, xl) + _dot_nt(wl_ref[...], xh)) + b_ref[...][:, 0:1]
    tm = lt.shape[1]
    neg = jnp.float32(-jnp.inf)
    row8 = lax.broadcasted_iota(jnp.int32, (EPG, tm), 0).astype(F32)
    is_g = row8 < N_GROUPS
    gl = jnp.where(is_g, lt[0:EPG], neg)
    gmax = jnp.max(gl, axis=0, keepdims=True)
    gsum = jnp.sum(jnp.where(is_g, jnp.exp(gl - gmax), 0.0), axis=0, keepdims=True)
    grp = jnp.min(jnp.where(gl == gmax, row8, float(EPG)), axis=0, keepdims=True)
    p_grp = 1.0 / gsum
    el = lt[EPG:2 * EPG]
    for g in range(1, N_GROUPS):
        el = jnp.where(grp == float(g), lt[EPG * (g + 1):EPG * (g + 2)], el)
    m1 = jnp.max(el, axis=0, keepdims=True)
    i1 = jnp.min(jnp.where(el == m1, row8, float(EPG)), axis=0, keepdims=True)
    el2 = jnp.where(row8 == i1, neg, el)
    m2 = jnp.max(el2, axis=0, keepdims=True)
    i2 = jnp.min(jnp.where((row8 != i1) & (el2 == m2), row8, float(EPG)), axis=0, keepdims=True)
    e2 = jnp.exp(m2 - m1)
    g1 = p_grp / (1.0 + e2)
    g2 = p_grp * e2 / (1.0 + e2)
    eid1 = grp * EPG + i1
    eid2 = grp * EPG + i2
    rowe = lax.broadcasted_iota(jnp.int32, (N_EXPERTS, tm), 0).astype(F32)
    hit1 = rowe == eid1
    hit2 = rowe == eid2
    onehot = (hit1 | hit2).astype(BF16)
    upper = (lax.broadcasted_iota(jnp.int32, (tm, tm), 0) < lax.broadcasted_iota(jnp.int32, (tm, tm), 1)).astype(BF16)
    before = _dot(onehot, upper) + run_ref[...][:, 0:1]
    r1 = jnp.sum(jnp.where(hit1, before, 0.0), axis=0, keepdims=True)
    r2 = jnp.sum(jnp.where(hit2, before, 0.0), axis=0, keepdims=True)
    run_ref[...] += jnp.sum(onehot.astype(F32), axis=1, keepdims=True)
    cnt_ref[...] = run_ref[...]
    out_t = jnp.concatenate([eid1, eid2, g1, g2, r1, r2, jnp.zeros((2, tm), F32)], axis=0)
    ot_ref[...] = out_t
    o_ref[...] = jnp.concatenate([out_t, jnp.zeros((128 - 8, tm), F32)], axis=0).T


def _router(h8, w_group, b_group, w_router, b_router):
    t = h8.shape[0] // 8
    w = jnp.zeros((128, D_MODEL), F32)
    w = w.at[0:N_GROUPS].set(w_group.astype(F32).T).at[EPG:EPG + N_EXPERTS].set(w_router.astype(F32).T)
    wh = w.astype(BF16)
    wl = (w - wh.astype(F32)).astype(BF16)
    bias = jnp.zeros((128,), F32).at[0:N_GROUPS].set(b_group.astype(F32)).at[EPG:EPG + N_EXPERTS].set(
        b_router.astype(F32))
    bias = jnp.broadcast_to(bias[:, None], (128, 128))
    tm = ROW_TILE
    out, out_t, cnt = pl.pallas_call(
        _router_kernel,
        grid=(t // tm,),
        in_specs=[pl.BlockSpec((tm * 8, 128), lambda i: (i, 0)),
                  pl.BlockSpec((128, D_MODEL), lambda i: (0, 0)),
                  pl.BlockSpec((128, D_MODEL), lambda i: (0, 0)),
                  pl.BlockSpec((128, 128), lambda i: (0, 0))],
        out_specs=[pl.BlockSpec((tm, 128), lambda i: (i, 0)),
                   pl.BlockSpec((8, tm), lambda i: (0, i)),
                   pl.BlockSpec((N_EXPERTS, 128), lambda i: (0, 0))],
        out_shape=[jax.ShapeDtypeStruct((t, 128), F32), jax.ShapeDtypeStruct((8, t), F32),
                   jax.ShapeDtypeStruct((N_EXPERTS, 128), F32)],
        scratch_shapes=[pltpu.VMEM((N_EXPERTS, 128), F32)],
        compiler_params=_cparams(1),
        name="moe_router",
    )(h8, wh, wl, bias)
    eid = out_t[0:2].astype(jnp.int32)
    rank = out_t[4:6].astype(jnp.int32)
    counts = cnt[:, 0].astype(jnp.int32)
    return out, eid, rank, counts


def _token_tile(ref, tok):
    return ref.at[pl.ds(pl.multiple_of(tok * 8, 8), 8)]


def _experts_kernel(be_ref, nu_ref, src_hbm, h8_hbm, w1_ref, w3_ref, w2_ref, ys_ref,
                    w13_s, w2_s, xbuf, idx_smem, sem, isem):
    i = pl.program_id(0)
    n_used = nu_ref[0]
    used = i < n_used
    slot = i % N_ROW_BUFS

    blk_per_fetch = IDX_FETCH // MOE_ROWS

    def fetch_indices(blk):
        @pl.when(blk % blk_per_fetch == 0)
        def _():
            off = pl.multiple_of(blk * MOE_ROWS, IDX_FETCH)
            cp = pltpu.make_async_copy(src_hbm.at[pl.ds(off, IDX_FETCH)], idx_smem, isem)
            cp.start()
            cp.wait()

    def start_rows(blk, slot_, r0, n):
        base = (blk % blk_per_fetch) * MOE_ROWS
        toks = [idx_smem[base + r0 + u] for u in range(n)]
        for u in range(n):
            pltpu.make_async_copy(_token_tile(h8_hbm, toks[u]), _token_tile(xbuf.at[slot_], r0 + u),
                                  sem.at[slot_]).start(priority=u % 2)

    def gather_block(blk, slot_):
        fetch_indices(blk)
        def issue(r8, _):
            start_rows(blk, slot_, r8 * 8, 8)
            return 0
        lax.fori_loop(0, MOE_ROWS // 8, issue, 0)

    @pl.when(i == 0)
    def _():
        gather_block(0, 0)
        @pl.when(n_used > 1)
        def _():
            gather_block(1, 1)

    nxt = jnp.minimum(i + 2, n_used - 1)
    slot_n = (i + 2) % N_ROW_BUFS

    @pl.when(used)
    def _():
        fetch_indices(nxt)
        pltpu.make_async_copy(h8_hbm.at[pl.ds(0, MOE_ROWS * 8)], xbuf.at[slot], sem.at[slot]).wait()

    prev = be_ref[jnp.maximum(i - 1, 0)]
    fresh = (i == 0) | (be_ref[i] != prev)

    @pl.when(used & fresh)
    def _():
        w13_s[:, :D_EXPERT] = w1_ref[...].astype(BF16)
        w13_s[:, D_EXPERT:] = w3_ref[...].astype(BF16)
        w2_s[...] = w2_ref[...].astype(BF16)

    @pl.when(used)
    def _():
        n_tiles = 2 * D_EXPERT // MXU_COLS
        n_tiles2 = D_MODEL // MXU_COLS
        rows_per_group = MOE_ROWS // (n_tiles + n_tiles2)
        x = _load_token_tiles(xbuf.at[slot], MOE_ROWS).astype(BF16)
        parts = []
        for c in range(n_tiles):
            start_rows(nxt, slot_n, c * rows_per_group, rows_per_group)
            parts.append(_dot(x, w13_s[:, c * MXU_COLS:(c + 1) * MXU_COLS]))
        h = jnp.concatenate(parts, axis=-1)
        h1 = h[:, :D_EXPERT]
        hdn = ((h1 * jax.nn.sigmoid(h1)) * h[:, D_EXPERT:]).astype(BF16)
        for c in range(n_tiles2):
            start_rows(nxt, slot_n, (n_tiles + c) * rows_per_group, rows_per_group)
            yc = _dot(hdn, w2_s[:, c * MXU_COLS:(c + 1) * MXU_COLS])
            for k in range(MXU_COLS // 128):
                ys_ref[pl.ds(c * (MXU_COLS // 128) + k, MOE_ROWS, stride=8), :] = yc[:, k * 128:(k + 1) * 128]

    def wait_rows(slot_):
        pltpu.make_async_copy(h8_hbm.at[pl.ds(0, MOE_ROWS * 8)], xbuf.at[slot_], sem.at[slot_]).wait()

    @pl.when(i == n_used - 1)
    def _():
        wait_rows(slot_n)
        @pl.when(n_used > 1)
        def _():
            wait_rows((i + 1) % N_ROW_BUFS)

    @pl.when(jnp.logical_not(used))
    def _():
        ys_ref[...] = jnp.zeros_like(ys_ref)


def _experts(h8, src, blk_e, n_used, w1, w3, w2, layer, cap):
    nblk = cap // MOE_ROWS
    return pl.pallas_call(
        _experts_kernel,
        grid_spec=pltpu.PrefetchScalarGridSpec(
            num_scalar_prefetch=2,
            grid=(nblk,),
            in_specs=[pl.BlockSpec(memory_space=pl.ANY),
                      pl.BlockSpec(memory_space=pl.ANY),
                      pl.BlockSpec((None, None, D_MODEL, D_EXPERT), lambda i, be, nu: (layer, be[i], 0, 0)),
                      pl.BlockSpec((None, None, D_MODEL, D_EXPERT), lambda i, be, nu: (layer, be[i], 0, 0)),
                      pl.BlockSpec((None, None, D_EXPERT, D_MODEL), lambda i, be, nu: (layer, be[i], 0, 0))],
            out_specs=pl.BlockSpec((MOE_ROWS * 8, 128), lambda i, be, nu: (i, 0)),
            scratch_shapes=[pltpu.VMEM((D_MODEL, 2 * D_EXPERT), BF16),
                            pltpu.VMEM((D_EXPERT, D_MODEL), BF16),
                            pltpu.VMEM((N_ROW_BUFS, MOE_ROWS * 8, 128), F32),
                            pltpu.SMEM((IDX_FETCH,), jnp.int32),
                            pltpu.SemaphoreType.DMA((N_ROW_BUFS,)),
                            pltpu.SemaphoreType.DMA(())]),
        out_shape=jax.ShapeDtypeStruct((cap * 8, 128), F32),
        compiler_params=_cparams(1),
        name="moe_experts",
    )(blk_e, n_used, src, h8, w1, w3, w2)


def _combine_kernel(dst_hbm, ys_hbm, h_ref, gate_ref, g_ref, b_ref, o_ref, idx_smem, y0_ref, y1_ref, sem, isem):
    i = pl.program_id(0)
    tm = ROW_TILE
    slot = i % 2

    def gather_tile(tile, slot_):
        cp = pltpu.make_async_copy(dst_hbm.at[pl.ds(tile * 2 * tm, 2 * tm)], idx_smem, isem)
        cp.start()
        cp.wait()
        def issue(t, _):
            pltpu.make_async_copy(_token_tile(ys_hbm, idx_smem[t]), _token_tile(y0_ref.at[slot_], t),
                                  sem.at[slot_]).start(priority=0)
            pltpu.make_async_copy(_token_tile(ys_hbm, idx_smem[tm + t]), _token_tile(y1_ref.at[slot_], t),
                                  sem.at[slot_]).start(priority=1)
            return 0
        lax.fori_loop(0, tm, issue, 0, unroll=8)

    @pl.when(i == 0)
    def _():
        gather_tile(0, 0)

    @pl.when(i + 1 < pl.num_programs(0))
    def _():
        gather_tile(i + 1, 1 - slot)

    pltpu.make_async_copy(ys_hbm.at[pl.ds(0, tm * 8)], y0_ref.at[slot], sem.at[slot]).wait()
    pltpu.make_async_copy(ys_hbm.at[pl.ds(0, tm * 8)], y1_ref.at[slot], sem.at[slot]).wait()
    gate = gate_ref[...]
    y = (_load_token_tiles(y0_ref.at[slot], tm) * gate[:, 2:3]
         + _load_token_tiles(y1_ref.at[slot], tm) * gate[:, 3:4])
    o_ref[...] = _layer_norm(ALPHA * _load_token_tiles(h_ref, tm) + y, g_ref[...], b_ref[...])


def _combine_ln(dst, ys, h8, rout, ln_g, ln_b):
    t = h8.shape[0] // 8
    tm = ROW_TILE
    return pl.pallas_call(
        _combine_kernel,
        grid_spec=pltpu.PrefetchScalarGridSpec(
            num_scalar_prefetch=0,
            grid=(t // tm,),
            in_specs=[pl.BlockSpec(memory_space=pl.ANY),
                      pl.BlockSpec(memory_space=pl.ANY),
                      pl.BlockSpec((tm * 8, 128), lambda i: (i, 0)),
                      pl.BlockSpec((tm, 128), lambda i: (i, 0)),
                      pl.BlockSpec((1, D_MODEL), lambda i: (0, 0)),
                      pl.BlockSpec((1, D_MODEL), lambda i: (0, 0))],
            out_specs=pl.BlockSpec((tm, D_MODEL), lambda i: (i, 0)),
            scratch_shapes=[pltpu.SMEM((2 * tm,), jnp.int32),
                            pltpu.VMEM((2, tm * 8, 128), F32),
                            pltpu.VMEM((2, tm * 8, 128), F32),
                            pltpu.SemaphoreType.DMA((2,)),
                            pltpu.SemaphoreType.DMA(())]),
        out_shape=jax.ShapeDtypeStruct((t, D_MODEL), F32),
        compiler_params=_cparams(1),
        name="moe_combine_ln",
    )(dst, ys, h8, rout, ln_g, ln_b)


def _moe_layer(h8, w_group, b_group, w_router, b_router, w1, w3, w2, layer, ln_g, ln_b):
    t = h8.shape[0] // 8
    rout, eid, rank, counts = _router(h8, w_group, b_group, w_router, b_router)
    padded = (counts + MOE_ROWS - 1) // MOE_ROWS * MOE_ROWS
    pad_end = jnp.cumsum(padded)
    pad_start = pad_end - padded
    start_of = jnp.sum(jnp.where(eid[..., None] == jnp.arange(N_EXPERTS), pad_start, 0), axis=-1)
    dst = (start_of + rank).astype(jnp.int32)
    cap = 2 * t + N_EXPERTS * MOE_ROWS
    nblk = cap // MOE_ROWS
    tok = jnp.broadcast_to(jnp.arange(t, dtype=jnp.int32), (2, t))
    blk_e = jnp.minimum(jnp.sum(pad_end[None, :] <= (jnp.arange(nblk) * MOE_ROWS)[:, None], axis=1),
                        N_EXPERTS - 1).astype(jnp.int32)
    n_used = (pad_end[-1] // MOE_ROWS).astype(jnp.int32).reshape(1)
    _, by_row = lax.sort((dst.reshape(-1), tok.reshape(-1)), num_keys=1)
    cstart = jnp.cumsum(counts) - counts
    blk_first = cstart[blk_e] + (jnp.arange(nblk) * MOE_ROWS - pad_start[blk_e])
    blk_last = cstart[blk_e] + jnp.maximum(counts[blk_e] - 1, 0)
    idx = jnp.minimum(blk_first[:, None] + jnp.arange(MOE_ROWS)[None, :], blk_last[:, None])
    src = by_row[jnp.clip(idx, 0, 2 * t - 1)].reshape(-1)
    dst_tiles = jnp.transpose(dst.reshape(2, t // ROW_TILE, ROW_TILE), (1, 0, 2)).reshape(-1)
    ys = _experts(h8, src, blk_e, n_used, w1, w3, w2, layer, cap)
    return _combine_ln(dst_tiles, ys, h8, rout, ln_g, ln_b)


def kernel(x, ln_mix_g, ln_mix_b, ln_ffn_g, ln_ffn_b, ab_w_in, s5_lam_re, s5_lam_im, s5_log_dt, s5_b_re, s5_b_im,
           s5_c_re, s5_c_im, s5_d, s5_glu_w, s5_glu_b, gla_gate_w, gla_gate_b, gla_norm_g, ab_w_out, c_w_in,
           c_sink, c_w_out, rel_bias, moe_w_group, moe_b_group, moe_w_router, moe_b_router, moe_w1, moe_w3, moe_w2):
    n_batch, seq, _ = x.shape
    t = n_batch * seq
    xt = x.reshape(t, D_MODEL)
    row = lambda v: v.astype(F32).reshape(1, -1)

    w_in0 = jnp.pad(ab_w_in[0], ((0, 0), (0, AB_IN_PAD - AB_IN))).astype(BF16)
    h0, u_t = _inproj0(xt, w_in0)
    n_steps = seq // (S5_SEGS * S5_CHUNK)
    kcat, r1e, r2, coef, ptab = _s5_prep(s5_lam_re[0], s5_lam_im[0], s5_log_dt[0], s5_b_re[0], s5_b_im[0],
                                         s5_c_re[0], s5_c_im[0], s5_d[0], n_steps)
    ya = _s5_mixer(u_t, kcat, r1e, r2, coef, ptab, n_batch, n_steps)
    gw = gla_gate_w[0].astype(F32)
    wg = jnp.zeros((128, 2 * GLA_KEY), F32)
    wg = wg.at[0:GLA_RANK, 0:GLA_KEY].set(gw[0]).at[GLA_RANK:2 * GLA_RANK, GLA_KEY:].set(gw[1]).astype(BF16)
    bg = gla_gate_b[0].astype(F32).reshape(1, 2 * GLA_KEY)
    o_f, o_b = _gla_mixer(h0, wg, bg, n_batch, seq)
    h8 = _mix0_out(ya, o_f, o_b, h0, xt, s5_glu_w[0].astype(BF16), row(s5_glu_b[0]), row(gla_norm_g[0]),
                      ab_w_out[0].astype(BF16), row(ln_mix_g[0]), row(ln_mix_b[0]))
    h = _moe_layer(h8, moe_w_group[0], moe_b_group[0], moe_w_router[0], moe_b_router[0],
                   moe_w1, moe_w3, moe_w2, 0, row(ln_ffn_g[0]), row(ln_ffn_b[0]))

    h3 = _matmul(h, c_w_in[0].astype(BF16), BF16)
    o = _attn_mixer(h3, rel_bias, c_sink[0], n_batch, seq)
    h8 = _proj_ln(o, c_w_out[0].astype(BF16), h, row(ln_mix_g[1]), row(ln_mix_b[1]))
    h = _moe_layer(h8, moe_w_group[1], moe_b_group[1], moe_w_router[1], moe_b_router[1],
                   moe_w1, moe_w3, moe_w2, 1, row(ln_ffn_g[1]), row(ln_ffn_b[1]))
    return h.reshape(n_batch, seq, D_MODEL)
```

```python
import functools
import math

import jax
import jax.numpy as jnp
from jax import lax
from jax.experimental import pallas as pl
from jax.experimental.pallas import tpu as pltpu

F32 = jnp.float32
BF16 = jnp.bfloat16

D_MODEL = 1024
S5_WIDTH = 512
S5_GROUP_CH = 16
S5_GROUPS = 32
S5_STATE = 64
GLA_HEADS = 4
GLA_DV = 128
GLA_DK = 64
GLA_KEY = 256
GLA_WIDTH = 512
GLA_RANK = 16
GLA_TAU = 16.0
GLA_CHUNK = 64
AB_IN = 2080
AB_IN_PAD = 2176
HEAD_DIM = 64
N_HEADS = 16
N_KV = 4
GQA = 4
ATT_DIM = 1024
KV_DIM = 256
WINDOW = 128
ATT_BLOCK = 128
ATT_SCALE = HEAD_DIM ** -0.5
REL_BUCKETS = 32
REL_MAX_DIST = 128
NEG_INF = -1e30
N_GROUPS = 4
EPG = 8
N_EXPERTS = 32
D_EXPERT = 512
LN_EPS = 1e-5
RMS_EPS = 1e-6
DEPTH = 2
ALPHA = (2 * DEPTH) ** 0.25

S5_CHUNK = 16
S5_SEGS = 8
S5_SEG_PAD = 4
ROW_TILE = 512
GLA_ROWS = 256
ATT_ROWS = 512
MOE_ROWS = 256
IDX_FETCH = 1024
N_ROW_BUFS = 3
MXU_COLS = 256
VMEM_LIMIT = 56 * 1024 * 1024


def _cparams(n_axes):
    return pltpu.CompilerParams(dimension_semantics=("arbitrary",) * n_axes,
                                vmem_limit_bytes=VMEM_LIMIT)


def _dot(a, b):
    return jnp.dot(a, b, preferred_element_type=F32)


def _dot_nt(a, b):
    return lax.dot_general(a, b, (((1,), (1,)), ((), ())), preferred_element_type=F32)


def _dot_tn(a, b):
    return lax.dot_general(a, b, (((0,), (0,)), ((), ())), preferred_element_type=F32)


def _layer_norm(r, g, b):
    mu = jnp.mean(r, axis=-1, keepdims=True)
    c = r - mu
    var = jnp.mean(c * c, axis=-1, keepdims=True)
    return c * lax.rsqrt(var + LN_EPS) * g + b


def _store_token_tiles(ref, val):
    n = val.shape[0]
    for k in range(D_MODEL // 128):
        ref[pl.ds(k, n, stride=8), :] = val[:, k * 128:(k + 1) * 128]


def _load_token_tiles(ref, n):
    return jnp.concatenate([ref[pl.ds(k, n, stride=8), :] for k in range(D_MODEL // 128)], axis=-1)


def _s5_prep(lam_re, lam_im, log_dt, b_re, b_im, c_re, c_im, d, n_steps):
    tc = S5_CHUNK
    g, p, c = S5_GROUPS, S5_STATE, S5_GROUP_CH
    lr = jnp.minimum(lam_re.astype(F32), -1e-4)
    li = lam_im.astype(F32)
    dt = jnp.exp(log_dt.astype(F32))[..., None]
    mag = jnp.exp(lr * dt)
    ar = mag * jnp.cos(li * dt)
    ai = mag * jnp.sin(li * dt)
    den = lr * lr + li * li
    nr = ar - 1.0
    coef_r = (nr * lr + ai * li) / den
    coef_i = (ai * lr - nr * li) / den
    br_ = b_re.astype(F32)
    bi_ = b_im.astype(F32)
    bbr = coef_r[..., None] * br_ - coef_i[..., None] * bi_
    bbi = coef_r[..., None] * bi_ + coef_i[..., None] * br_
    cr = c_re.astype(F32)
    ci = c_im.astype(F32)
    bbr_t = jnp.swapaxes(bbr, -1, -2)
    bbi_t = jnp.swapaxes(bbi, -1, -2)
    npair = g // 2

    def apow(n):
        nn = n.astype(F32)[:, None, None, None]
        m_ = jnp.exp(nn * (lr * dt)[None])
        ang = nn * (li * dt)[None]
        return m_ * jnp.cos(ang), m_ * jnp.sin(ang)

    pr, pi = apow(jnp.arange(tc + 1))
    hr = pr[:, :, :, None, :] * bbr_t[None] - pi[:, :, :, None, :] * bbi_t[None]
    hi = pr[:, :, :, None, :] * bbi_t[None] + pi[:, :, :, None, :] * bbr_t[None]
    kk = jnp.einsum('dgoq,jdgcq->jdgco', cr, hr) - jnp.einsum('dgoq,jdgcq->jdgco', ci, hi)
    k0 = kk[0, 0] + kk[0, 1] + d.astype(F32).reshape(g, c)[:, :, None] * jnp.eye(c, dtype=F32)
    slab = jnp.concatenate([kk[tc - 1:0:-1, 1], k0[None], kk[1:tc, 0]], axis=0)
    kcat = jnp.transpose(slab, (1, 2, 0, 3)).reshape(g, c, (2 * tc - 1) * c)
    kcat = jnp.pad(kcat, ((0, 0), (0, 0), (0, 2 * tc * c - kcat.shape[-1]))).reshape(npair, 2, c, 2 * tc * c)

    def pair_blockdiag(m):
        m2 = m.reshape((npair, 2) + m.shape[1:])
        z = jnp.zeros_like(m2[:, 0])
        return jnp.concatenate([jnp.concatenate([m2[:, 0], z], axis=-1),
                                jnp.concatenate([z, m2[:, 1]], axis=-1)], axis=1)

    def w_plane(h_, lag_sel, dirn):
        w = jnp.transpose(h_[lag_sel, dirn], (1, 0, 2, 3)).reshape(g, tc * c, p)
        return pair_blockdiag(w)
    s_fwd = tc - 1 - jnp.arange(tc)
    s_bwd = jnp.arange(tc)
    r1e = jnp.concatenate([w_plane(hr, s_fwd, 0), w_plane(hi, s_fwd, 0), w_plane(hr, s_bwd, 1), w_plane(hi, s_bwd, 1)],
                          axis=2)

    lane = jnp.arange(tc * c)
    rep_t = (lane[None, :] // c == jnp.arange(tc)[:, None]).astype(F32)
    tile_co = (lane[None, :] % c == jnp.arange(c)[:, None]).astype(F32)
    spread = lambda a, sel: jnp.dot(a, sel, precision=lax.Precision.HIGHEST)
    def readout(e, dirn):
        pr_l = spread(jnp.transpose(pr[e, dirn], (1, 2, 0)).reshape(g * p, tc), rep_t)
        pi_l = spread(jnp.transpose(pi[e, dirn], (1, 2, 0)).reshape(g * p, tc), rep_t)
        cr_l = spread(jnp.transpose(cr[dirn], (0, 2, 1)).reshape(g * p, c), tile_co)
        ci_l = spread(jnp.transpose(ci[dirn], (0, 2, 1)).reshape(g * p, c), tile_co)
        zr = (cr_l * pr_l - ci_l * pi_l).reshape(g, p, tc * c)
        zi = (cr_l * pi_l + ci_l * pr_l).reshape(g, p, tc * c)
        return pair_blockdiag(zr), pair_blockdiag(-zi)
    vf_r, vf_i = readout(jnp.arange(tc) + 1, 0)
    vb_r, vb_i = readout(tc - jnp.arange(tc), 1)
    r2 = jnp.concatenate([vf_r, vf_i, vb_r, vb_i], axis=1)

    steps = jnp.arange(n_steps)
    a16r, a16i = apow(jnp.array([tc]))
    apr, api = apow(steps * tc)
    anr, ani = apow(jnp.array([tc * n_steps]))
    def lanes(x):
        n_ = x.shape[0]
        return jnp.transpose(x.reshape(n_, npair, 2 * p), (1, 0, 2))
    coef = jnp.concatenate([lanes(a16r[:, 0]), lanes(a16i[:, 0]), lanes(a16r[:, 1]), lanes(a16i[:, 1]),
                            lanes(anr[:, 0]), lanes(ani[:, 0]), lanes(anr[:, 1]), lanes(ani[:, 1])], axis=1)
    rev = n_steps - 1 - steps
    ptab = jnp.stack([lanes(apr[:, 0]), lanes(api[:, 0]), lanes(apr[rev, 1]), lanes(api[rev, 1])], axis=1)
    return kcat, r1e.astype(BF16), r2.astype(BF16), coef, ptab


def _gelu_tanh(x):
    return 0.5 * x * (1.0 + jnp.tanh(math.sqrt(2.0 / math.pi) * (x + 0.044715 * (x * x * x))))


def _s5_kernel(u_ref, kcat_ref, r1e_ref, r2_ref, coef_ref, ptab_ref, y_ref, kt_ref, yi_ref, e_ref, s_ref,
               *, n_batch, n_steps):
    pitch = n_steps + S5_SEG_PAD
    blk = S5_CHUNK * S5_GROUP_CH
    kt_ref[...] = jnp.zeros_like(kt_ref)
    for gi in range(2):
        slab = kcat_ref[gi]
        for s in range(S5_CHUNK):
            off = (S5_CHUNK - 1 - s) * S5_GROUP_CH
            kt_ref[gi * blk + s * S5_GROUP_CH:gi * blk + (s + 1) * S5_GROUP_CH, gi * blk:(gi + 1) * blk] = (
                slab[:, off:off + blk].astype(BF16))
    u = u_ref[...]
    yi_ref[...] = _dot(u, kt_ref[...])
    e = _dot(u, r1e_ref[...])
    n_seg = n_batch * S5_SEGS
    for k in range(4):
        for sg in range(n_seg):
            e_ref[k, sg * pitch:sg * pitch + n_steps, :] = e[sg * n_steps:(sg + 1) * n_steps, k * 128:(k + 1) * 128]
    coef = coef_ref[...]
    a16fr, a16fi, a16br, a16bi = coef[0:1], coef[1:2], coef[2:3], coef[3:4]
    anfr, anfi, anbr, anbi = coef[4:5], coef[5:6], coef[6:7], coef[7:8]
    zero = jnp.zeros((S5_SEGS, 128), F32)
    sub = lax.broadcasted_iota(jnp.int32, (S5_SEGS, 128), 0)

    def seg_rows(b, j):
        return pl.ds(b * S5_SEGS * pitch + j, S5_SEGS, stride=pitch)

    def local_step(j, carry):
        out = []
        for b in range(n_batch):
            sfr, sfi, sbr, sbi = carry[b]
            rf = seg_rows(b, j)
            rb = seg_rows(b, n_steps - 1 - j)
            s_ref[0, rf, :] = sfr
            s_ref[1, rf, :] = sfi
            s_ref[2, rb, :] = sbr
            s_ref[3, rb, :] = sbi
            efr = e_ref[0, rf, :]
            efi = e_ref[1, rf, :]
            ebr = e_ref[2, rb, :]
            ebi = e_ref[3, rb, :]
            out.append((a16fr * sfr - a16fi * sfi + efr, a16fr * sfi + a16fi * sfr + efi,
                        a16br * sbr - a16bi * sbi + ebr, a16br * sbi + a16bi * sbr + ebi))
        return tuple(out)

    ends = lax.fori_loop(0, n_steps, local_step, tuple((zero, zero, zero, zero) for _ in range(n_batch)))

    carries = []
    for b in range(n_batch):
        efr, efi, ebr, ebi = ends[b]
        cfr, cfi, cbr, cbi = zero, zero, zero, zero
        for _ in range(S5_SEGS - 1):
            tfr = anfr * cfr - anfi * cfi + efr
            tfi = anfr * cfi + anfi * cfr + efi
            cfr = jnp.where(sub == 0, 0.0, pltpu.roll(tfr, 1, 0))
            cfi = jnp.where(sub == 0, 0.0, pltpu.roll(tfi, 1, 0))
            tbr = anbr * cbr - anbi * cbi + ebr
            tbi = anbr * cbi + anbi * cbr + ebi
            cbr = jnp.where(sub == S5_SEGS - 1, 0.0, pltpu.roll(tbr, S5_SEGS - 1, 0))
            cbi = jnp.where(sub == S5_SEGS - 1, 0.0, pltpu.roll(tbi, S5_SEGS - 1, 0))
        carries.append((cfr, cfi, cbr, cbi))

    def fix_step(j, _):
        pfr = ptab_ref[0, pl.ds(j, 1), :]
        pfi = ptab_ref[1, pl.ds(j, 1), :]
        pbr = ptab_ref[2, pl.ds(j, 1), :]
        pbi = ptab_ref[3, pl.ds(j, 1), :]
        for b in range(n_batch):
            cfr, cfi, cbr, cbi = carries[b]
            rj = seg_rows(b, j)
            s_ref[0, rj, :] = s_ref[0, rj, :] + (pfr * cfr - pfi * cfi)
            s_ref[1, rj, :] = s_ref[1, rj, :] + (pfr * cfi + pfi * cfr)
            s_ref[2, rj, :] = s_ref[2, rj, :] + (pbr * cbr - pbi * cbi)
            s_ref[3, rj, :] = s_ref[3, rj, :] + (pbr * cbi + pbi * cbr)
        return 0

    lax.fori_loop(0, n_steps, fix_step, 0)

    s_all = jnp.concatenate(
        [jnp.concatenate([s_ref[k, sg * pitch:sg * pitch + n_steps, :] for sg in range(n_seg)], axis=0)
         for k in range(4)], axis=-1)
    y = yi_ref[...] + _dot(s_all.astype(BF16), r2_ref[...])
    y_ref[...] = _gelu_tanh(y).astype(y_ref.dtype)


def _s5_mixer(u_t, kcat, r1e, r2, coef, ptab, n_batch, n_steps):
    npair, m, _ = u_t.shape
    scan_rows = n_batch * S5_SEGS * (n_steps + S5_SEG_PAD)
    kern = functools.partial(_s5_kernel, n_batch=n_batch, n_steps=n_steps)
    return pl.pallas_call(
        kern,
        grid=(npair,),
        in_specs=[pl.BlockSpec((None, m, 512), lambda i: (i, 0, 0)),
                  pl.BlockSpec((None, 2, S5_GROUP_CH, 512), lambda i: (i, 0, 0, 0)),
                  pl.BlockSpec((None, 512, 512), lambda i: (i, 0, 0)),
                  pl.BlockSpec((None, 512, 512), lambda i: (i, 0, 0)),
                  pl.BlockSpec((None, 8, 128), lambda i: (i, 0, 0)),
                  pl.BlockSpec((None, 4, n_steps, 128), lambda i: (i, 0, 0, 0))],
        out_specs=pl.BlockSpec((None, m, 512), lambda i: (i, 0, 0)),
        out_shape=jax.ShapeDtypeStruct((npair, m, 512), BF16),
        scratch_shapes=[pltpu.VMEM((512, 512), BF16), pltpu.VMEM((m, 512), F32),
                        pltpu.VMEM((4, scan_rows, 128), F32), pltpu.VMEM((4, scan_rows, 128), F32)],
        compiler_params=_cparams(1),
        name="s5_scan",
    )(u_t, kcat, r1e, r2, coef, ptab)


def _inproj0_kernel(x_ref, w_ref, h_ref, ut_ref, u_s):
    h = _dot(x_ref[...].astype(BF16), w_ref[...])
    h_ref[...] = h[:, S5_WIDTH:].astype(h_ref.dtype)
    n_lane_blk = S5_WIDTH // 128
    for k in range(n_lane_blk):
        u_s[k] = h[:, k * 128:(k + 1) * 128]
    n_chunk = ROW_TILE // S5_CHUNK
    rows = [jnp.concatenate([u_s[k, pl.ds(s, n_chunk, stride=S5_CHUNK), :] for k in range(n_lane_blk)], axis=-1)
            for s in range(S5_CHUNK)]
    ch = S5_GROUP_CH
    for p in range(S5_GROUPS // 2):
        ut_ref[p] = jnp.concatenate([r[:, (2 * p + gi) * ch:(2 * p + gi + 1) * ch] for gi in range(2) for r in rows],
                                    axis=-1).astype(ut_ref.dtype)


def _inproj0(x, w):
    t, k = x.shape
    n = w.shape[1]
    tm = ROW_TILE
    npair = S5_GROUPS // 2
    return pl.pallas_call(
        _inproj0_kernel,
        grid=(t // tm,),
        in_specs=[pl.BlockSpec((tm, k), lambda i: (i, 0)),
                  pl.BlockSpec((k, n), lambda i: (0, 0))],
        out_specs=[pl.BlockSpec((tm, n - S5_WIDTH), lambda i: (i, 0)),
                   pl.BlockSpec((npair, tm // S5_CHUNK, 512), lambda i: (0, i, 0))],
        out_shape=[jax.ShapeDtypeStruct((t, n - S5_WIDTH), BF16),
                   jax.ShapeDtypeStruct((npair, t // S5_CHUNK, 512), BF16)],
        scratch_shapes=[pltpu.VMEM((S5_WIDTH // 128, tm, 128), F32)],
        compiler_params=_cparams(1),
        name="inproj0",
    )(x, w)


def _gla_kernel(qf_ref, kf_ref, vf_ref, lf_ref, qb_ref, kb_ref, vb_ref, lb_ref, wg_ref, bg_ref,
                of_ref, ob_ref, st_ref):
    @pl.when(pl.program_id(1) == 0)
    def _():
        st_ref[...] = jnp.zeros_like(st_ref)

    n_chunks = GLA_ROWS // GLA_CHUNK
    row = lax.broadcasted_iota(jnp.int32, (GLA_CHUNK, GLA_CHUNK), 0)
    col = lax.broadcasted_iota(jnp.int32, (GLA_CHUNK, GLA_CHUNK), 1)
    mask_f = col <= row
    mask_b = col > row
    brow = lax.broadcasted_iota(jnp.int32, (GLA_ROWS, GLA_ROWS), 0)
    bcol = lax.broadcasted_iota(jnp.int32, (GLA_ROWS, GLA_ROWS), 1)
    shift = GLA_CHUNK.bit_length() - 1
    same = (brow >> shift) == (bcol >> shift)
    ones_blk = same.astype(BF16)
    tri_f = (same & (bcol <= brow)).astype(BF16)
    tri_b = (same & (bcol >= brow)).astype(BF16)
    wg = wg_ref[...]
    bg = bg_ref[...]

    def direction(q_ref, k_ref, v_ref, l_ref, o_ref, d):
        z = _dot(l_ref[...], wg[:, d * GLA_KEY:(d + 1) * GLA_KEY]) + bg[:, d * GLA_KEY:(d + 1) * GLA_KEY]
        log_a = jax.nn.log_sigmoid(z) * (1.0 / GLA_TAU)
        la_hi = log_a.astype(BF16)
        la_lo = (log_a - la_hi.astype(F32)).astype(BF16)
        tri = tri_f if d == 0 else tri_b
        bc = _dot(tri, la_hi) + _dot(tri, la_lo)
        tot = _dot(ones_blk, la_hi) + _dot(ones_blk, la_lo)
        q = q_ref[...].astype(F32) * (GLA_DK ** -0.5)
        k = k_ref[...].astype(F32)
        qd_all = (q * jnp.exp(bc)).astype(BF16)
        kd_all = (k * jnp.exp(-bc)).astype(BF16)
        kc_all = (k * jnp.exp(tot - bc)).astype(BF16)
        decay_all = jnp.exp(tot)
        order = range(n_chunks) if d == 0 else range(n_chunks - 1, -1, -1)
        mask = mask_f if d == 0 else mask_b
        states = [st_ref[d, h] for h in range(GLA_HEADS)]
        for c in order:
            sl = slice(c * GLA_CHUNK, (c + 1) * GLA_CHUNK)
            qd, kd, kc = qd_all[sl], kd_all[sl], kc_all[sl]
            decay = decay_all[c * GLA_CHUNK:c * GLA_CHUNK + 1]
            v = v_ref[sl, :]
            outs = []
            for h in range(GLA_HEADS):
                ks = slice(h * GLA_DK, (h + 1) * GLA_DK)
                vs = slice(h * GLA_DV, (h + 1) * GLA_DV)
                s = jnp.where(mask, _dot_nt(qd[:, ks], kd[:, ks]), 0.0).astype(BF16)
                st = states[h]
                o = _dot(s, v[:, vs]) + _dot_nt(qd[:, ks], st.astype(BF16))
                states[h] = st * decay[:, ks] + _dot_tn(v[:, vs], kc[:, ks])
                outs.append(o)
            o_ref[sl, :] = jnp.concatenate(outs, axis=-1).astype(o_ref.dtype)
        for h in range(GLA_HEADS):
            st_ref[d, h] = states[h]

    direction(qf_ref, kf_ref, vf_ref, lf_ref, of_ref, 0)
    direction(qb_ref, kb_ref, vb_ref, lb_ref, ob_ref, 1)


def _gla_mixer(h0, wg, bg, n_batch, seq):
    nb = seq // GLA_ROWS
    r = GLA_ROWS
    fwd = lambda b, i: b * nb + i
    bwd = lambda b, i: b * nb + (nb - 1 - i)
    def spec(width, colblk, rowfn):
        return pl.BlockSpec((r, width), lambda b, i: (rowfn(b, i), colblk))
    in_specs = [spec(256, 0, fwd), spec(256, 1, fwd), spec(512, 1, fwd), spec(128, 12, fwd),
                spec(256, 0, bwd), spec(256, 1, bwd), spec(512, 1, bwd), spec(128, 12, bwd),
                pl.BlockSpec((128, 512), lambda b, i: (0, 0)),
                pl.BlockSpec((1, 512), lambda b, i: (0, 0))]
    out_specs = [pl.BlockSpec((r, 512), lambda b, i: (fwd(b, i), 0)),
                 pl.BlockSpec((r, 512), lambda b, i: (bwd(b, i), 0))]
    t = n_batch * seq
    return pl.pallas_call(
        _gla_kernel,
        grid=(n_batch, nb),
        in_specs=in_specs,
        out_specs=out_specs,
        out_shape=[jax.ShapeDtypeStruct((t, 512), BF16), jax.ShapeDtypeStruct((t, 512), BF16)],
        scratch_shapes=[pltpu.VMEM((2, GLA_HEADS, GLA_DV, GLA_DK), F32)],
        compiler_params=_cparams(2),
        name="gla_chunked",
    )(h0, h0, h0, h0, h0, h0, h0, h0, wg, bg)


def _mix0_out_kernel(y_ref, of_ref, ob_ref, go_ref, x_ref, gw_ref, gb_ref, ng_ref, wo_ref, lg_ref, lb_ref,
                     h8_ref, ya_s):
    n_chunk = ROW_TILE // S5_CHUNK
    ys = [y_ref[p].astype(F32) for p in range(S5_GROUPS // 2)]
    n_lane_blk = S5_WIDTH // 128
    ch = S5_GROUP_CH
    blk = S5_CHUNK * ch
    for t in range(S5_CHUNK):
        for k in range(n_lane_blk):
            ya_s[k, pl.ds(t, n_chunk, stride=S5_CHUNK), :] = jnp.concatenate(
                [y[:, gi * blk + t * ch:gi * blk + (t + 1) * ch] for y in ys[4 * k:4 * k + 4] for gi in range(2)],
                axis=-1)
    yaf = jnp.concatenate([ya_s[k] for k in range(n_lane_blk)], axis=-1)
    ya = yaf.astype(BF16)
    gate = _dot(ya, gw_ref[...]) + gb_ref[...]
    ya2 = yaf * jax.nn.sigmoid(gate)
    o = of_ref[...].astype(F32) + ob_ref[...].astype(F32)
    ng = ng_ref[...]
    parts = []
    for h in range(GLA_HEADS):
        oh = o[:, h * GLA_DV:(h + 1) * GLA_DV]
        ms = jnp.mean(oh * oh, axis=-1, keepdims=True)
        parts.append(oh * lax.rsqrt(ms + RMS_EPS) * ng[:, h * GLA_DV:(h + 1) * GLA_DV])
    yb = jnp.concatenate(parts, axis=-1) * jax.nn.silu(go_ref[...].astype(F32))
    wo = wo_ref[...]
    y = _dot(ya2.astype(BF16), wo[:S5_WIDTH]) + _dot(yb.astype(BF16), wo[S5_WIDTH:])
    _store_token_tiles(h8_ref, _layer_norm(ALPHA * x_ref[...] + y, lg_ref[...], lb_ref[...]))


def _mix0_out(ya, o_f, o_b, h0, x, glu_w, glu_b, norm_g, w_out, ln_g, ln_b):
    t = x.shape[0]
    tm = ROW_TILE
    row = lambda w: pl.BlockSpec((tm, w), lambda i: (i, 0))
    full = lambda a, b: pl.BlockSpec((a, b), lambda i: (0, 0))
    return pl.pallas_call(
        _mix0_out_kernel,
        grid=(t // tm,),
        in_specs=[pl.BlockSpec((S5_GROUPS // 2, tm // S5_CHUNK, 512), lambda i: (0, i, 0)),
                  row(512), row(512),
                  pl.BlockSpec((tm, 512), lambda i: (i, 2)),
                  row(1024), full(512, 512), full(1, 512), full(1, 512), full(1024, 1024),
                  full(1, 1024), full(1, 1024)],
        out_specs=pl.BlockSpec((tm * 8, 128), lambda i: (i, 0)),
        out_shape=jax.ShapeDtypeStruct((t * 8, 128), F32),
        scratch_shapes=[pltpu.VMEM((S5_WIDTH // 128, tm, 128), F32)],
        compiler_params=_cparams(1),
        name="mix0_out_ln",
    )(ya, o_f, o_b, h0, x, glu_w, glu_b, norm_g, w_out, ln_g, ln_b)


def _attn_kernel(q_ref, kc_ref, vc_ref, kp_ref, vp_ref, kn_ref, vn_ref, bias_ref, sink_ref, o_ref, *, seq):
    blk = pl.program_id(1)
    n_blk = pl.num_programs(1)
    n_sub = ATT_ROWS // ATT_BLOCK
    kw = ATT_BLOCK + 2 * WINDOW
    nq = GQA * ATT_BLOCK
    key_row = lax.broadcasted_iota(jnp.int32, (kw, nq), 0)
    pen_first = jnp.where((blk == 0) & (key_row < WINDOW), NEG_INF, 0.0)
    pen_last = jnp.where((blk == n_blk - 1) & (key_row >= ATT_BLOCK + WINDOW), NEG_INF, 0.0)
    kall = jnp.concatenate([kp_ref[...], kc_ref[...], kn_ref[...]], axis=0)
    vall = jnp.concatenate([vp_ref[...], vc_ref[...], vn_ref[...]], axis=0)
    for s in range(n_sub):
        kwin = kall[s * ATT_BLOCK:s * ATT_BLOCK + kw]
        vwin = vall[s * ATT_BLOCK:s * ATT_BLOCK + kw]
        q = q_ref[s * ATT_BLOCK:(s + 1) * ATT_BLOCK, :] * ATT_SCALE
        outs_t = []
        for kv in range(N_KV):
            qs = jnp.concatenate([q[:, (kv * GQA + g) * HEAD_DIM:(kv * GQA + g + 1) * HEAD_DIM]
                                  for g in range(GQA)], axis=0)
            kh = kwin[:, kv * HEAD_DIM:(kv + 1) * HEAD_DIM]
            vh = vwin[:, kv * HEAD_DIM:(kv + 1) * HEAD_DIM]
            st = _dot_nt(kh, qs) + bias_ref[kv]
            if s == 0:
                st = st + pen_first
            if s == n_sub - 1:
                st = st + pen_last
            sink = sink_ref[kv]
            m = jnp.maximum(jnp.max(st, axis=0, keepdims=True), sink)
            p = jnp.exp(st - m)
            den = jnp.sum(p, axis=0, keepdims=True) + jnp.exp(sink - m)
            outs_t.append(_dot_tn(vh, p.astype(BF16)) / den)
        o = jnp.concatenate(outs_t, axis=0).T
        pieces = [o[g * ATT_BLOCK:(g + 1) * ATT_BLOCK, kv * HEAD_DIM:(kv + 1) * HEAD_DIM]
                  for kv in range(N_KV) for g in range(GQA)]
        o_ref[s * ATT_BLOCK:(s + 1) * ATT_BLOCK, :] = jnp.concatenate(pieces, axis=-1).astype(o_ref.dtype)


def _t5_bucket(rel):
    nb = REL_BUCKETS // 2
    max_exact = nb // 2
    ret = (rel > 0).astype(jnp.int32) * nb
    n = jnp.abs(rel)
    large = max_exact + (jnp.log(jnp.maximum(n, 1).astype(F32) / max_exact)
                         / math.log(REL_MAX_DIST / max_exact) * (nb - max_exact)).astype(jnp.int32)
    large = jnp.minimum(large, nb - 1)
    return ret + jnp.where(n < max_exact, n, large)


def _attn_mixer(h3, rel_bias, sink, n_batch, seq):
    kw = ATT_BLOCK + 2 * WINDOW
    n_rel = kw + ATT_BLOCK - 1
    rel = jnp.arange(n_rel) - (n_rel - 1) // 2
    tb = rel_bias.astype(F32)[_t5_bucket(rel)]
    tb = jnp.where((jnp.abs(rel) <= WINDOW)[:, None], tb, NEG_INF)
    tb = jnp.pad(tb.T, ((0, 0), (0, 1)))
    flat = jnp.tile(tb, (1, ATT_BLOCK))[:, ATT_BLOCK - 1:ATT_BLOCK - 1 + ATT_BLOCK * n_rel]
    bias = flat.reshape(N_HEADS, ATT_BLOCK, n_rel)[:, :, :kw]
    bias = jnp.transpose(bias.reshape(N_KV, GQA, ATT_BLOCK, kw), (0, 3, 1, 2)).reshape(N_KV, kw, GQA * ATT_BLOCK)
    sink_rows = jnp.repeat(sink.astype(F32).reshape(N_KV, GQA), ATT_BLOCK, axis=1)[:, None, :]
    nblk = seq // ATT_ROWS
    sub = ATT_ROWS // ATT_BLOCK
    n128 = seq // ATT_BLOCK
    cur = lambda b, i: b * nblk + i
    prev = lambda b, i: b * n128 + jnp.maximum(i * sub - 1, 0)
    nxt = lambda b, i: b * n128 + jnp.minimum((i + 1) * sub, n128 - 1)
    t = n_batch * seq
    kern = functools.partial(_attn_kernel, seq=seq)
    return pl.pallas_call(
        kern,
        grid=(n_batch, nblk),
        in_specs=[pl.BlockSpec((ATT_ROWS, ATT_DIM), lambda b, i: (cur(b, i), 0)),
                  pl.BlockSpec((ATT_ROWS, KV_DIM), lambda b, i: (cur(b, i), 4)),
                  pl.BlockSpec((ATT_ROWS, KV_DIM), lambda b, i: (cur(b, i), 5)),
                  pl.BlockSpec((ATT_BLOCK, KV_DIM), lambda b, i: (prev(b, i), 4)),
                  pl.BlockSpec((ATT_BLOCK, KV_DIM), lambda b, i: (prev(b, i), 5)),
                  pl.BlockSpec((ATT_BLOCK, KV_DIM), lambda b, i: (nxt(b, i), 4)),
                  pl.BlockSpec((ATT_BLOCK, KV_DIM), lambda b, i: (nxt(b, i), 5)),
                  pl.BlockSpec((N_KV, kw, GQA * ATT_BLOCK), lambda b, i: (0, 0, 0)),
                  pl.BlockSpec((N_KV, 1, GQA * ATT_BLOCK), lambda b, i: (0, 0, 0))],
        out_specs=pl.BlockSpec((ATT_ROWS, ATT_DIM), lambda b, i: (cur(b, i), 0)),
        out_shape=jax.ShapeDtypeStruct((t, ATT_DIM), BF16),
        compiler_params=_cparams(2),
        name="window_gqa",
    )(h3, h3, h3, h3, h3, h3, h3, bias, sink_rows)


def _proj_ln_kernel(a_ref, w_ref, x_ref, g_ref, b_ref, h8_ref):
    y = _dot(a_ref[...], w_ref[...])
    _store_token_tiles(h8_ref, _layer_norm(ALPHA * x_ref[...] + y, g_ref[...], b_ref[...]))


def _proj_ln(a, w, x, g, b):
    t, k = a.shape
    tm = ROW_TILE
    return pl.pallas_call(
        _proj_ln_kernel,
        grid=(t // tm,),
        in_specs=[pl.BlockSpec((tm, k), lambda i: (i, 0)),
                  pl.BlockSpec((k, D_MODEL), lambda i: (0, 0)),
                  pl.BlockSpec((tm, D_MODEL), lambda i: (i, 0)),
                  pl.BlockSpec((1, D_MODEL), lambda i: (0, 0)),
                  pl.BlockSpec((1, D_MODEL), lambda i: (0, 0))],
        out_specs=pl.BlockSpec((tm * 8, 128), lambda i: (i, 0)),
        out_shape=jax.ShapeDtypeStruct((t * 8, 128), F32),
        compiler_params=_cparams(1),
        name="proj_ln",
    )(a, w, x, g, b)


def _router_kernel(h_ref, wh_ref, wl_ref, b_ref, o_ref, ot_ref, cnt_ref, run_ref):
    @pl.when(pl.program_id(0) == 0)
    def _():
        run_ref[...] = jnp.zeros_like(run_ref)

    x = _load_token_tiles(h_ref, ROW_TILE)
    xh = x.astype(BF16)
    xl = (x - xh.astype(F32)).astype(BF16)
    wh = wh_ref[...]
    lt = _dot_nt(wh, xh) + (_dot_nt(wh, xl) + _dot_nt(wl_ref[...], xh)) + b_ref[...][:, 0:1]
    tm = lt.shape[1]
    neg = jnp.float32(-jnp.inf)
    row8 = lax.broadcasted_iota(jnp.int32, (EPG, tm), 0).astype(F32)
    is_g = row8 < N_GROUPS
    gl = jnp.where(is_g, lt[0:EPG], neg)
    gmax = jnp.max(gl, axis=0, keepdims=True)
    gsum = jnp.sum(jnp.where(is_g, jnp.exp(gl - gmax), 0.0), axis=0, keepdims=True)
    grp = jnp.min(jnp.where(gl == gmax, row8, float(EPG)), axis=0, keepdims=True)
    p_grp = 1.0 / gsum
    el = lt[EPG:2 * EPG]
    for g in range(1, N_GROUPS):
        el = jnp.where(grp == float(g), lt[EPG * (g + 1):EPG * (g + 2)], el)
    m1 = jnp.max(el, axis=0, keepdims=True)
    i1 = jnp.min(jnp.where(el == m1, row8, float(EPG)), axis=0, keepdims=True)
    el2 = jnp.where(row8 == i1, neg, el)
    m2 = jnp.max(el2, axis=0, keepdims=True)
    i2 = jnp.min(jnp.where((row8 != i1) & (el2 == m2), row8, float(EPG)), axis=0, keepdims=True)
    e2 = jnp.exp(m2 - m1)
    g1 = p_grp / (1.0 + e2)
    g2 = p_grp * e2 / (1.0 + e2)
    eid1 = grp * EPG + i1
    eid2 = grp * EPG + i2
    rowe = lax.broadcasted_iota(jnp.int32, (N_EXPERTS, tm), 0).astype(F32)
    hit1 = rowe == eid1
    hit2 = rowe == eid2
    onehot = (hit1 | hit2).astype(BF16)
    upper = (lax.broadcasted_iota(jnp.int32, (tm, tm), 0) < lax.broadcasted_iota(jnp.int32, (tm, tm), 1)).astype(BF16)
    before = _dot(onehot, upper) + run_ref[...][:, 0:1]
    r1 = jnp.sum(jnp.where(hit1, before, 0.0), axis=0, keepdims=True)
    r2 = jnp.sum(jnp.where(hit2, before, 0.0), axis=0, keepdims=True)
    run_ref[...] += jnp.sum(onehot.astype(F32), axis=1, keepdims=True)
    cnt_ref[...] = run_ref[...]
    out_t = jnp.concatenate([eid1, eid2, g1, g2, r1, r2, jnp.zeros((2, tm), F32)], axis=0)
    ot_ref[...] = out_t
    o_ref[...] = jnp.concatenate([out_t, jnp.zeros((128 - 8, tm), F32)], axis=0).T


def _router(h8, w_group, b_group, w_router, b_router):
    t = h8.shape[0] // 8
    w = jnp.zeros((128, D_MODEL), F32)
    w = w.at[0:N_GROUPS].set(w_group.astype(F32).T).at[EPG:EPG + N_EXPERTS].set(w_router.astype(F32).T)
    wh = w.astype(BF16)
    wl = (w - wh.astype(F32)).astype(BF16)
    bias = jnp.zeros((128,), F32).at[0:N_GROUPS].set(b_group.astype(F32)).at[EPG:EPG + N_EXPERTS].set(
        b_router.astype(F32))
    bias = jnp.broadcast_to(bias[:, None], (128, 128))
    tm = ROW_TILE
    out, out_t, cnt = pl.pallas_call(
        _router_kernel,
        grid=(t // tm,),
        in_specs=[pl.BlockSpec((tm * 8, 128), lambda i: (i, 0)),
                  pl.BlockSpec((128, D_MODEL), lambda i: (0, 0)),
                  pl.BlockSpec((128, D_MODEL), lambda i: (0, 0)),
                  pl.BlockSpec((128, 128), lambda i: (0, 0))],
        out_specs=[pl.BlockSpec((tm, 128), lambda i: (i, 0)),
                   pl.BlockSpec((8, tm), lambda i: (0, i)),
                   pl.BlockSpec((N_EXPERTS, 128), lambda i: (0, 0))],
        out_shape=[jax.ShapeDtypeStruct((t, 128), F32), jax.ShapeDtypeStruct((8, t), F32),
                   jax.ShapeDtypeStruct((N_EXPERTS, 128), F32)],
        scratch_shapes=[pltpu.VMEM((N_EXPERTS, 128), F32)],
        compiler_params=_cparams(1),
        name="moe_router",
    )(h8, wh, wl, bias)
    eid = out_t[0:2].astype(jnp.int32)
    rank = out_t[4:6].astype(jnp.int32)
    counts = cnt[:, 0].astype(jnp.int32)
    return out, eid, rank, counts


def _token_tile(ref, tok):
    return ref.at[pl.ds(pl.multiple_of(tok * 8, 8), 8)]


def _experts_kernel(be_ref, nu_ref, src_hbm, h8_hbm, w1_ref, w3_ref, w2_ref, ys_ref,
                    w13_s, w2_s, xbuf, idx_smem, sem, isem):
    i = pl.program_id(0)
    n_used = nu_ref[0]
    used = i < n_used
    slot = i % N_ROW_BUFS

    blk_per_fetch = IDX_FETCH // MOE_ROWS

    def fetch_indices(blk):
        @pl.when(blk % blk_per_fetch == 0)
        def _():
            off = pl.multiple_of(blk * MOE_ROWS, IDX_FETCH)
            cp = pltpu.make_async_copy(src_hbm.at[pl.ds(off, IDX_FETCH)], idx_smem, isem)
            cp.start()
            cp.wait()

    def start_rows(blk, slot_, r0, n):
        base = (blk % blk_per_fetch) * MOE_ROWS
        toks = [idx_smem[base + r0 + u] for u in range(n)]
        for u in range(n):
            pltpu.make_async_copy(_token_tile(h8_hbm, toks[u]), _token_tile(xbuf.at[slot_], r0 + u),
                                  sem.at[slot_]).start(priority=u % 2)

    def gather_block(blk, slot_):
        fetch_indices(blk)
        def issue(r8, _):
            start_rows(blk, slot_, r8 * 8, 8)
            return 0
        lax.fori_loop(0, MOE_ROWS // 8, issue, 0)

    @pl.when(i == 0)
    def _():
        gather_block(0, 0)
        @pl.when(n_used > 1)
        def _():
            gather_block(1, 1)

    nxt = jnp.minimum(i + 2, n_used - 1)
    slot_n = (i + 2) % N_ROW_BUFS

    @pl.when(used)
    def _():
        fetch_indices(nxt)
        pltpu.make_async_copy(h8_hbm.at[pl.ds(0, MOE_ROWS * 8)], xbuf.at[slot], sem.at[slot]).wait()

    prev = be_ref[jnp.maximum(i - 1, 0)]
    fresh = (i == 0) | (be_ref[i] != prev)

    @pl.when(used & fresh)
    def _():
        w13_s[:, :D_EXPERT] = w1_ref[...].astype(BF16)
        w13_s[:, D_EXPERT:] = w3_ref[...].astype(BF16)
        w2_s[...] = w2_ref[...].astype(BF16)

    @pl.when(used)
    def _():
        n_tiles = 2 * D_EXPERT // MXU_COLS
        n_tiles2 = D_MODEL // MXU_COLS
        rows_per_group = MOE_ROWS // (n_tiles + n_tiles2)
        x = _load_token_tiles(xbuf.at[slot], MOE_ROWS).astype(BF16)
        parts = []
        for c in range(n_tiles):
            start_rows(nxt, slot_n, c * rows_per_group, rows_per_group)
            parts.append(_dot(x, w13_s[:, c * MXU_COLS:(c + 1) * MXU_COLS]))
        h = jnp.concatenate(parts, axis=-1)
        h1 = h[:, :D_EXPERT]
        hdn = ((h1 * jax.nn.sigmoid(h1)) * h[:, D_EXPERT:]).astype(BF16)
        for c in range(n_tiles2):
            start_rows(nxt, slot_n, (n_tiles + c) * rows_per_group, rows_per_group)
            yc = _dot(hdn, w2_s[:, c * MXU_COLS:(c + 1) * MXU_COLS])
            for k in range(MXU_COLS // 128):
                ys_ref[pl.ds(c * (MXU_COLS // 128) + k, MOE_ROWS, stride=8), :] = yc[:, k * 128:(k + 1) * 128]

    def wait_rows(slot_):
        pltpu.make_async_copy(h8_hbm.at[pl.ds(0, MOE_ROWS * 8)], xbuf.at[slot_], sem.at[slot_]).wait()

    @pl.when(i == n_used - 1)
    def _():
        wait_rows(slot_n)
        @pl.when(n_used > 1)
        def _():
            wait_rows((i + 1) % N_ROW_BUFS)

    @pl.when(jnp.logical_not(used))
    def _():
        ys_ref[...] = jnp.zeros_like(ys_ref)


def _experts(h8, src, blk_e, n_used, w1, w3, w2, layer, cap):
    nblk = cap // MOE_ROWS
    return pl.pallas_call(
        _experts_kernel,
        grid_spec=pltpu.PrefetchScalarGridSpec(
            num_scalar_prefetch=2,
            grid=(nblk,),
            in_specs=[pl.BlockSpec(memory_space=pl.ANY),
                      pl.BlockSpec(memory_space=pl.ANY),
                      pl.BlockSpec((None, None, D_MODEL, D_EXPERT), lambda i, be, nu: (layer, be[i], 0, 0)),
                      pl.BlockSpec((None, None, D_MODEL, D_EXPERT), lambda i, be, nu: (layer, be[i], 0, 0)),
                      pl.BlockSpec((None, None, D_EXPERT, D_MODEL), lambda i, be, nu: (layer, be[i], 0, 0))],
            out_specs=pl.BlockSpec((MOE_ROWS * 8, 128), lambda i, be, nu: (i, 0)),
            scratch_shapes=[pltpu.VMEM((D_MODEL, 2 * D_EXPERT), BF16),
                            pltpu.VMEM((D_EXPERT, D_MODEL), BF16),
                            pltpu.VMEM((N_ROW_BUFS, MOE_ROWS * 8, 128), F32),
                            pltpu.SMEM((IDX_FETCH,), jnp.int32),
                            pltpu.SemaphoreType.DMA((N_ROW_BUFS,)),
                            pltpu.SemaphoreType.DMA(())]),
        out_shape=jax.ShapeDtypeStruct((cap * 8, 128), F32),
        compiler_params=_cparams(1),
        name="moe_experts",
    )(blk_e, n_used, src, h8, w1, w3, w2)


def _combine_kernel(dst_hbm, ys_hbm, h_ref, gate_ref, g_ref, b_ref, *rest, with_proj):
    if with_proj:
        w_ref, o_ref, p_ref, idx_smem, y0_ref, y1_ref, sem, isem = rest
    else:
        o_ref, idx_smem, y0_ref, y1_ref, sem, isem = rest
    i = pl.program_id(0)
    tm = ROW_TILE
    slot = i % 2

    def gather_tile(tile, slot_):
        cp = pltpu.make_async_copy(dst_hbm.at[pl.ds(tile * 2 * tm, 2 * tm)], idx_smem, isem)
        cp.start()
        cp.wait()
        def issue(t, _):
            pltpu.make_async_copy(_token_tile(ys_hbm, idx_smem[t]), _token_tile(y0_ref.at[slot_], t),
                                  sem.at[slot_]).start(priority=0)
            pltpu.make_async_copy(_token_tile(ys_hbm, idx_smem[tm + t]), _token_tile(y1_ref.at[slot_], t),
                                  sem.at[slot_]).start(priority=1)
            return 0
        lax.fori_loop(0, tm, issue, 0, unroll=8)

    @pl.when(i == 0)
    def _():
        gather_tile(0, 0)

    @pl.when(i + 1 < pl.num_programs(0))
    def _():
        gather_tile(i + 1, 1 - slot)

    pltpu.make_async_copy(ys_hbm.at[pl.ds(0, tm * 8)], y0_ref.at[slot], sem.at[slot]).wait()
    pltpu.make_async_copy(ys_hbm.at[pl.ds(0, tm * 8)], y1_ref.at[slot], sem.at[slot]).wait()
    gate = gate_ref[...]
    y = (_load_token_tiles(y0_ref.at[slot], tm) * gate[:, 2:3]
         + _load_token_tiles(y1_ref.at[slot], tm) * gate[:, 3:4])
    out = _layer_norm(ALPHA * _load_token_tiles(h_ref, tm) + y, g_ref[...], b_ref[...])
    o_ref[...] = out
    if with_proj:
        p_ref[...] = _dot(out.astype(BF16), w_ref[...]).astype(p_ref.dtype)


def _combine_ln(dst, ys, h8, rout, ln_g, ln_b, w_next=None):
    t = h8.shape[0] // 8
    tm = ROW_TILE
    with_proj = w_next is not None
    extra_in = [pl.BlockSpec(w_next.shape, lambda i: (0, 0))] if with_proj else []
    extra_out = [pl.BlockSpec((tm, w_next.shape[1]), lambda i: (i, 0))] if with_proj else []
    extra_shape = [jax.ShapeDtypeStruct((t, w_next.shape[1]), BF16)] if with_proj else []
    res = pl.pallas_call(
        functools.partial(_combine_kernel, with_proj=with_proj),
        grid_spec=pltpu.PrefetchScalarGridSpec(
            num_scalar_prefetch=0,
            grid=(t // tm,),
            in_specs=[pl.BlockSpec(memory_space=pl.ANY),
                      pl.BlockSpec(memory_space=pl.ANY),
                      pl.BlockSpec((tm * 8, 128), lambda i: (i, 0)),
                      pl.BlockSpec((tm, 128), lambda i: (i, 0)),
                      pl.BlockSpec((1, D_MODEL), lambda i: (0, 0)),
                      pl.BlockSpec((1, D_MODEL), lambda i: (0, 0))] + extra_in,
            out_specs=[pl.BlockSpec((tm, D_MODEL), lambda i: (i, 0))] + extra_out,
            scratch_shapes=[pltpu.SMEM((2 * tm,), jnp.int32),
                            pltpu.VMEM((2, tm * 8, 128), F32),
                            pltpu.VMEM((2, tm * 8, 128), F32),
                            pltpu.SemaphoreType.DMA((2,)),
                            pltpu.SemaphoreType.DMA(())]),
        out_shape=[jax.ShapeDtypeStruct((t, D_MODEL), F32)] + extra_shape,
        compiler_params=_cparams(1),
        name="moe_combine_ln",
    )(dst, ys, h8, rout, ln_g, ln_b, *([w_next] if with_proj else []))
    return tuple(res) if with_proj else res[0]


def _moe_layer(h8, w_group, b_group, w_router, b_router, w1, w3, w2, layer, ln_g, ln_b, w_next=None):
    t = h8.shape[0] // 8
    rout, eid, rank, counts = _router(h8, w_group, b_group, w_router, b_router)
    padded = (counts + MOE_ROWS - 1) // MOE_ROWS * MOE_ROWS
    pad_end = jnp.cumsum(padded)
    pad_start = pad_end - padded
    start_of = jnp.sum(jnp.where(eid[..., None] == jnp.arange(N_EXPERTS), pad_start, 0), axis=-1)
    dst = (start_of + rank).astype(jnp.int32)
    cap = 2 * t + N_EXPERTS * MOE_ROWS
    nblk = cap // MOE_ROWS
    tok = jnp.broadcast_to(jnp.arange(t, dtype=jnp.int32), (2, t))
    blk_e = jnp.minimum(jnp.sum(pad_end[None, :] <= (jnp.arange(nblk) * MOE_ROWS)[:, None], axis=1),
                        N_EXPERTS - 1).astype(jnp.int32)
    n_used = (pad_end[-1] // MOE_ROWS).astype(jnp.int32).reshape(1)
    _, by_row = lax.sort((dst.reshape(-1), tok.reshape(-1)), num_keys=1)
    cstart = jnp.cumsum(counts) - counts
    blk_first = cstart[blk_e] + (jnp.arange(nblk) * MOE_ROWS - pad_start[blk_e])
    blk_last = cstart[blk_e] + jnp.maximum(counts[blk_e] - 1, 0)
    idx = jnp.minimum(blk_first[:, None] + jnp.arange(MOE_ROWS)[None, :], blk_last[:, None])
    src = by_row[jnp.clip(idx, 0, 2 * t - 1)].reshape(-1)
    dst_tiles = jnp.transpose(dst.reshape(2, t // ROW_TILE, ROW_TILE), (1, 0, 2)).reshape(-1)
    ys = _experts(h8, src, blk_e, n_used, w1, w3, w2, layer, cap)
    return _combine_ln(dst_tiles, ys, h8, rout, ln_g, ln_b, w_next)


def kernel(x, ln_mix_g, ln_mix_b, ln_ffn_g, ln_ffn_b, ab_w_in, s5_lam_re, s5_lam_im, s5_log_dt, s5_b_re, s5_b_im,
           s5_c_re, s5_c_im, s5_d, s5_glu_w, s5_glu_b, gla_gate_w, gla_gate_b, gla_norm_g, ab_w_out, c_w_in,
           c_sink, c_w_out, rel_bias, moe_w_group, moe_b_group, moe_w_router, moe_b_router, moe_w1, moe_w3, moe_w2):
    n_batch, seq, _ = x.shape
    t = n_batch * seq
    xt = x.reshape(t, D_MODEL)
    row = lambda v: v.astype(F32).reshape(1, -1)

    w_in0 = jnp.pad(ab_w_in[0], ((0, 0), (0, AB_IN_PAD - AB_IN))).astype(BF16)
    h0, u_t = _inproj0(xt, w_in0)
    n_steps = seq // (S5_SEGS * S5_CHUNK)
    kcat, r1e, r2, coef, ptab = _s5_prep(s5_lam_re[0], s5_lam_im[0], s5_log_dt[0], s5_b_re[0], s5_b_im[0],
                                         s5_c_re[0], s5_c_im[0], s5_d[0], n_steps)
    ya = _s5_mixer(u_t, kcat, r1e, r2, coef, ptab, n_batch, n_steps)
    gw = gla_gate_w[0].astype(F32)
    wg = jnp.zeros((128, 2 * GLA_KEY), F32)
    wg = wg.at[0:GLA_RANK, 0:GLA_KEY].set(gw[0]).at[GLA_RANK:2 * GLA_RANK, GLA_KEY:].set(gw[1]).astype(BF16)
    bg = gla_gate_b[0].astype(F32).reshape(1, 2 * GLA_KEY)
    o_f, o_b = _gla_mixer(h0, wg, bg, n_batch, seq)
    h8 = _mix0_out(ya, o_f, o_b, h0, xt, s5_glu_w[0].astype(BF16), row(s5_glu_b[0]), row(gla_norm_g[0]),
                      ab_w_out[0].astype(BF16), row(ln_mix_g[0]), row(ln_mix_b[0]))
    h, h3 = _moe_layer(h8, moe_w_group[0], moe_b_group[0], moe_w_router[0], moe_b_router[0],
                       moe_w1, moe_w3, moe_w2, 0, row(ln_ffn_g[0]), row(ln_ffn_b[0]), c_w_in[0].astype(BF16))

    o = _attn_mixer(h3, rel_bias, c_sink[0], n_batch, seq)
    h8 = _proj_ln(o, c_w_out[0].astype(BF16), h, row(ln_mix_g[1]), row(ln_mix_b[1]))
    h = _moe_layer(h8, moe_w_group[1], moe_b_group[1], moe_w_router[1], moe_b_router[1],
                   moe_w1, moe_w3, moe_w2, 1, row(ln_ffn_g[1]), row(ln_ffn_b[1]))
    return h.reshape(n_batch, seq, D_MODEL)
```

```python
import functools
import math

import jax
import jax.numpy as jnp
from jax import lax
from jax.experimental import pallas as pl
from jax.experimental.pallas import tpu as pltpu

F32 = jnp.float32
BF16 = jnp.bfloat16

D_MODEL = 1024
S5_WIDTH = 512
S5_GROUP_CH = 16
S5_GROUPS = 32
S5_STATE = 64
GLA_HEADS = 4
GLA_DV = 128
GLA_DK = 64
GLA_KEY = 256
GLA_WIDTH = 512
GLA_RANK = 16
GLA_TAU = 16.0
GLA_CHUNK = 64
AB_IN = 2080
AB_IN_PAD = 2176
HEAD_DIM = 64
N_HEADS = 16
N_KV = 4
GQA = 4
ATT_DIM = 1024
KV_DIM = 256
WINDOW = 128
ATT_BLOCK = 128
ATT_SCALE = HEAD_DIM ** -0.5
REL_BUCKETS = 32
REL_MAX_DIST = 128
NEG_INF = -1e30
N_GROUPS = 4
EPG = 8
N_EXPERTS = 32
D_EXPERT = 512
LN_EPS = 1e-5
RMS_EPS = 1e-6
DEPTH = 2
ALPHA = (2 * DEPTH) ** 0.25

S5_CHUNK = 16
S5_SEGS = 8
S5_SEG_PAD = 4
ROW_TILE = 512
GLA_ROWS = 256
ATT_ROWS = 512
MOE_ROWS = 256
IDX_FETCH = 1024
N_ROW_BUFS = 3
MXU_COLS = 256
VMEM_LIMIT = 56 * 1024 * 1024


def _cparams(n_axes):
    return pltpu.CompilerParams(dimension_semantics=("arbitrary",) * n_axes,
                                vmem_limit_bytes=VMEM_LIMIT)


def _dot(a, b):
    return jnp.dot(a, b, preferred_element_type=F32)


def _dot_nt(a, b):
    return lax.dot_general(a, b, (((1,), (1,)), ((), ())), preferred_element_type=F32)


def _dot_tn(a, b):
    return lax.dot_general(a, b, (((0,), (0,)), ((), ())), preferred_element_type=F32)


def _layer_norm(r, g, b):
    mu = jnp.mean(r, axis=-1, keepdims=True)
    c = r - mu
    var = jnp.mean(c * c, axis=-1, keepdims=True)
    return c * lax.rsqrt(var + LN_EPS) * g + b


def _store_token_tiles(ref, val):
    n = val.shape[0]
    for k in range(D_MODEL // 128):
        ref[pl.ds(k, n, stride=8), :] = val[:, k * 128:(k + 1) * 128]


def _load_token_tiles(ref, n):
    return jnp.concatenate([ref[pl.ds(k, n, stride=8), :] for k in range(D_MODEL // 128)], axis=-1)


def _s5_prep(lam_re, lam_im, log_dt, b_re, b_im, c_re, c_im, d, n_steps):
    tc = S5_CHUNK
    g, p, c = S5_GROUPS, S5_STATE, S5_GROUP_CH
    lr = jnp.minimum(lam_re.astype(F32), -1e-4)
    li = lam_im.astype(F32)
    dt = jnp.exp(log_dt.astype(F32))[..., None]
    mag = jnp.exp(lr * dt)
    ar = mag * jnp.cos(li * dt)
    ai = mag * jnp.sin(li * dt)
    den = lr * lr + li * li
    nr = ar - 1.0
    coef_r = (nr * lr + ai * li) / den
    coef_i = (ai * lr - nr * li) / den
    br_ = b_re.astype(F32)
    bi_ = b_im.astype(F32)
    bbr = coef_r[..., None] * br_ - coef_i[..., None] * bi_
    bbi = coef_r[..., None] * bi_ + coef_i[..., None] * br_
    cr = c_re.astype(F32)
    ci = c_im.astype(F32)
    bbr_t = jnp.swapaxes(bbr, -1, -2)
    bbi_t = jnp.swapaxes(bbi, -1, -2)
    npair = g // 2

    def apow(n):
        nn = n.astype(F32)[:, None, None, None]
        m_ = jnp.exp(nn * (lr * dt)[None])
        ang = nn * (li * dt)[None]
        return m_ * jnp.cos(ang), m_ * jnp.sin(ang)

    pr, pi = apow(jnp.arange(tc + 1))
    hr = pr[:, :, :, None, :] * bbr_t[None] - pi[:, :, :, None, :] * bbi_t[None]
    hi = pr[:, :, :, None, :] * bbi_t[None] + pi[:, :, :, None, :] * bbr_t[None]
    kk = jnp.einsum('dgoq,jdgcq->jdgco', cr, hr) - jnp.einsum('dgoq,jdgcq->jdgco', ci, hi)
    k0 = kk[0, 0] + kk[0, 1] + d.astype(F32).reshape(g, c)[:, :, None] * jnp.eye(c, dtype=F32)
    slab = jnp.concatenate([kk[tc - 1:0:-1, 1], k0[None], kk[1:tc, 0]], axis=0)
    kcat = jnp.transpose(slab, (1, 2, 0, 3)).reshape(g, c, (2 * tc - 1) * c)
    kcat = jnp.pad(kcat, ((0, 0), (0, 0), (0, 2 * tc * c - kcat.shape[-1]))).reshape(npair, 2, c, 2 * tc * c)

    def pair_blockdiag(m):
        m2 = m.reshape((npair, 2) + m.shape[1:])
        z = jnp.zeros_like(m2[:, 0])
        return jnp.concatenate([jnp.concatenate([m2[:, 0], z], axis=-1),
                                jnp.concatenate([z, m2[:, 1]], axis=-1)], axis=1)

    def w_plane(h_, lag_sel, dirn):
        w = jnp.transpose(h_[lag_sel, dirn], (1, 0, 2, 3)).reshape(g, tc * c, p)
        return pair_blockdiag(w)
    s_fwd = tc - 1 - jnp.arange(tc)
    s_bwd = jnp.arange(tc)
    r1e = jnp.concatenate([w_plane(hr, s_fwd, 0), w_plane(hi, s_fwd, 0), w_plane(hr, s_bwd, 1), w_plane(hi, s_bwd, 1)],
                          axis=2)

    lane = jnp.arange(tc * c)
    rep_t = (lane[None, :] // c == jnp.arange(tc)[:, None]).astype(F32)
    tile_co = (lane[None, :] % c == jnp.arange(c)[:, None]).astype(F32)
    spread = lambda a, sel: jnp.dot(a, sel, precision=lax.Precision.HIGHEST)
    def readout(e, dirn):
        pr_l = spread(jnp.transpose(pr[e, dirn], (1, 2, 0)).reshape(g * p, tc), rep_t)
        pi_l = spread(jnp.transpose(pi[e, dirn], (1, 2, 0)).reshape(g * p, tc), rep_t)
        cr_l = spread(jnp.transpose(cr[dirn], (0, 2, 1)).reshape(g * p, c), tile_co)
        ci_l = spread(jnp.transpose(ci[dirn], (0, 2, 1)).reshape(g * p, c), tile_co)
        zr = (cr_l * pr_l - ci_l * pi_l).reshape(g, p, tc * c)
        zi = (cr_l * pi_l + ci_l * pr_l).reshape(g, p, tc * c)
        return pair_blockdiag(zr), pair_blockdiag(-zi)
    vf_r, vf_i = readout(jnp.arange(tc) + 1, 0)
    vb_r, vb_i = readout(tc - jnp.arange(tc), 1)
    r2 = jnp.concatenate([vf_r, vf_i, vb_r, vb_i], axis=1)

    steps = jnp.arange(n_steps)
    a16r, a16i = apow(jnp.array([tc]))
    apr, api = apow(steps * tc)
    anr, ani = apow(jnp.array([tc * n_steps]))
    def lanes(x):
        n_ = x.shape[0]
        return jnp.transpose(x.reshape(n_, npair, 2 * p), (1, 0, 2))
    coef = jnp.concatenate([lanes(a16r[:, 0]), lanes(a16i[:, 0]), lanes(a16r[:, 1]), lanes(a16i[:, 1]),
                            lanes(anr[:, 0]), lanes(ani[:, 0]), lanes(anr[:, 1]), lanes(ani[:, 1])], axis=1)
    rev = n_steps - 1 - steps
    ptab = jnp.stack([lanes(apr[:, 0]), lanes(api[:, 0]), lanes(apr[rev, 1]), lanes(api[rev, 1])], axis=1)
    return kcat, r1e.astype(BF16), r2.astype(BF16), coef, ptab


def _gelu_tanh(x):
    return 0.5 * x * (1.0 + jnp.tanh(math.sqrt(2.0 / math.pi) * (x + 0.044715 * (x * x * x))))


def _s5_kernel(u_ref, kcat_ref, r1e_ref, r2_ref, coef_ref, ptab_ref, y_ref, kt_ref, yi_ref, e_ref, s_ref,
               *, n_batch, n_steps):
    pitch = n_steps + S5_SEG_PAD
    blk = S5_CHUNK * S5_GROUP_CH
    kt_ref[...] = jnp.zeros_like(kt_ref)
    for gi in range(2):
        slab = kcat_ref[gi]
        for s in range(S5_CHUNK):
            off = (S5_CHUNK - 1 - s) * S5_GROUP_CH
            kt_ref[gi * blk + s * S5_GROUP_CH:gi * blk + (s + 1) * S5_GROUP_CH, gi * blk:(gi + 1) * blk] = (
                slab[:, off:off + blk].astype(BF16))
    u = u_ref[...]
    yi_ref[...] = _dot(u, kt_ref[...])
    e = _dot(u, r1e_ref[...])
    n_seg = n_batch * S5_SEGS
    for k in range(4):
        for sg in range(n_seg):
            e_ref[k, sg * pitch:sg * pitch + n_steps, :] = e[sg * n_steps:(sg + 1) * n_steps, k * 128:(k + 1) * 128]
    coef = coef_ref[...]
    a16fr, a16fi, a16br, a16bi = coef[0:1], coef[1:2], coef[2:3], coef[3:4]
    anfr, anfi, anbr, anbi = coef[4:5], coef[5:6], coef[6:7], coef[7:8]
    zero = jnp.zeros((S5_SEGS, 128), F32)
    sub = lax.broadcasted_iota(jnp.int32, (S5_SEGS, 128), 0)

    def seg_rows(b, j):
        return pl.ds(b * S5_SEGS * pitch + j, S5_SEGS, stride=pitch)

    def local_step(j, carry):
        out = []
        for b in range(n_batch):
            sfr, sfi, sbr, sbi = carry[b]
            rf = seg_rows(b, j)
            rb = seg_rows(b, n_steps - 1 - j)
            s_ref[0, rf, :] = sfr
            s_ref[1, rf, :] = sfi
            s_ref[2, rb, :] = sbr
            s_ref[3, rb, :] = sbi
            efr = e_ref[0, rf, :]
            efi = e_ref[1, rf, :]
            ebr = e_ref[2, rb, :]
            ebi = e_ref[3, rb, :]
            out.append((a16fr * sfr - a16fi * sfi + efr, a16fr * sfi + a16fi * sfr + efi,
                        a16br * sbr - a16bi * sbi + ebr, a16br * sbi + a16bi * sbr + ebi))
        return tuple(out)

    ends = lax.fori_loop(0, n_steps, local_step, tuple((zero, zero, zero, zero) for _ in range(n_batch)))

    carries = []
    for b in range(n_batch):
        efr, efi, ebr, ebi = ends[b]
        cfr, cfi, cbr, cbi = zero, zero, zero, zero
        for _ in range(S5_SEGS - 1):
            tfr = anfr * cfr - anfi * cfi + efr
            tfi = anfr * cfi + anfi * cfr + efi
            cfr = jnp.where(sub == 0, 0.0, pltpu.roll(tfr, 1, 0))
            cfi = jnp.where(sub == 0, 0.0, pltpu.roll(tfi, 1, 0))
            tbr = anbr * cbr - anbi * cbi + ebr
            tbi = anbr * cbi + anbi * cbr + ebi
            cbr = jnp.where(sub == S5_SEGS - 1, 0.0, pltpu.roll(tbr, S5_SEGS - 1, 0))
            cbi = jnp.where(sub == S5_SEGS - 1, 0.0, pltpu.roll(tbi, S5_SEGS - 1, 0))
        carries.append((cfr, cfi, cbr, cbi))

    def fix_step(j, _):
        pfr = ptab_ref[0, pl.ds(j, 1), :]
        pfi = ptab_ref[1, pl.ds(j, 1), :]
        pbr = ptab_ref[2, pl.ds(j, 1), :]
        pbi = ptab_ref[3, pl.ds(j, 1), :]
        for b in range(n_batch):
            cfr, cfi, cbr, cbi = carries[b]
            rj = seg_rows(b, j)
            s_ref[0, rj, :] = s_ref[0, rj, :] + (pfr * cfr - pfi * cfi)
            s_ref[1, rj, :] = s_ref[1, rj, :] + (pfr * cfi + pfi * cfr)
            s_ref[2, rj, :] = s_ref[2, rj, :] + (pbr * cbr - pbi * cbi)
            s_ref[3, rj, :] = s_ref[3, rj, :] + (pbr * cbi + pbi * cbr)
        return 0

    lax.fori_loop(0, n_steps, fix_step, 0)

    s_all = jnp.concatenate(
        [jnp.concatenate([s_ref[k, sg * pitch:sg * pitch + n_steps, :] for sg in range(n_seg)], axis=0)
         for k in range(4)], axis=-1)
    y = yi_ref[...] + _dot(s_all.astype(BF16), r2_ref[...])
    y_ref[...] = _gelu_tanh(y).astype(y_ref.dtype)


def _s5_mixer(u_t, kcat, r1e, r2, coef, ptab, n_batch, n_steps):
    npair, m, _ = u_t.shape
    scan_rows = n_batch * S5_SEGS * (n_steps + S5_SEG_PAD)
    kern = functools.partial(_s5_kernel, n_batch=n_batch, n_steps=n_steps)
    return pl.pallas_call(
        kern,
        grid=(npair,),
        in_specs=[pl.BlockSpec((None, m, 512), lambda i: (i, 0, 0)),
                  pl.BlockSpec((None, 2, S5_GROUP_CH, 512), lambda i: (i, 0, 0, 0)),
                  pl.BlockSpec((None, 512, 512), lambda i: (i, 0, 0)),
                  pl.BlockSpec((None, 512, 512), lambda i: (i, 0, 0)),
                  pl.BlockSpec((None, 8, 128), lambda i: (i, 0, 0)),
                  pl.BlockSpec((None, 4, n_steps, 128), lambda i: (i, 0, 0, 0))],
        out_specs=pl.BlockSpec((None, m, 512), lambda i: (i, 0, 0)),
        out_shape=jax.ShapeDtypeStruct((npair, m, 512), BF16),
        scratch_shapes=[pltpu.VMEM((512, 512), BF16), pltpu.VMEM((m, 512), F32),
                        pltpu.VMEM((4, scan_rows, 128), F32), pltpu.VMEM((4, scan_rows, 128), F32)],
        compiler_params=_cparams(1),
        name="s5_scan",
    )(u_t, kcat, r1e, r2, coef, ptab)


def _inproj0_kernel(x_ref, w_ref, h_ref, ut_ref, u_s):
    h = _dot(x_ref[...].astype(BF16), w_ref[...])
    h_ref[...] = h[:, S5_WIDTH:].astype(h_ref.dtype)
    n_lane_blk = S5_WIDTH // 128
    for k in range(n_lane_blk):
        u_s[k] = h[:, k * 128:(k + 1) * 128]
    n_chunk = ROW_TILE // S5_CHUNK
    rows = [jnp.concatenate([u_s[k, pl.ds(s, n_chunk, stride=S5_CHUNK), :] for k in range(n_lane_blk)], axis=-1)
            for s in range(S5_CHUNK)]
    ch = S5_GROUP_CH
    for p in range(S5_GROUPS // 2):
        ut_ref[p] = jnp.concatenate([r[:, (2 * p + gi) * ch:(2 * p + gi + 1) * ch] for gi in range(2) for r in rows],
                                    axis=-1).astype(ut_ref.dtype)


def _inproj0(x, w):
    t, k = x.shape
    n = w.shape[1]
    tm = ROW_TILE
    npair = S5_GROUPS // 2
    return pl.pallas_call(
        _inproj0_kernel,
        grid=(t // tm,),
        in_specs=[pl.BlockSpec((tm, k), lambda i: (i, 0)),
                  pl.BlockSpec((k, n), lambda i: (0, 0))],
        out_specs=[pl.BlockSpec((tm, n - S5_WIDTH), lambda i: (i, 0)),
                   pl.BlockSpec((npair, tm // S5_CHUNK, 512), lambda i: (0, i, 0))],
        out_shape=[jax.ShapeDtypeStruct((t, n - S5_WIDTH), BF16),
                   jax.ShapeDtypeStruct((npair, t // S5_CHUNK, 512), BF16)],
        scratch_shapes=[pltpu.VMEM((S5_WIDTH // 128, tm, 128), F32)],
        compiler_params=_cparams(1),
        name="inproj0",
    )(x, w)


def _gla_kernel(qf_ref, kf_ref, vf_ref, lf_ref, qb_ref, kb_ref, vb_ref, lb_ref, wg_ref, bg_ref,
                of_ref, ob_ref, st_ref):
    @pl.when(pl.program_id(1) == 0)
    def _():
        st_ref[...] = jnp.zeros_like(st_ref)

    n_chunks = GLA_ROWS // GLA_CHUNK
    row = lax.broadcasted_iota(jnp.int32, (GLA_CHUNK, GLA_CHUNK), 0)
    col = lax.broadcasted_iota(jnp.int32, (GLA_CHUNK, GLA_CHUNK), 1)
    mask_f = col <= row
    mask_b = col > row
    brow = lax.broadcasted_iota(jnp.int32, (GLA_ROWS, GLA_ROWS), 0)
    bcol = lax.broadcasted_iota(jnp.int32, (GLA_ROWS, GLA_ROWS), 1)
    shift = GLA_CHUNK.bit_length() - 1
    same = (brow >> shift) == (bcol >> shift)
    ones_blk = same.astype(BF16)
    tri_f = (same & (bcol <= brow)).astype(BF16)
    tri_b = (same & (bcol >= brow)).astype(BF16)
    wg = wg_ref[...]
    bg = bg_ref[...]

    def direction(q_ref, k_ref, v_ref, l_ref, o_ref, d):
        z = _dot(l_ref[...], wg[:, d * GLA_KEY:(d + 1) * GLA_KEY]) + bg[:, d * GLA_KEY:(d + 1) * GLA_KEY]
        log_a = jax.nn.log_sigmoid(z) * (1.0 / GLA_TAU)
        la_hi = log_a.astype(BF16)
        la_lo = (log_a - la_hi.astype(F32)).astype(BF16)
        tri = tri_f if d == 0 else tri_b
        bc = _dot(tri, la_hi) + _dot(tri, la_lo)
        tot = _dot(ones_blk, la_hi) + _dot(ones_blk, la_lo)
        q = q_ref[...].astype(F32) * (GLA_DK ** -0.5)
        k = k_ref[...].astype(F32)
        qd_all = (q * jnp.exp(bc)).astype(BF16)
        kd_all = (k * jnp.exp(-bc)).astype(BF16)
        kc_all = (k * jnp.exp(tot - bc)).astype(BF16)
        decay_all = jnp.exp(tot)
        order = range(n_chunks) if d == 0 else range(n_chunks - 1, -1, -1)
        mask = mask_f if d == 0 else mask_b
        states = [st_ref[d, h] for h in range(GLA_HEADS)]
        for c in order:
            sl = slice(c * GLA_CHUNK, (c + 1) * GLA_CHUNK)
            qd, kd, kc = qd_all[sl], kd_all[sl], kc_all[sl]
            decay = decay_all[c * GLA_CHUNK:c * GLA_CHUNK + 1]
            v = v_ref[sl, :]
            outs = []
            for h in range(GLA_HEADS):
                ks = slice(h * GLA_DK, (h + 1) * GLA_DK)
                vs = slice(h * GLA_DV, (h + 1) * GLA_DV)
                s = jnp.where(mask, _dot_nt(qd[:, ks], kd[:, ks]), 0.0).astype(BF16)
                st = states[h]
                o = _dot(s, v[:, vs]) + _dot_nt(qd[:, ks], st.astype(BF16))
                states[h] = st * decay[:, ks] + _dot_tn(v[:, vs], kc[:, ks])
                outs.append(o)
            o_ref[sl, :] = jnp.concatenate(outs, axis=-1).astype(o_ref.dtype)
        for h in range(GLA_HEADS):
            st_ref[d, h] = states[h]

    direction(qf_ref, kf_ref, vf_ref, lf_ref, of_ref, 0)
    direction(qb_ref, kb_ref, vb_ref, lb_ref, ob_ref, 1)


def _gla_mixer(h0, wg, bg, n_batch, seq):
    nb = seq // GLA_ROWS
    r = GLA_ROWS
    fwd = lambda b, i: b * nb + i
    bwd = lambda b, i: b * nb + (nb - 1 - i)
    def spec(width, colblk, rowfn):
        return pl.BlockSpec((r, width), lambda b, i: (rowfn(b, i), colblk))
    in_specs = [spec(256, 0, fwd), spec(256, 1, fwd), spec(512, 1, fwd), spec(128, 12, fwd),
                spec(256, 0, bwd), spec(256, 1, bwd), spec(512, 1, bwd), spec(128, 12, bwd),
                pl.BlockSpec((128, 512), lambda b, i: (0, 0)),
                pl.BlockSpec((1, 512), lambda b, i: (0, 0))]
    out_specs = [pl.BlockSpec((r, 512), lambda b, i: (fwd(b, i), 0)),
                 pl.BlockSpec((r, 512), lambda b, i: (bwd(b, i), 0))]
    t = n_batch * seq
    return pl.pallas_call(
        _gla_kernel,
        grid=(n_batch, nb),
        in_specs=in_specs,
        out_specs=out_specs,
        out_shape=[jax.ShapeDtypeStruct((t, 512), BF16), jax.ShapeDtypeStruct((t, 512), BF16)],
        scratch_shapes=[pltpu.VMEM((2, GLA_HEADS, GLA_DV, GLA_DK), F32)],
        compiler_params=_cparams(2),
        name="gla_chunked",
    )(h0, h0, h0, h0, h0, h0, h0, h0, wg, bg)


def _mix0_out_kernel(y_ref, of_ref, ob_ref, go_ref, x_ref, gw_ref, gb_ref, ng_ref, wo_ref, lg_ref, lb_ref,
                     h8_ref, ya_s):
    n_chunk = ROW_TILE // S5_CHUNK
    ys = [y_ref[p].astype(F32) for p in range(S5_GROUPS // 2)]
    n_lane_blk = S5_WIDTH // 128
    ch = S5_GROUP_CH
    blk = S5_CHUNK * ch
    for t in range(S5_CHUNK):
        for k in range(n_lane_blk):
            ya_s[k, pl.ds(t, n_chunk, stride=S5_CHUNK), :] = jnp.concatenate(
                [y[:, gi * blk + t * ch:gi * blk + (t + 1) * ch] for y in ys[4 * k:4 * k + 4] for gi in range(2)],
                axis=-1)
    yaf = jnp.concatenate([ya_s[k] for k in range(n_lane_blk)], axis=-1)
    ya = yaf.astype(BF16)
    gate = _dot(ya, gw_ref[...]) + gb_ref[...]
    ya2 = yaf * jax.nn.sigmoid(gate)
    o = of_ref[...].astype(F32) + ob_ref[...].astype(F32)
    ng = ng_ref[...]
    parts = []
    for h in range(GLA_HEADS):
        oh = o[:, h * GLA_DV:(h + 1) * GLA_DV]
        ms = jnp.mean(oh * oh, axis=-1, keepdims=True)
        parts.append(oh * lax.rsqrt(ms + RMS_EPS) * ng[:, h * GLA_DV:(h + 1) * GLA_DV])
    yb = jnp.concatenate(parts, axis=-1) * jax.nn.silu(go_ref[...].astype(F32))
    wo = wo_ref[...]
    y = _dot(ya2.astype(BF16), wo[:S5_WIDTH]) + _dot(yb.astype(BF16), wo[S5_WIDTH:])
    _store_token_tiles(h8_ref, _layer_norm(ALPHA * x_ref[...] + y, lg_ref[...], lb_ref[...]))


def _mix0_out(ya, o_f, o_b, h0, x, glu_w, glu_b, norm_g, w_out, ln_g, ln_b):
    t = x.shape[0]
    tm = ROW_TILE
    row = lambda w: pl.BlockSpec((tm, w), lambda i: (i, 0))
    full = lambda a, b: pl.BlockSpec((a, b), lambda i: (0, 0))
    return pl.pallas_call(
        _mix0_out_kernel,
        grid=(t // tm,),
        in_specs=[pl.BlockSpec((S5_GROUPS // 2, tm // S5_CHUNK, 512), lambda i: (0, i, 0)),
                  row(512), row(512),
                  pl.BlockSpec((tm, 512), lambda i: (i, 2)),
                  row(1024), full(512, 512), full(1, 512), full(1, 512), full(1024, 1024),
                  full(1, 1024), full(1, 1024)],
        out_specs=pl.BlockSpec((tm * 8, 128), lambda i: (i, 0)),
        out_shape=jax.ShapeDtypeStruct((t * 8, 128), F32),
        scratch_shapes=[pltpu.VMEM((S5_WIDTH // 128, tm, 128), F32)],
        compiler_params=_cparams(1),
        name="mix0_out_ln",
    )(ya, o_f, o_b, h0, x, glu_w, glu_b, norm_g, w_out, ln_g, ln_b)


def _attn_kernel(q_ref, kc_ref, vc_ref, kp_ref, vp_ref, kn_ref, vn_ref, bias_ref, sink_ref, o_ref, *, seq):
    blk = pl.program_id(1)
    n_blk = pl.num_programs(1)
    n_sub = ATT_ROWS // ATT_BLOCK
    kw = ATT_BLOCK + 2 * WINDOW
    nq = GQA * ATT_BLOCK
    key_row = lax.broadcasted_iota(jnp.int32, (kw, nq), 0)
    pen_first = jnp.where((blk == 0) & (key_row < WINDOW), NEG_INF, 0.0)
    pen_last = jnp.where((blk == n_blk - 1) & (key_row >= ATT_BLOCK + WINDOW), NEG_INF, 0.0)
    kall = jnp.concatenate([kp_ref[...], kc_ref[...], kn_ref[...]], axis=0)
    vall = jnp.concatenate([vp_ref[...], vc_ref[...], vn_ref[...]], axis=0)
    for s in range(n_sub):
        kwin = kall[s * ATT_BLOCK:s * ATT_BLOCK + kw]
        vwin = vall[s * ATT_BLOCK:s * ATT_BLOCK + kw]
        q = q_ref[s * ATT_BLOCK:(s + 1) * ATT_BLOCK, :] * ATT_SCALE
        outs_t = []
        for kv in range(N_KV):
            qs = jnp.concatenate([q[:, (kv * GQA + g) * HEAD_DIM:(kv * GQA + g + 1) * HEAD_DIM]
                                  for g in range(GQA)], axis=0)
            kh = kwin[:, kv * HEAD_DIM:(kv + 1) * HEAD_DIM]
            vh = vwin[:, kv * HEAD_DIM:(kv + 1) * HEAD_DIM]
            st = _dot_nt(kh, qs) + bias_ref[kv]
            if s == 0:
                st = st + pen_first
            if s == n_sub - 1:
                st = st + pen_last
            sink = sink_ref[kv]
            m = jnp.maximum(jnp.max(st, axis=0, keepdims=True), sink)
            p = jnp.exp(st - m)
            den = jnp.sum(p, axis=0, keepdims=True) + jnp.exp(sink - m)
            outs_t.append(_dot_tn(vh, p.astype(BF16)) / den)
        o = jnp.concatenate(outs_t, axis=0).T
        pieces = [o[g * ATT_BLOCK:(g + 1) * ATT_BLOCK, kv * HEAD_DIM:(kv + 1) * HEAD_DIM]
                  for kv in range(N_KV) for g in range(GQA)]
        o_ref[s * ATT_BLOCK:(s + 1) * ATT_BLOCK, :] = jnp.concatenate(pieces, axis=-1).astype(o_ref.dtype)


def _t5_bucket(rel):
    nb = REL_BUCKETS // 2
    max_exact = nb // 2
    ret = (rel > 0).astype(jnp.int32) * nb
    n = jnp.abs(rel)
    large = max_exact + (jnp.log(jnp.maximum(n, 1).astype(F32) / max_exact)
                         / math.log(REL_MAX_DIST / max_exact) * (nb - max_exact)).astype(jnp.int32)
    large = jnp.minimum(large, nb - 1)
    return ret + jnp.where(n < max_exact, n, large)


def _attn_mixer(h3, rel_bias, sink, n_batch, seq):
    kw = ATT_BLOCK + 2 * WINDOW
    n_rel = kw + ATT_BLOCK - 1
    rel = jnp.arange(n_rel) - (n_rel - 1) // 2
    tb = rel_bias.astype(F32)[_t5_bucket(rel)]
    tb = jnp.where((jnp.abs(rel) <= WINDOW)[:, None], tb, NEG_INF)
    tb = jnp.pad(tb.T, ((0, 0), (0, 1)))
    flat = jnp.tile(tb, (1, ATT_BLOCK))[:, ATT_BLOCK - 1:ATT_BLOCK - 1 + ATT_BLOCK * n_rel]
    bias = flat.reshape(N_HEADS, ATT_BLOCK, n_rel)[:, :, :kw]
    bias = jnp.transpose(bias.reshape(N_KV, GQA, ATT_BLOCK, kw), (0, 3, 1, 2)).reshape(N_KV, kw, GQA * ATT_BLOCK)
    sink_rows = jnp.repeat(sink.astype(F32).reshape(N_KV, GQA), ATT_BLOCK, axis=1)[:, None, :]
    nblk = seq // ATT_ROWS
    sub = ATT_ROWS // ATT_BLOCK
    n128 = seq // ATT_BLOCK
    cur = lambda b, i: b * nblk + i
    prev = lambda b, i: b * n128 + jnp.maximum(i * sub - 1, 0)
    nxt = lambda b, i: b * n128 + jnp.minimum((i + 1) * sub, n128 - 1)
    t = n_batch * seq
    kern = functools.partial(_attn_kernel, seq=seq)
    return pl.pallas_call(
        kern,
        grid=(n_batch, nblk),
        in_specs=[pl.BlockSpec((ATT_ROWS, ATT_DIM), lambda b, i: (cur(b, i), 0)),
                  pl.BlockSpec((ATT_ROWS, KV_DIM), lambda b, i: (cur(b, i), 4)),
                  pl.BlockSpec((ATT_ROWS, KV_DIM), lambda b, i: (cur(b, i), 5)),
                  pl.BlockSpec((ATT_BLOCK, KV_DIM), lambda b, i: (prev(b, i), 4)),
                  pl.BlockSpec((ATT_BLOCK, KV_DIM), lambda b, i: (prev(b, i), 5)),
                  pl.BlockSpec((ATT_BLOCK, KV_DIM), lambda b, i: (nxt(b, i), 4)),
                  pl.BlockSpec((ATT_BLOCK, KV_DIM), lambda b, i: (nxt(b, i), 5)),
                  pl.BlockSpec((N_KV, kw, GQA * ATT_BLOCK), lambda b, i: (0, 0, 0)),
                  pl.BlockSpec((N_KV, 1, GQA * ATT_BLOCK), lambda b, i: (0, 0, 0))],
        out_specs=pl.BlockSpec((ATT_ROWS, ATT_DIM), lambda b, i: (cur(b, i), 0)),
        out_shape=jax.ShapeDtypeStruct((t, ATT_DIM), BF16),
        compiler_params=_cparams(2),
        name="window_gqa",
    )(h3, h3, h3, h3, h3, h3, h3, bias, sink_rows)


def _proj_ln_kernel(a_ref, w_ref, x_ref, g_ref, b_ref, h8_ref):
    y = _dot(a_ref[...], w_ref[...])
    _store_token_tiles(h8_ref, _layer_norm(ALPHA * x_ref[...] + y, g_ref[...], b_ref[...]))


def _proj_ln(a, w, x, g, b):
    t, k = a.shape
    tm = ROW_TILE
    return pl.pallas_call(
        _proj_ln_kernel,
        grid=(t // tm,),
        in_specs=[pl.BlockSpec((tm, k), lambda i: (i, 0)),
                  pl.BlockSpec((k, D_MODEL), lambda i: (0, 0)),
                  pl.BlockSpec((tm, D_MODEL), lambda i: (i, 0)),
                  pl.BlockSpec((1, D_MODEL), lambda i: (0, 0)),
                  pl.BlockSpec((1, D_MODEL), lambda i: (0, 0))],
        out_specs=pl.BlockSpec((tm * 8, 128), lambda i: (i, 0)),
        out_shape=jax.ShapeDtypeStruct((t * 8, 128), F32),
        compiler_params=_cparams(1),
        name="proj_ln",
    )(a, w, x, g, b)


def _router_kernel(h_ref, wh_ref, wl_ref, b_ref, o_ref, ot_ref, cnt_ref, run_ref):
    @pl.when(pl.program_id(0) == 0)
    def _():
        run_ref[...] = jnp.zeros_like(run_ref)

    x = _load_token_tiles(h_ref, ROW_TILE)
    xh = x.astype(BF16)
    xl = (x - xh.astype(F32)).astype(BF16)
    wh = wh_ref[...]
    lt = _dot_nt(wh, xh) + (_dot_nt(wh, xl) + _dot_nt(wl_ref[...], xh)) + b_ref[...][:, 0:1]
    tm = lt.shape[1]
    neg = jnp.float32(-jnp.inf)
    row8 = lax.broadcasted_iota(jnp.int32, (EPG, tm), 0).astype(F32)
    is_g = row8 < N_GROUPS
    gl = jnp.where(is_g, lt[0:EPG], neg)
    gmax = jnp.max(gl, axis=0, keepdims=True)
    gsum = jnp.sum(jnp.where(is_g, jnp.exp(gl - gmax), 0.0), axis=0, keepdims=True)
    grp = jnp.min(jnp.where(gl == gmax, row8, float(EPG)), axis=0, keepdims=True)
    p_grp = 1.0 / gsum
    el = lt[EPG:2 * EPG]
    for g in range(1, N_GROUPS):
        el = jnp.where(grp == float(g), lt[EPG * (g + 1):EPG * (g + 2)], el)
    m1 = jnp.max(el, axis=0, keepdims=True)
    i1 = jnp.min(jnp.where(el == m1, row8, float(EPG)), axis=0, keepdims=True)
    el2 = jnp.where(row8 == i1, neg, el)
    m2 = jnp.max(el2, axis=0, keepdims=True)
    i2 = jnp.min(jnp.where((row8 != i1) & (el2 == m2), row8, float(EPG)), axis=0, keepdims=True)
    e2 = jnp.exp(m2 - m1)
    g1 = p_grp / (1.0 + e2)
    g2 = p_grp * e2 / (1.0 + e2)
    eid1 = grp * EPG + i1
    eid2 = grp * EPG + i2
    rowe = lax.broadcasted_iota(jnp.int32, (N_EXPERTS, tm), 0).astype(F32)
    hit1 = rowe == eid1
    hit2 = rowe == eid2
    onehot = (hit1 | hit2).astype(BF16)
    upper = (lax.broadcasted_iota(jnp.int32, (tm, tm), 0) < lax.broadcasted_iota(jnp.int32, (tm, tm), 1)).astype(BF16)
    before = _dot(onehot, upper) + run_ref[...][:, 0:1]
    r1 = jnp.sum(jnp.where(hit1, before, 0.0), axis=0, keepdims=True)
    r2 = jnp.sum(jnp.where(hit2, before, 0.0), axis=0, keepdims=True)
    run_ref[...] += jnp.sum(onehot.astype(F32), axis=1, keepdims=True)
    cnt_ref[...] = run_ref[...]
    out_t = jnp.concatenate([eid1, eid2, g1, g2, r1, r2, jnp.zeros((2, tm), F32)], axis=0)
    ot_ref[...] = out_t
    o_ref[...] = jnp.concatenate([out_t, jnp.zeros((128 - 8, tm), F32)], axis=0).T


def _router(h8, w_group, b_group, w_router, b_router):
    t = h8.shape[0] // 8
    w = jnp.zeros((128, D_MODEL), F32)
    w = w.at[0:N_GROUPS].set(w_group.astype(F32).T).at[EPG:EPG + N_EXPERTS].set(w_router.astype(F32).T)
    wh = w.astype(BF16)
    wl = (w - wh.astype(F32)).astype(BF16)
    bias = jnp.zeros((128,), F32).at[0:N_GROUPS].set(b_group.astype(F32)).at[EPG:EPG + N_EXPERTS].set(
        b_router.astype(F32))
    bias = jnp.broadcast_to(bias[:, None], (128, 128))
    tm = ROW_TILE
    out, out_t, cnt = pl.pallas_call(
        _router_kernel,
        grid=(t // tm,),
        in_specs=[pl.BlockSpec((tm * 8, 128), lambda i: (i, 0)),
                  pl.BlockSpec((128, D_MODEL), lambda i: (0, 0)),
                  pl.BlockSpec((128, D_MODEL), lambda i: (0, 0)),
                  pl.BlockSpec((128, 128), lambda i: (0, 0))],
        out_specs=[pl.BlockSpec((tm, 128), lambda i: (i, 0)),
                   pl.BlockSpec((8, tm), lambda i: (0, i)),
                   pl.BlockSpec((N_EXPERTS, 128), lambda i: (0, 0))],
        out_shape=[jax.ShapeDtypeStruct((t, 128), F32), jax.ShapeDtypeStruct((8, t), F32),
                   jax.ShapeDtypeStruct((N_EXPERTS, 128), F32)],
        scratch_shapes=[pltpu.VMEM((N_EXPERTS, 128), F32)],
        compiler_params=_cparams(1),
        name="moe_router",
    )(h8, wh, wl, bias)
    eid = out_t[0:2].astype(jnp.int32)
    rank = out_t[4:6].astype(jnp.int32)
    counts = cnt[:, 0].astype(jnp.int32)
    return out, eid, rank, counts


def _token_tile(ref, tok):
    return ref.at[pl.ds(pl.multiple_of(tok * 8, 8), 8)]


def _experts_kernel(be_ref, nu_ref, src_hbm, h8_hbm, w1_ref, w3_ref, w2_ref, ys_ref,
                    w13_s, w2_s, xbuf, idx_smem, sem, isem):
    i = pl.program_id(0)
    n_used = nu_ref[0]
    used = i < n_used
    slot = i % N_ROW_BUFS

    blk_per_fetch = IDX_FETCH // MOE_ROWS

    def fetch_indices(blk):
        @pl.when(blk % blk_per_fetch == 0)
        def _():
            off = pl.multiple_of(blk * MOE_ROWS, IDX_FETCH)
            cp = pltpu.make_async_copy(src_hbm.at[pl.ds(off, IDX_FETCH)], idx_smem, isem)
            cp.start()
            cp.wait()

    def start_rows(blk, slot_, r0, n):
        base = (blk % blk_per_fetch) * MOE_ROWS
        toks = [idx_smem[base + r0 + u] for u in range(n)]
        for u in range(n):
            pltpu.make_async_copy(_token_tile(h8_hbm, toks[u]), _token_tile(xbuf.at[slot_], r0 + u),
                                  sem.at[slot_]).start(priority=u % 2)

    def gather_block(blk, slot_):
        fetch_indices(blk)
        def issue(r8, _):
            start_rows(blk, slot_, r8 * 8, 8)
            return 0
        lax.fori_loop(0, MOE_ROWS // 8, issue, 0)

    @pl.when(i == 0)
    def _():
        gather_block(0, 0)
        @pl.when(n_used > 1)
        def _():
            gather_block(1, 1)

    nxt = jnp.minimum(i + 2, n_used - 1)
    slot_n = (i + 2) % N_ROW_BUFS

    @pl.when(used)
    def _():
        fetch_indices(nxt)
        pltpu.make_async_copy(h8_hbm.at[pl.ds(0, MOE_ROWS * 8)], xbuf.at[slot], sem.at[slot]).wait()

    prev = be_ref[jnp.maximum(i - 1, 0)]
    fresh = (i == 0) | (be_ref[i] != prev)

    @pl.when(used & fresh)
    def _():
        w13_s[:, :D_EXPERT] = w1_ref[...].astype(BF16)
        w13_s[:, D_EXPERT:] = w3_ref[...].astype(BF16)
        w2_s[...] = w2_ref[...].astype(BF16)

    @pl.when(used)
    def _():
        n_tiles = 2 * D_EXPERT // MXU_COLS
        n_tiles2 = D_MODEL // MXU_COLS
        rows_per_group = MOE_ROWS // (n_tiles + n_tiles2)
        x = _load_token_tiles(xbuf.at[slot], MOE_ROWS).astype(BF16)
        parts = []
        for c in range(n_tiles):
            start_rows(nxt, slot_n, c * rows_per_group, rows_per_group)
            parts.append(_dot(x, w13_s[:, c * MXU_COLS:(c + 1) * MXU_COLS]))
        h = jnp.concatenate(parts, axis=-1)
        h1 = h[:, :D_EXPERT]
        hdn = ((h1 * jax.nn.sigmoid(h1)) * h[:, D_EXPERT:]).astype(BF16)
        for c in range(n_tiles2):
            start_rows(nxt, slot_n, (n_tiles + c) * rows_per_group, rows_per_group)
            yc = _dot(hdn, w2_s[:, c * MXU_COLS:(c + 1) * MXU_COLS])
            for k in range(MXU_COLS // 128):
                ys_ref[pl.ds(c * (MXU_COLS // 128) + k, MOE_ROWS, stride=8), :] = yc[:, k * 128:(k + 1) * 128]

    def wait_rows(slot_):
        pltpu.make_async_copy(h8_hbm.at[pl.ds(0, MOE_ROWS * 8)], xbuf.at[slot_], sem.at[slot_]).wait()

    @pl.when(i == n_used - 1)
    def _():
        wait_rows(slot_n)
        @pl.when(n_used > 1)
        def _():
            wait_rows((i + 1) % N_ROW_BUFS)

    @pl.when(jnp.logical_not(used))
    def _():
        ys_ref[...] = jnp.zeros_like(ys_ref)


def _experts(h8, src, blk_e, n_used, w1, w3, w2, layer, cap):
    nblk = cap // MOE_ROWS
    return pl.pallas_call(
        _experts_kernel,
        grid_spec=pltpu.PrefetchScalarGridSpec(
            num_scalar_prefetch=2,
            grid=(nblk,),
            in_specs=[pl.BlockSpec(memory_space=pl.ANY),
                      pl.BlockSpec(memory_space=pl.ANY),
                      pl.BlockSpec((None, None, D_MODEL, D_EXPERT), lambda i, be, nu: (layer, be[i], 0, 0)),
                      pl.BlockSpec((None, None, D_MODEL, D_EXPERT), lambda i, be, nu: (layer, be[i], 0, 0)),
                      pl.BlockSpec((None, None, D_EXPERT, D_MODEL), lambda i, be, nu: (layer, be[i], 0, 0))],
            out_specs=pl.BlockSpec((MOE_ROWS * 8, 128), lambda i, be, nu: (i, 0)),
            scratch_shapes=[pltpu.VMEM((D_MODEL, 2 * D_EXPERT), BF16),
                            pltpu.VMEM((D_EXPERT, D_MODEL), BF16),
                            pltpu.VMEM((N_ROW_BUFS, MOE_ROWS * 8, 128), F32),
                            pltpu.SMEM((IDX_FETCH,), jnp.int32),
                            pltpu.SemaphoreType.DMA((N_ROW_BUFS,)),
                            pltpu.SemaphoreType.DMA(())]),
        out_shape=jax.ShapeDtypeStruct((cap * 8, 128), F32),
        compiler_params=_cparams(1),
        name="moe_experts",
    )(blk_e, n_used, src, h8, w1, w3, w2)


def _combine_kernel(dst_hbm, ys_hbm, h_ref, gate_ref, g_ref, b_ref, *rest, with_proj):
    if with_proj:
        w_ref, o_ref, p_ref, idx_smem, y0_ref, y1_ref, sem, isem = rest
    else:
        o_ref, idx_smem, y0_ref, y1_ref, sem, isem = rest
    i = pl.program_id(0)
    n_prog = pl.num_programs(0)
    tm = ROW_TILE
    slot = i % 2
    n_groups = 8
    grp = tm // n_groups

    def fetch_indices(tile):
        cp = pltpu.make_async_copy(dst_hbm.at[pl.ds(tile * 2 * tm, 2 * tm)], idx_smem, isem)
        cp.start()
        cp.wait()

    def start_rows(slot_, t0, n):
        r0 = [idx_smem[t0 + u] for u in range(n)]
        r1 = [idx_smem[tm + t0 + u] for u in range(n)]
        for u in range(n):
            pltpu.make_async_copy(_token_tile(ys_hbm, r0[u]), _token_tile(y0_ref.at[slot_], t0 + u),
                                  sem.at[slot_]).start(priority=0)
            pltpu.make_async_copy(_token_tile(ys_hbm, r1[u]), _token_tile(y1_ref.at[slot_], t0 + u),
                                  sem.at[slot_]).start(priority=1)

    def wait_tile(slot_):
        pltpu.make_async_copy(ys_hbm.at[pl.ds(0, tm * 8)], y0_ref.at[slot_], sem.at[slot_]).wait()
        pltpu.make_async_copy(ys_hbm.at[pl.ds(0, tm * 8)], y1_ref.at[slot_], sem.at[slot_]).wait()

    @pl.when(i == 0)
    def _():
        fetch_indices(0)
        def issue(t8, _):
            start_rows(0, t8 * 8, 8)
            return 0
        lax.fori_loop(0, tm // 8, issue, 0)

    fetch_indices(jnp.minimum(i + 1, n_prog - 1))
    wait_tile(slot)
    start_rows(1 - slot, 0, grp)
    gate = gate_ref[...]
    y = (_load_token_tiles(y0_ref.at[slot], tm) * gate[:, 2:3]
         + _load_token_tiles(y1_ref.at[slot], tm) * gate[:, 3:4])
    out = _layer_norm(ALPHA * _load_token_tiles(h_ref, tm) + y, g_ref[...], b_ref[...])
    o_ref[...] = out
    start_rows(1 - slot, grp, grp)
    if with_proj:
        n_tiles = w_ref.shape[1] // MXU_COLS
        out_b = out.astype(BF16)
        for c in range(n_tiles):
            g = 2 + c * (n_groups - 2) // n_tiles
            g_end = 2 + (c + 1) * (n_groups - 2) // n_tiles
            for gg in range(g, g_end):
                start_rows(1 - slot, gg * grp, grp)
            p_ref[:, c * MXU_COLS:(c + 1) * MXU_COLS] = _dot(
                out_b, w_ref[:, c * MXU_COLS:(c + 1) * MXU_COLS]).astype(p_ref.dtype)
    else:
        for gg in range(2, n_groups):
            start_rows(1 - slot, gg * grp, grp)

    @pl.when(i == n_prog - 1)
    def _():
        wait_tile(1 - slot)


def _combine_ln(dst, ys, h8, rout, ln_g, ln_b, w_next=None):
    t = h8.shape[0] // 8
    tm = ROW_TILE
    with_proj = w_next is not None
    extra_in = [pl.BlockSpec(w_next.shape, lambda i: (0, 0))] if with_proj else []
    extra_out = [pl.BlockSpec((tm, w_next.shape[1]), lambda i: (i, 0))] if with_proj else []
    extra_shape = [jax.ShapeDtypeStruct((t, w_next.shape[1]), BF16)] if with_proj else []
    res = pl.pallas_call(
        functools.partial(_combine_kernel, with_proj=with_proj),
        grid_spec=pltpu.PrefetchScalarGridSpec(
            num_scalar_prefetch=0,
            grid=(t // tm,),
            in_specs=[pl.BlockSpec(memory_space=pl.ANY),
                      pl.BlockSpec(memory_space=pl.ANY),
                      pl.BlockSpec((tm * 8, 128), lambda i: (i, 0)),
                      pl.BlockSpec((tm, 128), lambda i: (i, 0)),
                      pl.BlockSpec((1, D_MODEL), lambda i: (0, 0)),
                      pl.BlockSpec((1, D_MODEL), lambda i: (0, 0))] + extra_in,
            out_specs=[pl.BlockSpec((tm, D_MODEL), lambda i: (i, 0))] + extra_out,
            scratch_shapes=[pltpu.SMEM((2 * tm,), jnp.int32),
                            pltpu.VMEM((2, tm * 8, 128), F32),
                            pltpu.VMEM((2, tm * 8, 128), F32),
                            pltpu.SemaphoreType.DMA((2,)),
                            pltpu.SemaphoreType.DMA(())]),
        out_shape=[jax.ShapeDtypeStruct((t, D_MODEL), F32)] + extra_shape,
        compiler_params=_cparams(1),
        name="moe_combine_ln",
    )(dst, ys, h8, rout, ln_g, ln_b, *([w_next] if with_proj else []))
    return tuple(res) if with_proj else res[0]


def _moe_layer(h8, w_group, b_group, w_router, b_router, w1, w3, w2, layer, ln_g, ln_b, w_next=None):
    t = h8.shape[0] // 8
    rout, eid, rank, counts = _router(h8, w_group, b_group, w_router, b_router)
    padded = (counts + MOE_ROWS - 1) // MOE_ROWS * MOE_ROWS
    pad_end = jnp.cumsum(padded)
    pad_start = pad_end - padded
    start_of = jnp.sum(jnp.where(eid[..., None] == jnp.arange(N_EXPERTS), pad_start, 0), axis=-1)
    dst = (start_of + rank).astype(jnp.int32)
    cap = 2 * t + N_EXPERTS * MOE_ROWS
    nblk = cap // MOE_ROWS
    tok = jnp.broadcast_to(jnp.arange(t, dtype=jnp.int32), (2, t))
    blk_e = jnp.minimum(jnp.sum(pad_end[None, :] <= (jnp.arange(nblk) * MOE_ROWS)[:, None], axis=1),
                        N_EXPERTS - 1).astype(jnp.int32)
    n_used = (pad_end[-1] // MOE_ROWS).astype(jnp.int32).reshape(1)
    _, by_row = lax.sort((dst.reshape(-1), tok.reshape(-1)), num_keys=1)
    cstart = jnp.cumsum(counts) - counts
    blk_first = cstart[blk_e] + (jnp.arange(nblk) * MOE_ROWS - pad_start[blk_e])
    blk_last = cstart[blk_e] + jnp.maximum(counts[blk_e] - 1, 0)
    idx = jnp.minimum(blk_first[:, None] + jnp.arange(MOE_ROWS)[None, :], blk_last[:, None])
    src = by_row[jnp.clip(idx, 0, 2 * t - 1)].reshape(-1)
    dst_tiles = jnp.transpose(dst.reshape(2, t // ROW_TILE, ROW_TILE), (1, 0, 2)).reshape(-1)
    ys = _experts(h8, src, blk_e, n_used, w1, w3, w2, layer, cap)
    return _combine_ln(dst_tiles, ys, h8, rout, ln_g, ln_b, w_next)


def kernel(x, ln_mix_g, ln_mix_b, ln_ffn_g, ln_ffn_b, ab_w_in, s5_lam_re, s5_lam_im, s5_log_dt, s5_b_re, s5_b_im,
           s5_c_re, s5_c_im, s5_d, s5_glu_w, s5_glu_b, gla_gate_w, gla_gate_b, gla_norm_g, ab_w_out, c_w_in,
           c_sink, c_w_out, rel_bias, moe_w_group, moe_b_group, moe_w_router, moe_b_router, moe_w1, moe_w3, moe_w2):
    n_batch, seq, _ = x.shape
    t = n_batch * seq
    xt = x.reshape(t, D_MODEL)
    row = lambda v: v.astype(F32).reshape(1, -1)

    w_in0 = jnp.pad(ab_w_in[0], ((0, 0), (0, AB_IN_PAD - AB_IN))).astype(BF16)
    h0, u_t = _inproj0(xt, w_in0)
    n_steps = seq // (S5_SEGS * S5_CHUNK)
    kcat, r1e, r2, coef, ptab = _s5_prep(s5_lam_re[0], s5_lam_im[0], s5_log_dt[0], s5_b_re[0], s5_b_im[0],
                                         s5_c_re[0], s5_c_im[0], s5_d[0], n_steps)
    ya = _s5_mixer(u_t, kcat, r1e, r2, coef, ptab, n_batch, n_steps)
    gw = gla_gate_w[0].astype(F32)
    wg = jnp.zeros((128, 2 * GLA_KEY), F32)
    wg = wg.at[0:GLA_RANK, 0:GLA_KEY].set(gw[0]).at[GLA_RANK:2 * GLA_RANK, GLA_KEY:].set(gw[1]).astype(BF16)
    bg = gla_gate_b[0].astype(F32).reshape(1, 2 * GLA_KEY)
    o_f, o_b = _gla_mixer(h0, wg, bg, n_batch, seq)
    h8 = _mix0_out(ya, o_f, o_b, h0, xt, s5_glu_w[0].astype(BF16), row(s5_glu_b[0]), row(gla_norm_g[0]),
                      ab_w_out[0].astype(BF16), row(ln_mix_g[0]), row(ln_mix_b[0]))
    h, h3 = _moe_layer(h8, moe_w_group[0], moe_b_group[0], moe_w_router[0], moe_b_router[0],
                       moe_w1, moe_w3, moe_w2, 0, row(ln_ffn_g[0]), row(ln_ffn_b[0]), c_w_in[0].astype(BF16))

    o = _attn_mixer(h3, rel_bias, c_sink[0], n_batch, seq)
    h8 = _proj_ln(o, c_w_out[0].astype(BF16), h, row(ln_mix_g[1]), row(ln_mix_b[1]))
    h = _moe_layer(h8, moe_w_group[1], moe_b_group[1], moe_w_router[1], moe_b_router[1],
                   moe_w1, moe_w3, moe_w2, 1, row(ln_ffn_g[1]), row(ln_ffn_b[1]))
    return h.reshape(n_batch, seq, D_MODEL)
```

```python
import functools
import math

import jax
import jax.numpy as jnp
from jax import lax
from jax.experimental import pallas as pl
from jax.experimental.pallas import tpu as pltpu

F32 = jnp.float32
BF16 = jnp.bfloat16

D_MODEL = 1024
S5_WIDTH = 512
S5_GROUP_CH = 16
S5_GROUPS = 32
S5_STATE = 64
GLA_HEADS = 4
GLA_DV = 128
GLA_DK = 64
GLA_KEY = 256
GLA_WIDTH = 512
GLA_RANK = 16
GLA_TAU = 16.0
GLA_CHUNK = 64
AB_IN = 2080
AB_IN_PAD = 2176
HEAD_DIM = 64
N_HEADS = 16
N_KV = 4
GQA = 4
ATT_DIM = 1024
KV_DIM = 256
WINDOW = 128
ATT_BLOCK = 128
ATT_SCALE = HEAD_DIM ** -0.5
REL_BUCKETS = 32
REL_MAX_DIST = 128
NEG_INF = -1e30
N_GROUPS = 4
EPG = 8
N_EXPERTS = 32
D_EXPERT = 512
LN_EPS = 1e-5
RMS_EPS = 1e-6
DEPTH = 2
ALPHA = (2 * DEPTH) ** 0.25

S5_CHUNK = 16
S5_SEGS = 8
S5_SEG_PAD = 4
ROW_TILE = 512
GLA_ROWS = 256
ATT_ROWS = 512
MOE_ROWS = 256
IDX_FETCH = 1024
N_ROW_BUFS = 3
MXU_COLS = 256
VMEM_LIMIT = 56 * 1024 * 1024


def _cparams(n_axes):
    return pltpu.CompilerParams(dimension_semantics=("arbitrary",) * n_axes,
                                vmem_limit_bytes=VMEM_LIMIT)


def _dot(a, b):
    return jnp.dot(a, b, preferred_element_type=F32)


def _dot_nt(a, b):
    return lax.dot_general(a, b, (((1,), (1,)), ((), ())), preferred_element_type=F32)


def _dot_tn(a, b):
    return lax.dot_general(a, b, (((0,), (0,)), ((), ())), preferred_element_type=F32)


def _layer_norm(r, g, b):
    mu = jnp.mean(r, axis=-1, keepdims=True)
    c = r - mu
    var = jnp.mean(c * c, axis=-1, keepdims=True)
    return c * lax.rsqrt(var + LN_EPS) * g + b


def _store_token_tiles(ref, val):
    n = val.shape[0]
    for k in range(D_MODEL // 128):
        ref[pl.ds(k, n, stride=8), :] = val[:, k * 128:(k + 1) * 128]


def _load_token_tiles(ref, n):
    return jnp.concatenate([ref[pl.ds(k, n, stride=8), :] for k in range(D_MODEL // 128)], axis=-1)


def _s5_prep(lam_re, lam_im, log_dt, b_re, b_im, c_re, c_im, d, n_steps):
    tc = S5_CHUNK
    g, p, c = S5_GROUPS, S5_STATE, S5_GROUP_CH
    lr = jnp.minimum(lam_re.astype(F32), -1e-4)
    li = lam_im.astype(F32)
    dt = jnp.exp(log_dt.astype(F32))[..., None]
    mag = jnp.exp(lr * dt)
    ar = mag * jnp.cos(li * dt)
    ai = mag * jnp.sin(li * dt)
    den = lr * lr + li * li
    nr = ar - 1.0
    coef_r = (nr * lr + ai * li) / den
    coef_i = (ai * lr - nr * li) / den
    br_ = b_re.astype(F32)
    bi_ = b_im.astype(F32)
    bbr = coef_r[..., None] * br_ - coef_i[..., None] * bi_
    bbi = coef_r[..., None] * bi_ + coef_i[..., None] * br_
    cr = c_re.astype(F32)
    ci = c_im.astype(F32)
    bbr_t = jnp.swapaxes(bbr, -1, -2)
    bbi_t = jnp.swapaxes(bbi, -1, -2)
    npair = g // 2

    def apow(n):
        nn = n.astype(F32)[:, None, None, None]
        m_ = jnp.exp(nn * (lr * dt)[None])
        ang = nn * (li * dt)[None]
        return m_ * jnp.cos(ang), m_ * jnp.sin(ang)

    pr, pi = apow(jnp.arange(tc + 1))
    hr = pr[:, :, :, None, :] * bbr_t[None] - pi[:, :, :, None, :] * bbi_t[None]
    hi = pr[:, :, :, None, :] * bbi_t[None] + pi[:, :, :, None, :] * bbr_t[None]
    kk = jnp.einsum('dgoq,jdgcq->jdgco', cr, hr) - jnp.einsum('dgoq,jdgcq->jdgco', ci, hi)
    k0 = kk[0, 0] + kk[0, 1] + d.astype(F32).reshape(g, c)[:, :, None] * jnp.eye(c, dtype=F32)
    slab = jnp.concatenate([kk[tc - 1:0:-1, 1], k0[None], kk[1:tc, 0]], axis=0)
    kcat = jnp.transpose(slab, (1, 2, 0, 3)).reshape(g, c, (2 * tc - 1) * c)
    kcat = jnp.pad(kcat, ((0, 0), (0, 0), (0, 2 * tc * c - kcat.shape[-1]))).reshape(npair, 2, c, 2 * tc * c)

    def pair_blockdiag(m):
        m2 = m.reshape((npair, 2) + m.shape[1:])
        z = jnp.zeros_like(m2[:, 0])
        return jnp.concatenate([jnp.concatenate([m2[:, 0], z], axis=-1),
                                jnp.concatenate([z, m2[:, 1]], axis=-1)], axis=1)

    def w_plane(h_, lag_sel, dirn):
        w = jnp.transpose(h_[lag_sel, dirn], (1, 0, 2, 3)).reshape(g, tc * c, p)
        return pair_blockdiag(w)
    s_fwd = tc - 1 - jnp.arange(tc)
    s_bwd = jnp.arange(tc)
    r1e = jnp.concatenate([w_plane(hr, s_fwd, 0), w_plane(hi, s_fwd, 0), w_plane(hr, s_bwd, 1), w_plane(hi, s_bwd, 1)],
                          axis=2)

    lane = jnp.arange(tc * c)
    rep_t = (lane[None, :] // c == jnp.arange(tc)[:, None]).astype(F32)
    tile_co = (lane[None, :] % c == jnp.arange(c)[:, None]).astype(F32)
    spread = lambda a, sel: jnp.dot(a, sel, precision=lax.Precision.HIGHEST)
    def readout(e, dirn):
        pr_l = spread(jnp.transpose(pr[e, dirn], (1, 2, 0)).reshape(g * p, tc), rep_t)
        pi_l = spread(jnp.transpose(pi[e, dirn], (1, 2, 0)).reshape(g * p, tc), rep_t)
        cr_l = spread(jnp.transpose(cr[dirn], (0, 2, 1)).reshape(g * p, c), tile_co)
        ci_l = spread(jnp.transpose(ci[dirn], (0, 2, 1)).reshape(g * p, c), tile_co)
        zr = (cr_l * pr_l - ci_l * pi_l).reshape(g, p, tc * c)
        zi = (cr_l * pi_l + ci_l * pr_l).reshape(g, p, tc * c)
        return pair_blockdiag(zr), pair_blockdiag(-zi)
    vf_r, vf_i = readout(jnp.arange(tc) + 1, 0)
    vb_r, vb_i = readout(tc - jnp.arange(tc), 1)
    r2 = jnp.concatenate([vf_r, vf_i, vb_r, vb_i], axis=1)

    steps = jnp.arange(n_steps)
    a16r, a16i = apow(jnp.array([tc]))
    apr, api = apow(steps * tc)
    anr, ani = apow(jnp.array([tc * n_steps]))
    def lanes(x):
        n_ = x.shape[0]
        return jnp.transpose(x.reshape(n_, npair, 2 * p), (1, 0, 2))
    coef = jnp.concatenate([lanes(a16r[:, 0]), lanes(a16i[:, 0]), lanes(a16r[:, 1]), lanes(a16i[:, 1]),
                            lanes(anr[:, 0]), lanes(ani[:, 0]), lanes(anr[:, 1]), lanes(ani[:, 1])], axis=1)
    rev = n_steps - 1 - steps
    ptab = jnp.stack([lanes(apr[:, 0]), lanes(api[:, 0]), lanes(apr[rev, 1]), lanes(api[rev, 1])], axis=1)
    return kcat, r1e.astype(BF16), r2.astype(BF16), coef, ptab


def _gelu_tanh(x):
    return 0.5 * x * (1.0 + jnp.tanh(math.sqrt(2.0 / math.pi) * (x + 0.044715 * (x * x * x))))


def _s5_kernel(u_ref, kcat_ref, r1e_ref, r2_ref, coef_ref, ptab_ref, y_ref, kt_ref, yi_ref, e_ref, s_ref,
               *, n_batch, n_steps):
    pitch = n_steps + S5_SEG_PAD
    blk = S5_CHUNK * S5_GROUP_CH
    kt_ref[...] = jnp.zeros_like(kt_ref)
    for gi in range(2):
        slab = kcat_ref[gi]
        for s in range(S5_CHUNK):
            off = (S5_CHUNK - 1 - s) * S5_GROUP_CH
            kt_ref[gi * blk + s * S5_GROUP_CH:gi * blk + (s + 1) * S5_GROUP_CH, gi * blk:(gi + 1) * blk] = (
                slab[:, off:off + blk].astype(BF16))
    u = u_ref[...]
    yi_ref[...] = _dot(u, kt_ref[...])
    e = _dot(u, r1e_ref[...])
    n_seg = n_batch * S5_SEGS
    for k in range(4):
        for sg in range(n_seg):
            e_ref[k, sg * pitch:sg * pitch + n_steps, :] = e[sg * n_steps:(sg + 1) * n_steps, k * 128:(k + 1) * 128]
    coef = coef_ref[...]
    a16fr, a16fi, a16br, a16bi = coef[0:1], coef[1:2], coef[2:3], coef[3:4]
    anfr, anfi, anbr, anbi = coef[4:5], coef[5:6], coef[6:7], coef[7:8]
    zero = jnp.zeros((S5_SEGS, 128), F32)
    sub = lax.broadcasted_iota(jnp.int32, (S5_SEGS, 128), 0)

    def seg_rows(b, j):
        return pl.ds(b * S5_SEGS * pitch + j, S5_SEGS, stride=pitch)

    def local_step(j, carry):
        out = []
        for b in range(n_batch):
            sfr, sfi, sbr, sbi = carry[b]
            rf = seg_rows(b, j)
            rb = seg_rows(b, n_steps - 1 - j)
            s_ref[0, rf, :] = sfr
            s_ref[1, rf, :] = sfi
            s_ref[2, rb, :] = sbr
            s_ref[3, rb, :] = sbi
            efr = e_ref[0, rf, :]
            efi = e_ref[1, rf, :]
            ebr = e_ref[2, rb, :]
            ebi = e_ref[3, rb, :]
            out.append((a16fr * sfr - a16fi * sfi + efr, a16fr * sfi + a16fi * sfr + efi,
                        a16br * sbr - a16bi * sbi + ebr, a16br * sbi + a16bi * sbr + ebi))
        return tuple(out)

    ends = lax.fori_loop(0, n_steps, local_step, tuple((zero, zero, zero, zero) for _ in range(n_batch)))

    carries = []
    for b in range(n_batch):
        efr, efi, ebr, ebi = ends[b]
        cfr, cfi, cbr, cbi = zero, zero, zero, zero
        for _ in range(S5_SEGS - 1):
            tfr = anfr * cfr - anfi * cfi + efr
            tfi = anfr * cfi + anfi * cfr + efi
            cfr = jnp.where(sub == 0, 0.0, pltpu.roll(tfr, 1, 0))
            cfi = jnp.where(sub == 0, 0.0, pltpu.roll(tfi, 1, 0))
            tbr = anbr * cbr - anbi * cbi + ebr
            tbi = anbr * cbi + anbi * cbr + ebi
            cbr = jnp.where(sub == S5_SEGS - 1, 0.0, pltpu.roll(tbr, S5_SEGS - 1, 0))
            cbi = jnp.where(sub == S5_SEGS - 1, 0.0, pltpu.roll(tbi, S5_SEGS - 1, 0))
        carries.append((cfr, cfi, cbr, cbi))

    def fix_step(j, _):
        pfr = ptab_ref[0, pl.ds(j, 1), :]
        pfi = ptab_ref[1, pl.ds(j, 1), :]
        pbr = ptab_ref[2, pl.ds(j, 1), :]
        pbi = ptab_ref[3, pl.ds(j, 1), :]
        for b in range(n_batch):
            cfr, cfi, cbr, cbi = carries[b]
            rj = seg_rows(b, j)
            s_ref[0, rj, :] = s_ref[0, rj, :] + (pfr * cfr - pfi * cfi)
            s_ref[1, rj, :] = s_ref[1, rj, :] + (pfr * cfi + pfi * cfr)
            s_ref[2, rj, :] = s_ref[2, rj, :] + (pbr * cbr - pbi * cbi)
            s_ref[3, rj, :] = s_ref[3, rj, :] + (pbr * cbi + pbi * cbr)
        return 0

    lax.fori_loop(0, n_steps, fix_step, 0)

    s_all = jnp.concatenate(
        [jnp.concatenate([s_ref[k, sg * pitch:sg * pitch + n_steps, :] for sg in range(n_seg)], axis=0)
         for k in range(4)], axis=-1)
    y = yi_ref[...] + _dot(s_all.astype(BF16), r2_ref[...])
    y_ref[...] = _gelu_tanh(y).astype(y_ref.dtype)


def _s5_mixer(u_t, kcat, r1e, r2, coef, ptab, n_batch, n_steps):
    npair, m, _ = u_t.shape
    scan_rows = n_batch * S5_SEGS * (n_steps + S5_SEG_PAD)
    kern = functools.partial(_s5_kernel, n_batch=n_batch, n_steps=n_steps)
    return pl.pallas_call(
        kern,
        grid=(npair,),
        in_specs=[pl.BlockSpec((None, m, 512), lambda i: (i, 0, 0)),
                  pl.BlockSpec((None, 2, S5_GROUP_CH, 512), lambda i: (i, 0, 0, 0)),
                  pl.BlockSpec((None, 512, 512), lambda i: (i, 0, 0)),
                  pl.BlockSpec((None, 512, 512), lambda i: (i, 0, 0)),
                  pl.BlockSpec((None, 8, 128), lambda i: (i, 0, 0)),
                  pl.BlockSpec((None, 4, n_steps, 128), lambda i: (i, 0, 0, 0))],
        out_specs=pl.BlockSpec((None, m, 512), lambda i: (i, 0, 0)),
        out_shape=jax.ShapeDtypeStruct((npair, m, 512), BF16),
        scratch_shapes=[pltpu.VMEM((512, 512), BF16), pltpu.VMEM((m, 512), F32),
                        pltpu.VMEM((4, scan_rows, 128), F32), pltpu.VMEM((4, scan_rows, 128), F32)],
        compiler_params=_cparams(1),
        name="s5_scan",
    )(u_t, kcat, r1e, r2, coef, ptab)


def _inproj0_kernel(x_ref, w_ref, h_ref, ut_ref, u_s):
    h = _dot(x_ref[...].astype(BF16), w_ref[...])
    h_ref[...] = h[:, S5_WIDTH:].astype(h_ref.dtype)
    n_lane_blk = S5_WIDTH // 128
    for k in range(n_lane_blk):
        u_s[k] = h[:, k * 128:(k + 1) * 128]
    n_chunk = ROW_TILE // S5_CHUNK
    rows = [jnp.concatenate([u_s[k, pl.ds(s, n_chunk, stride=S5_CHUNK), :] for k in range(n_lane_blk)], axis=-1)
            for s in range(S5_CHUNK)]
    ch = S5_GROUP_CH
    for p in range(S5_GROUPS // 2):
        ut_ref[p] = jnp.concatenate([r[:, (2 * p + gi) * ch:(2 * p + gi + 1) * ch] for gi in range(2) for r in rows],
                                    axis=-1).astype(ut_ref.dtype)


def _inproj0(x, w):
    t, k = x.shape
    n = w.shape[1]
    tm = ROW_TILE
    npair = S5_GROUPS // 2
    return pl.pallas_call(
        _inproj0_kernel,
        grid=(t // tm,),
        in_specs=[pl.BlockSpec((tm, k), lambda i: (i, 0)),
                  pl.BlockSpec((k, n), lambda i: (0, 0))],
        out_specs=[pl.BlockSpec((tm, n - S5_WIDTH), lambda i: (i, 0)),
                   pl.BlockSpec((npair, tm // S5_CHUNK, 512), lambda i: (0, i, 0))],
        out_shape=[jax.ShapeDtypeStruct((t, n - S5_WIDTH), BF16),
                   jax.ShapeDtypeStruct((npair, t // S5_CHUNK, 512), BF16)],
        scratch_shapes=[pltpu.VMEM((S5_WIDTH // 128, tm, 128), F32)],
        compiler_params=_cparams(1),
        name="inproj0",
    )(x, w)


def _gla_kernel(qf_ref, kf_ref, vf_ref, lf_ref, qb_ref, kb_ref, vb_ref, lb_ref, wg_ref, bg_ref,
                of_ref, ob_ref, st_ref):
    @pl.when(pl.program_id(1) == 0)
    def _():
        st_ref[...] = jnp.zeros_like(st_ref)

    n_chunks = GLA_ROWS // GLA_CHUNK
    row = lax.broadcasted_iota(jnp.int32, (GLA_CHUNK, GLA_CHUNK), 0)
    col = lax.broadcasted_iota(jnp.int32, (GLA_CHUNK, GLA_CHUNK), 1)
    mask_f = col <= row
    mask_b = col > row
    brow = lax.broadcasted_iota(jnp.int32, (GLA_ROWS, GLA_ROWS), 0)
    bcol = lax.broadcasted_iota(jnp.int32, (GLA_ROWS, GLA_ROWS), 1)
    shift = GLA_CHUNK.bit_length() - 1
    same = (brow >> shift) == (bcol >> shift)
    ones_blk = same.astype(BF16)
    tri_f = (same & (bcol <= brow)).astype(BF16)
    tri_b = (same & (bcol >= brow)).astype(BF16)
    wg = wg_ref[...]
    bg = bg_ref[...]

    def direction(q_ref, k_ref, v_ref, l_ref, o_ref, d):
        z = _dot(l_ref[...], wg[:, d * GLA_KEY:(d + 1) * GLA_KEY]) + bg[:, d * GLA_KEY:(d + 1) * GLA_KEY]
        log_a = jax.nn.log_sigmoid(z) * (1.0 / GLA_TAU)
        la_hi = log_a.astype(BF16)
        la_lo = (log_a - la_hi.astype(F32)).astype(BF16)
        tri = tri_f if d == 0 else tri_b
        bc = _dot(tri, la_hi) + _dot(tri, la_lo)
        tot = _dot(ones_blk, la_hi) + _dot(ones_blk, la_lo)
        q = q_ref[...].astype(F32) * (GLA_DK ** -0.5)
        k = k_ref[...].astype(F32)
        qd_all = (q * jnp.exp(bc)).astype(BF16)
        kd_all = (k * jnp.exp(-bc)).astype(BF16)
        kc_all = (k * jnp.exp(tot - bc)).astype(BF16)
        decay_all = jnp.exp(tot)
        order = range(n_chunks) if d == 0 else range(n_chunks - 1, -1, -1)
        mask = mask_f if d == 0 else mask_b
        states = [st_ref[d, h] for h in range(GLA_HEADS)]
        for c in order:
            sl = slice(c * GLA_CHUNK, (c + 1) * GLA_CHUNK)
            qd, kd, kc = qd_all[sl], kd_all[sl], kc_all[sl]
            decay = decay_all[c * GLA_CHUNK:c * GLA_CHUNK + 1]
            v = v_ref[sl, :]
            outs = []
            for h in range(GLA_HEADS):
                ks = slice(h * GLA_DK, (h + 1) * GLA_DK)
                vs = slice(h * GLA_DV, (h + 1) * GLA_DV)
                s = jnp.where(mask, _dot_nt(qd[:, ks], kd[:, ks]), 0.0).astype(BF16)
                st = states[h]
                o = _dot(s, v[:, vs]) + _dot_nt(qd[:, ks], st.astype(BF16))
                states[h] = st * decay[:, ks] + _dot_tn(v[:, vs], kc[:, ks])
                outs.append(o)
            o_ref[sl, :] = jnp.concatenate(outs, axis=-1).astype(o_ref.dtype)
        for h in range(GLA_HEADS):
            st_ref[d, h] = states[h]

    direction(qf_ref, kf_ref, vf_ref, lf_ref, of_ref, 0)
    direction(qb_ref, kb_ref, vb_ref, lb_ref, ob_ref, 1)


def _gla_mixer(h0, wg, bg, n_batch, seq):
    nb = seq // GLA_ROWS
    r = GLA_ROWS
    fwd = lambda b, i: b * nb + i
    bwd = lambda b, i: b * nb + (nb - 1 - i)
    def spec(width, colblk, rowfn):
        return pl.BlockSpec((r, width), lambda b, i: (rowfn(b, i), colblk))
    in_specs = [spec(256, 0, fwd), spec(256, 1, fwd), spec(512, 1, fwd), spec(128, 12, fwd),
                spec(256, 0, bwd), spec(256, 1, bwd), spec(512, 1, bwd), spec(128, 12, bwd),
                pl.BlockSpec((128, 512), lambda b, i: (0, 0)),
                pl.BlockSpec((1, 512), lambda b, i: (0, 0))]
    out_specs = [pl.BlockSpec((r, 512), lambda b, i: (fwd(b, i), 0)),
                 pl.BlockSpec((r, 512), lambda b, i: (bwd(b, i), 0))]
    t = n_batch * seq
    return pl.pallas_call(
        _gla_kernel,
        grid=(n_batch, nb),
        in_specs=in_specs,
        out_specs=out_specs,
        out_shape=[jax.ShapeDtypeStruct((t, 512), BF16), jax.ShapeDtypeStruct((t, 512), BF16)],
        scratch_shapes=[pltpu.VMEM((2, GLA_HEADS, GLA_DV, GLA_DK), F32)],
        compiler_params=_cparams(2),
        name="gla_chunked",
    )(h0, h0, h0, h0, h0, h0, h0, h0, wg, bg)


def _mix0_out_kernel(y_ref, of_ref, ob_ref, go_ref, x_ref, gw_ref, gb_ref, ng_ref, wo_ref, lg_ref, lb_ref,
                     h8_ref, ya_s):
    n_chunk = ROW_TILE // S5_CHUNK
    ys = [y_ref[p].astype(F32) for p in range(S5_GROUPS // 2)]
    n_lane_blk = S5_WIDTH // 128
    ch = S5_GROUP_CH
    blk = S5_CHUNK * ch
    for t in range(S5_CHUNK):
        for k in range(n_lane_blk):
            ya_s[k, pl.ds(t, n_chunk, stride=S5_CHUNK), :] = jnp.concatenate(
                [y[:, gi * blk + t * ch:gi * blk + (t + 1) * ch] for y in ys[4 * k:4 * k + 4] for gi in range(2)],
                axis=-1)
    yaf = jnp.concatenate([ya_s[k] for k in range(n_lane_blk)], axis=-1)
    ya = yaf.astype(BF16)
    gate = _dot(ya, gw_ref[...]) + gb_ref[...]
    ya2 = yaf * jax.nn.sigmoid(gate)
    o = of_ref[...].astype(F32) + ob_ref[...].astype(F32)
    ng = ng_ref[...]
    parts = []
    for h in range(GLA_HEADS):
        oh = o[:, h * GLA_DV:(h + 1) * GLA_DV]
        ms = jnp.mean(oh * oh, axis=-1, keepdims=True)
        parts.append(oh * lax.rsqrt(ms + RMS_EPS) * ng[:, h * GLA_DV:(h + 1) * GLA_DV])
    yb = jnp.concatenate(parts, axis=-1) * jax.nn.silu(go_ref[...].astype(F32))
    wo = wo_ref[...]
    y = _dot(ya2.astype(BF16), wo[:S5_WIDTH]) + _dot(yb.astype(BF16), wo[S5_WIDTH:])
    _store_token_tiles(h8_ref, _layer_norm(ALPHA * x_ref[...] + y, lg_ref[...], lb_ref[...]))


def _mix0_out(ya, o_f, o_b, h0, x, glu_w, glu_b, norm_g, w_out, ln_g, ln_b):
    t = x.shape[0]
    tm = ROW_TILE
    row = lambda w: pl.BlockSpec((tm, w), lambda i: (i, 0))
    full = lambda a, b: pl.BlockSpec((a, b), lambda i: (0, 0))
    return pl.pallas_call(
        _mix0_out_kernel,
        grid=(t // tm,),
        in_specs=[pl.BlockSpec((S5_GROUPS // 2, tm // S5_CHUNK, 512), lambda i: (0, i, 0)),
                  row(512), row(512),
                  pl.BlockSpec((tm, 512), lambda i: (i, 2)),
                  row(1024), full(512, 512), full(1, 512), full(1, 512), full(1024, 1024),
                  full(1, 1024), full(1, 1024)],
        out_specs=pl.BlockSpec((tm * 8, 128), lambda i: (i, 0)),
        out_shape=jax.ShapeDtypeStruct((t * 8, 128), F32),
        scratch_shapes=[pltpu.VMEM((S5_WIDTH // 128, tm, 128), F32)],
        compiler_params=_cparams(1),
        name="mix0_out_ln",
    )(ya, o_f, o_b, h0, x, glu_w, glu_b, norm_g, w_out, ln_g, ln_b)


def _attn_kernel(q_ref, kc_ref, vc_ref, kp_ref, vp_ref, kn_ref, vn_ref, bias_ref, sink_ref, o_ref, *, seq):
    blk = pl.program_id(1)
    n_blk = pl.num_programs(1)
    n_sub = ATT_ROWS // ATT_BLOCK
    kw = ATT_BLOCK + 2 * WINDOW
    nq = GQA * ATT_BLOCK
    key_row = lax.broadcasted_iota(jnp.int32, (kw, nq), 0)
    pen_first = jnp.where((blk == 0) & (key_row < WINDOW), NEG_INF, 0.0)
    pen_last = jnp.where((blk == n_blk - 1) & (key_row >= ATT_BLOCK + WINDOW), NEG_INF, 0.0)
    kall = jnp.concatenate([kp_ref[...], kc_ref[...], kn_ref[...]], axis=0)
    vall = jnp.concatenate([vp_ref[...], vc_ref[...], vn_ref[...]], axis=0)
    for s in range(n_sub):
        kwin = kall[s * ATT_BLOCK:s * ATT_BLOCK + kw]
        vwin = vall[s * ATT_BLOCK:s * ATT_BLOCK + kw]
        q = q_ref[s * ATT_BLOCK:(s + 1) * ATT_BLOCK, :] * ATT_SCALE
        outs_t = []
        for kv in range(N_KV):
            qs = jnp.concatenate([q[:, (kv * GQA + g) * HEAD_DIM:(kv * GQA + g + 1) * HEAD_DIM]
                                  for g in range(GQA)], axis=0)
            kh = kwin[:, kv * HEAD_DIM:(kv + 1) * HEAD_DIM]
            vh = vwin[:, kv * HEAD_DIM:(kv + 1) * HEAD_DIM]
            st = _dot_nt(kh, qs) + bias_ref[kv]
            if s == 0:
                st = st + pen_first
            if s == n_sub - 1:
                st = st + pen_last
            sink = sink_ref[kv]
            m = jnp.maximum(jnp.max(st, axis=0, keepdims=True), sink)
            p = jnp.exp(st - m)
            den = jnp.sum(p, axis=0, keepdims=True) + jnp.exp(sink - m)
            outs_t.append(_dot_tn(vh, p.astype(BF16)) / den)
        o = jnp.concatenate(outs_t, axis=0).T
        pieces = [o[g * ATT_BLOCK:(g + 1) * ATT_BLOCK, kv * HEAD_DIM:(kv + 1) * HEAD_DIM]
                  for kv in range(N_KV) for g in range(GQA)]
        o_ref[s * ATT_BLOCK:(s + 1) * ATT_BLOCK, :] = jnp.concatenate(pieces, axis=-1).astype(o_ref.dtype)


def _t5_bucket(rel):
    nb = REL_BUCKETS // 2
    max_exact = nb // 2
    ret = (rel > 0).astype(jnp.int32) * nb
    n = jnp.abs(rel)
    large = max_exact + (jnp.log(jnp.maximum(n, 1).astype(F32) / max_exact)
                         / math.log(REL_MAX_DIST / max_exact) * (nb - max_exact)).astype(jnp.int32)
    large = jnp.minimum(large, nb - 1)
    return ret + jnp.where(n < max_exact, n, large)


def _attn_mixer(h3, rel_bias, sink, n_batch, seq):
    kw = ATT_BLOCK + 2 * WINDOW
    n_rel = kw + ATT_BLOCK - 1
    rel = jnp.arange(n_rel) - (n_rel - 1) // 2
    tb = rel_bias.astype(F32)[_t5_bucket(rel)]
    tb = jnp.where((jnp.abs(rel) <= WINDOW)[:, None], tb, NEG_INF)
    tb = jnp.pad(tb.T, ((0, 0), (0, 1)))
    flat = jnp.tile(tb, (1, ATT_BLOCK))[:, ATT_BLOCK - 1:ATT_BLOCK - 1 + ATT_BLOCK * n_rel]
    bias = flat.reshape(N_HEADS, ATT_BLOCK, n_rel)[:, :, :kw]
    bias = jnp.transpose(bias.reshape(N_KV, GQA, ATT_BLOCK, kw), (0, 3, 1, 2)).reshape(N_KV, kw, GQA * ATT_BLOCK)
    sink_rows = jnp.repeat(sink.astype(F32).reshape(N_KV, GQA), ATT_BLOCK, axis=1)[:, None, :]
    nblk = seq // ATT_ROWS
    sub = ATT_ROWS // ATT_BLOCK
    n128 = seq // ATT_BLOCK
    cur = lambda b, i: b * nblk + i
    prev = lambda b, i: b * n128 + jnp.maximum(i * sub - 1, 0)
    nxt = lambda b, i: b * n128 + jnp.minimum((i + 1) * sub, n128 - 1)
    t = n_batch * seq
    kern = functools.partial(_attn_kernel, seq=seq)
    return pl.pallas_call(
        kern,
        grid=(n_batch, nblk),
        in_specs=[pl.BlockSpec((ATT_ROWS, ATT_DIM), lambda b, i: (cur(b, i), 0)),
                  pl.BlockSpec((ATT_ROWS, KV_DIM), lambda b, i: (cur(b, i), 4)),
                  pl.BlockSpec((ATT_ROWS, KV_DIM), lambda b, i: (cur(b, i), 5)),
                  pl.BlockSpec((ATT_BLOCK, KV_DIM), lambda b, i: (prev(b, i), 4)),
                  pl.BlockSpec((ATT_BLOCK, KV_DIM), lambda b, i: (prev(b, i), 5)),
                  pl.BlockSpec((ATT_BLOCK, KV_DIM), lambda b, i: (nxt(b, i), 4)),
                  pl.BlockSpec((ATT_BLOCK, KV_DIM), lambda b, i: (nxt(b, i), 5)),
                  pl.BlockSpec((N_KV, kw, GQA * ATT_BLOCK), lambda b, i: (0, 0, 0)),
                  pl.BlockSpec((N_KV, 1, GQA * ATT_BLOCK), lambda b, i: (0, 0, 0))],
        out_specs=pl.BlockSpec((ATT_ROWS, ATT_DIM), lambda b, i: (cur(b, i), 0)),
        out_shape=jax.ShapeDtypeStruct((t, ATT_DIM), BF16),
        compiler_params=_cparams(2),
        name="window_gqa",
    )(h3, h3, h3, h3, h3, h3, h3, bias, sink_rows)


def _proj_ln_kernel(a_ref, w_ref, x_ref, g_ref, b_ref, h8_ref):
    y = _dot(a_ref[...], w_ref[...])
    _store_token_tiles(h8_ref, _layer_norm(ALPHA * x_ref[...] + y, g_ref[...], b_ref[...]))


def _proj_ln(a, w, x, g, b):
    t, k = a.shape
    tm = ROW_TILE
    return pl.pallas_call(
        _proj_ln_kernel,
        grid=(t // tm,),
        in_specs=[pl.BlockSpec((tm, k), lambda i: (i, 0)),
                  pl.BlockSpec((k, D_MODEL), lambda i: (0, 0)),
                  pl.BlockSpec((tm, D_MODEL), lambda i: (i, 0)),
                  pl.BlockSpec((1, D_MODEL), lambda i: (0, 0)),
                  pl.BlockSpec((1, D_MODEL), lambda i: (0, 0))],
        out_specs=pl.BlockSpec((tm * 8, 128), lambda i: (i, 0)),
        out_shape=jax.ShapeDtypeStruct((t * 8, 128), F32),
        compiler_params=_cparams(1),
        name="proj_ln",
    )(a, w, x, g, b)


def _router_kernel(h_ref, wh_ref, wl_ref, b_ref, o_ref, ot_ref, cnt_ref, run_ref):
    @pl.when(pl.program_id(0) == 0)
    def _():
        run_ref[...] = jnp.zeros_like(run_ref)

    x = _load_token_tiles(h_ref, ROW_TILE)
    xh = x.astype(BF16)
    xl = (x - xh.astype(F32)).astype(BF16)
    wh = wh_ref[...]
    lt = _dot_nt(wh, xh) + (_dot_nt(wh, xl) + _dot_nt(wl_ref[...], xh)) + b_ref[...][:, 0:1]
    tm = lt.shape[1]
    neg = jnp.float32(-jnp.inf)
    row8 = lax.broadcasted_iota(jnp.int32, (EPG, tm), 0).astype(F32)
    is_g = row8 < N_GROUPS
    gl = jnp.where(is_g, lt[0:EPG], neg)
    gmax = jnp.max(gl, axis=0, keepdims=True)
    gsum = jnp.sum(jnp.where(is_g, jnp.exp(gl - gmax), 0.0), axis=0, keepdims=True)
    grp = jnp.min(jnp.where(gl == gmax, row8, float(EPG)), axis=0, keepdims=True)
    p_grp = 1.0 / gsum
    el = lt[EPG:2 * EPG]
    for g in range(1, N_GROUPS):
        el = jnp.where(grp == float(g), lt[EPG * (g + 1):EPG * (g + 2)], el)
    m1 = jnp.max(el, axis=0, keepdims=True)
    i1 = jnp.min(jnp.where(el == m1, row8, float(EPG)), axis=0, keepdims=True)
    el2 = jnp.where(row8 == i1, neg, el)
    m2 = jnp.max(el2, axis=0, keepdims=True)
    i2 = jnp.min(jnp.where((row8 != i1) & (el2 == m2), row8, float(EPG)), axis=0, keepdims=True)
    e2 = jnp.exp(m2 - m1)
    g1 = p_grp / (1.0 + e2)
    g2 = p_grp * e2 / (1.0 + e2)
    eid1 = grp * EPG + i1
    eid2 = grp * EPG + i2
    rowe = lax.broadcasted_iota(jnp.int32, (N_EXPERTS, tm), 0).astype(F32)
    hit1 = rowe == eid1
    hit2 = rowe == eid2
    onehot = (hit1 | hit2).astype(BF16)
    upper = (lax.broadcasted_iota(jnp.int32, (tm, tm), 0) < lax.broadcasted_iota(jnp.int32, (tm, tm), 1)).astype(BF16)
    before = _dot(onehot, upper) + run_ref[...][:, 0:1]
    r1 = jnp.sum(jnp.where(hit1, before, 0.0), axis=0, keepdims=True)
    r2 = jnp.sum(jnp.where(hit2, before, 0.0), axis=0, keepdims=True)
    run_ref[...] += jnp.sum(onehot.astype(F32), axis=1, keepdims=True)
    cnt_ref[...] = run_ref[...]
    out_t = jnp.concatenate([eid1, eid2, g1, g2, r1, r2, jnp.zeros((2, tm), F32)], axis=0)
    ot_ref[...] = out_t
    o_ref[...] = jnp.concatenate([out_t, jnp.zeros((128 - 8, tm), F32)], axis=0).T


def _router(h8, w_group, b_group, w_router, b_router):
    t = h8.shape[0] // 8
    w = jnp.zeros((128, D_MODEL), F32)
    w = w.at[0:N_GROUPS].set(w_group.astype(F32).T).at[EPG:EPG + N_EXPERTS].set(w_router.astype(F32).T)
    wh = w.astype(BF16)
    wl = (w - wh.astype(F32)).astype(BF16)
    bias = jnp.zeros((128,), F32).at[0:N_GROUPS].set(b_group.astype(F32)).at[EPG:EPG + N_EXPERTS].set(
        b_router.astype(F32))
    bias = jnp.broadcast_to(bias[:, None], (128, 128))
    tm = ROW_TILE
    out, out_t, cnt = pl.pallas_call(
        _router_kernel,
        grid=(t // tm,),
        in_specs=[pl.BlockSpec((tm * 8, 128), lambda i: (i, 0)),
                  pl.BlockSpec((128, D_MODEL), lambda i: (0, 0)),
                  pl.BlockSpec((128, D_MODEL), lambda i: (0, 0)),
                  pl.BlockSpec((128, 128), lambda i: (0, 0))],
        out_specs=[pl.BlockSpec((tm, 128), lambda i: (i, 0)),
                   pl.BlockSpec((8, tm), lambda i: (0, i)),
                   pl.BlockSpec((N_EXPERTS, 128), lambda i: (0, 0))],
        out_shape=[jax.ShapeDtypeStruct((t, 128), F32), jax.ShapeDtypeStruct((8, t), F32),
                   jax.ShapeDtypeStruct((N_EXPERTS, 128), F32)],
        scratch_shapes=[pltpu.VMEM((N_EXPERTS, 128), F32)],
        compiler_params=_cparams(1),
        name="moe_router",
    )(h8, wh, wl, bias)
    asg = jnp.concatenate([out_t[0:2], out_t[4:6]], axis=0).astype(jnp.int32)
    counts = cnt[:, 0].astype(jnp.int32)
    return out, asg, counts


def _token_tile(ref, tok):
    return ref.at[pl.ds(pl.multiple_of(tok * 8, 8), 8)]


def _experts_kernel(be_ref, nu_ref, src_hbm, h8_hbm, w1_ref, w3_ref, w2_ref, ys_ref,
                    w13_s, w2_s, xbuf, idx_smem, sem, isem):
    i = pl.program_id(0)
    n_used = nu_ref[0]
    used = i < n_used
    slot = i % N_ROW_BUFS

    blk_per_fetch = IDX_FETCH // MOE_ROWS

    def fetch_indices(blk):
        @pl.when(blk % blk_per_fetch == 0)
        def _():
            off = pl.multiple_of(blk * MOE_ROWS, IDX_FETCH)
            cp = pltpu.make_async_copy(src_hbm.at[pl.ds(off, IDX_FETCH)], idx_smem, isem)
            cp.start()
            cp.wait()

    def start_rows(blk, slot_, r0, n):
        base = (blk % blk_per_fetch) * MOE_ROWS
        toks = [idx_smem[base + r0 + u] for u in range(n)]
        for u in range(n):
            pltpu.make_async_copy(_token_tile(h8_hbm, toks[u]), _token_tile(xbuf.at[slot_], r0 + u),
                                  sem.at[slot_]).start(priority=u % 2)

    def gather_block(blk, slot_):
        fetch_indices(blk)
        def issue(r8, _):
            start_rows(blk, slot_, r8 * 8, 8)
            return 0
        lax.fori_loop(0, MOE_ROWS // 8, issue, 0)

    @pl.when(i == 0)
    def _():
        gather_block(0, 0)
        @pl.when(n_used > 1)
        def _():
            gather_block(1, 1)

    nxt = jnp.minimum(i + 2, n_used - 1)
    slot_n = (i + 2) % N_ROW_BUFS

    @pl.when(used)
    def _():
        fetch_indices(nxt)
        pltpu.make_async_copy(h8_hbm.at[pl.ds(0, MOE_ROWS * 8)], xbuf.at[slot], sem.at[slot]).wait()

    prev = be_ref[jnp.maximum(i - 1, 0)]
    fresh = (i == 0) | (be_ref[i] != prev)

    @pl.when(used & fresh)
    def _():
        w13_s[:, :D_EXPERT] = w1_ref[...].astype(BF16)
        w13_s[:, D_EXPERT:] = w3_ref[...].astype(BF16)
        w2_s[...] = w2_ref[...].astype(BF16)

    @pl.when(used)
    def _():
        n_tiles = 2 * D_EXPERT // MXU_COLS
        n_tiles2 = D_MODEL // MXU_COLS
        rows_per_group = MOE_ROWS // (n_tiles + n_tiles2)
        x = _load_token_tiles(xbuf.at[slot], MOE_ROWS).astype(BF16)
        parts = []
        for c in range(n_tiles):
            start_rows(nxt, slot_n, c * rows_per_group, rows_per_group)
            parts.append(_dot(x, w13_s[:, c * MXU_COLS:(c + 1) * MXU_COLS]))
        h = jnp.concatenate(parts, axis=-1)
        h1 = h[:, :D_EXPERT]
        hdn = ((h1 * jax.nn.sigmoid(h1)) * h[:, D_EXPERT:]).astype(BF16)
        for c in range(n_tiles2):
            start_rows(nxt, slot_n, (n_tiles + c) * rows_per_group, rows_per_group)
            yc = _dot(hdn, w2_s[:, c * MXU_COLS:(c + 1) * MXU_COLS])
            for k in range(MXU_COLS // 128):
                ys_ref[pl.ds(c * (MXU_COLS // 128) + k, MOE_ROWS, stride=8), :] = yc[:, k * 128:(k + 1) * 128]

    def wait_rows(slot_):
        pltpu.make_async_copy(h8_hbm.at[pl.ds(0, MOE_ROWS * 8)], xbuf.at[slot_], sem.at[slot_]).wait()

    @pl.when(i == n_used - 1)
    def _():
        wait_rows(slot_n)
        @pl.when(n_used > 1)
        def _():
            wait_rows((i + 1) % N_ROW_BUFS)

    @pl.when(jnp.logical_not(used))
    def _():
        ys_ref[...] = jnp.zeros_like(ys_ref)


def _experts(h8, src, blk_e, n_used, w1, w3, w2, layer, cap):
    nblk = cap // MOE_ROWS
    return pl.pallas_call(
        _experts_kernel,
        grid_spec=pltpu.PrefetchScalarGridSpec(
            num_scalar_prefetch=2,
            grid=(nblk,),
            in_specs=[pl.BlockSpec(memory_space=pl.ANY),
                      pl.BlockSpec(memory_space=pl.ANY),
                      pl.BlockSpec((None, None, D_MODEL, D_EXPERT), lambda i, be, nu: (layer, be[i], 0, 0)),
                      pl.BlockSpec((None, None, D_MODEL, D_EXPERT), lambda i, be, nu: (layer, be[i], 0, 0)),
                      pl.BlockSpec((None, None, D_EXPERT, D_MODEL), lambda i, be, nu: (layer, be[i], 0, 0))],
            out_specs=pl.BlockSpec((MOE_ROWS * 8, 128), lambda i, be, nu: (i, 0)),
            scratch_shapes=[pltpu.VMEM((D_MODEL, 2 * D_EXPERT), BF16),
                            pltpu.VMEM((D_EXPERT, D_MODEL), BF16),
                            pltpu.VMEM((N_ROW_BUFS, MOE_ROWS * 8, 128), F32),
                            pltpu.SMEM((IDX_FETCH,), jnp.int32),
                            pltpu.SemaphoreType.DMA((N_ROW_BUFS,)),
                            pltpu.SemaphoreType.DMA(())]),
        out_shape=jax.ShapeDtypeStruct((cap * 8, 128), F32),
        compiler_params=_cparams(1),
        name="moe_experts",
    )(blk_e, n_used, src, h8, w1, w3, w2)


def _combine_kernel(ps_ref, asg_hbm, ys_hbm, h_ref, gate_ref, g_ref, b_ref, *rest, with_proj):
    if with_proj:
        w_ref, o_ref, p_ref, idx_smem, y0_ref, y1_ref, sem, isem = rest
    else:
        o_ref, idx_smem, y0_ref, y1_ref, sem, isem = rest
    i = pl.program_id(0)
    n_prog = pl.num_programs(0)
    tm = ROW_TILE
    slot = i % 2
    n_groups = 8
    grp = tm // n_groups

    def fetch_indices(tile):
        cp = pltpu.make_async_copy(asg_hbm.at[pl.ds(tile * 4 * tm, 4 * tm)], idx_smem, isem)
        cp.start()
        cp.wait()

    def start_rows(slot_, t0, n):
        r0 = [ps_ref[idx_smem[t0 + u]] + idx_smem[2 * tm + t0 + u] for u in range(n)]
        r1 = [ps_ref[idx_smem[tm + t0 + u]] + idx_smem[3 * tm + t0 + u] for u in range(n)]
        for u in range(n):
            pltpu.make_async_copy(_token_tile(ys_hbm, r0[u]), _token_tile(y0_ref.at[slot_], t0 + u),
                                  sem.at[slot_]).start(priority=0)
            pltpu.make_async_copy(_token_tile(ys_hbm, r1[u]), _token_tile(y1_ref.at[slot_], t0 + u),
                                  sem.at[slot_]).start(priority=1)

    def wait_tile(slot_):
        pltpu.make_async_copy(ys_hbm.at[pl.ds(0, tm * 8)], y0_ref.at[slot_], sem.at[slot_]).wait()
        pltpu.make_async_copy(ys_hbm.at[pl.ds(0, tm * 8)], y1_ref.at[slot_], sem.at[slot_]).wait()

    @pl.when(i == 0)
    def _():
        fetch_indices(0)
        def issue(t8, _):
            start_rows(0, t8 * 8, 8)
            return 0
        lax.fori_loop(0, tm // 8, issue, 0)

    fetch_indices(jnp.minimum(i + 1, n_prog - 1))
    if not with_proj:
        def issue_next(t8, _):
            start_rows(1 - slot, t8 * 8, 8)
            return 0
        lax.fori_loop(0, tm // 8, issue_next, 0)
    wait_tile(slot)
    if with_proj:
        start_rows(1 - slot, 0, grp)
    gate = gate_ref[...]
    y = (_load_token_tiles(y0_ref.at[slot], tm) * gate[:, 2:3]
         + _load_token_tiles(y1_ref.at[slot], tm) * gate[:, 3:4])
    out = _layer_norm(ALPHA * _load_token_tiles(h_ref, tm) + y, g_ref[...], b_ref[...])
    o_ref[...] = out
    if with_proj:
        start_rows(1 - slot, grp, grp)
        n_tiles = w_ref.shape[1] // MXU_COLS
        out_b = out.astype(BF16)
        for c in range(n_tiles):
            g = 2 + c * (n_groups - 2) // n_tiles
            g_end = 2 + (c + 1) * (n_groups - 2) // n_tiles
            for gg in range(g, g_end):
                start_rows(1 - slot, gg * grp, grp)
            p_ref[:, c * MXU_COLS:(c + 1) * MXU_COLS] = _dot(
                out_b, w_ref[:, c * MXU_COLS:(c + 1) * MXU_COLS]).astype(p_ref.dtype)

    @pl.when(i == n_prog - 1)
    def _():
        wait_tile(1 - slot)


def _combine_ln(pad_start, asg, ys, h8, rout, ln_g, ln_b, w_next=None):
    t = h8.shape[0] // 8
    tm = ROW_TILE
    with_proj = w_next is not None
    extra_in = [pl.BlockSpec(w_next.shape, lambda i, ps: (0, 0))] if with_proj else []
    extra_out = [pl.BlockSpec((tm, w_next.shape[1]), lambda i, ps: (i, 0))] if with_proj else []
    extra_shape = [jax.ShapeDtypeStruct((t, w_next.shape[1]), BF16)] if with_proj else []
    res = pl.pallas_call(
        functools.partial(_combine_kernel, with_proj=with_proj),
        grid_spec=pltpu.PrefetchScalarGridSpec(
            num_scalar_prefetch=1,
            grid=(t // tm,),
            in_specs=[pl.BlockSpec(memory_space=pl.ANY),
                      pl.BlockSpec(memory_space=pl.ANY),
                      pl.BlockSpec((tm * 8, 128), lambda i, ps: (i, 0)),
                      pl.BlockSpec((tm, 128), lambda i, ps: (i, 0)),
                      pl.BlockSpec((1, D_MODEL), lambda i, ps: (0, 0)),
                      pl.BlockSpec((1, D_MODEL), lambda i, ps: (0, 0))] + extra_in,
            out_specs=[pl.BlockSpec((tm, D_MODEL), lambda i, ps: (i, 0))] + extra_out,
            scratch_shapes=[pltpu.SMEM((4 * tm,), jnp.int32),
                            pltpu.VMEM((2, tm * 8, 128), F32),
                            pltpu.VMEM((2, tm * 8, 128), F32),
                            pltpu.SemaphoreType.DMA((2,)),
                            pltpu.SemaphoreType.DMA(())]),
        out_shape=[jax.ShapeDtypeStruct((t, D_MODEL), F32)] + extra_shape,
        compiler_params=_cparams(1),
        name="moe_combine_ln",
    )(pad_start, asg, ys, h8, rout, ln_g, ln_b, *([w_next] if with_proj else []))
    return tuple(res) if with_proj else res[0]


def _moe_layer(h8, w_group, b_group, w_router, b_router, w1, w3, w2, layer, ln_g, ln_b, w_next=None):
    t = h8.shape[0] // 8
    rout, asg, counts = _router(h8, w_group, b_group, w_router, b_router)
    padded = (counts + MOE_ROWS - 1) // MOE_ROWS * MOE_ROWS
    pad_end = jnp.cumsum(padded)
    pad_start = (pad_end - padded).astype(jnp.int32)
    cap = 2 * t + N_EXPERTS * MOE_ROWS
    nblk = cap // MOE_ROWS
    blk_e = jnp.minimum(jnp.sum(pad_end[None, :] <= (jnp.arange(nblk) * MOE_ROWS)[:, None], axis=1),
                        N_EXPERTS - 1).astype(jnp.int32)
    n_used = (pad_end[-1] // MOE_ROWS).astype(jnp.int32).reshape(1)
    key = (asg[0:2] * (2 * t) + asg[2:4]).reshape(-1)
    tok = jnp.broadcast_to(jnp.arange(t, dtype=jnp.int32), (2, t)).reshape(-1)
    _, by_row = lax.sort((key, tok), num_keys=1)
    cstart = jnp.cumsum(counts) - counts
    blk_first = cstart[blk_e] + (jnp.arange(nblk) * MOE_ROWS - pad_start[blk_e])
    blk_last = cstart[blk_e] + jnp.maximum(counts[blk_e] - 1, 0)
    idx = jnp.minimum(blk_first[:, None] + jnp.arange(MOE_ROWS)[None, :], blk_last[:, None])
    src = by_row[jnp.clip(idx, 0, 2 * t - 1)].reshape(-1)
    asg_tiles = jnp.transpose(asg.reshape(4, t // ROW_TILE, ROW_TILE), (1, 0, 2)).reshape(-1)
    ys = _experts(h8, src, blk_e, n_used, w1, w3, w2, layer, cap)
    return _combine_ln(pad_start, asg_tiles, ys, h8, rout, ln_g, ln_b, w_next)


def kernel(x, ln_mix_g, ln_mix_b, ln_ffn_g, ln_ffn_b, ab_w_in, s5_lam_re, s5_lam_im, s5_log_dt, s5_b_re, s5_b_im,
           s5_c_re, s5_c_im, s5_d, s5_glu_w, s5_glu_b, gla_gate_w, gla_gate_b, gla_norm_g, ab_w_out, c_w_in,
           c_sink, c_w_out, rel_bias, moe_w_group, moe_b_group, moe_w_router, moe_b_router, moe_w1, moe_w3, moe_w2):
    n_batch, seq, _ = x.shape
    t = n_batch * seq
    xt = x.reshape(t, D_MODEL)
    row = lambda v: v.astype(F32).reshape(1, -1)

    w_in0 = jnp.pad(ab_w_in[0], ((0, 0), (0, AB_IN_PAD - AB_IN))).astype(BF16)
    h0, u_t = _inproj0(xt, w_in0)
    n_steps = seq // (S5_SEGS * S5_CHUNK)
    kcat, r1e, r2, coef, ptab = _s5_prep(s5_lam_re[0], s5_lam_im[0], s5_log_dt[0], s5_b_re[0], s5_b_im[0],
                                         s5_c_re[0], s5_c_im[0], s5_d[0], n_steps)
    ya = _s5_mixer(u_t, kcat, r1e, r2, coef, ptab, n_batch, n_steps)
    gw = gla_gate_w[0].astype(F32)
    wg = jnp.zeros((128, 2 * GLA_KEY), F32)
    wg = wg.at[0:GLA_RANK, 0:GLA_KEY].set(gw[0]).at[GLA_RANK:2 * GLA_RANK, GLA_KEY:].set(gw[1]).astype(BF16)
    bg = gla_gate_b[0].astype(F32).reshape(1, 2 * GLA_KEY)
    o_f, o_b = _gla_mixer(h0, wg, bg, n_batch, seq)
    h8 = _mix0_out(ya, o_f, o_b, h0, xt, s5_glu_w[0].astype(BF16), row(s5_glu_b[0]), row(gla_norm_g[0]),
                      ab_w_out[0].astype(BF16), row(ln_mix_g[0]), row(ln_mix_b[0]))
    h, h3 = _moe_layer(h8, moe_w_group[0], moe_b_group[0], moe_w_router[0], moe_b_router[0],
                       moe_w1, moe_w3, moe_w2, 0, row(ln_ffn_g[0]), row(ln_ffn_b[0]), c_w_in[0].astype(BF16))

    o = _attn_mixer(h3, rel_bias, c_sink[0], n_batch, seq)
    h8 = _proj_ln(o, c_w_out[0].astype(BF16), h, row(ln_mix_g[1]), row(ln_mix_b[1]))
    h = _moe_layer(h8, moe_w_group[1], moe_b_group[1], moe_w_router[1], moe_b_router[1],
                   moe_w1, moe_w3, moe_w2, 1, row(ln_ffn_g[1]), row(ln_ffn_b[1]))
    return h.reshape(n_batch, seq, D_MODEL)
```

```python
import functools
import math

import jax
import jax.numpy as jnp
from jax import lax
from jax.experimental import pallas as pl
from jax.experimental.pallas import tpu as pltpu

F32 = jnp.float32
BF16 = jnp.bfloat16

D_MODEL = 1024
S5_WIDTH = 512
S5_GROUP_CH = 16
S5_GROUPS = 32
S5_STATE = 64
GLA_HEADS = 4
GLA_DV = 128
GLA_DK = 64
GLA_KEY = 256
GLA_WIDTH = 512
GLA_RANK = 16
GLA_TAU = 16.0
GLA_CHUNK = 64
AB_IN = 2080
AB_IN_PAD = 2176
HEAD_DIM = 64
N_HEADS = 16
N_KV = 4
GQA = 4
ATT_DIM = 1024
KV_DIM = 256
WINDOW = 128
ATT_BLOCK = 128
ATT_SCALE = HEAD_DIM ** -0.5
REL_BUCKETS = 32
REL_MAX_DIST = 128
NEG_INF = -1e30
N_GROUPS = 4
EPG = 8
N_EXPERTS = 32
D_EXPERT = 512
LN_EPS = 1e-5
RMS_EPS = 1e-6
DEPTH = 2
ALPHA = (2 * DEPTH) ** 0.25

S5_CHUNK = 16
S5_SEGS = 8
S5_SEG_PAD = 4
ROW_TILE = 512
GLA_ROWS = 256
ATT_ROWS = 512
MOE_ROWS = 256
IDX_FETCH = 1024
N_ROW_BUFS = 3
MXU_COLS = 256
VMEM_LIMIT = 56 * 1024 * 1024


def _cparams(n_axes):
    return pltpu.CompilerParams(dimension_semantics=("arbitrary",) * n_axes,
                                vmem_limit_bytes=VMEM_LIMIT)


def _dot(a, b):
    return jnp.dot(a, b, preferred_element_type=F32)


def _dot_nt(a, b):
    return lax.dot_general(a, b, (((1,), (1,)), ((), ())), preferred_element_type=F32)


def _dot_tn(a, b):
    return lax.dot_general(a, b, (((0,), (0,)), ((), ())), preferred_element_type=F32)


def _layer_norm(r, g, b):
    mu = jnp.mean(r, axis=-1, keepdims=True)
    c = r - mu
    var = jnp.mean(c * c, axis=-1, keepdims=True)
    return c * lax.rsqrt(var + LN_EPS) * g + b


def _store_token_tiles(ref, val):
    n = val.shape[0]
    for k in range(D_MODEL // 128):
        ref[pl.ds(k, n, stride=8), :] = val[:, k * 128:(k + 1) * 128]


def _load_token_tiles(ref, n):
    return jnp.concatenate([ref[pl.ds(k, n, stride=8), :] for k in range(D_MODEL // 128)], axis=-1)


def _s5_prep(lam_re, lam_im, log_dt, b_re, b_im, c_re, c_im, d, n_steps):
    tc = S5_CHUNK
    g, p, c = S5_GROUPS, S5_STATE, S5_GROUP_CH
    lr = jnp.minimum(lam_re.astype(F32), -1e-4)
    li = lam_im.astype(F32)
    dt = jnp.exp(log_dt.astype(F32))[..., None]
    mag = jnp.exp(lr * dt)
    ar = mag * jnp.cos(li * dt)
    ai = mag * jnp.sin(li * dt)
    den = lr * lr + li * li
    nr = ar - 1.0
    coef_r = (nr * lr + ai * li) / den
    coef_i = (ai * lr - nr * li) / den
    br_ = b_re.astype(F32)
    bi_ = b_im.astype(F32)
    bbr = coef_r[..., None] * br_ - coef_i[..., None] * bi_
    bbi = coef_r[..., None] * bi_ + coef_i[..., None] * br_
    cr = c_re.astype(F32)
    ci = c_im.astype(F32)
    bbr_t = jnp.swapaxes(bbr, -1, -2)
    bbi_t = jnp.swapaxes(bbi, -1, -2)
    npair = g // 2

    def apow(n):
        nn = n.astype(F32)[:, None, None, None]
        m_ = jnp.exp(nn * (lr * dt)[None])
        ang = nn * (li * dt)[None]
        return m_ * jnp.cos(ang), m_ * jnp.sin(ang)

    pr, pi = apow(jnp.arange(tc + 1))
    hr = pr[:, :, :, None, :] * bbr_t[None] - pi[:, :, :, None, :] * bbi_t[None]
    hi = pr[:, :, :, None, :] * bbi_t[None] + pi[:, :, :, None, :] * bbr_t[None]
    kk = jnp.einsum('dgoq,jdgcq->jdgco', cr, hr) - jnp.einsum('dgoq,jdgcq->jdgco', ci, hi)
    k0 = kk[0, 0] + kk[0, 1] + d.astype(F32).reshape(g, c)[:, :, None] * jnp.eye(c, dtype=F32)
    slab = jnp.concatenate([kk[tc - 1:0:-1, 1], k0[None], kk[1:tc, 0]], axis=0)
    kcat = jnp.transpose(slab, (1, 2, 0, 3)).reshape(g, c, (2 * tc - 1) * c)
    kcat = jnp.pad(kcat, ((0, 0), (0, 0), (0, 2 * tc * c - kcat.shape[-1]))).reshape(npair, 2, c, 2 * tc * c)

    def pair_blockdiag(m):
        m2 = m.reshape((npair, 2) + m.shape[1:])
        z = jnp.zeros_like(m2[:, 0])
        return jnp.concatenate([jnp.concatenate([m2[:, 0], z], axis=-1),
                                jnp.concatenate([z, m2[:, 1]], axis=-1)], axis=1)

    def w_plane(h_, lag_sel, dirn):
        w = jnp.transpose(h_[lag_sel, dirn], (1, 0, 2, 3)).reshape(g, tc * c, p)
        return pair_blockdiag(w)
    s_fwd = tc - 1 - jnp.arange(tc)
    s_bwd = jnp.arange(tc)
    r1e = jnp.concatenate([w_plane(hr, s_fwd, 0), w_plane(hi, s_fwd, 0), w_plane(hr, s_bwd, 1), w_plane(hi, s_bwd, 1)],
                          axis=2)

    lane = jnp.arange(tc * c)
    rep_t = (lane[None, :] // c == jnp.arange(tc)[:, None]).astype(F32)
    tile_co = (lane[None, :] % c == jnp.arange(c)[:, None]).astype(F32)
    spread = lambda a, sel: jnp.dot(a, sel, precision=lax.Precision.HIGHEST)
    def readout(e, dirn):
        pr_l = spread(jnp.transpose(pr[e, dirn], (1, 2, 0)).reshape(g * p, tc), rep_t)
        pi_l = spread(jnp.transpose(pi[e, dirn], (1, 2, 0)).reshape(g * p, tc), rep_t)
        cr_l = spread(jnp.transpose(cr[dirn], (0, 2, 1)).reshape(g * p, c), tile_co)
        ci_l = spread(jnp.transpose(ci[dirn], (0, 2, 1)).reshape(g * p, c), tile_co)
        zr = (cr_l * pr_l - ci_l * pi_l).reshape(g, p, tc * c)
        zi = (cr_l * pi_l + ci_l * pr_l).reshape(g, p, tc * c)
        return pair_blockdiag(zr), pair_blockdiag(-zi)
    vf_r, vf_i = readout(jnp.arange(tc) + 1, 0)
    vb_r, vb_i = readout(tc - jnp.arange(tc), 1)
    r2 = jnp.concatenate([vf_r, vf_i, vb_r, vb_i], axis=1)

    steps = jnp.arange(n_steps)
    a16r, a16i = apow(jnp.array([tc]))
    apr, api = apow(steps * tc)
    anr, ani = apow(jnp.array([tc * n_steps]))
    def lanes(x):
        n_ = x.shape[0]
        return jnp.transpose(x.reshape(n_, npair, 2 * p), (1, 0, 2))
    coef = jnp.concatenate([lanes(a16r[:, 0]), lanes(a16i[:, 0]), lanes(a16r[:, 1]), lanes(a16i[:, 1]),
                            lanes(anr[:, 0]), lanes(ani[:, 0]), lanes(anr[:, 1]), lanes(ani[:, 1])], axis=1)
    rev = n_steps - 1 - steps
    ptab = jnp.stack([lanes(apr[:, 0]), lanes(api[:, 0]), lanes(apr[rev, 1]), lanes(api[rev, 1])], axis=1)
    return kcat, r1e.astype(BF16), r2.astype(BF16), coef, ptab


def _gelu_tanh(x):
    return 0.5 * x * (1.0 + jnp.tanh(math.sqrt(2.0 / math.pi) * (x + 0.044715 * (x * x * x))))


def _s5_kernel(u_ref, kcat_ref, r1e_ref, r2_ref, coef_ref, ptab_ref, y_ref, kt_ref, yi_ref, e_ref, s_ref,
               *, n_batch, n_steps):
    pitch = n_steps + S5_SEG_PAD
    blk = S5_CHUNK * S5_GROUP_CH
    kt_ref[...] = jnp.zeros_like(kt_ref)
    for gi in range(2):
        slab = kcat_ref[gi]
        for s in range(S5_CHUNK):
            off = (S5_CHUNK - 1 - s) * S5_GROUP_CH
            kt_ref[gi * blk + s * S5_GROUP_CH:gi * blk + (s + 1) * S5_GROUP_CH, gi * blk:(gi + 1) * blk] = (
                slab[:, off:off + blk].astype(BF16))
    u = u_ref[...]
    yi_ref[...] = _dot(u, kt_ref[...])
    e = _dot(u, r1e_ref[...])
    n_seg = n_batch * S5_SEGS
    for k in range(4):
        for sg in range(n_seg):
            e_ref[k, sg * pitch:sg * pitch + n_steps, :] = e[sg * n_steps:(sg + 1) * n_steps, k * 128:(k + 1) * 128]
    coef = coef_ref[...]
    a16fr, a16fi, a16br, a16bi = coef[0:1], coef[1:2], coef[2:3], coef[3:4]
    anfr, anfi, anbr, anbi = coef[4:5], coef[5:6], coef[6:7], coef[7:8]
    zero = jnp.zeros((S5_SEGS, 128), F32)
    sub = lax.broadcasted_iota(jnp.int32, (S5_SEGS, 128), 0)

    def seg_rows(b, j):
        return pl.ds(b * S5_SEGS * pitch + j, S5_SEGS, stride=pitch)

    def local_step(j, carry):
        out = []
        for b in range(n_batch):
            sfr, sfi, sbr, sbi = carry[b]
            rf = seg_rows(b, j)
            rb = seg_rows(b, n_steps - 1 - j)
            s_ref[0, rf, :] = sfr
            s_ref[1, rf, :] = sfi
            s_ref[2, rb, :] = sbr
            s_ref[3, rb, :] = sbi
            efr = e_ref[0, rf, :]
            efi = e_ref[1, rf, :]
            ebr = e_ref[2, rb, :]
            ebi = e_ref[3, rb, :]
            out.append((a16fr * sfr - a16fi * sfi + efr, a16fr * sfi + a16fi * sfr + efi,
                        a16br * sbr - a16bi * sbi + ebr, a16br * sbi + a16bi * sbr + ebi))
        return tuple(out)

    ends = lax.fori_loop(0, n_steps, local_step, tuple((zero, zero, zero, zero) for _ in range(n_batch)))

    carries = []
    for b in range(n_batch):
        efr, efi, ebr, ebi = ends[b]
        cfr, cfi, cbr, cbi = zero, zero, zero, zero
        for _ in range(S5_SEGS - 1):
            tfr = anfr * cfr - anfi * cfi + efr
            tfi = anfr * cfi + anfi * cfr + efi
            cfr = jnp.where(sub == 0, 0.0, pltpu.roll(tfr, 1, 0))
            cfi = jnp.where(sub == 0, 0.0, pltpu.roll(tfi, 1, 0))
            tbr = anbr * cbr - anbi * cbi + ebr
            tbi = anbr * cbi + anbi * cbr + ebi
            cbr = jnp.where(sub == S5_SEGS - 1, 0.0, pltpu.roll(tbr, S5_SEGS - 1, 0))
            cbi = jnp.where(sub == S5_SEGS - 1, 0.0, pltpu.roll(tbi, S5_SEGS - 1, 0))
        carries.append((cfr, cfi, cbr, cbi))

    def fix_step(j, _):
        pfr = ptab_ref[0, pl.ds(j, 1), :]
        pfi = ptab_ref[1, pl.ds(j, 1), :]
        pbr = ptab_ref[2, pl.ds(j, 1), :]
        pbi = ptab_ref[3, pl.ds(j, 1), :]
        for b in range(n_batch):
            cfr, cfi, cbr, cbi = carries[b]
            rj = seg_rows(b, j)
            s_ref[0, rj, :] = s_ref[0, rj, :] + (pfr * cfr - pfi * cfi)
            s_ref[1, rj, :] = s_ref[1, rj, :] + (pfr * cfi + pfi * cfr)
            s_ref[2, rj, :] = s_ref[2, rj, :] + (pbr * cbr - pbi * cbi)
            s_ref[3, rj, :] = s_ref[3, rj, :] + (pbr * cbi + pbi * cbr)
        return 0

    lax.fori_loop(0, n_steps, fix_step, 0)

    s_all = jnp.concatenate(
        [jnp.concatenate([s_ref[k, sg * pitch:sg * pitch + n_steps, :] for sg in range(n_seg)], axis=0)
         for k in range(4)], axis=-1)
    y = yi_ref[...] + _dot(s_all.astype(BF16), r2_ref[...])
    y_ref[...] = _gelu_tanh(y).astype(y_ref.dtype)


def _s5_mixer(u_t, kcat, r1e, r2, coef, ptab, n_batch, n_steps):
    npair, m, _ = u_t.shape
    scan_rows = n_batch * S5_SEGS * (n_steps + S5_SEG_PAD)
    kern = functools.partial(_s5_kernel, n_batch=n_batch, n_steps=n_steps)
    return pl.pallas_call(
        kern,
        grid=(npair,),
        in_specs=[pl.BlockSpec((None, m, 512), lambda i: (i, 0, 0)),
                  pl.BlockSpec((None, 2, S5_GROUP_CH, 512), lambda i: (i, 0, 0, 0)),
                  pl.BlockSpec((None, 512, 512), lambda i: (i, 0, 0)),
                  pl.BlockSpec((None, 512, 512), lambda i: (i, 0, 0)),
                  pl.BlockSpec((None, 8, 128), lambda i: (i, 0, 0)),
                  pl.BlockSpec((None, 4, n_steps, 128), lambda i: (i, 0, 0, 0))],
        out_specs=pl.BlockSpec((None, m, 512), lambda i: (i, 0, 0)),
        out_shape=jax.ShapeDtypeStruct((npair, m, 512), BF16),
        scratch_shapes=[pltpu.VMEM((512, 512), BF16), pltpu.VMEM((m, 512), F32),
                        pltpu.VMEM((4, scan_rows, 128), F32), pltpu.VMEM((4, scan_rows, 128), F32)],
        compiler_params=_cparams(1),
        name="s5_scan",
    )(u_t, kcat, r1e, r2, coef, ptab)


def _inproj0_kernel(x_ref, w_ref, h_ref, ut_ref, u_s):
    h = _dot(x_ref[...].astype(BF16), w_ref[...])
    h_ref[...] = h[:, S5_WIDTH:].astype(h_ref.dtype)
    n_lane_blk = S5_WIDTH // 128
    for k in range(n_lane_blk):
        u_s[k] = h[:, k * 128:(k + 1) * 128]
    n_chunk = ROW_TILE // S5_CHUNK
    rows = [jnp.concatenate([u_s[k, pl.ds(s, n_chunk, stride=S5_CHUNK), :] for k in range(n_lane_blk)], axis=-1)
            for s in range(S5_CHUNK)]
    ch = S5_GROUP_CH
    for p in range(S5_GROUPS // 2):
        ut_ref[p] = jnp.concatenate([r[:, (2 * p + gi) * ch:(2 * p + gi + 1) * ch] for gi in range(2) for r in rows],
                                    axis=-1).astype(ut_ref.dtype)


def _inproj0(x, w):
    t, k = x.shape
    n = w.shape[1]
    tm = ROW_TILE
    npair = S5_GROUPS // 2
    return pl.pallas_call(
        _inproj0_kernel,
        grid=(t // tm,),
        in_specs=[pl.BlockSpec((tm, k), lambda i: (i, 0)),
                  pl.BlockSpec((k, n), lambda i: (0, 0))],
        out_specs=[pl.BlockSpec((tm, n - S5_WIDTH), lambda i: (i, 0)),
                   pl.BlockSpec((npair, tm // S5_CHUNK, 512), lambda i: (0, i, 0))],
        out_shape=[jax.ShapeDtypeStruct((t, n - S5_WIDTH), BF16),
                   jax.ShapeDtypeStruct((npair, t // S5_CHUNK, 512), BF16)],
        scratch_shapes=[pltpu.VMEM((S5_WIDTH // 128, tm, 128), F32)],
        compiler_params=_cparams(1),
        name="inproj0",
    )(x, w)


def _gla_kernel(qf_ref, kf_ref, vf_ref, lf_ref, qb_ref, kb_ref, vb_ref, lb_ref, wg_ref, bg_ref,
                of_ref, ob_ref, st_ref):
    @pl.when(pl.program_id(1) == 0)
    def _():
        st_ref[...] = jnp.zeros_like(st_ref)

    n_chunks = GLA_ROWS // GLA_CHUNK
    row = lax.broadcasted_iota(jnp.int32, (GLA_CHUNK, GLA_CHUNK), 0)
    col = lax.broadcasted_iota(jnp.int32, (GLA_CHUNK, GLA_CHUNK), 1)
    mask_f = col <= row
    mask_b = col > row
    brow = lax.broadcasted_iota(jnp.int32, (GLA_ROWS, GLA_ROWS), 0)
    bcol = lax.broadcasted_iota(jnp.int32, (GLA_ROWS, GLA_ROWS), 1)
    shift = GLA_CHUNK.bit_length() - 1
    same = (brow >> shift) == (bcol >> shift)
    ones_blk = same.astype(BF16)
    tri_f = (same & (bcol <= brow)).astype(BF16)
    tri_b = (same & (bcol >= brow)).astype(BF16)
    wg = wg_ref[...]
    bg = bg_ref[...]

    def direction(q_ref, k_ref, v_ref, l_ref, o_ref, d):
        z = _dot(l_ref[...], wg[:, d * GLA_KEY:(d + 1) * GLA_KEY]) + bg[:, d * GLA_KEY:(d + 1) * GLA_KEY]
        log_a = jax.nn.log_sigmoid(z) * (1.0 / GLA_TAU)
        la_hi = log_a.astype(BF16)
        la_lo = (log_a - la_hi.astype(F32)).astype(BF16)
        tri = tri_f if d == 0 else tri_b
        bc = _dot(tri, la_hi) + _dot(tri, la_lo)
        tot = _dot(ones_blk, la_hi) + _dot(ones_blk, la_lo)
        q = q_ref[...].astype(F32) * (GLA_DK ** -0.5)
        k = k_ref[...].astype(F32)
        qd_all = (q * jnp.exp(bc)).astype(BF16)
        kd_all = (k * jnp.exp(-bc)).astype(BF16)
        kc_all = (k * jnp.exp(tot - bc)).astype(BF16)
        decay_all = jnp.exp(tot)
        order = range(n_chunks) if d == 0 else range(n_chunks - 1, -1, -1)
        mask = mask_f if d == 0 else mask_b
        states = [st_ref[d, h] for h in range(GLA_HEADS)]
        for c in order:
            sl = slice(c * GLA_CHUNK, (c + 1) * GLA_CHUNK)
            qd, kd, kc = qd_all[sl], kd_all[sl], kc_all[sl]
            decay = decay_all[c * GLA_CHUNK:c * GLA_CHUNK + 1]
            v = v_ref[sl, :]
            outs = []
            for h in range(GLA_HEADS):
                ks = slice(h * GLA_DK, (h + 1) * GLA_DK)
                vs = slice(h * GLA_DV, (h + 1) * GLA_DV)
                s = jnp.where(mask, _dot_nt(qd[:, ks], kd[:, ks]), 0.0).astype(BF16)
                st = states[h]
                o = _dot(s, v[:, vs]) + _dot_nt(qd[:, ks], st.astype(BF16))
                states[h] = st * decay[:, ks] + _dot_tn(v[:, vs], kc[:, ks])
                outs.append(o)
            o_ref[sl, :] = jnp.concatenate(outs, axis=-1).astype(o_ref.dtype)
        for h in range(GLA_HEADS):
            st_ref[d, h] = states[h]

    direction(qf_ref, kf_ref, vf_ref, lf_ref, of_ref, 0)
    direction(qb_ref, kb_ref, vb_ref, lb_ref, ob_ref, 1)


def _gla_mixer(h0, wg, bg, n_batch, seq):
    nb = seq // GLA_ROWS
    r = GLA_ROWS
    fwd = lambda b, i: b * nb + i
    bwd = lambda b, i: b * nb + (nb - 1 - i)
    def spec(width, colblk, rowfn):
        return pl.BlockSpec((r, width), lambda b, i: (rowfn(b, i), colblk))
    in_specs = [spec(256, 0, fwd), spec(256, 1, fwd), spec(512, 1, fwd), spec(128, 12, fwd),
                spec(256, 0, bwd), spec(256, 1, bwd), spec(512, 1, bwd), spec(128, 12, bwd),
                pl.BlockSpec((128, 512), lambda b, i: (0, 0)),
                pl.BlockSpec((1, 512), lambda b, i: (0, 0))]
    out_specs = [pl.BlockSpec((r, 512), lambda b, i: (fwd(b, i), 0)),
                 pl.BlockSpec((r, 512), lambda b, i: (bwd(b, i), 0))]
    t = n_batch * seq
    return pl.pallas_call(
        _gla_kernel,
        grid=(n_batch, nb),
        in_specs=in_specs,
        out_specs=out_specs,
        out_shape=[jax.ShapeDtypeStruct((t, 512), BF16), jax.ShapeDtypeStruct((t, 512), BF16)],
        scratch_shapes=[pltpu.VMEM((2, GLA_HEADS, GLA_DV, GLA_DK), F32)],
        compiler_params=_cparams(2),
        name="gla_chunked",
    )(h0, h0, h0, h0, h0, h0, h0, h0, wg, bg)


def _mix0_out_kernel(y_ref, of_ref, ob_ref, go_ref, x_ref, gw_ref, gb_ref, ng_ref, wo_ref, lg_ref, lb_ref,
                     h8_ref, ya_s):
    n_chunk = ROW_TILE // S5_CHUNK
    ys = [y_ref[p].astype(F32) for p in range(S5_GROUPS // 2)]
    n_lane_blk = S5_WIDTH // 128
    ch = S5_GROUP_CH
    blk = S5_CHUNK * ch
    for t in range(S5_CHUNK):
        for k in range(n_lane_blk):
            ya_s[k, pl.ds(t, n_chunk, stride=S5_CHUNK), :] = jnp.concatenate(
                [y[:, gi * blk + t * ch:gi * blk + (t + 1) * ch] for y in ys[4 * k:4 * k + 4] for gi in range(2)],
                axis=-1)
    yaf = jnp.concatenate([ya_s[k] for k in range(n_lane_blk)], axis=-1)
    ya = yaf.astype(BF16)
    gate = _dot(ya, gw_ref[...]) + gb_ref[...]
    ya2 = yaf * jax.nn.sigmoid(gate)
    o = of_ref[...].astype(F32) + ob_ref[...].astype(F32)
    ng = ng_ref[...]
    parts = []
    for h in range(GLA_HEADS):
        oh = o[:, h * GLA_DV:(h + 1) * GLA_DV]
        ms = jnp.mean(oh * oh, axis=-1, keepdims=True)
        parts.append(oh * lax.rsqrt(ms + RMS_EPS) * ng[:, h * GLA_DV:(h + 1) * GLA_DV])
    yb = jnp.concatenate(parts, axis=-1) * jax.nn.silu(go_ref[...].astype(F32))
    wo = wo_ref[...]
    y = _dot(ya2.astype(BF16), wo[:S5_WIDTH]) + _dot(yb.astype(BF16), wo[S5_WIDTH:])
    _store_token_tiles(h8_ref, _layer_norm(ALPHA * x_ref[...] + y, lg_ref[...], lb_ref[...]))


def _mix0_out(ya, o_f, o_b, h0, x, glu_w, glu_b, norm_g, w_out, ln_g, ln_b):
    t = x.shape[0]
    tm = ROW_TILE
    row = lambda w: pl.BlockSpec((tm, w), lambda i: (i, 0))
    full = lambda a, b: pl.BlockSpec((a, b), lambda i: (0, 0))
    return pl.pallas_call(
        _mix0_out_kernel,
        grid=(t // tm,),
        in_specs=[pl.BlockSpec((S5_GROUPS // 2, tm // S5_CHUNK, 512), lambda i: (0, i, 0)),
                  row(512), row(512),
                  pl.BlockSpec((tm, 512), lambda i: (i, 2)),
                  row(1024), full(512, 512), full(1, 512), full(1, 512), full(1024, 1024),
                  full(1, 1024), full(1, 1024)],
        out_specs=pl.BlockSpec((tm * 8, 128), lambda i: (i, 0)),
        out_shape=jax.ShapeDtypeStruct((t * 8, 128), F32),
        scratch_shapes=[pltpu.VMEM((S5_WIDTH // 128, tm, 128), F32)],
        compiler_params=_cparams(1),
        name="mix0_out_ln",
    )(ya, o_f, o_b, h0, x, glu_w, glu_b, norm_g, w_out, ln_g, ln_b)


def _attn_kernel(q_ref, kc_ref, vc_ref, kp_ref, vp_ref, kn_ref, vn_ref, bias_ref, sink_ref, o_ref, *, seq):
    blk = pl.program_id(1)
    n_blk = pl.num_programs(1)
    n_sub = ATT_ROWS // ATT_BLOCK
    kw = ATT_BLOCK + 2 * WINDOW
    nq = GQA * ATT_BLOCK
    key_row = lax.broadcasted_iota(jnp.int32, (kw, nq), 0)
    pen_first = jnp.where((blk == 0) & (key_row < WINDOW), NEG_INF, 0.0)
    pen_last = jnp.where((blk == n_blk - 1) & (key_row >= ATT_BLOCK + WINDOW), NEG_INF, 0.0)
    kall = jnp.concatenate([kp_ref[...], kc_ref[...], kn_ref[...]], axis=0)
    vall = jnp.concatenate([vp_ref[...], vc_ref[...], vn_ref[...]], axis=0)
    for s in range(n_sub):
        kwin = kall[s * ATT_BLOCK:s * ATT_BLOCK + kw]
        vwin = vall[s * ATT_BLOCK:s * ATT_BLOCK + kw]
        q = q_ref[s * ATT_BLOCK:(s + 1) * ATT_BLOCK, :] * ATT_SCALE
        outs_t = []
        for kv in range(N_KV):
            qs = jnp.concatenate([q[:, (kv * GQA + g) * HEAD_DIM:(kv * GQA + g + 1) * HEAD_DIM]
                                  for g in range(GQA)], axis=0)
            kh = kwin[:, kv * HEAD_DIM:(kv + 1) * HEAD_DIM]
            vh = vwin[:, kv * HEAD_DIM:(kv + 1) * HEAD_DIM]
            st = _dot_nt(kh, qs) + bias_ref[kv]
            if s == 0:
                st = st + pen_first
            if s == n_sub - 1:
                st = st + pen_last
            sink = sink_ref[kv]
            m = jnp.maximum(jnp.max(st, axis=0, keepdims=True), sink)
            p = jnp.exp(st - m)
            den = jnp.sum(p, axis=0, keepdims=True) + jnp.exp(sink - m)
            outs_t.append(_dot_tn(vh, p.astype(BF16)) / den)
        o = jnp.concatenate(outs_t, axis=0).T
        pieces = [o[g * ATT_BLOCK:(g + 1) * ATT_BLOCK, kv * HEAD_DIM:(kv + 1) * HEAD_DIM]
                  for kv in range(N_KV) for g in range(GQA)]
        o_ref[s * ATT_BLOCK:(s + 1) * ATT_BLOCK, :] = jnp.concatenate(pieces, axis=-1).astype(o_ref.dtype)


def _t5_bucket(rel):
    nb = REL_BUCKETS // 2
    max_exact = nb // 2
    ret = (rel > 0).astype(jnp.int32) * nb
    n = jnp.abs(rel)
    large = max_exact + (jnp.log(jnp.maximum(n, 1).astype(F32) / max_exact)
                         / math.log(REL_MAX_DIST / max_exact) * (nb - max_exact)).astype(jnp.int32)
    large = jnp.minimum(large, nb - 1)
    return ret + jnp.where(n < max_exact, n, large)


def _attn_mixer(h3, rel_bias, sink, n_batch, seq):
    kw = ATT_BLOCK + 2 * WINDOW
    n_rel = kw + ATT_BLOCK - 1
    rel = jnp.arange(n_rel) - (n_rel - 1) // 2
    tb = rel_bias.astype(F32)[_t5_bucket(rel)]
    tb = jnp.where((jnp.abs(rel) <= WINDOW)[:, None], tb, NEG_INF)
    tb = jnp.pad(tb.T, ((0, 0), (0, 1)))
    flat = jnp.tile(tb, (1, ATT_BLOCK))[:, ATT_BLOCK - 1:ATT_BLOCK - 1 + ATT_BLOCK * n_rel]
    bias = flat.reshape(N_HEADS, ATT_BLOCK, n_rel)[:, :, :kw]
    bias = jnp.transpose(bias.reshape(N_KV, GQA, ATT_BLOCK, kw), (0, 3, 1, 2)).reshape(N_KV, kw, GQA * ATT_BLOCK)
    sink_rows = jnp.repeat(sink.astype(F32).reshape(N_KV, GQA), ATT_BLOCK, axis=1)[:, None, :]
    nblk = seq // ATT_ROWS
    sub = ATT_ROWS // ATT_BLOCK
    n128 = seq // ATT_BLOCK
    cur = lambda b, i: b * nblk + i
    prev = lambda b, i: b * n128 + jnp.maximum(i * sub - 1, 0)
    nxt = lambda b, i: b * n128 + jnp.minimum((i + 1) * sub, n128 - 1)
    t = n_batch * seq
    kern = functools.partial(_attn_kernel, seq=seq)
    return pl.pallas_call(
        kern,
        grid=(n_batch, nblk),
        in_specs=[pl.BlockSpec((ATT_ROWS, ATT_DIM), lambda b, i: (cur(b, i), 0)),
                  pl.BlockSpec((ATT_ROWS, KV_DIM), lambda b, i: (cur(b, i), 4)),
                  pl.BlockSpec((ATT_ROWS, KV_DIM), lambda b, i: (cur(b, i), 5)),
                  pl.BlockSpec((ATT_BLOCK, KV_DIM), lambda b, i: (prev(b, i), 4)),
                  pl.BlockSpec((ATT_BLOCK, KV_DIM), lambda b, i: (prev(b, i), 5)),
                  pl.BlockSpec((ATT_BLOCK, KV_DIM), lambda b, i: (nxt(b, i), 4)),
                  pl.BlockSpec((ATT_BLOCK, KV_DIM), lambda b, i: (nxt(b, i), 5)),
                  pl.BlockSpec((N_KV, kw, GQA * ATT_BLOCK), lambda b, i: (0, 0, 0)),
                  pl.BlockSpec((N_KV, 1, GQA * ATT_BLOCK), lambda b, i: (0, 0, 0))],
        out_specs=pl.BlockSpec((ATT_ROWS, ATT_DIM), lambda b, i: (cur(b, i), 0)),
        out_shape=jax.ShapeDtypeStruct((t, ATT_DIM), BF16),
        compiler_params=_cparams(2),
        name="window_gqa",
    )(h3, h3, h3, h3, h3, h3, h3, bias, sink_rows)


def _proj_ln_kernel(a_ref, w_ref, x_ref, g_ref, b_ref, h8_ref):
    y = _dot(a_ref[...], w_ref[...])
    _store_token_tiles(h8_ref, _layer_norm(ALPHA * x_ref[...] + y, g_ref[...], b_ref[...]))


def _proj_ln(a, w, x, g, b):
    t, k = a.shape
    tm = ROW_TILE
    return pl.pallas_call(
        _proj_ln_kernel,
        grid=(t // tm,),
        in_specs=[pl.BlockSpec((tm, k), lambda i: (i, 0)),
                  pl.BlockSpec((k, D_MODEL), lambda i: (0, 0)),
                  pl.BlockSpec((tm, D_MODEL), lambda i: (i, 0)),
                  pl.BlockSpec((1, D_MODEL), lambda i: (0, 0)),
                  pl.BlockSpec((1, D_MODEL), lambda i: (0, 0))],
        out_specs=pl.BlockSpec((tm * 8, 128), lambda i: (i, 0)),
        out_shape=jax.ShapeDtypeStruct((t * 8, 128), F32),
        compiler_params=_cparams(1),
        name="proj_ln",
    )(a, w, x, g, b)


def _router_kernel(h_ref, wh_ref, wl_ref, b_ref, o_ref, ot_ref, cnt_ref, run_ref):
    @pl.when(pl.program_id(0) == 0)
    def _():
        run_ref[...] = jnp.zeros_like(run_ref)

    x = _load_token_tiles(h_ref, ROW_TILE)
    xh = x.astype(BF16)
    xl = (x - xh.astype(F32)).astype(BF16)
    wh = wh_ref[...]
    lt = _dot_nt(wh, xh) + (_dot_nt(wh, xl) + _dot_nt(wl_ref[...], xh)) + b_ref[...][:, 0:1]
    tm = lt.shape[1]
    neg = jnp.float32(-jnp.inf)
    row8 = lax.broadcasted_iota(jnp.int32, (EPG, tm), 0).astype(F32)
    is_g = row8 < N_GROUPS
    gl = jnp.where(is_g, lt[0:EPG], neg)
    gmax = jnp.max(gl, axis=0, keepdims=True)
    gsum = jnp.sum(jnp.where(is_g, jnp.exp(gl - gmax), 0.0), axis=0, keepdims=True)
    grp = jnp.min(jnp.where(gl == gmax, row8, float(EPG)), axis=0, keepdims=True)
    p_grp = 1.0 / gsum
    el = lt[EPG:2 * EPG]
    for g in range(1, N_GROUPS):
        el = jnp.where(grp == float(g), lt[EPG * (g + 1):EPG * (g + 2)], el)
    m1 = jnp.max(el, axis=0, keepdims=True)
    i1 = jnp.min(jnp.where(el == m1, row8, float(EPG)), axis=0, keepdims=True)
    el2 = jnp.where(row8 == i1, neg, el)
    m2 = jnp.max(el2, axis=0, keepdims=True)
    i2 = jnp.min(jnp.where((row8 != i1) & (el2 == m2), row8, float(EPG)), axis=0, keepdims=True)
    e2 = jnp.exp(m2 - m1)
    g1 = p_grp / (1.0 + e2)
    g2 = p_grp * e2 / (1.0 + e2)
    eid1 = grp * EPG + i1
    eid2 = grp * EPG + i2
    rowe = lax.broadcasted_iota(jnp.int32, (N_EXPERTS, tm), 0).astype(F32)
    hit1 = rowe == eid1
    hit2 = rowe == eid2
    onehot = (hit1 | hit2).astype(BF16)
    upper = (lax.broadcasted_iota(jnp.int32, (tm, tm), 0) < lax.broadcasted_iota(jnp.int32, (tm, tm), 1)).astype(BF16)
    before = _dot(onehot, upper) + run_ref[...][:, 0:1]
    r1 = jnp.sum(jnp.where(hit1, before, 0.0), axis=0, keepdims=True)
    r2 = jnp.sum(jnp.where(hit2, before, 0.0), axis=0, keepdims=True)
    run_ref[...] += jnp.sum(onehot.astype(F32), axis=1, keepdims=True)
    cnt_ref[...] = run_ref[...]
    out_t = jnp.concatenate([eid1, eid2, g1, g2, r1, r2, jnp.zeros((2, tm), F32)], axis=0)
    ot_ref[...] = out_t
    o_ref[...] = jnp.concatenate([out_t, jnp.zeros((128 - 8, tm), F32)], axis=0).T


def _router(h8, w_group, b_group, w_router, b_router):
    t = h8.shape[0] // 8
    w = jnp.zeros((128, D_MODEL), F32)
    w = w.at[0:N_GROUPS].set(w_group.astype(F32).T).at[EPG:EPG + N_EXPERTS].set(w_router.astype(F32).T)
    wh = w.astype(BF16)
    wl = (w - wh.astype(F32)).astype(BF16)
    bias = jnp.zeros((128,), F32).at[0:N_GROUPS].set(b_group.astype(F32)).at[EPG:EPG + N_EXPERTS].set(
        b_router.astype(F32))
    bias = jnp.broadcast_to(bias[:, None], (128, 128))
    tm = ROW_TILE
    out, out_t, cnt = pl.pallas_call(
        _router_kernel,
        grid=(t // tm,),
        in_specs=[pl.BlockSpec((tm * 8, 128), lambda i: (i, 0)),
                  pl.BlockSpec((128, D_MODEL), lambda i: (0, 0)),
                  pl.BlockSpec((128, D_MODEL), lambda i: (0, 0)),
                  pl.BlockSpec((128, 128), lambda i: (0, 0))],
        out_specs=[pl.BlockSpec((tm, 128), lambda i: (i, 0)),
                   pl.BlockSpec((8, tm), lambda i: (0, i)),
                   pl.BlockSpec((N_EXPERTS, 128), lambda i: (0, 0))],
        out_shape=[jax.ShapeDtypeStruct((t, 128), F32), jax.ShapeDtypeStruct((8, t), F32),
                   jax.ShapeDtypeStruct((N_EXPERTS, 128), F32)],
        scratch_shapes=[pltpu.VMEM((N_EXPERTS, 128), F32)],
        compiler_params=_cparams(1),
        name="moe_router",
    )(h8, wh, wl, bias)
    eid = out_t[0:2].astype(jnp.int32)
    rank = out_t[4:6].astype(jnp.int32)
    counts = cnt[:, 0].astype(jnp.int32)
    return out, eid, rank, counts


def _token_tile(ref, tok):
    return ref.at[pl.ds(pl.multiple_of(tok * 8, 8), 8)]


def _experts_kernel(be_ref, nu_ref, src_hbm, h8_hbm, w1_ref, w3_ref, w2_ref, ys_ref,
                    w13_s, w2_s, xbuf, idx_smem, sem, isem):
    i = pl.program_id(0)
    n_used = nu_ref[0]
    used = i < n_used
    slot = i % N_ROW_BUFS

    blk_per_fetch = IDX_FETCH // MOE_ROWS

    def fetch_indices(blk):
        @pl.when(blk % blk_per_fetch == 0)
        def _():
            off = pl.multiple_of(blk * MOE_ROWS, IDX_FETCH)
            cp = pltpu.make_async_copy(src_hbm.at[pl.ds(off, IDX_FETCH)], idx_smem, isem)
            cp.start()
            cp.wait()

    def start_rows(blk, slot_, r0, n):
        base = (blk % blk_per_fetch) * MOE_ROWS
        toks = [idx_smem[base + r0 + u] for u in range(n)]
        for u in range(n):
            pltpu.make_async_copy(_token_tile(h8_hbm, toks[u]), _token_tile(xbuf.at[slot_], r0 + u),
                                  sem.at[slot_]).start(priority=1)

    def gather_block(blk, slot_):
        fetch_indices(blk)
        def issue(r8, _):
            start_rows(blk, slot_, r8 * 8, 8)
            return 0
        lax.fori_loop(0, MOE_ROWS // 8, issue, 0)

    @pl.when(i == 0)
    def _():
        gather_block(0, 0)
        @pl.when(n_used > 1)
        def _():
            gather_block(1, 1)

    nxt = jnp.minimum(i + 2, n_used - 1)
    slot_n = (i + 2) % N_ROW_BUFS

    @pl.when(used)
    def _():
        fetch_indices(nxt)
        pltpu.make_async_copy(h8_hbm.at[pl.ds(0, MOE_ROWS * 8)], xbuf.at[slot], sem.at[slot]).wait()

    prev = be_ref[jnp.maximum(i - 1, 0)]
    fresh = (i == 0) | (be_ref[i] != prev)

    @pl.when(used & fresh)
    def _():
        w13_s[:, :D_EXPERT] = w1_ref[...].astype(BF16)
        w13_s[:, D_EXPERT:] = w3_ref[...].astype(BF16)
        w2_s[...] = w2_ref[...].astype(BF16)

    @pl.when(used)
    def _():
        n_tiles = 2 * D_EXPERT // MXU_COLS
        n_tiles2 = D_MODEL // MXU_COLS
        rows_per_group = MOE_ROWS // (n_tiles + n_tiles2)
        x = _load_token_tiles(xbuf.at[slot], MOE_ROWS).astype(BF16)
        parts = []
        for c in range(n_tiles):
            start_rows(nxt, slot_n, c * rows_per_group, rows_per_group)
            parts.append(_dot(x, w13_s[:, c * MXU_COLS:(c + 1) * MXU_COLS]))
        h = jnp.concatenate(parts, axis=-1)
        h1 = h[:, :D_EXPERT]
        hdn = ((h1 * jax.nn.sigmoid(h1)) * h[:, D_EXPERT:]).astype(BF16)
        for c in range(n_tiles2):
            start_rows(nxt, slot_n, (n_tiles + c) * rows_per_group, rows_per_group)
            yc = _dot(hdn, w2_s[:, c * MXU_COLS:(c + 1) * MXU_COLS])
            for k in range(MXU_COLS // 128):
                ys_ref[pl.ds(c * (MXU_COLS // 128) + k, MOE_ROWS, stride=8), :] = yc[:, k * 128:(k + 1) * 128]

    def wait_rows(slot_):
        pltpu.make_async_copy(h8_hbm.at[pl.ds(0, MOE_ROWS * 8)], xbuf.at[slot_], sem.at[slot_]).wait()

    @pl.when(i == n_used - 1)
    def _():
        wait_rows(slot_n)
        @pl.when(n_used > 1)
        def _():
            wait_rows((i + 1) % N_ROW_BUFS)

    @pl.when(jnp.logical_not(used))
    def _():
        ys_ref[...] = jnp.zeros_like(ys_ref)


def _experts(h8, src, blk_e, n_used, w1, w3, w2, layer, cap):
    nblk = cap // MOE_ROWS
    return pl.pallas_call(
        _experts_kernel,
        grid_spec=pltpu.PrefetchScalarGridSpec(
            num_scalar_prefetch=2,
            grid=(nblk,),
            in_specs=[pl.BlockSpec(memory_space=pl.ANY),
                      pl.BlockSpec(memory_space=pl.ANY),
                      pl.BlockSpec((None, None, D_MODEL, D_EXPERT), lambda i, be, nu: (layer, be[i], 0, 0)),
                      pl.BlockSpec((None, None, D_MODEL, D_EXPERT), lambda i, be, nu: (layer, be[i], 0, 0)),
                      pl.BlockSpec((None, None, D_EXPERT, D_MODEL), lambda i, be, nu: (layer, be[i], 0, 0))],
            out_specs=pl.BlockSpec((MOE_ROWS * 8, 128), lambda i, be, nu: (i, 0)),
            scratch_shapes=[pltpu.VMEM((D_MODEL, 2 * D_EXPERT), BF16),
                            pltpu.VMEM((D_EXPERT, D_MODEL), BF16),
                            pltpu.VMEM((N_ROW_BUFS, MOE_ROWS * 8, 128), F32),
                            pltpu.SMEM((IDX_FETCH,), jnp.int32),
                            pltpu.SemaphoreType.DMA((N_ROW_BUFS,)),
                            pltpu.SemaphoreType.DMA(())]),
        out_shape=jax.ShapeDtypeStruct((cap * 8, 128), F32),
        compiler_params=_cparams(1),
        name="moe_experts",
    )(blk_e, n_used, src, h8, w1, w3, w2)


def _combine_kernel(dst_hbm, ys_hbm, h_ref, gate_ref, g_ref, b_ref, *rest, with_proj):
    if with_proj:
        w_ref, o_ref, p_ref, idx_smem, y0_ref, y1_ref, sem, isem = rest
    else:
        o_ref, idx_smem, y0_ref, y1_ref, sem, isem = rest
    i = pl.program_id(0)
    n_prog = pl.num_programs(0)
    tm = ROW_TILE
    slot = i % 2
    n_groups = 8
    grp = tm // n_groups

    def fetch_indices(tile):
        cp = pltpu.make_async_copy(dst_hbm.at[pl.ds(tile * 2 * tm, 2 * tm)], idx_smem, isem)
        cp.start()
        cp.wait()

    def start_rows(slot_, t0, n):
        r0 = [idx_smem[t0 + u] for u in range(n)]
        r1 = [idx_smem[tm + t0 + u] for u in range(n)]
        for u in range(n):
            pltpu.make_async_copy(_token_tile(ys_hbm, r0[u]), _token_tile(y0_ref.at[slot_], t0 + u),
                                  sem.at[slot_]).start(priority=0)
            pltpu.make_async_copy(_token_tile(ys_hbm, r1[u]), _token_tile(y1_ref.at[slot_], t0 + u),
                                  sem.at[slot_]).start(priority=1)

    def wait_tile(slot_):
        pltpu.make_async_copy(ys_hbm.at[pl.ds(0, tm * 8)], y0_ref.at[slot_], sem.at[slot_]).wait()
        pltpu.make_async_copy(ys_hbm.at[pl.ds(0, tm * 8)], y1_ref.at[slot_], sem.at[slot_]).wait()

    @pl.when(i == 0)
    def _():
        fetch_indices(0)
        def issue(t8, _):
            start_rows(0, t8 * 8, 8)
            return 0
        lax.fori_loop(0, tm // 8, issue, 0)

    fetch_indices(jnp.minimum(i + 1, n_prog - 1))
    if not with_proj:
        def issue_next(t8, _):
            start_rows(1 - slot, t8 * 8, 8)
            return 0
        lax.fori_loop(0, tm // 8, issue_next, 0)
    wait_tile(slot)
    if with_proj:
        start_rows(1 - slot, 0, grp)
    gate = gate_ref[...]
    y = (_load_token_tiles(y0_ref.at[slot], tm) * gate[:, 2:3]
         + _load_token_tiles(y1_ref.at[slot], tm) * gate[:, 3:4])
    out = _layer_norm(ALPHA * _load_token_tiles(h_ref, tm) + y, g_ref[...], b_ref[...])
    o_ref[...] = out
    if with_proj:
        start_rows(1 - slot, grp, grp)
        n_tiles = w_ref.shape[1] // MXU_COLS
        out_b = out.astype(BF16)
        for c in range(n_tiles):
            g = 2 + c * (n_groups - 2) // n_tiles
            g_end = 2 + (c + 1) * (n_groups - 2) // n_tiles
            for gg in range(g, g_end):
                start_rows(1 - slot, gg * grp, grp)
            p_ref[:, c * MXU_COLS:(c + 1) * MXU_COLS] = _dot(
                out_b, w_ref[:, c * MXU_COLS:(c + 1) * MXU_COLS]).astype(p_ref.dtype)

    @pl.when(i == n_prog - 1)
    def _():
        wait_tile(1 - slot)


def _combine_ln(dst, ys, h8, rout, ln_g, ln_b, w_next=None):
    t = h8.shape[0] // 8
    tm = ROW_TILE
    with_proj = w_next is not None
    extra_in = [pl.BlockSpec(w_next.shape, lambda i: (0, 0))] if with_proj else []
    extra_out = [pl.BlockSpec((tm, w_next.shape[1]), lambda i: (i, 0))] if with_proj else []
    extra_shape = [jax.ShapeDtypeStruct((t, w_next.shape[1]), BF16)] if with_proj else []
    res = pl.pallas_call(
        functools.partial(_combine_kernel, with_proj=with_proj),
        grid_spec=pltpu.PrefetchScalarGridSpec(
            num_scalar_prefetch=0,
            grid=(t // tm,),
            in_specs=[pl.BlockSpec(memory_space=pl.ANY),
                      pl.BlockSpec(memory_space=pl.ANY),
                      pl.BlockSpec((tm * 8, 128), lambda i: (i, 0)),
                      pl.BlockSpec((tm, 128), lambda i: (i, 0)),
                      pl.BlockSpec((1, D_MODEL), lambda i: (0, 0)),
                      pl.BlockSpec((1, D_MODEL), lambda i: (0, 0))] + extra_in,
            out_specs=[pl.BlockSpec((tm, D_MODEL), lambda i: (i, 0))] + extra_out,
            scratch_shapes=[pltpu.SMEM((2 * tm,), jnp.int32),
                            pltpu.VMEM((2, tm * 8, 128), F32),
                            pltpu.VMEM((2, tm * 8, 128), F32),
                            pltpu.SemaphoreType.DMA((2,)),
                            pltpu.SemaphoreType.DMA(())]),
        out_shape=[jax.ShapeDtypeStruct((t, D_MODEL), F32)] + extra_shape,
        compiler_params=_cparams(1),
        name="moe_combine_ln",
    )(dst, ys, h8, rout, ln_g, ln_b, *([w_next] if with_proj else []))
    return tuple(res) if with_proj else res[0]


def _moe_layer(h8, w_group, b_group, w_router, b_router, w1, w3, w2, layer, ln_g, ln_b, w_next=None):
    t = h8.shape[0] // 8
    rout, eid, rank, counts = _router(h8, w_group, b_group, w_router, b_router)
    padded = (counts + MOE_ROWS - 1) // MOE_ROWS * MOE_ROWS
    pad_end = jnp.cumsum(padded)
    pad_start = pad_end - padded
    start_of = jnp.sum(jnp.where(eid[..., None] == jnp.arange(N_EXPERTS), pad_start, 0), axis=-1)
    dst = (start_of + rank).astype(jnp.int32)
    cap = 2 * t + N_EXPERTS * MOE_ROWS
    nblk = cap // MOE_ROWS
    tok = jnp.broadcast_to(jnp.arange(t, dtype=jnp.int32), (2, t))
    blk_e = jnp.minimum(jnp.sum(pad_end[None, :] <= (jnp.arange(nblk) * MOE_ROWS)[:, None], axis=1),
                        N_EXPERTS - 1).astype(jnp.int32)
    n_used = (pad_end[-1] // MOE_ROWS).astype(jnp.int32).reshape(1)
    _, by_row = lax.sort((dst.reshape(-1), tok.reshape(-1)), num_keys=1)
    cstart = jnp.cumsum(counts) - counts
    blk_first = cstart[blk_e] + (jnp.arange(nblk) * MOE_ROWS - pad_start[blk_e])
    blk_last = cstart[blk_e] + jnp.maximum(counts[blk_e] - 1, 0)
    idx = jnp.minimum(blk_first[:, None] + jnp.arange(MOE_ROWS)[None, :], blk_last[:, None])
    src = by_row[jnp.clip(idx, 0, 2 * t - 1)].reshape(-1)
    dst_tiles = jnp.transpose(dst.reshape(2, t // ROW_TILE, ROW_TILE), (1, 0, 2)).reshape(-1)
    ys = _experts(h8, src, blk_e, n_used, w1, w3, w2, layer, cap)
    return _combine_ln(dst_tiles, ys, h8, rout, ln_g, ln_b, w_next)


def kernel(x, ln_mix_g, ln_mix_b, ln_ffn_g, ln_ffn_b, ab_w_in, s5_lam_re, s5_lam_im, s5_log_dt, s5_b_re, s5_b_im,
           s5_c_re, s5_c_im, s5_d, s5_glu_w, s5_glu_b, gla_gate_w, gla_gate_b, gla_norm_g, ab_w_out, c_w_in,
           c_sink, c_w_out, rel_bias, moe_w_group, moe_b_group, moe_w_router, moe_b_router, moe_w1, moe_w3, moe_w2):
    n_batch, seq, _ = x.shape
    t = n_batch * seq
    xt = x.reshape(t, D_MODEL)
    row = lambda v: v.astype(F32).reshape(1, -1)

    w_in0 = jnp.pad(ab_w_in[0], ((0, 0), (0, AB_IN_PAD - AB_IN))).astype(BF16)
    h0, u_t = _inproj0(xt, w_in0)
    n_steps = seq // (S5_SEGS * S5_CHUNK)
    kcat, r1e, r2, coef, ptab = _s5_prep(s5_lam_re[0], s5_lam_im[0], s5_log_dt[0], s5_b_re[0], s5_b_im[0],
                                         s5_c_re[0], s5_c_im[0], s5_d[0], n_steps)
    ya = _s5_mixer(u_t, kcat, r1e, r2, coef, ptab, n_batch, n_steps)
    gw = gla_gate_w[0].astype(F32)
    wg = jnp.zeros((128, 2 * GLA_KEY), F32)
    wg = wg.at[0:GLA_RANK, 0:GLA_KEY].set(gw[0]).at[GLA_RANK:2 * GLA_RANK, GLA_KEY:].set(gw[1]).astype(BF16)
    bg = gla_gate_b[0].astype(F32).reshape(1, 2 * GLA_KEY)
    o_f, o_b = _gla_mixer(h0, wg, bg, n_batch, seq)
    h8 = _mix0_out(ya, o_f, o_b, h0, xt, s5_glu_w[0].astype(BF16), row(s5_glu_b[0]), row(gla_norm_g[0]),
                      ab_w_out[0].astype(BF16), row(ln_mix_g[0]), row(ln_mix_b[0]))
    h, h3 = _moe_layer(h8, moe_w_group[0], moe_b_group[0], moe_w_router[0], moe_b_router[0],
                       moe_w1, moe_w3, moe_w2, 0, row(ln_ffn_g[0]), row(ln_ffn_b[0]), c_w_in[0].astype(BF16))

    o = _attn_mixer(h3, rel_bias, c_sink[0], n_batch, seq)
    h8 = _proj_ln(o, c_w_out[0].astype(BF16), h, row(ln_mix_g[1]), row(ln_mix_b[1]))
    h = _moe_layer(h8, moe_w_group[1], moe_b_group[1], moe_w_router[1], moe_b_router[1],
                   moe_w1, moe_w3, moe_w2, 1, row(ln_ffn_g[1]), row(ln_ffn_b[1]))
    return h.reshape(n_batch, seq, D_MODEL)
```

```python
import functools
import math

import jax
import jax.numpy as jnp
from jax import lax
from jax.experimental import pallas as pl
from jax.experimental.pallas import tpu as pltpu

F32 = jnp.float32
BF16 = jnp.bfloat16

D_MODEL = 1024
S5_WIDTH = 512
S5_GROUP_CH = 16
S5_GROUPS = 32
S5_STATE = 64
GLA_HEADS = 4
GLA_DV = 128
GLA_DK = 64
GLA_KEY = 256
GLA_WIDTH = 512
GLA_RANK = 16
GLA_TAU = 16.0
GLA_CHUNK = 64
AB_IN = 2080
AB_IN_PAD = 2176
HEAD_DIM = 64
N_HEADS = 16
N_KV = 4
GQA = 4
ATT_DIM = 1024
KV_DIM = 256
WINDOW = 128
ATT_BLOCK = 128
ATT_SCALE = HEAD_DIM ** -0.5
REL_BUCKETS = 32
REL_MAX_DIST = 128
NEG_INF = -1e30
N_GROUPS = 4
EPG = 8
N_EXPERTS = 32
D_EXPERT = 512
LN_EPS = 1e-5
RMS_EPS = 1e-6
DEPTH = 2
ALPHA = (2 * DEPTH) ** 0.25

S5_CHUNK = 16
S5_SEGS = 8
S5_SEG_PAD = 4
ROW_TILE = 512
GLA_ROWS = 256
ATT_ROWS = 512
MOE_ROWS = 256
IDX_FETCH = 1024
N_ROW_BUFS = 3
MXU_COLS = 256
VMEM_LIMIT = 56 * 1024 * 1024


def _cparams(n_axes):
    return pltpu.CompilerParams(dimension_semantics=("arbitrary",) * n_axes,
                                vmem_limit_bytes=VMEM_LIMIT)


def _dot(a, b):
    return jnp.dot(a, b, preferred_element_type=F32)


def _dot_nt(a, b):
    return lax.dot_general(a, b, (((1,), (1,)), ((), ())), preferred_element_type=F32)


def _dot_tn(a, b):
    return lax.dot_general(a, b, (((0,), (0,)), ((), ())), preferred_element_type=F32)


def _layer_norm(r, g, b):
    mu = jnp.mean(r, axis=-1, keepdims=True)
    c = r - mu
    var = jnp.mean(c * c, axis=-1, keepdims=True)
    return c * lax.rsqrt(var + LN_EPS) * g + b


def _store_token_tiles(ref, val):
    n = val.shape[0]
    for k in range(D_MODEL // 128):
        ref[pl.ds(k, n, stride=8), :] = val[:, k * 128:(k + 1) * 128]


def _load_token_tiles(ref, n):
    return jnp.concatenate([ref[pl.ds(k, n, stride=8), :] for k in range(D_MODEL // 128)], axis=-1)


def _s5_prep(lam_re, lam_im, log_dt, b_re, b_im, c_re, c_im, d, n_steps):
    tc = S5_CHUNK
    g, p, c = S5_GROUPS, S5_STATE, S5_GROUP_CH
    lr = jnp.minimum(lam_re.astype(F32), -1e-4)
    li = lam_im.astype(F32)
    dt = jnp.exp(log_dt.astype(F32))[..., None]
    mag = jnp.exp(lr * dt)
    ar = mag * jnp.cos(li * dt)
    ai = mag * jnp.sin(li * dt)
    den = lr * lr + li * li
    nr = ar - 1.0
    coef_r = (nr * lr + ai * li) / den
    coef_i = (ai * lr - nr * li) / den
    br_ = b_re.astype(F32)
    bi_ = b_im.astype(F32)
    bbr = coef_r[..., None] * br_ - coef_i[..., None] * bi_
    bbi = coef_r[..., None] * bi_ + coef_i[..., None] * br_
    cr = c_re.astype(F32)
    ci = c_im.astype(F32)
    bbr_t = jnp.swapaxes(bbr, -1, -2)
    bbi_t = jnp.swapaxes(bbi, -1, -2)
    npair = g // 2

    def apow(n):
        nn = n.astype(F32)[:, None, None, None]
        m_ = jnp.exp(nn * (lr * dt)[None])
        ang = nn * (li * dt)[None]
        return m_ * jnp.cos(ang), m_ * jnp.sin(ang)

    pr, pi = apow(jnp.arange(tc + 1))
    hr = pr[:, :, :, None, :] * bbr_t[None] - pi[:, :, :, None, :] * bbi_t[None]
    hi = pr[:, :, :, None, :] * bbi_t[None] + pi[:, :, :, None, :] * bbr_t[None]
    kk = jnp.einsum('dgoq,jdgcq->jdgco', cr, hr) - jnp.einsum('dgoq,jdgcq->jdgco', ci, hi)
    k0 = kk[0, 0] + kk[0, 1] + d.astype(F32).reshape(g, c)[:, :, None] * jnp.eye(c, dtype=F32)
    slab = jnp.concatenate([kk[tc - 1:0:-1, 1], k0[None], kk[1:tc, 0]], axis=0)
    kcat = jnp.transpose(slab, (1, 2, 0, 3)).reshape(g, c, (2 * tc - 1) * c)
    kcat = jnp.pad(kcat, ((0, 0), (0, 0), (0, 2 * tc * c - kcat.shape[-1]))).reshape(npair, 2, c, 2 * tc * c)

    def pair_blockdiag(m):
        m2 = m.reshape((npair, 2) + m.shape[1:])
        z = jnp.zeros_like(m2[:, 0])
        return jnp.concatenate([jnp.concatenate([m2[:, 0], z], axis=-1),
                                jnp.concatenate([z, m2[:, 1]], axis=-1)], axis=1)

    def w_plane(h_, lag_sel, dirn):
        w = jnp.transpose(h_[lag_sel, dirn], (1, 0, 2, 3)).reshape(g, tc * c, p)
        return pair_blockdiag(w)
    s_fwd = tc - 1 - jnp.arange(tc)
    s_bwd = jnp.arange(tc)
    r1e = jnp.concatenate([w_plane(hr, s_fwd, 0), w_plane(hi, s_fwd, 0), w_plane(hr, s_bwd, 1), w_plane(hi, s_bwd, 1)],
                          axis=2)

    lane = jnp.arange(tc * c)
    rep_t = (lane[None, :] // c == jnp.arange(tc)[:, None]).astype(F32)
    tile_co = (lane[None, :] % c == jnp.arange(c)[:, None]).astype(F32)
    spread = lambda a, sel: jnp.dot(a, sel, precision=lax.Precision.HIGHEST)
    def readout(e, dirn):
        pr_l = spread(jnp.transpose(pr[e, dirn], (1, 2, 0)).reshape(g * p, tc), rep_t)
        pi_l = spread(jnp.transpose(pi[e, dirn], (1, 2, 0)).reshape(g * p, tc), rep_t)
        cr_l = spread(jnp.transpose(cr[dirn], (0, 2, 1)).reshape(g * p, c), tile_co)
        ci_l = spread(jnp.transpose(ci[dirn], (0, 2, 1)).reshape(g * p, c), tile_co)
        zr = (cr_l * pr_l - ci_l * pi_l).reshape(g, p, tc * c)
        zi = (cr_l * pi_l + ci_l * pr_l).reshape(g, p, tc * c)
        return pair_blockdiag(zr), pair_blockdiag(-zi)
    vf_r, vf_i = readout(jnp.arange(tc) + 1, 0)
    vb_r, vb_i = readout(tc - jnp.arange(tc), 1)
    r2 = jnp.concatenate([vf_r, vf_i, vb_r, vb_i], axis=1)

    steps = jnp.arange(n_steps)
    a16r, a16i = apow(jnp.array([tc]))
    apr, api = apow(steps * tc)
    anr, ani = apow(jnp.array([tc * n_steps]))
    def lanes(x):
        n_ = x.shape[0]
        return jnp.transpose(x.reshape(n_, npair, 2 * p), (1, 0, 2))
    coef = jnp.concatenate([lanes(a16r[:, 0]), lanes(a16i[:, 0]), lanes(a16r[:, 1]), lanes(a16i[:, 1]),
                            lanes(anr[:, 0]), lanes(ani[:, 0]), lanes(anr[:, 1]), lanes(ani[:, 1])], axis=1)
    rev = n_steps - 1 - steps
    ptab = jnp.stack([lanes(apr[:, 0]), lanes(api[:, 0]), lanes(apr[rev, 1]), lanes(api[rev, 1])], axis=1)
    return kcat, r1e.astype(BF16), r2.astype(BF16), coef, ptab


def _gelu_tanh(x):
    return 0.5 * x * (1.0 + jnp.tanh(math.sqrt(2.0 / math.pi) * (x + 0.044715 * (x * x * x))))


def _s5_kernel(u_ref, kcat_ref, r1e_ref, r2_ref, coef_ref, ptab_ref, y_ref, kt_ref, yi_ref, e_ref, s_ref,
               *, n_batch, n_steps):
    pitch = n_steps + S5_SEG_PAD
    blk = S5_CHUNK * S5_GROUP_CH
    kt_ref[...] = jnp.zeros_like(kt_ref)
    for gi in range(2):
        slab = kcat_ref[gi]
        for s in range(S5_CHUNK):
            off = (S5_CHUNK - 1 - s) * S5_GROUP_CH
            kt_ref[gi * blk + s * S5_GROUP_CH:gi * blk + (s + 1) * S5_GROUP_CH, gi * blk:(gi + 1) * blk] = (
                slab[:, off:off + blk].astype(BF16))
    u = u_ref[...]
    yi_ref[...] = _dot(u, kt_ref[...])
    e = _dot(u, r1e_ref[...])
    n_seg = n_batch * S5_SEGS
    for k in range(4):
        for sg in range(n_seg):
            e_ref[k, sg * pitch:sg * pitch + n_steps, :] = e[sg * n_steps:(sg + 1) * n_steps, k * 128:(k + 1) * 128]
    coef = coef_ref[...]
    a16fr, a16fi, a16br, a16bi = coef[0:1], coef[1:2], coef[2:3], coef[3:4]
    anfr, anfi, anbr, anbi = coef[4:5], coef[5:6], coef[6:7], coef[7:8]
    zero = jnp.zeros((S5_SEGS, 128), F32)
    sub = lax.broadcasted_iota(jnp.int32, (S5_SEGS, 128), 0)

    def seg_rows(b, j):
        return pl.ds(b * S5_SEGS * pitch + j, S5_SEGS, stride=pitch)

    def local_step(j, carry):
        out = []
        for b in range(n_batch):
            sfr, sfi, sbr, sbi = carry[b]
            rf = seg_rows(b, j)
            rb = seg_rows(b, n_steps - 1 - j)
            s_ref[0, rf, :] = sfr
            s_ref[1, rf, :] = sfi
            s_ref[2, rb, :] = sbr
            s_ref[3, rb, :] = sbi
            efr = e_ref[0, rf, :]
            efi = e_ref[1, rf, :]
            ebr = e_ref[2, rb, :]
            ebi = e_ref[3, rb, :]
            out.append((a16fr * sfr - a16fi * sfi + efr, a16fr * sfi + a16fi * sfr + efi,
                        a16br * sbr - a16bi * sbi + ebr, a16br * sbi + a16bi * sbr + ebi))
        return tuple(out)

    ends = lax.fori_loop(0, n_steps, local_step, tuple((zero, zero, zero, zero) for _ in range(n_batch)))

    carries = []
    for b in range(n_batch):
        efr, efi, ebr, ebi = ends[b]
        cfr, cfi, cbr, cbi = zero, zero, zero, zero
        for _ in range(S5_SEGS - 1):
            tfr = anfr * cfr - anfi * cfi + efr
            tfi = anfr * cfi + anfi * cfr + efi
            cfr = jnp.where(sub == 0, 0.0, pltpu.roll(tfr, 1, 0))
            cfi = jnp.where(sub == 0, 0.0, pltpu.roll(tfi, 1, 0))
            tbr = anbr * cbr - anbi * cbi + ebr
            tbi = anbr * cbi + anbi * cbr + ebi
            cbr = jnp.where(sub == S5_SEGS - 1, 0.0, pltpu.roll(tbr, S5_SEGS - 1, 0))
            cbi = jnp.where(sub == S5_SEGS - 1, 0.0, pltpu.roll(tbi, S5_SEGS - 1, 0))
        carries.append((cfr, cfi, cbr, cbi))

    def fix_step(j, _):
        pfr = ptab_ref[0, pl.ds(j, 1), :]
        pfi = ptab_ref[1, pl.ds(j, 1), :]
        pbr = ptab_ref[2, pl.ds(j, 1), :]
        pbi = ptab_ref[3, pl.ds(j, 1), :]
        for b in range(n_batch):
            cfr, cfi, cbr, cbi = carries[b]
            rj = seg_rows(b, j)
            s_ref[0, rj, :] = s_ref[0, rj, :] + (pfr * cfr - pfi * cfi)
            s_ref[1, rj, :] = s_ref[1, rj, :] + (pfr * cfi + pfi * cfr)
            s_ref[2, rj, :] = s_ref[2, rj, :] + (pbr * cbr - pbi * cbi)
            s_ref[3, rj, :] = s_ref[3, rj, :] + (pbr * cbi + pbi * cbr)
        return 0

    lax.fori_loop(0, n_steps, fix_step, 0)

    s_all = jnp.concatenate(
        [jnp.concatenate([s_ref[k, sg * pitch:sg * pitch + n_steps, :] for sg in range(n_seg)], axis=0)
         for k in range(4)], axis=-1)
    y = yi_ref[...] + _dot(s_all.astype(BF16), r2_ref[...])
    y_ref[...] = _gelu_tanh(y).astype(y_ref.dtype)


def _s5_mixer(u_t, kcat, r1e, r2, coef, ptab, n_batch, n_steps):
    npair, m, _ = u_t.shape
    scan_rows = n_batch * S5_SEGS * (n_steps + S5_SEG_PAD)
    kern = functools.partial(_s5_kernel, n_batch=n_batch, n_steps=n_steps)
    return pl.pallas_call(
        kern,
        grid=(npair,),
        in_specs=[pl.BlockSpec((None, m, 512), lambda i: (i, 0, 0)),
                  pl.BlockSpec((None, 2, S5_GROUP_CH, 512), lambda i: (i, 0, 0, 0)),
                  pl.BlockSpec((None, 512, 512), lambda i: (i, 0, 0)),
                  pl.BlockSpec((None, 512, 512), lambda i: (i, 0, 0)),
                  pl.BlockSpec((None, 8, 128), lambda i: (i, 0, 0)),
                  pl.BlockSpec((None, 4, n_steps, 128), lambda i: (i, 0, 0, 0))],
        out_specs=pl.BlockSpec((None, m, 512), lambda i: (i, 0, 0)),
        out_shape=jax.ShapeDtypeStruct((npair, m, 512), BF16),
        scratch_shapes=[pltpu.VMEM((512, 512), BF16), pltpu.VMEM((m, 512), F32),
                        pltpu.VMEM((4, scan_rows, 128), F32), pltpu.VMEM((4, scan_rows, 128), F32)],
        compiler_params=_cparams(1),
        name="s5_scan",
    )(u_t, kcat, r1e, r2, coef, ptab)


def _inproj0_kernel(x_ref, w_ref, h_ref, ut_ref, u_s):
    h = _dot(x_ref[...].astype(BF16), w_ref[...])
    h_ref[...] = h[:, S5_WIDTH:].astype(h_ref.dtype)
    n_lane_blk = S5_WIDTH // 128
    for k in range(n_lane_blk):
        u_s[k] = h[:, k * 128:(k + 1) * 128]
    n_chunk = ROW_TILE // S5_CHUNK
    rows = [jnp.concatenate([u_s[k, pl.ds(s, n_chunk, stride=S5_CHUNK), :] for k in range(n_lane_blk)], axis=-1)
            for s in range(S5_CHUNK)]
    ch = S5_GROUP_CH
    for p in range(S5_GROUPS // 2):
        ut_ref[p] = jnp.concatenate([r[:, (2 * p + gi) * ch:(2 * p + gi + 1) * ch] for gi in range(2) for r in rows],
                                    axis=-1).astype(ut_ref.dtype)


def _inproj0(x, w):
    t, k = x.shape
    n = w.shape[1]
    tm = ROW_TILE
    npair = S5_GROUPS // 2
    return pl.pallas_call(
        _inproj0_kernel,
        grid=(t // tm,),
        in_specs=[pl.BlockSpec((tm, k), lambda i: (i, 0)),
                  pl.BlockSpec((k, n), lambda i: (0, 0))],
        out_specs=[pl.BlockSpec((tm, n - S5_WIDTH), lambda i: (i, 0)),
                   pl.BlockSpec((npair, tm // S5_CHUNK, 512), lambda i: (0, i, 0))],
        out_shape=[jax.ShapeDtypeStruct((t, n - S5_WIDTH), BF16),
                   jax.ShapeDtypeStruct((npair, t // S5_CHUNK, 512), BF16)],
        scratch_shapes=[pltpu.VMEM((S5_WIDTH // 128, tm, 128), F32)],
        compiler_params=_cparams(1),
        name="inproj0",
    )(x, w)


def _gla_kernel(qf_ref, kf_ref, vf_ref, lf_ref, qb_ref, kb_ref, vb_ref, lb_ref, wg_ref, bg_ref,
                of_ref, ob_ref, st_ref):
    @pl.when(pl.program_id(1) == 0)
    def _():
        st_ref[...] = jnp.zeros_like(st_ref)

    n_chunks = GLA_ROWS // GLA_CHUNK
    row = lax.broadcasted_iota(jnp.int32, (GLA_CHUNK, GLA_CHUNK), 0)
    col = lax.broadcasted_iota(jnp.int32, (GLA_CHUNK, GLA_CHUNK), 1)
    mask_f = col <= row
    mask_b = col > row
    brow = lax.broadcasted_iota(jnp.int32, (GLA_ROWS, GLA_ROWS), 0)
    bcol = lax.broadcasted_iota(jnp.int32, (GLA_ROWS, GLA_ROWS), 1)
    shift = GLA_CHUNK.bit_length() - 1
    same = (brow >> shift) == (bcol >> shift)
    ones_blk = same.astype(BF16)
    tri_f = (same & (bcol <= brow)).astype(BF16)
    tri_b = (same & (bcol >= brow)).astype(BF16)
    wg = wg_ref[...]
    bg = bg_ref[...]

    def direction(q_ref, k_ref, v_ref, l_ref, o_ref, d):
        z = _dot(l_ref[...], wg[:, d * GLA_KEY:(d + 1) * GLA_KEY]) + bg[:, d * GLA_KEY:(d + 1) * GLA_KEY]
        log_a = jax.nn.log_sigmoid(z) * (1.0 / GLA_TAU)
        la_hi = log_a.astype(BF16)
        la_lo = (log_a - la_hi.astype(F32)).astype(BF16)
        tri = tri_f if d == 0 else tri_b
        bc = _dot(tri, la_hi) + _dot(tri, la_lo)
        tot = _dot(ones_blk, la_hi) + _dot(ones_blk, la_lo)
        q = q_ref[...].astype(F32) * (GLA_DK ** -0.5)
        k = k_ref[...].astype(F32)
        qd_all = (q * jnp.exp(bc)).astype(BF16)
        kd_all = (k * jnp.exp(-bc)).astype(BF16)
        kc_all = (k * jnp.exp(tot - bc)).astype(BF16)
        decay_all = jnp.exp(tot)
        order = range(n_chunks) if d == 0 else range(n_chunks - 1, -1, -1)
        mask = mask_f if d == 0 else mask_b
        states = [st_ref[d, h] for h in range(GLA_HEADS)]
        for c in order:
            sl = slice(c * GLA_CHUNK, (c + 1) * GLA_CHUNK)
            qd, kd, kc = qd_all[sl], kd_all[sl], kc_all[sl]
            decay = decay_all[c * GLA_CHUNK:c * GLA_CHUNK + 1]
            v = v_ref[sl, :]
            outs = []
            for h in range(GLA_HEADS):
                ks = slice(h * GLA_DK, (h + 1) * GLA_DK)
                vs = slice(h * GLA_DV, (h + 1) * GLA_DV)
                s = jnp.where(mask, _dot_nt(qd[:, ks], kd[:, ks]), 0.0).astype(BF16)
                st = states[h]
                o = _dot(s, v[:, vs]) + _dot_nt(qd[:, ks], st.astype(BF16))
                states[h] = st * decay[:, ks] + _dot_tn(v[:, vs], kc[:, ks])
                outs.append(o)
            o_ref[sl, :] = jnp.concatenate(outs, axis=-1).astype(o_ref.dtype)
        for h in range(GLA_HEADS):
            st_ref[d, h] = states[h]

    direction(qf_ref, kf_ref, vf_ref, lf_ref, of_ref, 0)
    direction(qb_ref, kb_ref, vb_ref, lb_ref, ob_ref, 1)


def _gla_mixer(h0, wg, bg, n_batch, seq):
    nb = seq // GLA_ROWS
    r = GLA_ROWS
    fwd = lambda b, i: b * nb + i
    bwd = lambda b, i: b * nb + (nb - 1 - i)
    def spec(width, colblk, rowfn):
        return pl.BlockSpec((r, width), lambda b, i: (rowfn(b, i), colblk))
    in_specs = [spec(256, 0, fwd), spec(256, 1, fwd), spec(512, 1, fwd), spec(128, 12, fwd),
                spec(256, 0, bwd), spec(256, 1, bwd), spec(512, 1, bwd), spec(128, 12, bwd),
                pl.BlockSpec((128, 512), lambda b, i: (0, 0)),
                pl.BlockSpec((1, 512), lambda b, i: (0, 0))]
    out_specs = [pl.BlockSpec((r, 512), lambda b, i: (fwd(b, i), 0)),
                 pl.BlockSpec((r, 512), lambda b, i: (bwd(b, i), 0))]
    t = n_batch * seq
    return pl.pallas_call(
        _gla_kernel,
        grid=(n_batch, nb),
        in_specs=in_specs,
        out_specs=out_specs,
        out_shape=[jax.ShapeDtypeStruct((t, 512), BF16), jax.ShapeDtypeStruct((t, 512), BF16)],
        scratch_shapes=[pltpu.VMEM((2, GLA_HEADS, GLA_DV, GLA_DK), F32)],
        compiler_params=_cparams(2),
        name="gla_chunked",
    )(h0, h0, h0, h0, h0, h0, h0, h0, wg, bg)


def _mix0_out_kernel(y_ref, of_ref, ob_ref, go_ref, x_ref, gw_ref, gb_ref, ng_ref, wo_ref, lg_ref, lb_ref,
                     h8_ref, ya_s):
    n_chunk = ROW_TILE // S5_CHUNK
    ys = [y_ref[p].astype(F32) for p in range(S5_GROUPS // 2)]
    n_lane_blk = S5_WIDTH // 128
    ch = S5_GROUP_CH
    blk = S5_CHUNK * ch
    for t in range(S5_CHUNK):
        for k in range(n_lane_blk):
            ya_s[k, pl.ds(t, n_chunk, stride=S5_CHUNK), :] = jnp.concatenate(
                [y[:, gi * blk + t * ch:gi * blk + (t + 1) * ch] for y in ys[4 * k:4 * k + 4] for gi in range(2)],
                axis=-1)
    yaf = jnp.concatenate([ya_s[k] for k in range(n_lane_blk)], axis=-1)
    ya = yaf.astype(BF16)
    gate = _dot(ya, gw_ref[...]) + gb_ref[...]
    ya2 = yaf * jax.nn.sigmoid(gate)
    o = of_ref[...].astype(F32) + ob_ref[...].astype(F32)
    ng = ng_ref[...]
    parts = []
    for h in range(GLA_HEADS):
        oh = o[:, h * GLA_DV:(h + 1) * GLA_DV]
        ms = jnp.mean(oh * oh, axis=-1, keepdims=True)
        parts.append(oh * lax.rsqrt(ms + RMS_EPS) * ng[:, h * GLA_DV:(h + 1) * GLA_DV])
    yb = jnp.concatenate(parts, axis=-1) * jax.nn.silu(go_ref[...].astype(F32))
    wo = wo_ref[...]
    y = _dot(ya2.astype(BF16), wo[:S5_WIDTH]) + _dot(yb.astype(BF16), wo[S5_WIDTH:])
    _store_token_tiles(h8_ref, _layer_norm(ALPHA * x_ref[...] + y, lg_ref[...], lb_ref[...]))


def _mix0_out(ya, o_f, o_b, h0, x, glu_w, glu_b, norm_g, w_out, ln_g, ln_b):
    t = x.shape[0]
    tm = ROW_TILE
    row = lambda w: pl.BlockSpec((tm, w), lambda i: (i, 0))
    full = lambda a, b: pl.BlockSpec((a, b), lambda i: (0, 0))
    return pl.pallas_call(
        _mix0_out_kernel,
        grid=(t // tm,),
        in_specs=[pl.BlockSpec((S5_GROUPS // 2, tm // S5_CHUNK, 512), lambda i: (0, i, 0)),
                  row(512), row(512),
                  pl.BlockSpec((tm, 512), lambda i: (i, 2)),
                  row(1024), full(512, 512), full(1, 512), full(1, 512), full(1024, 1024),
                  full(1, 1024), full(1, 1024)],
        out_specs=pl.BlockSpec((tm * 8, 128), lambda i: (i, 0)),
        out_shape=jax.ShapeDtypeStruct((t * 8, 128), F32),
        scratch_shapes=[pltpu.VMEM((S5_WIDTH // 128, tm, 128), F32)],
        compiler_params=_cparams(1),
        name="mix0_out_ln",
    )(ya, o_f, o_b, h0, x, glu_w, glu_b, norm_g, w_out, ln_g, ln_b)


def _attn_kernel(q_ref, kc_ref, vc_ref, kp_ref, vp_ref, kn_ref, vn_ref, bias_ref, sink_ref, o_ref, *, seq):
    blk = pl.program_id(1)
    n_blk = pl.num_programs(1)
    n_sub = ATT_ROWS // ATT_BLOCK
    kw = ATT_BLOCK + 2 * WINDOW
    nq = GQA * ATT_BLOCK
    key_row = lax.broadcasted_iota(jnp.int32, (kw, nq), 0)
    pen_first = jnp.where((blk == 0) & (key_row < WINDOW), NEG_INF, 0.0)
    pen_last = jnp.where((blk == n_blk - 1) & (key_row >= ATT_BLOCK + WINDOW), NEG_INF, 0.0)
    kall = jnp.concatenate([kp_ref[...], kc_ref[...], kn_ref[...]], axis=0)
    vall = jnp.concatenate([vp_ref[...], vc_ref[...], vn_ref[...]], axis=0)
    for s in range(n_sub):
        kwin = kall[s * ATT_BLOCK:s * ATT_BLOCK + kw]
        vwin = vall[s * ATT_BLOCK:s * ATT_BLOCK + kw]
        q = q_ref[s * ATT_BLOCK:(s + 1) * ATT_BLOCK, :] * ATT_SCALE
        outs_t = []
        for kv in range(N_KV):
            qs = jnp.concatenate([q[:, (kv * GQA + g) * HEAD_DIM:(kv * GQA + g + 1) * HEAD_DIM]
                                  for g in range(GQA)], axis=0)
            kh = kwin[:, kv * HEAD_DIM:(kv + 1) * HEAD_DIM]
            vh = vwin[:, kv * HEAD_DIM:(kv + 1) * HEAD_DIM]
            st = _dot_nt(kh, qs) + bias_ref[kv]
            if s == 0:
                st = st + pen_first
            if s == n_sub - 1:
                st = st + pen_last
            sink = sink_ref[kv]
            m = jnp.maximum(jnp.max(st, axis=0, keepdims=True), sink)
            p = jnp.exp(st - m)
            den = jnp.sum(p, axis=0, keepdims=True) + jnp.exp(sink - m)
            outs_t.append(_dot_tn(vh, p.astype(BF16)) / den)
        o = jnp.concatenate(outs_t, axis=0).T
        pieces = [o[g * ATT_BLOCK:(g + 1) * ATT_BLOCK, kv * HEAD_DIM:(kv + 1) * HEAD_DIM]
                  for kv in range(N_KV) for g in range(GQA)]
        o_ref[s * ATT_BLOCK:(s + 1) * ATT_BLOCK, :] = jnp.concatenate(pieces, axis=-1).astype(o_ref.dtype)


def _t5_bucket(rel):
    nb = REL_BUCKETS // 2
    max_exact = nb // 2
    ret = (rel > 0).astype(jnp.int32) * nb
    n = jnp.abs(rel)
    large = max_exact + (jnp.log(jnp.maximum(n, 1).astype(F32) / max_exact)
                         / math.log(REL_MAX_DIST / max_exact) * (nb - max_exact)).astype(jnp.int32)
    large = jnp.minimum(large, nb - 1)
    return ret + jnp.where(n < max_exact, n, large)


def _attn_mixer(h3, rel_bias, sink, n_batch, seq):
    kw = ATT_BLOCK + 2 * WINDOW
    n_rel = kw + ATT_BLOCK - 1
    rel = jnp.arange(n_rel) - (n_rel - 1) // 2
    tb = rel_bias.astype(F32)[_t5_bucket(rel)]
    tb = jnp.where((jnp.abs(rel) <= WINDOW)[:, None], tb, NEG_INF)
    tb = jnp.pad(tb.T, ((0, 0), (0, 1)))
    flat = jnp.tile(tb, (1, ATT_BLOCK))[:, ATT_BLOCK - 1:ATT_BLOCK - 1 + ATT_BLOCK * n_rel]
    bias = flat.reshape(N_HEADS, ATT_BLOCK, n_rel)[:, :, :kw]
    bias = jnp.transpose(bias.reshape(N_KV, GQA, ATT_BLOCK, kw), (0, 3, 1, 2)).reshape(N_KV, kw, GQA * ATT_BLOCK)
    sink_rows = jnp.repeat(sink.astype(F32).reshape(N_KV, GQA), ATT_BLOCK, axis=1)[:, None, :]
    nblk = seq // ATT_ROWS
    sub = ATT_ROWS // ATT_BLOCK
    n128 = seq // ATT_BLOCK
    cur = lambda b, i: b * nblk + i
    prev = lambda b, i: b * n128 + jnp.maximum(i * sub - 1, 0)
    nxt = lambda b, i: b * n128 + jnp.minimum((i + 1) * sub, n128 - 1)
    t = n_batch * seq
    kern = functools.partial(_attn_kernel, seq=seq)
    return pl.pallas_call(
        kern,
        grid=(n_batch, nblk),
        in_specs=[pl.BlockSpec((ATT_ROWS, ATT_DIM), lambda b, i: (cur(b, i), 0)),
                  pl.BlockSpec((ATT_ROWS, KV_DIM), lambda b, i: (cur(b, i), 4)),
                  pl.BlockSpec((ATT_ROWS, KV_DIM), lambda b, i: (cur(b, i), 5)),
                  pl.BlockSpec((ATT_BLOCK, KV_DIM), lambda b, i: (prev(b, i), 4)),
                  pl.BlockSpec((ATT_BLOCK, KV_DIM), lambda b, i: (prev(b, i), 5)),
                  pl.BlockSpec((ATT_BLOCK, KV_DIM), lambda b, i: (nxt(b, i), 4)),
                  pl.BlockSpec((ATT_BLOCK, KV_DIM), lambda b, i: (nxt(b, i), 5)),
                  pl.BlockSpec((N_KV, kw, GQA * ATT_BLOCK), lambda b, i: (0, 0, 0)),
                  pl.BlockSpec((N_KV, 1, GQA * ATT_BLOCK), lambda b, i: (0, 0, 0))],
        out_specs=pl.BlockSpec((ATT_ROWS, ATT_DIM), lambda b, i: (cur(b, i), 0)),
        out_shape=jax.ShapeDtypeStruct((t, ATT_DIM), BF16),
        compiler_params=_cparams(2),
        name="window_gqa",
    )(h3, h3, h3, h3, h3, h3, h3, bias, sink_rows)


def _proj_ln_kernel(a_ref, w_ref, x_ref, g_ref, b_ref, h8_ref):
    y = _dot(a_ref[...], w_ref[...])
    _store_token_tiles(h8_ref, _layer_norm(ALPHA * x_ref[...] + y, g_ref[...], b_ref[...]))


def _proj_ln(a, w, x, g, b):
    t, k = a.shape
    tm = ROW_TILE
    return pl.pallas_call(
        _proj_ln_kernel,
        grid=(t // tm,),
        in_specs=[pl.BlockSpec((tm, k), lambda i: (i, 0)),
                  pl.BlockSpec((k, D_MODEL), lambda i: (0, 0)),
                  pl.BlockSpec((tm, D_MODEL), lambda i: (i, 0)),
                  pl.BlockSpec((1, D_MODEL), lambda i: (0, 0)),
                  pl.BlockSpec((1, D_MODEL), lambda i: (0, 0))],
        out_specs=pl.BlockSpec((tm * 8, 128), lambda i: (i, 0)),
        out_shape=jax.ShapeDtypeStruct((t * 8, 128), F32),
        compiler_params=_cparams(1),
        name="proj_ln",
    )(a, w, x, g, b)


def _router_kernel(h_ref, wh_ref, wl_ref, b_ref, o_ref, ot_ref, cnt_ref, run_ref, upper_ref):
    @pl.when(pl.program_id(0) == 0)
    def _():
        run_ref[...] = jnp.zeros_like(run_ref)
        tri = (lax.broadcasted_iota(jnp.int32, upper_ref.shape, 0) < lax.broadcasted_iota(jnp.int32, upper_ref.shape, 1))
        upper_ref[...] = tri.astype(BF16)

    x = _load_token_tiles(h_ref, ROW_TILE)
    xh = x.astype(BF16)
    xl = (x - xh.astype(F32)).astype(BF16)
    wh = wh_ref[...]
    lt = _dot_nt(wh, xh) + (_dot_nt(wh, xl) + _dot_nt(wl_ref[...], xh)) + b_ref[...][:, 0:1]
    tm = lt.shape[1]
    neg = jnp.float32(-jnp.inf)
    row8 = lax.broadcasted_iota(jnp.int32, (EPG, tm), 0).astype(F32)
    is_g = row8 < N_GROUPS
    gl = jnp.where(is_g, lt[0:EPG], neg)
    gmax = jnp.max(gl, axis=0, keepdims=True)
    gsum = jnp.sum(jnp.where(is_g, jnp.exp(gl - gmax), 0.0), axis=0, keepdims=True)
    grp = jnp.min(jnp.where(gl == gmax, row8, float(EPG)), axis=0, keepdims=True)
    p_grp = 1.0 / gsum
    el = lt[EPG:2 * EPG]
    for g in range(1, N_GROUPS):
        el = jnp.where(grp == float(g), lt[EPG * (g + 1):EPG * (g + 2)], el)
    m1 = jnp.max(el, axis=0, keepdims=True)
    i1 = jnp.min(jnp.where(el == m1, row8, float(EPG)), axis=0, keepdims=True)
    el2 = jnp.where(row8 == i1, neg, el)
    m2 = jnp.max(el2, axis=0, keepdims=True)
    i2 = jnp.min(jnp.where((row8 != i1) & (el2 == m2), row8, float(EPG)), axis=0, keepdims=True)
    e2 = jnp.exp(m2 - m1)
    g1 = p_grp / (1.0 + e2)
    g2 = p_grp * e2 / (1.0 + e2)
    eid1 = grp * EPG + i1
    eid2 = grp * EPG + i2
    rowe = lax.broadcasted_iota(jnp.int32, (N_EXPERTS, tm), 0).astype(F32)
    hit1 = rowe == eid1
    hit2 = rowe == eid2
    onehot = (hit1 | hit2).astype(BF16)
    before = _dot(onehot, upper_ref[...]) + run_ref[...][:, 0:1]
    r1 = jnp.sum(jnp.where(hit1, before, 0.0), axis=0, keepdims=True)
    r2 = jnp.sum(jnp.where(hit2, before, 0.0), axis=0, keepdims=True)
    run_ref[...] += jnp.sum(onehot.astype(F32), axis=1, keepdims=True)
    cnt_ref[...] = run_ref[...]
    out_t = jnp.concatenate([eid1, eid2, g1, g2, r1, r2, jnp.zeros((2, tm), F32)], axis=0)
    ot_ref[...] = out_t
    o_ref[...] = jnp.concatenate([out_t, jnp.zeros((128 - 8, tm), F32)], axis=0).T


def _router(h8, w_group, b_group, w_router, b_router):
    t = h8.shape[0] // 8
    w = jnp.zeros((128, D_MODEL), F32)
    w = w.at[0:N_GROUPS].set(w_group.astype(F32).T).at[EPG:EPG + N_EXPERTS].set(w_router.astype(F32).T)
    wh = w.astype(BF16)
    wl = (w - wh.astype(F32)).astype(BF16)
    bias = jnp.zeros((128,), F32).at[0:N_GROUPS].set(b_group.astype(F32)).at[EPG:EPG + N_EXPERTS].set(
        b_router.astype(F32))
    bias = jnp.broadcast_to(bias[:, None], (128, 128))
    tm = ROW_TILE
    out, out_t, cnt = pl.pallas_call(
        _router_kernel,
        grid=(t // tm,),
        in_specs=[pl.BlockSpec((tm * 8, 128), lambda i: (i, 0)),
                  pl.BlockSpec((128, D_MODEL), lambda i: (0, 0)),
                  pl.BlockSpec((128, D_MODEL), lambda i: (0, 0)),
                  pl.BlockSpec((128, 128), lambda i: (0, 0))],
        out_specs=[pl.BlockSpec((tm, 128), lambda i: (i, 0)),
                   pl.BlockSpec((8, tm), lambda i: (0, i)),
                   pl.BlockSpec((N_EXPERTS, 128), lambda i: (0, 0))],
        out_shape=[jax.ShapeDtypeStruct((t, 128), F32), jax.ShapeDtypeStruct((8, t), F32),
                   jax.ShapeDtypeStruct((N_EXPERTS, 128), F32)],
        scratch_shapes=[pltpu.VMEM((N_EXPERTS, 128), F32), pltpu.VMEM((tm, tm), BF16)],
        compiler_params=_cparams(1),
        name="moe_router",
    )(h8, wh, wl, bias)
    eid = out_t[0:2].astype(jnp.int32)
    rank = out_t[4:6].astype(jnp.int32)
    counts = cnt[:, 0].astype(jnp.int32)
    return out, eid, rank, counts


def _token_tile(ref, tok):
    return ref.at[pl.ds(pl.multiple_of(tok * 8, 8), 8)]


def _experts_kernel(be_ref, nu_ref, src_hbm, h8_hbm, w1_ref, w3_ref, w2_ref, ys_ref,
                    w13_s, w2_s, xbuf, idx_smem, sem, isem):
    i = pl.program_id(0)
    n_used = nu_ref[0]
    used = i < n_used
    slot = i % N_ROW_BUFS

    blk_per_fetch = IDX_FETCH // MOE_ROWS

    def fetch_indices(blk):
        @pl.when(blk % blk_per_fetch == 0)
        def _():
            off = pl.multiple_of(blk * MOE_ROWS, IDX_FETCH)
            cp = pltpu.make_async_copy(src_hbm.at[pl.ds(off, IDX_FETCH)], idx_smem, isem)
            cp.start()
            cp.wait()

    def start_rows(blk, slot_, r0, n):
        base = (blk % blk_per_fetch) * MOE_ROWS
        toks = [idx_smem[base + r0 + u] for u in range(n)]
        for u in range(n):
            pltpu.make_async_copy(_token_tile(h8_hbm, toks[u]), _token_tile(xbuf.at[slot_], r0 + u),
                                  sem.at[slot_]).start(priority=u % 2)

    def gather_block(blk, slot_):
        fetch_indices(blk)
        def issue(r8, _):
            start_rows(blk, slot_, r8 * 8, 8)
            return 0
        lax.fori_loop(0, MOE_ROWS // 8, issue, 0)

    @pl.when(i == 0)
    def _():
        gather_block(0, 0)
        @pl.when(n_used > 1)
        def _():
            gather_block(1, 1)

    nxt = jnp.minimum(i + 2, n_used - 1)
    slot_n = (i + 2) % N_ROW_BUFS

    @pl.when(used)
    def _():
        fetch_indices(nxt)
        pltpu.make_async_copy(h8_hbm.at[pl.ds(0, MOE_ROWS * 8)], xbuf.at[slot], sem.at[slot]).wait()

    prev = be_ref[jnp.maximum(i - 1, 0)]
    fresh = (i == 0) | (be_ref[i] != prev)

    @pl.when(used & fresh)
    def _():
        w13_s[:, :D_EXPERT] = w1_ref[...].astype(BF16)
        w13_s[:, D_EXPERT:] = w3_ref[...].astype(BF16)
        w2_s[...] = w2_ref[...].astype(BF16)

    @pl.when(used)
    def _():
        n_tiles = 2 * D_EXPERT // MXU_COLS
        n_tiles2 = D_MODEL // MXU_COLS
        rows_per_group = MOE_ROWS // (n_tiles + n_tiles2)
        x = _load_token_tiles(xbuf.at[slot], MOE_ROWS).astype(BF16)
        parts = []
        for c in range(n_tiles):
            start_rows(nxt, slot_n, c * rows_per_group, rows_per_group)
            parts.append(_dot(x, w13_s[:, c * MXU_COLS:(c + 1) * MXU_COLS]))
        h = jnp.concatenate(parts, axis=-1)
        h1 = h[:, :D_EXPERT]
        hdn = ((h1 * jax.nn.sigmoid(h1)) * h[:, D_EXPERT:]).astype(BF16)
        for c in range(n_tiles2):
            start_rows(nxt, slot_n, (n_tiles + c) * rows_per_group, rows_per_group)
            yc = _dot(hdn, w2_s[:, c * MXU_COLS:(c + 1) * MXU_COLS])
            for k in range(MXU_COLS // 128):
                ys_ref[pl.ds(c * (MXU_COLS // 128) + k, MOE_ROWS, stride=8), :] = yc[:, k * 128:(k + 1) * 128]

    def wait_rows(slot_):
        pltpu.make_async_copy(h8_hbm.at[pl.ds(0, MOE_ROWS * 8)], xbuf.at[slot_], sem.at[slot_]).wait()

    @pl.when(i == n_used - 1)
    def _():
        wait_rows(slot_n)
        @pl.when(n_used > 1)
        def _():
            wait_rows((i + 1) % N_ROW_BUFS)

    @pl.when(jnp.logical_not(used))
    def _():
        ys_ref[...] = jnp.zeros_like(ys_ref)


def _experts(h8, src, blk_e, n_used, w1, w3, w2, layer, cap):
    nblk = cap // MOE_ROWS
    return pl.pallas_call(
        _experts_kernel,
        grid_spec=pltpu.PrefetchScalarGridSpec(
            num_scalar_prefetch=2,
            grid=(nblk,),
            in_specs=[pl.BlockSpec(memory_space=pl.ANY),
                      pl.BlockSpec(memory_space=pl.ANY),
                      pl.BlockSpec((None, None, D_MODEL, D_EXPERT), lambda i, be, nu: (layer, be[i], 0, 0)),
                      pl.BlockSpec((None, None, D_MODEL, D_EXPERT), lambda i, be, nu: (layer, be[i], 0, 0)),
                      pl.BlockSpec((None, None, D_EXPERT, D_MODEL), lambda i, be, nu: (layer, be[i], 0, 0))],
            out_specs=pl.BlockSpec((MOE_ROWS * 8, 128), lambda i, be, nu: (i, 0)),
            scratch_shapes=[pltpu.VMEM((D_MODEL, 2 * D_EXPERT), BF16),
                            pltpu.VMEM((D_EXPERT, D_MODEL), BF16),
                            pltpu.VMEM((N_ROW_BUFS, MOE_ROWS * 8, 128), F32),
                            pltpu.SMEM((IDX_FETCH,), jnp.int32),
                            pltpu.SemaphoreType.DMA((N_ROW_BUFS,)),
                            pltpu.SemaphoreType.DMA(())]),
        out_shape=jax.ShapeDtypeStruct((cap * 8, 128), F32),
        compiler_params=_cparams(1),
        name="moe_experts",
    )(blk_e, n_used, src, h8, w1, w3, w2)


def _combine_kernel(dst_hbm, ys_hbm, h_ref, gate_ref, g_ref, b_ref, *rest, with_proj):
    if with_proj:
        w_ref, o_ref, p_ref, idx_smem, y0_ref, y1_ref, sem, isem = rest
    else:
        o_ref, idx_smem, y0_ref, y1_ref, sem, isem = rest
    i = pl.program_id(0)
    n_prog = pl.num_programs(0)
    tm = ROW_TILE
    slot = i % 2
    n_groups = 8
    grp = tm // n_groups

    def fetch_indices(tile):
        cp = pltpu.make_async_copy(dst_hbm.at[pl.ds(tile * 2 * tm, 2 * tm)], idx_smem, isem)
        cp.start()
        cp.wait()

    def start_rows(slot_, t0, n):
        r0 = [idx_smem[t0 + u] for u in range(n)]
        r1 = [idx_smem[tm + t0 + u] for u in range(n)]
        for u in range(n):
            pltpu.make_async_copy(_token_tile(ys_hbm, r0[u]), _token_tile(y0_ref.at[slot_], t0 + u),
                                  sem.at[slot_]).start(priority=0)
            pltpu.make_async_copy(_token_tile(ys_hbm, r1[u]), _token_tile(y1_ref.at[slot_], t0 + u),
                                  sem.at[slot_]).start(priority=1)

    def wait_tile(slot_):
        pltpu.make_async_copy(ys_hbm.at[pl.ds(0, tm * 8)], y0_ref.at[slot_], sem.at[slot_]).wait()
        pltpu.make_async_copy(ys_hbm.at[pl.ds(0, tm * 8)], y1_ref.at[slot_], sem.at[slot_]).wait()

    @pl.when(i == 0)
    def _():
        fetch_indices(0)
        def issue(t8, _):
            start_rows(0, t8 * 8, 8)
            return 0
        lax.fori_loop(0, tm // 8, issue, 0)

    fetch_indices(jnp.minimum(i + 1, n_prog - 1))
    if not with_proj:
        def issue_next(t8, _):
            start_rows(1 - slot, t8 * 8, 8)
            return 0
        lax.fori_loop(0, tm // 8, issue_next, 0)
    wait_tile(slot)
    if with_proj:
        start_rows(1 - slot, 0, grp)
    gate = gate_ref[...]
    y = (_load_token_tiles(y0_ref.at[slot], tm) * gate[:, 2:3]
         + _load_token_tiles(y1_ref.at[slot], tm) * gate[:, 3:4])
    out = _layer_norm(ALPHA * _load_token_tiles(h_ref, tm) + y, g_ref[...], b_ref[...])
    o_ref[...] = out
    if with_proj:
        start_rows(1 - slot, grp, grp)
        n_tiles = w_ref.shape[1] // MXU_COLS
        out_b = out.astype(BF16)
        for c in range(n_tiles):
            g = 2 + c * (n_groups - 2) // n_tiles
            g_end = 2 + (c + 1) * (n_groups - 2) // n_tiles
            for gg in range(g, g_end):
                start_rows(1 - slot, gg * grp, grp)
            p_ref[:, c * MXU_COLS:(c + 1) * MXU_COLS] = _dot(
                out_b, w_ref[:, c * MXU_COLS:(c + 1) * MXU_COLS]).astype(p_ref.dtype)

    @pl.when(i == n_prog - 1)
    def _():
        wait_tile(1 - slot)


def _combine_ln(dst, ys, h8, rout, ln_g, ln_b, w_next=None):
    t = h8.shape[0] // 8
    tm = ROW_TILE
    with_proj = w_next is not None
    extra_in = [pl.BlockSpec(w_next.shape, lambda i: (0, 0))] if with_proj else []
    extra_out = [pl.BlockSpec((tm, w_next.shape[1]), lambda i: (i, 0))] if with_proj else []
    extra_shape = [jax.ShapeDtypeStruct((t, w_next.shape[1]), BF16)] if with_proj else []
    res = pl.pallas_call(
        functools.partial(_combine_kernel, with_proj=with_proj),
        grid_spec=pltpu.PrefetchScalarGridSpec(
            num_scalar_prefetch=0,
            grid=(t // tm,),
            in_specs=[pl.BlockSpec(memory_space=pl.ANY),
                      pl.BlockSpec(memory_space=pl.ANY),
                      pl.BlockSpec((tm * 8, 128), lambda i: (i, 0)),
                      pl.BlockSpec((tm, 128), lambda i: (i, 0)),
                      pl.BlockSpec((1, D_MODEL), lambda i: (0, 0)),
                      pl.BlockSpec((1, D_MODEL), lambda i: (0, 0))] + extra_in,
            out_specs=[pl.BlockSpec((tm, D_MODEL), lambda i: (i, 0))] + extra_out,
            scratch_shapes=[pltpu.SMEM((2 * tm,), jnp.int32),
                            pltpu.VMEM((2, tm * 8, 128), F32),
                            pltpu.VMEM((2, tm * 8, 128), F32),
                            pltpu.SemaphoreType.DMA((2,)),
                            pltpu.SemaphoreType.DMA(())]),
        out_shape=[jax.ShapeDtypeStruct((t, D_MODEL), F32)] + extra_shape,
        compiler_params=_cparams(1),
        name="moe_combine_ln",
    )(dst, ys, h8, rout, ln_g, ln_b, *([w_next] if with_proj else []))
    return tuple(res) if with_proj else res[0]


def _moe_layer(h8, w_group, b_group, w_router, b_router, w1, w3, w2, layer, ln_g, ln_b, w_next=None):
    t = h8.shape[0] // 8
    rout, eid, rank, counts = _router(h8, w_group, b_group, w_router, b_router)
    padded = (counts + MOE_ROWS - 1) // MOE_ROWS * MOE_ROWS
    pad_end = jnp.cumsum(padded)
    pad_start = pad_end - padded
    start_of = jnp.sum(jnp.where(eid[..., None] == jnp.arange(N_EXPERTS), pad_start, 0), axis=-1)
    dst = (start_of + rank).astype(jnp.int32)
    cap = 2 * t + N_EXPERTS * MOE_ROWS
    nblk = cap // MOE_ROWS
    tok = jnp.broadcast_to(jnp.arange(t, dtype=jnp.int32), (2, t))
    blk_e = jnp.minimum(jnp.sum(pad_end[None, :] <= (jnp.arange(nblk) * MOE_ROWS)[:, None], axis=1),
                        N_EXPERTS - 1).astype(jnp.int32)
    n_used = (pad_end[-1] // MOE_ROWS).astype(jnp.int32).reshape(1)
    _, by_row = lax.sort((dst.reshape(-1), tok.reshape(-1)), num_keys=1)
    cstart = jnp.cumsum(counts) - counts
    blk_first = cstart[blk_e] + (jnp.arange(nblk) * MOE_ROWS - pad_start[blk_e])
    blk_last = cstart[blk_e] + jnp.maximum(counts[blk_e] - 1, 0)
    idx = jnp.minimum(blk_first[:, None] + jnp.arange(MOE_ROWS)[None, :], blk_last[:, None])
    src = by_row[jnp.clip(idx, 0, 2 * t - 1)].reshape(-1)
    dst_tiles = jnp.transpose(dst.reshape(2, t // ROW_TILE, ROW_TILE), (1, 0, 2)).reshape(-1)
    ys = _experts(h8, src, blk_e, n_used, w1, w3, w2, layer, cap)
    return _combine_ln(dst_tiles, ys, h8, rout, ln_g, ln_b, w_next)


def kernel(x, ln_mix_g, ln_mix_b, ln_ffn_g, ln_ffn_b, ab_w_in, s5_lam_re, s5_lam_im, s5_log_dt, s5_b_re, s5_b_im,
           s5_c_re, s5_c_im, s5_d, s5_glu_w, s5_glu_b, gla_gate_w, gla_gate_b, gla_norm_g, ab_w_out, c_w_in,
           c_sink, c_w_out, rel_bias, moe_w_group, moe_b_group, moe_w_router, moe_b_router, moe_w1, moe_w3, moe_w2):
    n_batch, seq, _ = x.shape
    t = n_batch * seq
    xt = x.reshape(t, D_MODEL)
    row = lambda v: v.astype(F32).reshape(1, -1)

    w_in0 = jnp.pad(ab_w_in[0], ((0, 0), (0, AB_IN_PAD - AB_IN))).astype(BF16)
    h0, u_t = _inproj0(xt, w_in0)
    n_steps = seq // (S5_SEGS * S5_CHUNK)
    kcat, r1e, r2, coef, ptab = _s5_prep(s5_lam_re[0], s5_lam_im[0], s5_log_dt[0], s5_b_re[0], s5_b_im[0],
                                         s5_c_re[0], s5_c_im[0], s5_d[0], n_steps)
    ya = _s5_mixer(u_t, kcat, r1e, r2, coef, ptab, n_batch, n_steps)
    gw = gla_gate_w[0].astype(F32)
    wg = jnp.zeros((128, 2 * GLA_KEY), F32)
    wg = wg.at[0:GLA_RANK, 0:GLA_KEY].set(gw[0]).at[GLA_RANK:2 * GLA_RANK, GLA_KEY:].set(gw[1]).astype(BF16)
    bg = gla_gate_b[0].astype(F32).reshape(1, 2 * GLA_KEY)
    o_f, o_b = _gla_mixer(h0, wg, bg, n_batch, seq)
    h8 = _mix0_out(ya, o_f, o_b, h0, xt, s5_glu_w[0].astype(BF16), row(s5_glu_b[0]), row(gla_norm_g[0]),
                      ab_w_out[0].astype(BF16), row(ln_mix_g[0]), row(ln_mix_b[0]))
    h, h3 = _moe_layer(h8, moe_w_group[0], moe_b_group[0], moe_w_router[0], moe_b_router[0],
                       moe_w1, moe_w3, moe_w2, 0, row(ln_ffn_g[0]), row(ln_ffn_b[0]), c_w_in[0].astype(BF16))

    o = _attn_mixer(h3, rel_bias, c_sink[0], n_batch, seq)
    h8 = _proj_ln(o, c_w_out[0].astype(BF16), h, row(ln_mix_g[1]), row(ln_mix_b[1]))
    h = _moe_layer(h8, moe_w_group[1], moe_b_group[1], moe_w_router[1], moe_b_router[1],
                   moe_w1, moe_w3, moe_w2, 1, row(ln_ffn_g[1]), row(ln_ffn_b[1]))
    return h.reshape(n_batch, seq, D_MODEL)
```

```python
import functools
import math

import jax
import jax.numpy as jnp
from jax import lax
from jax.experimental import pallas as pl
from jax.experimental.pallas import tpu as pltpu

F32 = jnp.float32
BF16 = jnp.bfloat16

D_MODEL = 1024
S5_WIDTH = 512
S5_GROUP_CH = 16
S5_GROUPS = 32
S5_STATE = 64
GLA_HEADS = 4
GLA_DV = 128
GLA_DK = 64
GLA_KEY = 256
GLA_WIDTH = 512
GLA_RANK = 16
GLA_TAU = 16.0
GLA_CHUNK = 64
AB_IN = 2080
AB_IN_PAD = 2176
HEAD_DIM = 64
N_HEADS = 16
N_KV = 4
GQA = 4
ATT_DIM = 1024
KV_DIM = 256
WINDOW = 128
ATT_BLOCK = 128
ATT_SCALE = HEAD_DIM ** -0.5
REL_BUCKETS = 32
REL_MAX_DIST = 128
NEG_INF = -1e30
N_GROUPS = 4
EPG = 8
N_EXPERTS = 32
D_EXPERT = 512
LN_EPS = 1e-5
RMS_EPS = 1e-6
DEPTH = 2
ALPHA = (2 * DEPTH) ** 0.25

S5_CHUNK = 16
S5_SEGS = 8
S5_SEG_PAD = 4
ROW_TILE = 512
GLA_ROWS = 256
ATT_ROWS = 512
MOE_ROWS = 256
IDX_FETCH = 1024
N_ROW_BUFS = 3
MXU_COLS = 256
VMEM_LIMIT = 56 * 1024 * 1024


def _cparams(n_axes):
    return pltpu.CompilerParams(dimension_semantics=("arbitrary",) * n_axes,
                                vmem_limit_bytes=VMEM_LIMIT)


def _dot(a, b):
    return jnp.dot(a, b, preferred_element_type=F32)


def _dot_nt(a, b):
    return lax.dot_general(a, b, (((1,), (1,)), ((), ())), preferred_element_type=F32)


def _dot_tn(a, b):
    return lax.dot_general(a, b, (((0,), (0,)), ((), ())), preferred_element_type=F32)


def _layer_norm(r, g, b):
    mu = jnp.mean(r, axis=-1, keepdims=True)
    c = r - mu
    var = jnp.mean(c * c, axis=-1, keepdims=True)
    return c * lax.rsqrt(var + LN_EPS) * g + b


def _store_token_tiles(ref, val):
    n = val.shape[0]
    for k in range(D_MODEL // 128):
        ref[pl.ds(k, n, stride=8), :] = val[:, k * 128:(k + 1) * 128]


def _load_token_tiles(ref, n):
    return jnp.concatenate([ref[pl.ds(k, n, stride=8), :] for k in range(D_MODEL // 128)], axis=-1)


def _s5_prep(lam_re, lam_im, log_dt, b_re, b_im, c_re, c_im, d, n_steps):
    tc = S5_CHUNK
    g, p, c = S5_GROUPS, S5_STATE, S5_GROUP_CH
    lr = jnp.minimum(lam_re.astype(F32), -1e-4)
    li = lam_im.astype(F32)
    dt = jnp.exp(log_dt.astype(F32))[..., None]
    mag = jnp.exp(lr * dt)
    ar = mag * jnp.cos(li * dt)
    ai = mag * jnp.sin(li * dt)
    den = lr * lr + li * li
    nr = ar - 1.0
    coef_r = (nr * lr + ai * li) / den
    coef_i = (ai * lr - nr * li) / den
    br_ = b_re.astype(F32)
    bi_ = b_im.astype(F32)
    bbr = coef_r[..., None] * br_ - coef_i[..., None] * bi_
    bbi = coef_r[..., None] * bi_ + coef_i[..., None] * br_
    cr = c_re.astype(F32)
    ci = c_im.astype(F32)
    bbr_t = jnp.swapaxes(bbr, -1, -2)
    bbi_t = jnp.swapaxes(bbi, -1, -2)
    npair = g // 2

    def apow(n):
        nn = n.astype(F32)[:, None, None, None]
        m_ = jnp.exp(nn * (lr * dt)[None])
        ang = nn * (li * dt)[None]
        return m_ * jnp.cos(ang), m_ * jnp.sin(ang)

    pr, pi = apow(jnp.arange(tc + 1))
    hr = pr[:, :, :, None, :] * bbr_t[None] - pi[:, :, :, None, :] * bbi_t[None]
    hi = pr[:, :, :, None, :] * bbi_t[None] + pi[:, :, :, None, :] * bbr_t[None]
    kk = jnp.einsum('dgoq,jdgcq->jdgco', cr, hr) - jnp.einsum('dgoq,jdgcq->jdgco', ci, hi)
    k0 = kk[0, 0] + kk[0, 1] + d.astype(F32).reshape(g, c)[:, :, None] * jnp.eye(c, dtype=F32)
    slab = jnp.concatenate([kk[tc - 1:0:-1, 1], k0[None], kk[1:tc, 0]], axis=0)
    kcat = jnp.transpose(slab, (1, 2, 0, 3)).reshape(g, c, (2 * tc - 1) * c)
    kcat = jnp.pad(kcat, ((0, 0), (0, 0), (0, 2 * tc * c - kcat.shape[-1]))).reshape(npair, 2, c, 2 * tc * c)

    def pair_blockdiag(m):
        m2 = m.reshape((npair, 2) + m.shape[1:])
        z = jnp.zeros_like(m2[:, 0])
        return jnp.concatenate([jnp.concatenate([m2[:, 0], z], axis=-1),
                                jnp.concatenate([z, m2[:, 1]], axis=-1)], axis=1)

    def w_plane(h_, lag_sel, dirn):
        w = jnp.transpose(h_[lag_sel, dirn], (1, 0, 2, 3)).reshape(g, tc * c, p)
        return pair_blockdiag(w)
    s_fwd = tc - 1 - jnp.arange(tc)
    s_bwd = jnp.arange(tc)
    r1e = jnp.concatenate([w_plane(hr, s_fwd, 0), w_plane(hi, s_fwd, 0), w_plane(hr, s_bwd, 1), w_plane(hi, s_bwd, 1)],
                          axis=2)

    lane = jnp.arange(tc * c)
    rep_t = (lane[None, :] // c == jnp.arange(tc)[:, None]).astype(F32)
    tile_co = (lane[None, :] % c == jnp.arange(c)[:, None]).astype(F32)
    spread = lambda a, sel: jnp.dot(a, sel, precision=lax.Precision.HIGHEST)
    def readout(e, dirn):
        pr_l = spread(jnp.transpose(pr[e, dirn], (1, 2, 0)).reshape(g * p, tc), rep_t)
        pi_l = spread(jnp.transpose(pi[e, dirn], (1, 2, 0)).reshape(g * p, tc), rep_t)
        cr_l = spread(jnp.transpose(cr[dirn], (0, 2, 1)).reshape(g * p, c), tile_co)
        ci_l = spread(jnp.transpose(ci[dirn], (0, 2, 1)).reshape(g * p, c), tile_co)
        zr = (cr_l * pr_l - ci_l * pi_l).reshape(g, p, tc * c)
        zi = (cr_l * pi_l + ci_l * pr_l).reshape(g, p, tc * c)
        return pair_blockdiag(zr), pair_blockdiag(-zi)
    vf_r, vf_i = readout(jnp.arange(tc) + 1, 0)
    vb_r, vb_i = readout(tc - jnp.arange(tc), 1)
    r2 = jnp.concatenate([vf_r, vf_i, vb_r, vb_i], axis=1)

    steps = jnp.arange(n_steps)
    a16r, a16i = apow(jnp.array([tc]))
    apr, api = apow(steps * tc)
    anr, ani = apow(jnp.array([tc * n_steps]))
    def lanes(x):
        n_ = x.shape[0]
        return jnp.transpose(x.reshape(n_, npair, 2 * p), (1, 0, 2))
    coef = jnp.concatenate([lanes(a16r[:, 0]), lanes(a16i[:, 0]), lanes(a16r[:, 1]), lanes(a16i[:, 1]),
                            lanes(anr[:, 0]), lanes(ani[:, 0]), lanes(anr[:, 1]), lanes(ani[:, 1])], axis=1)
    rev = n_steps - 1 - steps
    ptab = jnp.stack([lanes(apr[:, 0]), lanes(api[:, 0]), lanes(apr[rev, 1]), lanes(api[rev, 1])], axis=1)
    return kcat, r1e.astype(BF16), r2.astype(BF16), coef, ptab


def _gelu_tanh(x):
    return 0.5 * x * (1.0 + jnp.tanh(math.sqrt(2.0 / math.pi) * (x + 0.044715 * (x * x * x))))


def _s5_kernel(u_ref, kcat_ref, r1e_ref, r2_ref, coef_ref, ptab_ref, y_ref, kt_ref, yi_ref, e_ref, s_ref,
               *, n_batch, n_steps):
    pitch = n_steps + S5_SEG_PAD
    blk = S5_CHUNK * S5_GROUP_CH
    kt_ref[...] = jnp.zeros_like(kt_ref)
    for gi in range(2):
        slab = kcat_ref[gi]
        for s in range(S5_CHUNK):
            off = (S5_CHUNK - 1 - s) * S5_GROUP_CH
            kt_ref[gi * blk + s * S5_GROUP_CH:gi * blk + (s + 1) * S5_GROUP_CH, gi * blk:(gi + 1) * blk] = (
                slab[:, off:off + blk].astype(BF16))
    u = u_ref[...]
    yi_ref[...] = _dot(u, kt_ref[...])
    e = _dot(u, r1e_ref[...])
    n_seg = n_batch * S5_SEGS
    for k in range(4):
        for sg in range(n_seg):
            e_ref[k, sg * pitch:sg * pitch + n_steps, :] = e[sg * n_steps:(sg + 1) * n_steps, k * 128:(k + 1) * 128]
    coef = coef_ref[...]
    a16fr, a16fi, a16br, a16bi = coef[0:1], coef[1:2], coef[2:3], coef[3:4]
    anfr, anfi, anbr, anbi = coef[4:5], coef[5:6], coef[6:7], coef[7:8]
    zero = jnp.zeros((S5_SEGS, 128), F32)
    sub = lax.broadcasted_iota(jnp.int32, (S5_SEGS, 128), 0)

    def seg_rows(b, j):
        return pl.ds(b * S5_SEGS * pitch + j, S5_SEGS, stride=pitch)

    def local_step(j, carry):
        out = []
        for b in range(n_batch):
            sfr, sfi, sbr, sbi = carry[b]
            rf = seg_rows(b, j)
            rb = seg_rows(b, n_steps - 1 - j)
            s_ref[0, rf, :] = sfr
            s_ref[1, rf, :] = sfi
            s_ref[2, rb, :] = sbr
            s_ref[3, rb, :] = sbi
            efr = e_ref[0, rf, :]
            efi = e_ref[1, rf, :]
            ebr = e_ref[2, rb, :]
            ebi = e_ref[3, rb, :]
            out.append((a16fr * sfr - a16fi * sfi + efr, a16fr * sfi + a16fi * sfr + efi,
                        a16br * sbr - a16bi * sbi + ebr, a16br * sbi + a16bi * sbr + ebi))
        return tuple(out)

    ends = lax.fori_loop(0, n_steps, local_step, tuple((zero, zero, zero, zero) for _ in range(n_batch)))

    carries = []
    for b in range(n_batch):
        efr, efi, ebr, ebi = ends[b]
        cfr, cfi, cbr, cbi = zero, zero, zero, zero
        for _ in range(S5_SEGS - 1):
            tfr = anfr * cfr - anfi * cfi + efr
            tfi = anfr * cfi + anfi * cfr + efi
            cfr = jnp.where(sub == 0, 0.0, pltpu.roll(tfr, 1, 0))
            cfi = jnp.where(sub == 0, 0.0, pltpu.roll(tfi, 1, 0))
            tbr = anbr * cbr - anbi * cbi + ebr
            tbi = anbr * cbi + anbi * cbr + ebi
            cbr = jnp.where(sub == S5_SEGS - 1, 0.0, pltpu.roll(tbr, S5_SEGS - 1, 0))
            cbi = jnp.where(sub == S5_SEGS - 1, 0.0, pltpu.roll(tbi, S5_SEGS - 1, 0))
        carries.append((cfr, cfi, cbr, cbi))

    def fix_step(j, _):
        pfr = ptab_ref[0, pl.ds(j, 1), :]
        pfi = ptab_ref[1, pl.ds(j, 1), :]
        pbr = ptab_ref[2, pl.ds(j, 1), :]
        pbi = ptab_ref[3, pl.ds(j, 1), :]
        for b in range(n_batch):
            cfr, cfi, cbr, cbi = carries[b]
            rj = seg_rows(b, j)
            s_ref[0, rj, :] = s_ref[0, rj, :] + (pfr * cfr - pfi * cfi)
            s_ref[1, rj, :] = s_ref[1, rj, :] + (pfr * cfi + pfi * cfr)
            s_ref[2, rj, :] = s_ref[2, rj, :] + (pbr * cbr - pbi * cbi)
            s_ref[3, rj, :] = s_ref[3, rj, :] + (pbr * cbi + pbi * cbr)
        return 0

    lax.fori_loop(0, n_steps, fix_step, 0)

    s_all = jnp.concatenate(
        [jnp.concatenate([s_ref[k, sg * pitch:sg * pitch + n_steps, :] for sg in range(n_seg)], axis=0)
         for k in range(4)], axis=-1)
    y = yi_ref[...] + _dot(s_all.astype(BF16), r2_ref[...])
    y_ref[...] = _gelu_tanh(y).astype(y_ref.dtype)


def _s5_mixer(u_t, kcat, r1e, r2, coef, ptab, n_batch, n_steps):
    npair, m, _ = u_t.shape
    scan_rows = n_batch * S5_SEGS * (n_steps + S5_SEG_PAD)
    kern = functools.partial(_s5_kernel, n_batch=n_batch, n_steps=n_steps)
    return pl.pallas_call(
        kern,
        grid=(npair,),
        in_specs=[pl.BlockSpec((None, m, 512), lambda i: (i, 0, 0)),
                  pl.BlockSpec((None, 2, S5_GROUP_CH, 512), lambda i: (i, 0, 0, 0)),
                  pl.BlockSpec((None, 512, 512), lambda i: (i, 0, 0)),
                  pl.BlockSpec((None, 512, 512), lambda i: (i, 0, 0)),
                  pl.BlockSpec((None, 8, 128), lambda i: (i, 0, 0)),
                  pl.BlockSpec((None, 4, n_steps, 128), lambda i: (i, 0, 0, 0))],
        out_specs=pl.BlockSpec((None, m, 512), lambda i: (i, 0, 0)),
        out_shape=jax.ShapeDtypeStruct((npair, m, 512), BF16),
        scratch_shapes=[pltpu.VMEM((512, 512), BF16), pltpu.VMEM((m, 512), F32),
                        pltpu.VMEM((4, scan_rows, 128), F32), pltpu.VMEM((4, scan_rows, 128), F32)],
        compiler_params=_cparams(1),
        name="s5_scan",
    )(u_t, kcat, r1e, r2, coef, ptab)


def _inproj0_kernel(x_ref, w_ref, h_ref, ut_ref, u_s):
    h = _dot(x_ref[...].astype(BF16), w_ref[...])
    h_ref[...] = h[:, S5_WIDTH:].astype(h_ref.dtype)
    n_lane_blk = S5_WIDTH // 128
    for k in range(n_lane_blk):
        u_s[k] = h[:, k * 128:(k + 1) * 128]
    n_chunk = ROW_TILE // S5_CHUNK
    rows = [jnp.concatenate([u_s[k, pl.ds(s, n_chunk, stride=S5_CHUNK), :] for k in range(n_lane_blk)], axis=-1)
            for s in range(S5_CHUNK)]
    ch = S5_GROUP_CH
    for p in range(S5_GROUPS // 2):
        ut_ref[p] = jnp.concatenate([r[:, (2 * p + gi) * ch:(2 * p + gi + 1) * ch] for gi in range(2) for r in rows],
                                    axis=-1).astype(ut_ref.dtype)


def _inproj0(x, w):
    t, k = x.shape
    n = w.shape[1]
    tm = ROW_TILE
    npair = S5_GROUPS // 2
    return pl.pallas_call(
        _inproj0_kernel,
        grid=(t // tm,),
        in_specs=[pl.BlockSpec((tm, k), lambda i: (i, 0)),
                  pl.BlockSpec((k, n), lambda i: (0, 0))],
        out_specs=[pl.BlockSpec((tm, n - S5_WIDTH), lambda i: (i, 0)),
                   pl.BlockSpec((npair, tm // S5_CHUNK, 512), lambda i: (0, i, 0))],
        out_shape=[jax.ShapeDtypeStruct((t, n - S5_WIDTH), BF16),
                   jax.ShapeDtypeStruct((npair, t // S5_CHUNK, 512), BF16)],
        scratch_shapes=[pltpu.VMEM((S5_WIDTH // 128, tm, 128), F32)],
        compiler_params=_cparams(1),
        name="inproj0",
    )(x, w)


def _gla_kernel(qf_ref, kf_ref, vf_ref, lf_ref, qb_ref, kb_ref, vb_ref, lb_ref, wg_ref, bg_ref,
                of_ref, ob_ref, st_ref):
    @pl.when(pl.program_id(1) == 0)
    def _():
        st_ref[...] = jnp.zeros_like(st_ref)

    n_chunks = GLA_ROWS // GLA_CHUNK
    row = lax.broadcasted_iota(jnp.int32, (GLA_CHUNK, GLA_CHUNK), 0)
    col = lax.broadcasted_iota(jnp.int32, (GLA_CHUNK, GLA_CHUNK), 1)
    mask_f = col <= row
    mask_b = col > row
    brow = lax.broadcasted_iota(jnp.int32, (GLA_ROWS, GLA_ROWS), 0)
    bcol = lax.broadcasted_iota(jnp.int32, (GLA_ROWS, GLA_ROWS), 1)
    shift = GLA_CHUNK.bit_length() - 1
    same = (brow >> shift) == (bcol >> shift)
    tri_f = (same & (bcol <= brow)).astype(BF16)
    tri_b = (same & (bcol >= brow)).astype(BF16)
    wg = wg_ref[...]
    bg = bg_ref[...]

    def direction(q_ref, k_ref, v_ref, l_ref, o_ref, d):
        z = _dot(l_ref[...], wg[:, d * GLA_KEY:(d + 1) * GLA_KEY]) + bg[:, d * GLA_KEY:(d + 1) * GLA_KEY]
        log_a = jax.nn.log_sigmoid(z) * (1.0 / GLA_TAU)
        la_hi = log_a.astype(BF16)
        la_lo = (log_a - la_hi.astype(F32)).astype(BF16)
        tri = tri_f if d == 0 else tri_b
        bc = _dot(tri, la_hi) + _dot(tri, la_lo)
        tot_c = jnp.sum(log_a.reshape(n_chunks, GLA_CHUNK, GLA_KEY), axis=1, keepdims=True)
        tot = jnp.broadcast_to(tot_c, (n_chunks, GLA_CHUNK, GLA_KEY)).reshape(GLA_ROWS, GLA_KEY)
        q = q_ref[...].astype(F32) * (GLA_DK ** -0.5)
        k = k_ref[...].astype(F32)
        qd_all = (q * jnp.exp(bc)).astype(BF16)
        kd_all = (k * jnp.exp(-bc)).astype(BF16)
        kc_all = (k * jnp.exp(tot - bc)).astype(BF16)
        decay_all = jnp.exp(tot)
        order = range(n_chunks) if d == 0 else range(n_chunks - 1, -1, -1)
        mask = mask_f if d == 0 else mask_b
        states = [st_ref[d, h] for h in range(GLA_HEADS)]
        for c in order:
            sl = slice(c * GLA_CHUNK, (c + 1) * GLA_CHUNK)
            qd, kd, kc = qd_all[sl], kd_all[sl], kc_all[sl]
            decay = decay_all[c * GLA_CHUNK:c * GLA_CHUNK + 1]
            v = v_ref[sl, :]
            outs = []
            for h in range(GLA_HEADS):
                ks = slice(h * GLA_DK, (h + 1) * GLA_DK)
                vs = slice(h * GLA_DV, (h + 1) * GLA_DV)
                s = jnp.where(mask, _dot_nt(qd[:, ks], kd[:, ks]), 0.0).astype(BF16)
                st = states[h]
                o = _dot(s, v[:, vs]) + _dot_nt(qd[:, ks], st.astype(BF16))
                states[h] = st * decay[:, ks] + _dot_tn(v[:, vs], kc[:, ks])
                outs.append(o)
            o_ref[sl, :] = jnp.concatenate(outs, axis=-1).astype(o_ref.dtype)
        for h in range(GLA_HEADS):
            st_ref[d, h] = states[h]

    direction(qf_ref, kf_ref, vf_ref, lf_ref, of_ref, 0)
    direction(qb_ref, kb_ref, vb_ref, lb_ref, ob_ref, 1)


def _gla_mixer(h0, wg, bg, n_batch, seq):
    nb = seq // GLA_ROWS
    r = GLA_ROWS
    fwd = lambda b, i: b * nb + i
    bwd = lambda b, i: b * nb + (nb - 1 - i)
    def spec(width, colblk, rowfn):
        return pl.BlockSpec((r, width), lambda b, i: (rowfn(b, i), colblk))
    in_specs = [spec(256, 0, fwd), spec(256, 1, fwd), spec(512, 1, fwd), spec(128, 12, fwd),
                spec(256, 0, bwd), spec(256, 1, bwd), spec(512, 1, bwd), spec(128, 12, bwd),
                pl.BlockSpec((128, 512), lambda b, i: (0, 0)),
                pl.BlockSpec((1, 512), lambda b, i: (0, 0))]
    out_specs = [pl.BlockSpec((r, 512), lambda b, i: (fwd(b, i), 0)),
                 pl.BlockSpec((r, 512), lambda b, i: (bwd(b, i), 0))]
    t = n_batch * seq
    return pl.pallas_call(
        _gla_kernel,
        grid=(n_batch, nb),
        in_specs=in_specs,
        out_specs=out_specs,
        out_shape=[jax.ShapeDtypeStruct((t, 512), BF16), jax.ShapeDtypeStruct((t, 512), BF16)],
        scratch_shapes=[pltpu.VMEM((2, GLA_HEADS, GLA_DV, GLA_DK), F32)],
        compiler_params=_cparams(2),
        name="gla_chunked",
    )(h0, h0, h0, h0, h0, h0, h0, h0, wg, bg)


def _mix0_out_kernel(y_ref, of_ref, ob_ref, go_ref, x_ref, gw_ref, gb_ref, ng_ref, wo_ref, lg_ref, lb_ref,
                     h8_ref, ya_s):
    n_chunk = ROW_TILE // S5_CHUNK
    ys = [y_ref[p].astype(F32) for p in range(S5_GROUPS // 2)]
    n_lane_blk = S5_WIDTH // 128
    ch = S5_GROUP_CH
    blk = S5_CHUNK * ch
    for t in range(S5_CHUNK):
        for k in range(n_lane_blk):
            ya_s[k, pl.ds(t, n_chunk, stride=S5_CHUNK), :] = jnp.concatenate(
                [y[:, gi * blk + t * ch:gi * blk + (t + 1) * ch] for y in ys[4 * k:4 * k + 4] for gi in range(2)],
                axis=-1)
    yaf = jnp.concatenate([ya_s[k] for k in range(n_lane_blk)], axis=-1)
    ya = yaf.astype(BF16)
    gate = _dot(ya, gw_ref[...]) + gb_ref[...]
    ya2 = yaf * jax.nn.sigmoid(gate)
    o = of_ref[...].astype(F32) + ob_ref[...].astype(F32)
    ng = ng_ref[...]
    parts = []
    for h in range(GLA_HEADS):
        oh = o[:, h * GLA_DV:(h + 1) * GLA_DV]
        ms = jnp.mean(oh * oh, axis=-1, keepdims=True)
        parts.append(oh * lax.rsqrt(ms + RMS_EPS) * ng[:, h * GLA_DV:(h + 1) * GLA_DV])
    yb = jnp.concatenate(parts, axis=-1) * jax.nn.silu(go_ref[...].astype(F32))
    wo = wo_ref[...]
    y = _dot(ya2.astype(BF16), wo[:S5_WIDTH]) + _dot(yb.astype(BF16), wo[S5_WIDTH:])
    _store_token_tiles(h8_ref, _layer_norm(ALPHA * x_ref[...] + y, lg_ref[...], lb_ref[...]))


def _mix0_out(ya, o_f, o_b, h0, x, glu_w, glu_b, norm_g, w_out, ln_g, ln_b):
    t = x.shape[0]
    tm = ROW_TILE
    row = lambda w: pl.BlockSpec((tm, w), lambda i: (i, 0))
    full = lambda a, b: pl.BlockSpec((a, b), lambda i: (0, 0))
    return pl.pallas_call(
        _mix0_out_kernel,
        grid=(t // tm,),
        in_specs=[pl.BlockSpec((S5_GROUPS // 2, tm // S5_CHUNK, 512), lambda i: (0, i, 0)),
                  row(512), row(512),
                  pl.BlockSpec((tm, 512), lambda i: (i, 2)),
                  row(1024), full(512, 512), full(1, 512), full(1, 512), full(1024, 1024),
                  full(1, 1024), full(1, 1024)],
        out_specs=pl.BlockSpec((tm * 8, 128), lambda i: (i, 0)),
        out_shape=jax.ShapeDtypeStruct((t * 8, 128), F32),
        scratch_shapes=[pltpu.VMEM((S5_WIDTH // 128, tm, 128), F32)],
        compiler_params=_cparams(1),
        name="mix0_out_ln",
    )(ya, o_f, o_b, h0, x, glu_w, glu_b, norm_g, w_out, ln_g, ln_b)


def _attn_kernel(q_ref, kc_ref, vc_ref, kp_ref, vp_ref, kn_ref, vn_ref, bias_ref, sink_ref, o_ref, *, seq):
    blk = pl.program_id(1)
    n_blk = pl.num_programs(1)
    n_sub = ATT_ROWS // ATT_BLOCK
    kw = ATT_BLOCK + 2 * WINDOW
    nq = GQA * ATT_BLOCK
    key_row = lax.broadcasted_iota(jnp.int32, (kw, nq), 0)
    pen_first = jnp.where((blk == 0) & (key_row < WINDOW), NEG_INF, 0.0)
    pen_last = jnp.where((blk == n_blk - 1) & (key_row >= ATT_BLOCK + WINDOW), NEG_INF, 0.0)
    kall = jnp.concatenate([kp_ref[...], kc_ref[...], kn_ref[...]], axis=0)
    vall = jnp.concatenate([vp_ref[...], vc_ref[...], vn_ref[...]], axis=0)
    for s in range(n_sub):
        kwin = kall[s * ATT_BLOCK:s * ATT_BLOCK + kw]
        vwin = vall[s * ATT_BLOCK:s * ATT_BLOCK + kw]
        q = q_ref[s * ATT_BLOCK:(s + 1) * ATT_BLOCK, :] * ATT_SCALE
        outs_t = []
        for kv in range(N_KV):
            qs = jnp.concatenate([q[:, (kv * GQA + g) * HEAD_DIM:(kv * GQA + g + 1) * HEAD_DIM]
                                  for g in range(GQA)], axis=0)
            kh = kwin[:, kv * HEAD_DIM:(kv + 1) * HEAD_DIM]
            vh = vwin[:, kv * HEAD_DIM:(kv + 1) * HEAD_DIM]
            st = _dot_nt(kh, qs) + bias_ref[kv]
            if s == 0:
                st = st + pen_first
            if s == n_sub - 1:
                st = st + pen_last
            sink = sink_ref[kv]
            m = jnp.maximum(jnp.max(st, axis=0, keepdims=True), sink)
            p = jnp.exp(st - m)
            den = jnp.sum(p, axis=0, keepdims=True) + jnp.exp(sink - m)
            outs_t.append(_dot_tn(vh, p.astype(BF16)) / den)
        o = jnp.concatenate(outs_t, axis=0).T
        pieces = [o[g * ATT_BLOCK:(g + 1) * ATT_BLOCK, kv * HEAD_DIM:(kv + 1) * HEAD_DIM]
                  for kv in range(N_KV) for g in range(GQA)]
        o_ref[s * ATT_BLOCK:(s + 1) * ATT_BLOCK, :] = jnp.concatenate(pieces, axis=-1).astype(o_ref.dtype)


def _t5_bucket(rel):
    nb = REL_BUCKETS // 2
    max_exact = nb // 2
    ret = (rel > 0).astype(jnp.int32) * nb
    n = jnp.abs(rel)
    large = max_exact + (jnp.log(jnp.maximum(n, 1).astype(F32) / max_exact)
                         / math.log(REL_MAX_DIST / max_exact) * (nb - max_exact)).astype(jnp.int32)
    large = jnp.minimum(large, nb - 1)
    return ret + jnp.where(n < max_exact, n, large)


def _attn_mixer(h3, rel_bias, sink, n_batch, seq):
    kw = ATT_BLOCK + 2 * WINDOW
    n_rel = kw + ATT_BLOCK - 1
    rel = jnp.arange(n_rel) - (n_rel - 1) // 2
    tb = rel_bias.astype(F32)[_t5_bucket(rel)]
    tb = jnp.where((jnp.abs(rel) <= WINDOW)[:, None], tb, NEG_INF)
    tb = jnp.pad(tb.T, ((0, 0), (0, 1)))
    flat = jnp.tile(tb, (1, ATT_BLOCK))[:, ATT_BLOCK - 1:ATT_BLOCK - 1 + ATT_BLOCK * n_rel]
    bias = flat.reshape(N_HEADS, ATT_BLOCK, n_rel)[:, :, :kw]
    bias = jnp.transpose(bias.reshape(N_KV, GQA, ATT_BLOCK, kw), (0, 3, 1, 2)).reshape(N_KV, kw, GQA * ATT_BLOCK)
    sink_rows = jnp.repeat(sink.astype(F32).reshape(N_KV, GQA), ATT_BLOCK, axis=1)[:, None, :]
    nblk = seq // ATT_ROWS
    sub = ATT_ROWS // ATT_BLOCK
    n128 = seq // ATT_BLOCK
    cur = lambda b, i: b * nblk + i
    prev = lambda b, i: b * n128 + jnp.maximum(i * sub - 1, 0)
    nxt = lambda b, i: b * n128 + jnp.minimum((i + 1) * sub, n128 - 1)
    t = n_batch * seq
    kern = functools.partial(_attn_kernel, seq=seq)
    return pl.pallas_call(
        kern,
        grid=(n_batch, nblk),
        in_specs=[pl.BlockSpec((ATT_ROWS, ATT_DIM), lambda b, i: (cur(b, i), 0)),
                  pl.BlockSpec((ATT_ROWS, KV_DIM), lambda b, i: (cur(b, i), 4)),
                  pl.BlockSpec((ATT_ROWS, KV_DIM), lambda b, i: (cur(b, i), 5)),
                  pl.BlockSpec((ATT_BLOCK, KV_DIM), lambda b, i: (prev(b, i), 4)),
                  pl.BlockSpec((ATT_BLOCK, KV_DIM), lambda b, i: (prev(b, i), 5)),
                  pl.BlockSpec((ATT_BLOCK, KV_DIM), lambda b, i: (nxt(b, i), 4)),
                  pl.BlockSpec((ATT_BLOCK, KV_DIM), lambda b, i: (nxt(b, i), 5)),
                  pl.BlockSpec((N_KV, kw, GQA * ATT_BLOCK), lambda b, i: (0, 0, 0)),
                  pl.BlockSpec((N_KV, 1, GQA * ATT_BLOCK), lambda b, i: (0, 0, 0))],
        out_specs=pl.BlockSpec((ATT_ROWS, ATT_DIM), lambda b, i: (cur(b, i), 0)),
        out_shape=jax.ShapeDtypeStruct((t, ATT_DIM), BF16),
        compiler_params=_cparams(2),
        name="window_gqa",
    )(h3, h3, h3, h3, h3, h3, h3, bias, sink_rows)


def _proj_ln_kernel(a_ref, w_ref, x_ref, g_ref, b_ref, h8_ref):
    y = _dot(a_ref[...], w_ref[...])
    _store_token_tiles(h8_ref, _layer_norm(ALPHA * x_ref[...] + y, g_ref[...], b_ref[...]))


def _proj_ln(a, w, x, g, b):
    t, k = a.shape
    tm = ROW_TILE
    return pl.pallas_call(
        _proj_ln_kernel,
        grid=(t // tm,),
        in_specs=[pl.BlockSpec((tm, k), lambda i: (i, 0)),
                  pl.BlockSpec((k, D_MODEL), lambda i: (0, 0)),
                  pl.BlockSpec((tm, D_MODEL), lambda i: (i, 0)),
                  pl.BlockSpec((1, D_MODEL), lambda i: (0, 0)),
                  pl.BlockSpec((1, D_MODEL), lambda i: (0, 0))],
        out_specs=pl.BlockSpec((tm * 8, 128), lambda i: (i, 0)),
        out_shape=jax.ShapeDtypeStruct((t * 8, 128), F32),
        compiler_params=_cparams(1),
        name="proj_ln",
    )(a, w, x, g, b)


def _router_kernel(h_ref, wh_ref, wl_ref, b_ref, o_ref, ot_ref, cnt_ref, run_ref, upper_ref):
    @pl.when(pl.program_id(0) == 0)
    def _():
        run_ref[...] = jnp.zeros_like(run_ref)
        tri = (lax.broadcasted_iota(jnp.int32, upper_ref.shape, 0) < lax.broadcasted_iota(jnp.int32, upper_ref.shape, 1))
        upper_ref[...] = tri.astype(BF16)

    x = _load_token_tiles(h_ref, ROW_TILE)
    xh = x.astype(BF16)
    xl = (x - xh.astype(F32)).astype(BF16)
    wh = wh_ref[...]
    lt = _dot_nt(wh, xh) + (_dot_nt(wh, xl) + _dot_nt(wl_ref[...], xh)) + b_ref[...][:, 0:1]
    tm = lt.shape[1]
    neg = jnp.float32(-jnp.inf)
    row8 = lax.broadcasted_iota(jnp.int32, (EPG, tm), 0).astype(F32)
    is_g = row8 < N_GROUPS
    gl = jnp.where(is_g, lt[0:EPG], neg)
    gmax = jnp.max(gl, axis=0, keepdims=True)
    gsum = jnp.sum(jnp.where(is_g, jnp.exp(gl - gmax), 0.0), axis=0, keepdims=True)
    grp = jnp.min(jnp.where(gl == gmax, row8, float(EPG)), axis=0, keepdims=True)
    p_grp = 1.0 / gsum
    el = lt[EPG:2 * EPG]
    for g in range(1, N_GROUPS):
        el = jnp.where(grp == float(g), lt[EPG * (g + 1):EPG * (g + 2)], el)
    m1 = jnp.max(el, axis=0, keepdims=True)
    i1 = jnp.min(jnp.where(el == m1, row8, float(EPG)), axis=0, keepdims=True)
    el2 = jnp.where(row8 == i1, neg, el)
    m2 = jnp.max(el2, axis=0, keepdims=True)
    i2 = jnp.min(jnp.where((row8 != i1) & (el2 == m2), row8, float(EPG)), axis=0, keepdims=True)
    e2 = jnp.exp(m2 - m1)
    g1 = p_grp / (1.0 + e2)
    g2 = p_grp * e2 / (1.0 + e2)
    eid1 = grp * EPG + i1
    eid2 = grp * EPG + i2
    rowe = lax.broadcasted_iota(jnp.int32, (N_EXPERTS, tm), 0).astype(F32)
    hit1 = rowe == eid1
    hit2 = rowe == eid2
    onehot = (hit1 | hit2).astype(BF16)
    before = _dot(onehot, upper_ref[...]) + run_ref[...][:, 0:1]
    r1 = jnp.sum(jnp.where(hit1, before, 0.0), axis=0, keepdims=True)
    r2 = jnp.sum(jnp.where(hit2, before, 0.0), axis=0, keepdims=True)
    run_ref[...] += jnp.sum(onehot.astype(F32), axis=1, keepdims=True)
    cnt_ref[...] = run_ref[...]
    out_t = jnp.concatenate([eid1, eid2, g1, g2, r1, r2, jnp.zeros((2, tm), F32)], axis=0)
    ot_ref[...] = out_t
    o_ref[...] = jnp.concatenate([out_t, jnp.zeros((128 - 8, tm), F32)], axis=0).T


def _router(h8, w_group, b_group, w_router, b_router):
    t = h8.shape[0] // 8
    w = jnp.zeros((128, D_MODEL), F32)
    w = w.at[0:N_GROUPS].set(w_group.astype(F32).T).at[EPG:EPG + N_EXPERTS].set(w_router.astype(F32).T)
    wh = w.astype(BF16)
    wl = (w - wh.astype(F32)).astype(BF16)
    bias = jnp.zeros((128,), F32).at[0:N_GROUPS].set(b_group.astype(F32)).at[EPG:EPG + N_EXPERTS].set(
        b_router.astype(F32))
    bias = jnp.broadcast_to(bias[:, None], (128, 128))
    tm = ROW_TILE
    out, out_t, cnt = pl.pallas_call(
        _router_kernel,
        grid=(t // tm,),
        in_specs=[pl.BlockSpec((tm * 8, 128), lambda i: (i, 0)),
                  pl.BlockSpec((128, D_MODEL), lambda i: (0, 0)),
                  pl.BlockSpec((128, D_MODEL), lambda i: (0, 0)),
                  pl.BlockSpec((128, 128), lambda i: (0, 0))],
        out_specs=[pl.BlockSpec((tm, 128), lambda i: (i, 0)),
                   pl.BlockSpec((8, tm), lambda i: (0, i)),
                   pl.BlockSpec((N_EXPERTS, 128), lambda i: (0, 0))],
        out_shape=[jax.ShapeDtypeStruct((t, 128), F32), jax.ShapeDtypeStruct((8, t), F32),
                   jax.ShapeDtypeStruct((N_EXPERTS, 128), F32)],
        scratch_shapes=[pltpu.VMEM((N_EXPERTS, 128), F32), pltpu.VMEM((tm, tm), BF16)],
        compiler_params=_cparams(1),
        name="moe_router",
    )(h8, wh, wl, bias)
    eid = out_t[0:2].astype(jnp.int32)
    rank = out_t[4:6].astype(jnp.int32)
    counts = cnt[:, 0].astype(jnp.int32)
    return out, eid, rank, counts


def _token_tile(ref, tok):
    return ref.at[pl.ds(pl.multiple_of(tok * 8, 8), 8)]


def _experts_kernel(be_ref, nu_ref, src_hbm, h8_hbm, w1_ref, w3_ref, w2_ref, ys_ref,
                    w13_s, w2_s, xbuf, idx_smem, sem, isem):
    i = pl.program_id(0)
    n_used = nu_ref[0]
    used = i < n_used
    slot = i % N_ROW_BUFS

    blk_per_fetch = IDX_FETCH // MOE_ROWS

    def fetch_indices(blk):
        @pl.when(blk % blk_per_fetch == 0)
        def _():
            off = pl.multiple_of(blk * MOE_ROWS, IDX_FETCH)
            cp = pltpu.make_async_copy(src_hbm.at[pl.ds(off, IDX_FETCH)], idx_smem, isem)
            cp.start()
            cp.wait()

    def start_rows(blk, slot_, r0, n):
        base = (blk % blk_per_fetch) * MOE_ROWS
        toks = [idx_smem[base + r0 + u] for u in range(n)]
        for u in range(n):
            pltpu.make_async_copy(_token_tile(h8_hbm, toks[u]), _token_tile(xbuf.at[slot_], r0 + u),
                                  sem.at[slot_]).start(priority=u % 2)

    def gather_block(blk, slot_):
        fetch_indices(blk)
        def issue(r8, _):
            start_rows(blk, slot_, r8 * 8, 8)
            return 0
        lax.fori_loop(0, MOE_ROWS // 8, issue, 0)

    @pl.when(i == 0)
    def _():
        gather_block(0, 0)
        @pl.when(n_used > 1)
        def _():
            gather_block(1, 1)

    nxt = jnp.minimum(i + 2, n_used - 1)
    slot_n = (i + 2) % N_ROW_BUFS

    @pl.when(used)
    def _():
        fetch_indices(nxt)
        pltpu.make_async_copy(h8_hbm.at[pl.ds(0, MOE_ROWS * 8)], xbuf.at[slot], sem.at[slot]).wait()

    prev = be_ref[jnp.maximum(i - 1, 0)]
    fresh = (i == 0) | (be_ref[i] != prev)

    @pl.when(used & fresh)
    def _():
        w13_s[:, :D_EXPERT] = w1_ref[...].astype(BF16)
        w13_s[:, D_EXPERT:] = w3_ref[...].astype(BF16)
        w2_s[...] = w2_ref[...].astype(BF16)

    @pl.when(used)
    def _():
        n_tiles = 2 * D_EXPERT // MXU_COLS
        n_tiles2 = D_MODEL // MXU_COLS
        rows_per_group = MOE_ROWS // (n_tiles + n_tiles2)
        x = _load_token_tiles(xbuf.at[slot], MOE_ROWS).astype(BF16)
        parts = []
        for c in range(n_tiles):
            start_rows(nxt, slot_n, c * rows_per_group, rows_per_group)
            parts.append(_dot(x, w13_s[:, c * MXU_COLS:(c + 1) * MXU_COLS]))
        h = jnp.concatenate(parts, axis=-1)
        h1 = h[:, :D_EXPERT]
        hdn = ((h1 * jax.nn.sigmoid(h1)) * h[:, D_EXPERT:]).astype(BF16)
        for c in range(n_tiles2):
            start_rows(nxt, slot_n, (n_tiles + c) * rows_per_group, rows_per_group)
            yc = _dot(hdn, w2_s[:, c * MXU_COLS:(c + 1) * MXU_COLS])
            for k in range(MXU_COLS // 128):
                ys_ref[pl.ds(c * (MXU_COLS // 128) + k, MOE_ROWS, stride=8), :] = yc[:, k * 128:(k + 1) * 128]

    def wait_rows(slot_):
        pltpu.make_async_copy(h8_hbm.at[pl.ds(0, MOE_ROWS * 8)], xbuf.at[slot_], sem.at[slot_]).wait()

    @pl.when(i == n_used - 1)
    def _():
        wait_rows(slot_n)
        @pl.when(n_used > 1)
        def _():
            wait_rows((i + 1) % N_ROW_BUFS)

    @pl.when(jnp.logical_not(used))
    def _():
        ys_ref[...] = jnp.zeros_like(ys_ref)


def _experts(h8, src, blk_e, n_used, w1, w3, w2, layer, cap):
    nblk = cap // MOE_ROWS
    return pl.pallas_call(
        _experts_kernel,
        grid_spec=pltpu.PrefetchScalarGridSpec(
            num_scalar_prefetch=2,
            grid=(nblk,),
            in_specs=[pl.BlockSpec(memory_space=pl.ANY),
                      pl.BlockSpec(memory_space=pl.ANY),
                      pl.BlockSpec((None, None, D_MODEL, D_EXPERT), lambda i, be, nu: (layer, be[i], 0, 0)),
                      pl.BlockSpec((None, None, D_MODEL, D_EXPERT), lambda i, be, nu: (layer, be[i], 0, 0)),
                      pl.BlockSpec((None, None, D_EXPERT, D_MODEL), lambda i, be, nu: (layer, be[i], 0, 0))],
            out_specs=pl.BlockSpec((MOE_ROWS * 8, 128), lambda i, be, nu: (i, 0)),
            scratch_shapes=[pltpu.VMEM((D_MODEL, 2 * D_EXPERT), BF16),
                            pltpu.VMEM((D_EXPERT, D_MODEL), BF16),
                            pltpu.VMEM((N_ROW_BUFS, MOE_ROWS * 8, 128), F32),
                            pltpu.SMEM((IDX_FETCH,), jnp.int32),
                            pltpu.SemaphoreType.DMA((N_ROW_BUFS,)),
                            pltpu.SemaphoreType.DMA(())]),
        out_shape=jax.ShapeDtypeStruct((cap * 8, 128), F32),
        compiler_params=_cparams(1),
        name="moe_experts",
    )(blk_e, n_used, src, h8, w1, w3, w2)


def _combine_kernel(dst_hbm, ys_hbm, h_ref, gate_ref, g_ref, b_ref, *rest, with_proj):
    if with_proj:
        w_ref, o_ref, p_ref, idx_smem, y0_ref, y1_ref, sem, isem = rest
    else:
        o_ref, idx_smem, y0_ref, y1_ref, sem, isem = rest
    i = pl.program_id(0)
    n_prog = pl.num_programs(0)
    tm = ROW_TILE
    slot = i % 2
    n_groups = 8
    grp = tm // n_groups

    def fetch_indices(tile):
        cp = pltpu.make_async_copy(dst_hbm.at[pl.ds(tile * 2 * tm, 2 * tm)], idx_smem, isem)
        cp.start()
        cp.wait()

    def start_rows(slot_, t0, n):
        r0 = [idx_smem[t0 + u] for u in range(n)]
        r1 = [idx_smem[tm + t0 + u] for u in range(n)]
        for u in range(n):
            pltpu.make_async_copy(_token_tile(ys_hbm, r0[u]), _token_tile(y0_ref.at[slot_], t0 + u),
                                  sem.at[slot_]).start(priority=0)
            pltpu.make_async_copy(_token_tile(ys_hbm, r1[u]), _token_tile(y1_ref.at[slot_], t0 + u),
                                  sem.at[slot_]).start(priority=1)

    def wait_tile(slot_):
        pltpu.make_async_copy(ys_hbm.at[pl.ds(0, tm * 8)], y0_ref.at[slot_], sem.at[slot_]).wait()
        pltpu.make_async_copy(ys_hbm.at[pl.ds(0, tm * 8)], y1_ref.at[slot_], sem.at[slot_]).wait()

    @pl.when(i == 0)
    def _():
        fetch_indices(0)
        def issue(t8, _):
            start_rows(0, t8 * 8, 8)
            return 0
        lax.fori_loop(0, tm // 8, issue, 0)

    fetch_indices(jnp.minimum(i + 1, n_prog - 1))
    if not with_proj:
        def issue_next(t8, _):
            start_rows(1 - slot, t8 * 8, 8)
            return 0
        lax.fori_loop(0, tm // 8, issue_next, 0)
    wait_tile(slot)
    if with_proj:
        start_rows(1 - slot, 0, grp)
    gate = gate_ref[...]
    y = (_load_token_tiles(y0_ref.at[slot], tm) * gate[:, 2:3]
         + _load_token_tiles(y1_ref.at[slot], tm) * gate[:, 3:4])
    out = _layer_norm(ALPHA * _load_token_tiles(h_ref, tm) + y, g_ref[...], b_ref[...])
    o_ref[...] = out
    if with_proj:
        start_rows(1 - slot, grp, grp)
        n_tiles = w_ref.shape[1] // MXU_COLS
        out_b = out.astype(BF16)
        for c in range(n_tiles):
            g = 2 + c * (n_groups - 2) // n_tiles
            g_end = 2 + (c + 1) * (n_groups - 2) // n_tiles
            for gg in range(g, g_end):
                start_rows(1 - slot, gg * grp, grp)
            p_ref[:, c * MXU_COLS:(c + 1) * MXU_COLS] = _dot(
                out_b, w_ref[:, c * MXU_COLS:(c + 1) * MXU_COLS]).astype(p_ref.dtype)

    @pl.when(i == n_prog - 1)
    def _():
        wait_tile(1 - slot)


def _combine_ln(dst, ys, h8, rout, ln_g, ln_b, w_next=None):
    t = h8.shape[0] // 8
    tm = ROW_TILE
    with_proj = w_next is not None
    extra_in = [pl.BlockSpec(w_next.shape, lambda i: (0, 0))] if with_proj else []
    extra_out = [pl.BlockSpec((tm, w_next.shape[1]), lambda i: (i, 0))] if with_proj else []
    extra_shape = [jax.ShapeDtypeStruct((t, w_next.shape[1]), BF16)] if with_proj else []
    res = pl.pallas_call(
        functools.partial(_combine_kernel, with_proj=with_proj),
        grid_spec=pltpu.PrefetchScalarGridSpec(
            num_scalar_prefetch=0,
            grid=(t // tm,),
            in_specs=[pl.BlockSpec(memory_space=pl.ANY),
                      pl.BlockSpec(memory_space=pl.ANY),
                      pl.BlockSpec((tm * 8, 128), lambda i: (i, 0)),
                      pl.BlockSpec((tm, 128), lambda i: (i, 0)),
                      pl.BlockSpec((1, D_MODEL), lambda i: (0, 0)),
                      pl.BlockSpec((1, D_MODEL), lambda i: (0, 0))] + extra_in,
            out_specs=[pl.BlockSpec((tm, D_MODEL), lambda i: (i, 0))] + extra_out,
            scratch_shapes=[pltpu.SMEM((2 * tm,), jnp.int32),
                            pltpu.VMEM((2, tm * 8, 128), F32),
                            pltpu.VMEM((2, tm * 8, 128), F32),
                            pltpu.SemaphoreType.DMA((2,)),
                            pltpu.SemaphoreType.DMA(())]),
        out_shape=[jax.ShapeDtypeStruct((t, D_MODEL), F32)] + extra_shape,
        compiler_params=_cparams(1),
        name="moe_combine_ln",
    )(dst, ys, h8, rout, ln_g, ln_b, *([w_next] if with_proj else []))
    return tuple(res) if with_proj else res[0]


def _moe_layer(h8, w_group, b_group, w_router, b_router, w1, w3, w2, layer, ln_g, ln_b, w_next=None):
    t = h8.shape[0] // 8
    rout, eid, rank, counts = _router(h8, w_group, b_group, w_router, b_router)
    padded = (counts + MOE_ROWS - 1) // MOE_ROWS * MOE_ROWS
    pad_end = jnp.cumsum(padded)
    pad_start = pad_end - padded
    start_of = jnp.sum(jnp.where(eid[..., None] == jnp.arange(N_EXPERTS), pad_start, 0), axis=-1)
    dst = (start_of + rank).astype(jnp.int32)
    cap = 2 * t + N_EXPERTS * MOE_ROWS
    nblk = cap // MOE_ROWS
    tok = jnp.broadcast_to(jnp.arange(t, dtype=jnp.int32), (2, t))
    blk_e = jnp.minimum(jnp.sum(pad_end[None, :] <= (jnp.arange(nblk) * MOE_ROWS)[:, None], axis=1),
                        N_EXPERTS - 1).astype(jnp.int32)
    n_used = (pad_end[-1] // MOE_ROWS).astype(jnp.int32).reshape(1)
    _, by_row = lax.sort((dst.reshape(-1), tok.reshape(-1)), num_keys=1)
    cstart = jnp.cumsum(counts) - counts
    blk_first = cstart[blk_e] + (jnp.arange(nblk) * MOE_ROWS - pad_start[blk_e])
    blk_last = cstart[blk_e] + jnp.maximum(counts[blk_e] - 1, 0)
    idx = jnp.minimum(blk_first[:, None] + jnp.arange(MOE_ROWS)[None, :], blk_last[:, None])
    src = by_row[jnp.clip(idx, 0, 2 * t - 1)].reshape(-1)
    dst_tiles = jnp.transpose(dst.reshape(2, t // ROW_TILE, ROW_TILE), (1, 0, 2)).reshape(-1)
    ys = _experts(h8, src, blk_e, n_used, w1, w3, w2, layer, cap)
    return _combine_ln(dst_tiles, ys, h8, rout, ln_g, ln_b, w_next)


def kernel(x, ln_mix_g, ln_mix_b, ln_ffn_g, ln_ffn_b, ab_w_in, s5_lam_re, s5_lam_im, s5_log_dt, s5_b_re, s5_b_im,
           s5_c_re, s5_c_im, s5_d, s5_glu_w, s5_glu_b, gla_gate_w, gla_gate_b, gla_norm_g, ab_w_out, c_w_in,
           c_sink, c_w_out, rel_bias, moe_w_group, moe_b_group, moe_w_router, moe_b_router, moe_w1, moe_w3, moe_w2):
    n_batch, seq, _ = x.shape
    t = n_batch * seq
    xt = x.reshape(t, D_MODEL)
    row = lambda v: v.astype(F32).reshape(1, -1)

    w_in0 = jnp.pad(ab_w_in[0], ((0, 0), (0, AB_IN_PAD - AB_IN))).astype(BF16)
    h0, u_t = _inproj0(xt, w_in0)
    n_steps = seq // (S5_SEGS * S5_CHUNK)
    kcat, r1e, r2, coef, ptab = _s5_prep(s5_lam_re[0], s5_lam_im[0], s5_log_dt[0], s5_b_re[0], s5_b_im[0],
                                         s5_c_re[0], s5_c_im[0], s5_d[0], n_steps)
    ya = _s5_mixer(u_t, kcat, r1e, r2, coef, ptab, n_batch, n_steps)
    gw = gla_gate_w[0].astype(F32)
    wg = jnp.zeros((128, 2 * GLA_KEY), F32)
    wg = wg.at[0:GLA_RANK, 0:GLA_KEY].set(gw[0]).at[GLA_RANK:2 * GLA_RANK, GLA_KEY:].set(gw[1]).astype(BF16)
    bg = gla_gate_b[0].astype(F32).reshape(1, 2 * GLA_KEY)
    o_f, o_b = _gla_mixer(h0, wg, bg, n_batch, seq)
    h8 = _mix0_out(ya, o_f, o_b, h0, xt, s5_glu_w[0].astype(BF16), row(s5_glu_b[0]), row(gla_norm_g[0]),
                      ab_w_out[0].astype(BF16), row(ln_mix_g[0]), row(ln_mix_b[0]))
    h, h3 = _moe_layer(h8, moe_w_group[0], moe_b_group[0], moe_w_router[0], moe_b_router[0],
                       moe_w1, moe_w3, moe_w2, 0, row(ln_ffn_g[0]), row(ln_ffn_b[0]), c_w_in[0].astype(BF16))

    o = _attn_mixer(h3, rel_bias, c_sink[0], n_batch, seq)
    h8 = _proj_ln(o, c_w_out[0].astype(BF16), h, row(ln_mix_g[1]), row(ln_mix_b[1]))
    h = _moe_layer(h8, moe_w_group[1], moe_b_group[1], moe_w_router[1], moe_b_router[1],
                   moe_w1, moe_w3, moe_w2, 1, row(ln_ffn_g[1]), row(ln_ffn_b[1]))
    return h.reshape(n_batch, seq, D_MODEL)
```
